```python
import jax, jax.numpy as jnp
from jax import lax
import numpy as np

D_MODEL = 1024
BATCH = 8
SEQ = 4096
DEPTH = 2

CTX_LEN = 256
GRID_W = 64
D_RWKV = D_MODEL // 2
RWKV_HEAD = 64
N_RWKV_HEADS = D_RWKV // RWKV_HEAD
LORA_W = 64
LORA_A = 64
LORA_G = 128
N_DIR = 2
RWKV_COLS = 3 * D_RWKV + N_DIR * LORA_W + N_DIR * LORA_A + LORA_G
N_MLA_HEADS = 8
QK_NOPE = 64
QK_ROPE = 32
QK_DIM = QK_NOPE + QK_ROPE
V_HEAD = 64
Q_LORA = 384
KV_LORA = 256
MLA_COLS = Q_LORA + KV_LORA + QK_ROPE
IN_COLS = RWKV_COLS + MLA_COLS
D_MIX = D_RWKV + N_MLA_HEADS * V_HEAD
D_FF = 2816
CONV_W = 3
Q_BLOCK = 128
ROPE_BASE = 10000.0
DEEPNORM_ALPHA = (2 * DEPTH) ** 0.25
DEEPNORM_BETA = (8 * DEPTH) ** -0.25
LN_EPS = 1e-5
RMS_EPS = 1e-6
GN_EPS = 64e-5

kernel_name = "hymba_rwkv7_mla_convffn_deepnorm_dit"

F32 = jnp.float32


def layer_norm(x, g, b):
    xf = x.astype(F32)
    xc = xf - jnp.mean(xf, axis=-1, keepdims=True)
    var = jnp.mean(xc * xc, axis=-1, keepdims=True)
    return (xc * lax.rsqrt(var + LN_EPS) * g + b).astype(x.dtype)


def rms_norm(x, g):
    xf = x.astype(F32)
    return (xf * lax.rsqrt(jnp.mean(xf * xf, axis=-1, keepdims=True) + RMS_EPS) * g).astype(x.dtype)


def dwconv3(x, w, b=None):
    xp = jnp.pad(x, ((0, 0), (1, 1), (0, 0)))
    y = xp[:, :-2] * w[0] + xp[:, 1:-1] * w[1] + xp[:, 2:] * w[2]
    return y if b is None else y + b


def heads(x, n_heads, head_dim):
    return x.reshape(*x.shape[:-1], n_heads, head_dim)


def axial_rope_tables(n_tok):
    rows = n_tok // GRID_W
    row = jnp.repeat(jnp.arange(rows, dtype=F32), GRID_W)
    col = jnp.tile(jnp.arange(GRID_W, dtype=F32), rows)
    n_pairs = QK_ROPE // 4
    inv = ROPE_BASE ** (-jnp.arange(n_pairs, dtype=F32) / n_pairs)
    ang = jnp.concatenate([row[:, None] * inv, col[:, None] * inv], axis=-1)
    return jnp.cos(ang), jnp.sin(ang)


def apply_rope(x, cos, sin):
    x1, x2 = x[..., 0::2], x[..., 1::2]
    o1 = x1 * cos - x2 * sin
    o2 = x1 * sin + x2 * cos
    return jnp.stack([o1, o2], axis=-1).reshape(x.shape).astype(x.dtype)


def rwkv_prepare(z, w0, w_b, a0, a_b, g_b, k_k, k_a):
    bsz, n, _ = z.shape
    idx = [D_RWKV, 2 * D_RWKV, 3 * D_RWKV, 3 * D_RWKV + N_DIR * LORA_W,
           3 * D_RWKV + N_DIR * LORA_W + N_DIR * LORA_A]
    r, k, v, w_lo, a_lo, g_lo = jnp.split(z, idx, axis=-1)
    w_lo = w_lo.reshape(bsz, n, N_DIR, LORA_W)
    a_lo = a_lo.reshape(bsz, n, N_DIR, LORA_A)
    w_log = -jax.nn.softplus(-(w0 + jnp.einsum('btdl,dlc->btdc', jnp.tanh(w_lo), w_b))) - 0.5
    decay = jnp.exp(-jnp.exp(w_log.astype(F32)))
    a = jax.nn.sigmoid(a0 + jnp.einsum('btdl,dlc->btdc', a_lo, a_b))
    g = jax.nn.sigmoid(g_lo) @ g_b
    kk = heads((k * k_k).astype(F32), N_RWKV_HEADS, RWKV_HEAD)
    kk = kk / jnp.maximum(jnp.sqrt(jnp.sum(kk * kk, axis=-1, keepdims=True)), 1e-12)
    k_dir = k[:, :, None, :] * (1.0 + (a - 1.0) * k_a)
    hd = lambda t: heads(t, N_RWKV_HEADS, RWKV_HEAD)
    return hd(r), hd(v), kk, g, hd(decay), hd(a), hd(k_dir)


def wkv_scan(r, decay, k, v, kk, a, state0, reverse):
    xs = tuple(jnp.moveaxis(t.astype(F32), 1, 0) for t in (r, decay, k, v, kk, a))

    def step(s, inp):
        r_t, w_t, k_t, v_t, kk_t, a_t = inp
        sa = jnp.einsum('bhvk,bhk->bhv', s, kk_t)
        s = (s * w_t[:, :, None, :]
             - sa[..., None] * (kk_t * a_t)[:, :, None, :]
             + v_t[..., None] * k_t[:, :, None, :])
        return s, jnp.einsum('bhvk,bhk->bhv', s, r_t)

    s_fin, ys = lax.scan(step, state0, xs, reverse=reverse)
    return s_fin, jnp.moveaxis(ys, 0, 1)


def rwkv_output(y, r, v, g, k_dir, gn_g, gn_b, r_k, dtype):
    mu = jnp.mean(y, axis=-1, keepdims=True)
    yc = y - mu
    var = jnp.mean(yc * yc, axis=-1, keepdims=True)
    gn = (yc * lax.rsqrt(var + GN_EPS) * gn_g.reshape(N_RWKV_HEADS, RWKV_HEAD)
          + gn_b.reshape(N_RWKV_HEADS, RWKV_HEAD))
    bonus = jnp.sum(r[:, :, None] * k_dir * r_k, axis=(2, -1))[..., None] * v
    out = (gn + bonus).reshape(*y.shape[:2], D_RWKV) * g
    return out.astype(dtype)


def mla_prepare(z, q_norm_g, w_uq, kv_norm_g, w_ukv, cos, sin):
    bsz, n, _ = z.shape
    c_q, c_kv, k_r = jnp.split(z, [Q_LORA, Q_LORA + KV_LORA], axis=-1)
    q = (rms_norm(c_q, q_norm_g) @ w_uq).reshape(bsz, n, N_MLA_HEADS, QK_DIM)
    kv = (rms_norm(c_kv, kv_norm_g) @ w_ukv).reshape(bsz, n, N_MLA_HEADS, QK_NOPE + V_HEAD)
    q_nope, q_rope = q[..., :QK_NOPE], q[..., QK_NOPE:]
    k_nope, v = kv[..., :QK_NOPE], kv[..., QK_NOPE:]
    if cos is not None:
        q_rope = apply_rope(q_rope, cos[:, None, :], sin[:, None, :])
        k_r = apply_rope(k_r, cos, sin)
    k = jnp.concatenate([k_nope, jnp.broadcast_to(k_r[:, :, None, :], (bsz, n, N_MLA_HEADS, QK_ROPE))], axis=-1)
    q = jnp.concatenate([q_nope, q_rope], axis=-1)
    return q, k, v


def attend(q, k, v):
    s = jnp.einsum('bqhd,bkhd->bhqk', q, k).astype(F32) * (QK_DIM ** -0.5)
    p = jax.nn.softmax(s, axis=-1).astype(v.dtype)
    return jnp.einsum('bhqk,bkhd->bqhd', p, v)


def mla_latent_attention(q, k_lat, v_lat, k_ctx, v_ctx):
    bsz, n, h, dq = q.shape
    k_all = jnp.concatenate([k_lat, k_ctx], axis=1)
    v_all = jnp.concatenate([v_lat, v_ctx], axis=1)
    qb = q.reshape(bsz, n // Q_BLOCK, Q_BLOCK, h, dq).transpose(1, 0, 2, 3, 4)
    ob = lax.map(lambda qi: attend(qi, k_all, v_all), qb)
    return ob.transpose(1, 0, 2, 3, 4).reshape(bsz, n, h * V_HEAD)


def token_mixer(u_lat, u_ctx, cos, sin, need_ctx, w_in, rwkv_conv, w0, w_b, a0, a_b, g_b, k_k, k_a,
                r_k, gn_g, gn_b, q_norm_g, w_uq, kv_norm_g, w_ukv, w_o):
    bsz = u_lat.shape[0]
    p_lat = u_lat @ w_in
    p_ctx = u_ctx @ w_in
    rl = rwkv_prepare(dwconv3(p_lat[..., :RWKV_COLS], rwkv_conv), w0, w_b, a0, a_b, g_b, k_k, k_a)
    rc = rwkv_prepare(dwconv3(p_ctx[..., :RWKV_COLS], rwkv_conv), w0, w_b, a0, a_b, g_b, k_k, k_a)
    r_l, v_l, kk_l, g_l, dec_l, a_l, kd_l = rl
    r_c, v_c, kk_c, g_c, dec_c, a_c, kd_c = rc
    ys_lat, ys_ctx = [], []
    for d in range(N_DIR):
        rev = d == 1
        s0 = jnp.zeros((bsz, N_RWKV_HEADS, RWKV_HEAD, RWKV_HEAD), F32)
        s_ctx, y_c = wkv_scan(r_c, dec_c[:, :, d], kd_c[:, :, d], v_c, kk_c, a_c[:, :, d], s0, rev)
        _, y_l = wkv_scan(r_l, dec_l[:, :, d], kd_l[:, :, d], v_l, kk_l, a_l[:, :, d], s_ctx, rev)
        ys_lat.append(y_l)
        ys_ctx.append(y_c)
    rwkv_lat = rwkv_output(ys_lat[0] + ys_lat[1], r_l, v_l, g_l, kd_l, gn_g, gn_b, r_k, u_lat.dtype)
    q_l, k_l, v_lm = mla_prepare(p_lat[..., RWKV_COLS:], q_norm_g, w_uq, kv_norm_g, w_ukv, cos, sin)
    q_c, k_c, v_cm = mla_prepare(p_ctx[..., RWKV_COLS:], q_norm_g, w_uq, kv_norm_g, w_ukv, None, None)
    mla_lat = mla_latent_attention(q_l, k_l, v_lm, k_c, v_cm)
    m_lat = jnp.concatenate([rwkv_lat, mla_lat], axis=-1) @ w_o
    if not need_ctx:
        return m_lat, None
    rwkv_ctx = rwkv_output(ys_ctx[0] + ys_ctx[1], r_c, v_c, g_c, kd_c, gn_g, gn_b, r_k, u_ctx.dtype)
    mla_ctx = attend(q_c, k_c, v_cm).reshape(bsz, u_ctx.shape[1], N_MLA_HEADS * V_HEAD)
    m_ctx = jnp.concatenate([rwkv_ctx, mla_ctx], axis=-1) @ w_o
    return m_lat, m_ctx


def conv_ffn(u, w_up, conv_w, conv_b, w_down):
    h = dwconv3(u @ w_up, conv_w, conv_b)
    h_gate, h_val = jnp.split(h, 2, axis=-1)
    return (jax.nn.silu(h_gate) * h_val) @ w_down


def setup_inputs(seed: int = 0) -> dict:
    key = jax.random.key(seed)
    ks = iter(jax.random.split(key, 40))
    L = DEPTH

    def nrm(shape, scale):
        return scale * jax.random.normal(next(ks), shape, F32)

    center_tap = jnp.array([0.0, 1.0, 0.0], F32)[None, :, None]
    return {
        "x": nrm((BATCH, SEQ, D_MODEL), 1.0),
        "c": nrm((BATCH, D_MODEL), 1.0),
        "ctx": nrm((BATCH, CTX_LEN, D_MODEL), 1.0),
        "c_ctx": nrm((D_MODEL,), 1.0),
        "w_ada": nrm((L, D_MODEL, 6 * D_MODEL), 0.5 * D_MODEL ** -0.5),
        "b_ada": nrm((L, 6 * D_MODEL), 0.01),
        "w_in": nrm((L, D_MODEL, IN_COLS), D_MODEL ** -0.5),
        "rwkv_conv": center_tap + nrm((L, CONV_W, RWKV_COLS), 0.3),
        "w0": -6.0 + 5.0 * jax.random.uniform(next(ks), (L, N_DIR, D_RWKV), F32),
        "w_b": nrm((L, N_DIR, LORA_W, D_RWKV), 0.1),
        "a0": nrm((L, N_DIR, D_RWKV), 0.5),
        "a_b": nrm((L, N_DIR, LORA_A, D_RWKV), LORA_A ** -0.5),
        "g_b": nrm((L, LORA_G, D_RWKV), LORA_G ** -0.5),
        "k_k": 0.85 + nrm((L, D_RWKV), 0.1),
        "k_a": 1.0 + nrm((L, D_RWKV), 0.1),
        "r_k": nrm((L, N_RWKV_HEADS, RWKV_HEAD), 0.1),
        "gn_g": 1.0 + nrm((L, D_RWKV), 0.1),
        "gn_b": nrm((L, D_RWKV), 0.01),
        "q_norm_g": 1.0 + nrm((L, Q_LORA), 0.1),
        "w_uq": nrm((L, Q_LORA, N_MLA_HEADS * QK_DIM), Q_LORA ** -0.5),
        "kv_norm_g": 1.0 + nrm((L, KV_LORA), 0.1),
        "w_ukv": nrm((L, KV_LORA, N_MLA_HEADS * (QK_NOPE + V_HEAD)), KV_LORA ** -0.5),
        "w_o": nrm((L, D_MIX, D_MODEL), DEEPNORM_BETA * D_MIX ** -0.5),
        "ln1_g": 1.0 + nrm((L, D_MODEL), 0.1),
        "ln1_b": nrm((L, D_MODEL), 0.01),
        "w_up": nrm((L, D_MODEL, 2 * D_FF), D_MODEL ** -0.5),
        "ffn_conv_w": center_tap + nrm((L, CONV_W, 2 * D_FF), 0.3),
        "ffn_conv_b": nrm((L, 2 * D_FF), 0.01),
        "w_down": nrm((L, D_FF, D_MODEL), DEEPNORM_BETA * D_FF ** -0.5),
        "ln2_g": 1.0 + nrm((L, D_MODEL), 0.1),
        "ln2_b": nrm((L, D_MODEL), 0.01),
    }


def reference(x, c, ctx, c_ctx, w_ada, b_ada, w_in, rwkv_conv, w0, w_b, a0, a_b, g_b, k_k, k_a, r_k,
              gn_g, gn_b, q_norm_g, w_uq, kv_norm_g, w_ukv, w_o, ln1_g, ln1_b, w_up, ffn_conv_w,
              ffn_conv_b, w_down, ln2_g, ln2_b):
    alpha = DEEPNORM_ALPHA
    cos, sin = axial_rope_tables(x.shape[1])
    for l in range(DEPTH):
        need_ctx = l < DEPTH - 1
        mod_lat = jax.nn.silu(c) @ w_ada[l] + b_ada[l]
        mod_ctx = jax.nn.silu(c_ctx) @ w_ada[l] + b_ada[l]
        sh1, sc1, g1, sh2, sc2, g2 = jnp.split(mod_lat[:, None, :], 6, axis=-1)
        csh1, csc1, cg1, csh2, csc2, cg2 = jnp.split(mod_ctx, 6, axis=-1)
        m_lat, m_ctx = token_mixer(
            x * (1.0 + sc1) + sh1, ctx * (1.0 + csc1) + csh1, cos, sin, need_ctx,
            w_in[l], rwkv_conv[l], w0[l], w_b[l], a0[l], a_b[l], g_b[l], k_k[l], k_a[l], r_k[l],
            gn_g[l], gn_b[l], q_norm_g[l], w_uq[l], kv_norm_g[l], w_ukv[l], w_o[l])
        x = layer_norm(alpha * x + g1 * m_lat, ln1_g[l], ln1_b[l])
        f_lat = conv_ffn(x * (1.0 + sc2) + sh2, w_up[l], ffn_conv_w[l], ffn_conv_b[l], w_down[l])
        x = layer_norm(alpha * x + g2 * f_lat, ln2_g[l], ln2_b[l])
        if need_ctx:
            ctx = layer_norm(alpha * ctx + cg1 * m_ctx, ln1_g[l], ln1_b[l])
            f_ctx = conv_ffn(ctx * (1.0 + csc2) + csh2, w_up[l], ffn_conv_w[l], ffn_conv_b[l], w_down[l])
            ctx = layer_norm(alpha * ctx + cg2 * f_ctx, ln2_g[l], ln2_b[l])
    return x
```

```python
import functools

import jax
import jax.numpy as jnp
from jax import lax
from jax.experimental import pallas as pl
from jax.experimental.pallas import tpu as pltpu

F32 = jnp.float32
BF16 = jnp.bfloat16

D_MODEL = 1024
DEPTH = 2
GRID_W = 64
D_RWKV = 512
RWKV_HEAD = 64
N_HEADS = 8
LORA_W = 64
LORA_A = 64
LORA_G = 128
RWKV_COLS = 3 * D_RWKV + 2 * LORA_W + 2 * LORA_A + LORA_G
QK_NOPE = 64
QK_ROPE = 32
QK_DIM = QK_NOPE + QK_ROPE
V_HEAD = 64
Q_LORA = 384
KV_LORA = 256
D_FF = 2816
ROPE_BASE = 10000.0
DEEPNORM_ALPHA = (2 * DEPTH) ** 0.25
LN_EPS = 1e-5
RMS_EPS = 1e-6
GN_EPS = 64e-5

LANES = 128
SUBLANES = 8
HEAD_PAD = LANES
D_HEADS = N_HEADS * HEAD_PAD
TM = 256
CHUNK = 64
LOG_CHUNK = 6
PAIR = 2 * RWKV_HEAD
N_PAIRS = N_HEADS // 2
FF_CHUNK = 1408
N_FF = D_FF // FF_CHUNK
IN_COLS_PAD = RWKV_COLS + Q_LORA + KV_LORA + 2 * LANES
VMEM_LIMIT = 56 << 20


def _cparams(sem):
    return pltpu.CompilerParams(dimension_semantics=sem, vmem_limit_bytes=VMEM_LIMIT)


def _mm(a, b):
    return jnp.dot(a.astype(BF16), b.astype(BF16), preferred_element_type=F32)


def _mm_nt(a, b):
    return lax.dot_general(a.astype(BF16), b.astype(BF16), (((1,), (1,)), ((), ())),
                           preferred_element_type=F32)


def _split2(x):
    hi = x.astype(BF16)
    lo = (x - hi.astype(F32)).astype(BF16)
    return hi, lo


def _mm3(a, b):
    ah, al = _split2(a)
    bh, bl = _split2(b)
    d = functools.partial(jnp.dot, preferred_element_type=F32)
    return d(ah, bh) + (d(ah, bl) + d(al, bh))


def _mm_exact_rhs(a, b_bf16):
    a1 = a.astype(BF16)
    r1 = a - a1.astype(F32)
    a2 = r1.astype(BF16)
    a3 = (r1 - a2.astype(F32)).astype(BF16)
    d = functools.partial(jnp.dot, preferred_element_type=F32)
    return d(a1, b_bf16) + (d(a2, b_bf16) + d(a3, b_bf16))


def _mm_exact_lhs(a_bf16, b):
    b1 = b.astype(BF16)
    r1 = b - b1.astype(F32)
    b2 = r1.astype(BF16)
    b3 = (r1 - b2.astype(F32)).astype(BF16)
    d = functools.partial(jnp.dot, preferred_element_type=F32)
    return d(a_bf16, b1) + (d(a_bf16, b2) + d(a_bf16, b3))


def _sigmoid(x):
    return 1.0 / (1.0 + jnp.exp(-x))


def _layer_norm(x, g, b):
    mu = jnp.mean(x, axis=-1, keepdims=True)
    xc = x - mu
    var = jnp.mean(xc * xc, axis=-1, keepdims=True)
    return xc * lax.rsqrt(var + LN_EPS) * g + b


def _rms_norm(x, g):
    return x * lax.rsqrt(jnp.mean(x * x, axis=-1, keepdims=True) + RMS_EPS) * g


ADA_COLS = 1536


def _ada_kernel(c_ref, w_ref, b_ref, o_ref):
    c = c_ref[...]
    o_ref[...] = _mm3(c * _sigmoid(c), w_ref[...]) + b_ref[...]


def _ada_call(cc, w_ada, b_ada):
    n_l = w_ada.shape[0]
    rows = cc.shape[0]
    return pl.pallas_call(
        _ada_kernel,
        grid=(n_l, 6 * D_MODEL // ADA_COLS),
        in_specs=[
            pl.BlockSpec((rows, D_MODEL), lambda l, j: (0, 0)),
            pl.BlockSpec((None, D_MODEL, ADA_COLS), lambda l, j: (l, 0, j)),
            pl.BlockSpec((None, 1, ADA_COLS), lambda l, j: (l, 0, j)),
        ],
        out_specs=pl.BlockSpec((None, rows, ADA_COLS), lambda l, j: (l, 0, j)),
        out_shape=jax.ShapeDtypeStruct((n_l, rows, 6 * D_MODEL), F32),
        compiler_params=_cparams(("arbitrary", "arbitrary")),
        name="ada",
    )(cc, w_ada, b_ada.reshape(n_l, 1, 6 * D_MODEL))


def _in_kernel(x_ref, mod_ref, w_ref, p_ref, cq_ref, ckv_ref, kr_ref):
    m = mod_ref[...]
    u = (x_ref[...] * (1.0 + m[1:2]) + m[0:1]).astype(BF16)
    d = functools.partial(jnp.dot, preferred_element_type=F32)
    o = 0
    for ref in (p_ref, cq_ref, ckv_ref, kr_ref):
        n = ref.shape[-1]
        ref[...] = d(u, w_ref[:, o:o + n])
        o += n


def _mod_spec(n_ct):
    return pl.BlockSpec((None, SUBLANES, D_MODEL), lambda b, i, *_: (2 * b + (i >= n_ct).astype(jnp.int32), 0, 0))


def _row_spec(cols, rows=TM):
    return pl.BlockSpec((None, rows, cols), lambda b, i, *_: (b, i, 0))


def _full_spec(shape):
    nd = len(shape)
    return pl.BlockSpec(shape, lambda *_: (0,) * nd)


def _in_call(xs, mod, w_in_p, n_ct):
    bsz, tt, _ = xs.shape
    nt = tt // TM
    widths = (RWKV_COLS, Q_LORA, KV_LORA, 2 * LANES)
    return pl.pallas_call(
        _in_kernel,
        grid=(bsz, nt),
        in_specs=[_row_spec(D_MODEL), _mod_spec(n_ct), _full_spec(w_in_p.shape)],
        out_specs=[_row_spec(w) for w in widths],
        out_shape=[jax.ShapeDtypeStruct((bsz, tt, w), F32) for w in widths],
        compiler_params=_cparams(("parallel", "parallel")),
        name="in_proj",
    )(xs, mod, w_in_p)


def _halo_specs(cols, tt):
    per = TM // SUBLANES
    last = tt // SUBLANES - 1
    prev = pl.BlockSpec((None, SUBLANES, cols), lambda b, i, *_: (b, jnp.maximum(i * per - 1, 0), 0))
    nxt = pl.BlockSpec((None, SUBLANES, cols), lambda b, i, *_: (b, jnp.minimum((i + 1) * per, last), 0))
    return prev, nxt


def _halo_valid(i, n_ct, nt):
    pv = jnp.where((i != 0) & (i != n_ct), 1.0, 0.0).astype(F32)
    nv = jnp.where((i != n_ct - 1) & (i != nt - 1), 1.0, 0.0).astype(F32)
    return pv, nv


def _seg_sum(x, ones_bd):
    return _mm_exact_rhs(x, ones_bd)


def _prep_kernel(p_ref, pp_ref, pn_ref, conv_ref, w0_ref, wb_ref, a0_ref, ab_ref, gb_ref, vec_ref, ones_ref,
                 r_ref, v_ref, kk_ref, g_ref, bv_ref, lw_ref, kd_ref, bb_ref, *, n_ct, nt):
    i = pl.program_id(1)
    pv, nv = _halo_valid(i, n_ct, nt)
    p = p_ref[...]
    cw = conv_ref[...]
    rows = lax.broadcasted_iota(jnp.int32, (TM, 1), 0)
    p_prev = jnp.where(rows == 0, pp_ref[SUBLANES - 1:SUBLANES, :] * pv, pltpu.roll(p, 1, 0))
    p_next = jnp.where(rows == TM - 1, pn_ref[0:1, :] * nv, pltpu.roll(p, TM - 1, 0))
    z = p_prev * cw[0:1] + p * cw[1:2] + p_next * cw[2:3]

    c = D_RWKV
    r, k, v = z[:, 0:c], z[:, c:2 * c], z[:, 2 * c:3 * c]
    o = 3 * c
    w_lo = z[:, o:o + 2 * LORA_W]
    a_lo = z[:, o + 2 * LORA_W:o + 2 * LORA_W + 2 * LORA_A]
    g_lo = z[:, o + 2 * LORA_W + 2 * LORA_A:]
    vec = vec_ref[...]
    k_k, k_a, r_k = vec[0:1], vec[1:2], vec[2:3]
    ones_bd = ones_ref[...]

    t = -(w0_ref[0:1, :] + _mm(jnp.tanh(w_lo), wb_ref[...]))
    sp = jnp.maximum(t, 0.0) + jnp.log(1.0 + jnp.exp(-jnp.abs(t)))
    lw = -jnp.exp(-sp - 0.5)
    a = _sigmoid(a0_ref[0:1, :] + _mm(a_lo, ab_ref[...]))
    g = _mm(_sigmoid(g_lo), gb_ref[...])

    kk = k * k_k
    nrm = jnp.sqrt(_seg_sum(kk * kk, ones_bd))
    kk = kk / jnp.maximum(nrm, 1e-12)
    kd0 = k * (1.0 + (a[:, :c] - 1.0) * k_a)
    kd1 = k * (1.0 + (a[:, c:] - 1.0) * k_a)
    bonus = _seg_sum(r * (kd0 + kd1) * r_k, ones_bd) * v

    r_ref[...] = r
    v_ref[...] = v
    kk_ref[...] = kk
    g_ref[...] = g
    bv_ref[...] = bonus
    lw_ref[0] = lw[:, :c]
    lw_ref[1] = lw[:, c:]
    kd_ref[0] = kd0
    kd_ref[1] = kd1
    bb_ref[0] = kk * a[:, :c]
    bb_ref[1] = kk * a[:, c:]


def _prep_call(p_rwkv, prm, n_ct):
    bsz, tt, _ = p_rwkv.shape
    nt = tt // TM
    prev, nxt = _halo_specs(RWKV_COLS, tt)
    shared = jax.ShapeDtypeStruct((bsz, tt, D_RWKV), F32)
    per_dir = jax.ShapeDtypeStruct((2, bsz, tt, D_RWKV), F32)
    dir_spec = pl.BlockSpec((2, None, TM, D_RWKV), lambda b, i: (0, b, i, 0))
    params = [prm[n] for n in ("conv", "w0", "wb", "a0", "ab", "gb", "vec", "ones_bd")]
    return pl.pallas_call(
        functools.partial(_prep_kernel, n_ct=n_ct, nt=nt),
        grid=(bsz, nt),
        in_specs=[_row_spec(RWKV_COLS), prev, nxt] + [_full_spec(a.shape) for a in params],
        out_specs=[_row_spec(D_RWKV)] * 5 + [dir_spec] * 3,
        out_shape=[shared] * 5 + [per_dir] * 3,
        compiler_params=_cparams(("parallel", "parallel")),
        name="rwkv_prep",
    )(p_rwkv, p_rwkv, p_rwkv, *params)


def _wkv_kernel(r_ref, v_ref, kk_ref, lw_ref, kd_ref, bb_ref, y_ref, s_ref):
    step = pl.program_id(2)
    sign = 1 - 2 * pl.program_id(0)

    @pl.when(step == 0)
    def _():
        s_ref[...] = jnp.zeros_like(s_ref)

    n2 = 2 * CHUNK
    ri = lax.broadcasted_iota(jnp.int32, (n2, n2), 0)
    ci = lax.broadcasted_iota(jnp.int32, (n2, n2), 1)
    same_head = (ri >> LOG_CHUNK) == (ci >> LOG_CHUNK)
    rt, ct = ri & (CHUNK - 1), ci & (CHUNK - 1)
    order = (ct - rt) * sign
    strict = same_head & (order < 0)
    incl = same_head & (order <= 0)
    eye = ri == ci

    ti = lax.broadcasted_iota(jnp.int32, (CHUNK, CHUNK), 0)
    tj = lax.broadcasted_iota(jnp.int32, (CHUNK, CHUNK), 1)
    tri = jnp.where((tj - ti) * sign <= 0, 1.0, 0.0).astype(BF16)

    lw = lw_ref[...]
    lc = _mm_exact_lhs(tri, lw)
    lend = jnp.sum(lw, axis=0, keepdims=True)
    e_in = jnp.exp(lc)
    e_ex = jnp.exp(lc - lw)
    e_neg = jnp.exp(-lc)
    e_end = jnp.exp(lend - lc)
    p_end = jnp.exp(lend)

    kk, kd, bb = kk_ref[...], kd_ref[...], bb_ref[...]
    kt_all = kk * e_ex
    rt_all = r_ref[...] * e_in
    kh_all = kd * e_neg
    bh_all = bb * e_neg
    kc_all = kd * e_end
    bc_all = bb * e_end
    v_all = v_ref[...]

    def bdiag(x):
        return jnp.where(same_head, jnp.concatenate([x, x], axis=0), 0.0)

    ys = []
    for p in range(N_PAIRS):
        sl = slice(p * PAIR, (p + 1) * PAIR)
        kt, rr, vv = bdiag(kt_all[:, sl]), bdiag(rt_all[:, sl]), bdiag(v_all[:, sl])
        kc, bc = bdiag(kc_all[:, sl]), bdiag(bc_all[:, sl])
        bh, kh = bh_all[:, sl], kh_all[:, sl]
        gram = _mm_nt(jnp.concatenate([kt, rr], axis=0), jnp.concatenate([bh, bh, kh, kh], axis=0))
        a_kb = jnp.where(strict, gram[:n2, :n2], 0.0)
        a_kk = jnp.where(strict, gram[:n2, n2:], 0.0)
        a_rb = jnp.where(incl, gram[n2:, :n2], 0.0)
        a_rk = jnp.where(incl, gram[n2:, n2:], 0.0)

        t_inv = jnp.where(eye, 1.0, 0.0) - jnp.where((rt >> 1) == (ct >> 1), a_kb, 0.0)
        for ls in range(1, LOG_CHUNK):
            off = ((rt >> (ls + 1)) == (ct >> (ls + 1))) & ((rt >> ls) != (ct >> ls))
            t_inv = t_inv - _mm(t_inv, _mm(jnp.where(off, a_kb, 0.0), t_inv))

        uw = _mm(t_inv, jnp.concatenate([_mm(a_kk, vv), kt], axis=1))
        yq = jnp.concatenate([_mm(a_rk, vv), rr], axis=1) - _mm(a_rb, uw)
        diag_p = jnp.where(eye, jnp.broadcast_to(p_end[:, sl], (n2, n2)), 0.0)
        nm = jnp.concatenate([_mm(kc.T, vv), diag_p], axis=1) - _mm(bc.T, uw)

        s0 = s_ref[p]
        qm = _mm3(jnp.concatenate([yq[:, n2:], nm[:, n2:]], axis=0), s0)
        y_bd = qm[:n2] + yq[:, :n2]
        s_ref[p] = qm[n2:] + nm[:, :n2]
        ys.append(y_bd[:CHUNK] + y_bd[CHUNK:])
    y_ref[...] = jnp.concatenate(ys, axis=1)


def _wkv_call(r, v, kk, lw, kd, bb, n_ct):
    bsz, tt, c = r.shape
    nc = tt // CHUNK
    n_cc = n_ct * TM // CHUNK

    def chunk(d, s):
        rev = jnp.where(s < n_cc, n_cc - 1 - s, nc - 1 - (s - n_cc))
        return jnp.where(d == 0, s, rev)

    shared = pl.BlockSpec((None, CHUNK, c), lambda d, b, s: (b, chunk(d, s), 0))
    per_dir = pl.BlockSpec((None, None, CHUNK, c), lambda d, b, s: (d, b, chunk(d, s), 0))
    return pl.pallas_call(
        _wkv_kernel,
        grid=(2, bsz, nc),
        in_specs=[shared, shared, shared, per_dir, per_dir, per_dir],
        out_specs=per_dir,
        out_shape=jax.ShapeDtypeStruct((2, bsz, tt, c), F32),
        scratch_shapes=[pltpu.VMEM((N_PAIRS, PAIR, PAIR), F32)],
        compiler_params=_cparams(("parallel", "parallel", "arbitrary")),
        name="wkv_scan",
    )(r, v, kk, lw, kd, bb)


def _mla_kernel(cq_ref, ckv_ref, kr_ref, cos_ref, sin_ref, ng_ref, wq1_ref, wq2_ref, wkn_ref, wv_ref,
                q_ref, k_ref, v_ref):
    cos, sin = cos_ref[...], sin_ref[...]
    cos8 = jnp.concatenate([cos] * N_HEADS, axis=1)
    sin8 = jnp.concatenate([sin] * N_HEADS, axis=1)
    cqn = _rms_norm(cq_ref[...], ng_ref[0:1, :Q_LORA]).astype(BF16)
    d = functools.partial(jnp.dot, preferred_element_type=F32)
    q = d(cqn, wq1_ref[...]) * cos8 + d(cqn, wq2_ref[...]) * sin8
    q_ref[...] = (q * (QK_DIM ** -0.5)).astype(BF16)
    ckvn = _rms_norm(ckv_ref[...], ng_ref[1:2, :KV_LORA]).astype(BF16)
    kr = kr_ref[:, :LANES] * cos + kr_ref[:, LANES:] * sin
    k_ref[...] = (d(ckvn, wkn_ref[...]) + jnp.concatenate([kr] * N_HEADS, axis=1)).astype(BF16)
    v_ref[...] = d(ckvn, wv_ref[...]).astype(BF16)


def _mla_call(cq, ckv, kr2, cos_t, sin_t, prm):
    bsz, tt, _ = cq.shape
    nt = tt // TM
    tab = pl.BlockSpec((TM, LANES), lambda b, i: (i, 0))
    params = [prm[n] for n in ("norm_g", "wq1", "wq2", "wkn", "wv")]
    out = jax.ShapeDtypeStruct((bsz, tt, D_HEADS), BF16)
    return pl.pallas_call(
        _mla_kernel,
        grid=(bsz, nt),
        in_specs=[_row_spec(Q_LORA), _row_spec(KV_LORA), _row_spec(2 * LANES), tab, tab]
        + [_full_spec(a.shape) for a in params],
        out_specs=[_row_spec(D_HEADS)] * 3,
        out_shape=[out] * 3,
        compiler_params=_cparams(("parallel", "parallel")),
        name="mla_prep",
    )(cq, ckv, kr2, cos_t, sin_t, *params)


def _attn_kernel(q_ref, k_ref, v_ref, o_ref, *, n_ctx_rows, n_ct):
    i = pl.program_id(2)

    def attend(n_keys):
        s = _mm_nt(q_ref[...], k_ref[0:n_keys, :])
        p = jnp.exp(s - jnp.max(s, axis=-1, keepdims=True))
        l = jnp.sum(p, axis=-1, keepdims=True)
        o_ref[...] = (_mm(p, v_ref[0:n_keys, :]) / l).astype(o_ref.dtype)

    @pl.when(i < n_ct)
    def _():
        attend(n_ctx_rows)

    @pl.when(i >= n_ct)
    def _():
        attend(k_ref.shape[0])


def _attn_call(q, k, v, n_ct):
    bsz, tt, _ = q.shape
    nt = tt // TM
    q_spec = pl.BlockSpec((None, TM, HEAD_PAD), lambda b, h, i: (b, i, h))
    kv_spec = pl.BlockSpec((None, tt, HEAD_PAD), lambda b, h, i: (b, 0, h))
    return pl.pallas_call(
        functools.partial(_attn_kernel, n_ctx_rows=n_ct * TM, n_ct=n_ct),
        grid=(bsz, N_HEADS, nt),
        in_specs=[q_spec, kv_spec, kv_spec],
        out_specs=q_spec,
        out_shape=jax.ShapeDtypeStruct((bsz, tt, D_HEADS), BF16),
        compiler_params=_cparams(("parallel", "parallel", "arbitrary")),
        name="attention",
    )(q, k, v)


def _out_kernel(y_ref, bv_ref, g_ref, o_ref, x_ref, mod_ref, gn_ref, ones_ref, wor_ref, wom_ref, ln_ref, out_ref):
    ones_bd = ones_ref[...]
    y = y_ref[0] + y_ref[1]
    inv_n = 1.0 / RWKV_HEAD
    yc = y - _seg_sum(y, ones_bd) * inv_n
    var = _seg_sum(yc * yc, ones_bd) * inv_n
    gn = yc * lax.rsqrt(var + GN_EPS) * gn_ref[0:1] + gn_ref[1:2]
    ro = (gn + bv_ref[...]) * g_ref[...]
    m = _mm(ro, wor_ref[...]) + jnp.dot(o_ref[...], wom_ref[...], preferred_element_type=F32)
    mod = mod_ref[...]
    out_ref[...] = _layer_norm(DEEPNORM_ALPHA * x_ref[...] + mod[2:3] * m, ln_ref[0:1], ln_ref[1:2])


def _out_call(y, bv, g, o, xs, mod, prm, n_ct):
    bsz, tt, _ = xs.shape
    nt = tt // TM
    y_spec = pl.BlockSpec((2, None, TM, D_RWKV), lambda b, i: (0, b, i, 0))
    params = [prm[n] for n in ("gn", "ones_bd", "wo_r", "wo_m", "ln1")]
    return pl.pallas_call(
        _out_kernel,
        grid=(bsz, nt),
        in_specs=[y_spec, _row_spec(D_RWKV), _row_spec(D_RWKV), _row_spec(D_HEADS), _row_spec(D_MODEL),
                  _mod_spec(n_ct)] + [_full_spec(a.shape) for a in params],
        out_specs=_row_spec(D_MODEL),
        out_shape=jax.ShapeDtypeStruct((bsz, tt, D_MODEL), F32),
        compiler_params=_cparams(("parallel", "parallel")),
        name="mixer_out",
    )(y, bv, g, o, xs, mod, *params)


def _ffn_kernel(x_ref, xp_ref, xn_ref, mod_ref, wg_ref, wv_ref, cg_ref, cv_ref, wd_ref, ln_ref, out_ref,
                u_ref, acc_ref, *, n_ct, nt):
    i = pl.program_id(1)
    j = pl.program_id(2)
    mod = mod_ref[...]

    @pl.when(j == 0)
    def _():
        pv, nv = _halo_valid(i, n_ct, nt)
        sc, sh = 1.0 + mod[4:5], mod[3:4]
        u = jnp.concatenate([(xp_ref[...] * sc + sh) * pv, x_ref[...] * sc + sh, (xn_ref[...] * sc + sh) * nv], axis=0)
        u_ref[...] = u.astype(BF16)
        acc_ref[...] = jnp.zeros_like(acc_ref)

    u = u_ref[...]
    rows = TM + 2 * SUBLANES

    def conv(w_ref, c_ref):
        h = jnp.dot(u, w_ref[...], preferred_element_type=F32)
        c = c_ref[...]
        lo, hi = SUBLANES, SUBLANES + TM
        return (pltpu.roll(h, 1, 0)[lo:hi] * c[0:1] + h[lo:hi] * c[1:2]
                + pltpu.roll(h, rows - 1, 0)[lo:hi] * c[2:3] + c[3:4])

    hg = conv(wg_ref, cg_ref)
    hv = conv(wv_ref, cv_ref)
    act = hg * _sigmoid(hg) * hv
    acc_ref[...] += _mm(act, wd_ref[...])

    @pl.when(j == N_FF - 1)
    def _():
        out_ref[...] = _layer_norm(DEEPNORM_ALPHA * x_ref[...] + mod[5:6] * acc_ref[...], ln_ref[0:1], ln_ref[1:2])


def _ffn_call(xs, mod, prm, n_ct):
    bsz, tt, _ = xs.shape
    nt = tt // TM
    prev, nxt = _halo_specs(D_MODEL, tt)
    w_up, conv, w_down, ln2 = prm["w_up"], prm["ffn_conv"], prm["w_down"], prm["ln2"]
    return pl.pallas_call(
        functools.partial(_ffn_kernel, n_ct=n_ct, nt=nt),
        grid=(bsz, nt, N_FF),
        in_specs=[
            _row_spec(D_MODEL), prev, nxt, _mod_spec(n_ct),
            pl.BlockSpec((D_MODEL, FF_CHUNK), lambda b, i, j: (0, j)),
            pl.BlockSpec((D_MODEL, FF_CHUNK), lambda b, i, j: (0, N_FF + j)),
            pl.BlockSpec((SUBLANES, FF_CHUNK), lambda b, i, j: (0, j)),
            pl.BlockSpec((SUBLANES, FF_CHUNK), lambda b, i, j: (0, N_FF + j)),
            pl.BlockSpec((FF_CHUNK, D_MODEL), lambda b, i, j: (j, 0)),
            _full_spec(ln2.shape),
        ],
        out_specs=_row_spec(D_MODEL),
        out_shape=jax.ShapeDtypeStruct((bsz, tt, D_MODEL), F32),
        scratch_shapes=[pltpu.VMEM((TM + 2 * SUBLANES, D_MODEL), BF16), pltpu.VMEM((TM, D_MODEL), F32)],
        compiler_params=_cparams(("parallel", "parallel", "arbitrary")),
        name="conv_ffn",
    )(xs, xs, xs, mod, w_up, w_up, conv, conv, w_down, ln2)


def _pad_rows(rows, width):
    out = jnp.zeros((SUBLANES, width), F32)
    for n, r in enumerate(rows):
        out = out.at[n, :r.shape[0]].set(r.astype(F32))
    return out


def _pair_swap(w):
    w2 = w.reshape(*w.shape[:-1], w.shape[-1] // 2, 2)
    return jnp.stack([-w2[..., 1], w2[..., 0]], axis=-1).reshape(w.shape)


def _layer_params(l, w_in, rwkv_conv, w0, w_b, a0, a_b, g_b, k_k, k_a, r_k, gn_g, gn_b, q_norm_g, w_uq,
                  kv_norm_g, w_ukv, w_o, ln1_g, ln1_b, w_up, ffn_conv_w, ffn_conv_b, w_down, ln2_g, ln2_b):
    c = D_RWKV
    wi = w_in[l]
    o_mla = RWKV_COLS
    w_kr = wi[:, o_mla + Q_LORA + KV_LORA:]
    pad_l = jnp.zeros((D_MODEL, QK_NOPE), F32)
    pad_r = jnp.zeros((D_MODEL, LANES - QK_DIM), F32)
    w_in_p = jnp.concatenate(
        [wi[:, :o_mla + Q_LORA + KV_LORA], pad_l, w_kr, pad_r, pad_l, _pair_swap(w_kr), pad_r], axis=1).astype(BF16)

    def lora_bd(w):
        z = jnp.zeros_like(w[0])
        return jnp.concatenate([jnp.concatenate([w[0], z], 1), jnp.concatenate([z, w[1]], 1)], 0).astype(BF16)

    head = jnp.arange(c) // RWKV_HEAD
    ones_bd = (head[:, None] == head[None, :]).astype(BF16)

    wq = w_uq[l].reshape(Q_LORA, N_HEADS, QK_DIM)
    q_nope, q_rope = wq[..., :QK_NOPE], wq[..., QK_NOPE:]
    zq = lambda n: jnp.zeros((Q_LORA, N_HEADS, n), F32)
    wq1 = jnp.concatenate([q_nope, q_rope, zq(LANES - QK_DIM)], -1).reshape(Q_LORA, D_HEADS).astype(BF16)
    wq2 = jnp.concatenate([zq(QK_NOPE), _pair_swap(q_rope), zq(LANES - QK_DIM)], -1).reshape(Q_LORA, D_HEADS).astype(BF16)
    wkv = w_ukv[l].reshape(KV_LORA, N_HEADS, QK_NOPE + V_HEAD)
    zk = jnp.zeros((KV_LORA, N_HEADS, HEAD_PAD - QK_NOPE), F32)
    wkn = jnp.concatenate([wkv[..., :QK_NOPE], zk], -1).reshape(KV_LORA, D_HEADS).astype(BF16)
    wv = jnp.concatenate([wkv[..., QK_NOPE:], zk], -1).reshape(KV_LORA, D_HEADS).astype(BF16)
    wo = w_o[l]
    wo_m = wo[c:].reshape(N_HEADS, V_HEAD, D_MODEL)
    wo_m = jnp.concatenate([wo_m, jnp.zeros((N_HEADS, HEAD_PAD - V_HEAD, D_MODEL), F32)], 1)

    return {
        "w_in": w_in_p,
        "conv": _pad_rows(list(rwkv_conv[l]), RWKV_COLS),
        "w0": _pad_rows([w0[l].reshape(-1)], 2 * c),
        "wb": lora_bd(w_b[l]),
        "a0": _pad_rows([a0[l].reshape(-1)], 2 * c),
        "ab": lora_bd(a_b[l]),
        "gb": g_b[l].astype(BF16),
        "vec": _pad_rows([k_k[l], k_a[l], r_k[l].reshape(-1)], c),
        "ones_bd": ones_bd,
        "norm_g": _pad_rows([q_norm_g[l], kv_norm_g[l]], Q_LORA),
        "wq1": wq1, "wq2": wq2, "wkn": wkn, "wv": wv,
        "gn": _pad_rows([gn_g[l], gn_b[l]], c),
        "wo_r": wo[:c].astype(BF16),
        "wo_m": wo_m.reshape(D_HEADS, D_MODEL).astype(BF16),
        "ln1": _pad_rows([ln1_g[l], ln1_b[l]], D_MODEL),
        "w_up": w_up[l].astype(BF16),
        "ffn_conv": _pad_rows(list(ffn_conv_w[l]) + [ffn_conv_b[l]], 2 * D_FF),
        "w_down": w_down[l].astype(BF16),
        "ln2": _pad_rows([ln2_g[l], ln2_b[l]], D_MODEL),
    }


def _rope_tables(n_ctx, n_lat):
    rows = n_lat // GRID_W
    row = jnp.repeat(jnp.arange(rows, dtype=F32), GRID_W)
    col = jnp.tile(jnp.arange(GRID_W, dtype=F32), rows)
    n_pairs = QK_ROPE // 4
    inv = ROPE_BASE ** (-jnp.arange(n_pairs, dtype=F32) / n_pairs)
    ang = jnp.concatenate([row[:, None] * inv, col[:, None] * inv], axis=-1)
    cos = jnp.repeat(jnp.cos(ang), 2, axis=-1)
    sin = jnp.repeat(jnp.sin(ang), 2, axis=-1)
    tt = n_ctx + n_lat
    cos_t = jnp.ones((tt, LANES), F32).at[n_ctx:, QK_NOPE:QK_DIM].set(cos)
    sin_t = jnp.zeros((tt, LANES), F32).at[n_ctx:, QK_NOPE:QK_DIM].set(sin)
    return cos_t, sin_t


def kernel(x, c, ctx, c_ctx, w_ada, b_ada, w_in, rwkv_conv, w0, w_b, a0, a_b, g_b, k_k, k_a, r_k, gn_g, gn_b,
           q_norm_g, w_uq, kv_norm_g, w_ukv, w_o, ln1_g, ln1_b, w_up, ffn_conv_w, ffn_conv_b, w_down, ln2_g, ln2_b):
    bsz, n_lat, _ = x.shape
    n_ctx = ctx.shape[1]
    assert n_ctx % TM == 0 and n_lat % TM == 0 and n_lat % GRID_W == 0
    n_ct = n_ctx // TM
    n_l = w_ada.shape[0]

    ada_rows = -(-(bsz + 1) // SUBLANES) * SUBLANES
    cc = jnp.zeros((ada_rows, D_MODEL), F32).at[:bsz].set(c).at[bsz].set(c_ctx)
    mods = _ada_call(cc, w_ada, b_ada).reshape(n_l, ada_rows, 6, D_MODEL)
    cos_t, sin_t = _rope_tables(n_ctx, n_lat)
    xs = jnp.concatenate([ctx, x], axis=1)

    for l in range(n_l):
        prm = _layer_params(l, w_in, rwkv_conv, w0, w_b, a0, a_b, g_b, k_k, k_a, r_k, gn_g, gn_b, q_norm_g, w_uq,
                            kv_norm_g, w_ukv, w_o, ln1_g, ln1_b, w_up, ffn_conv_w, ffn_conv_b, w_down, ln2_g, ln2_b)
        m_ctx = jnp.broadcast_to(mods[l, bsz][None], (bsz, 6, D_MODEL))
        mod = jnp.stack([m_ctx, mods[l, :bsz]], axis=1)
        mod = jnp.pad(mod, ((0, 0), (0, 0), (0, SUBLANES - 6), (0, 0))).reshape(2 * bsz, SUBLANES, D_MODEL)

        p_rwkv, cq, ckv, kr2 = _in_call(xs, mod, prm["w_in"], n_ct)
        r, v, kk, g, bv, lw, kd, bb = _prep_call(p_rwkv, prm, n_ct)
        y = _wkv_call(r, v, kk, lw, kd, bb, n_ct)
        q, k, vm = _mla_call(cq, ckv, kr2, cos_t, sin_t, prm)
        o = _attn_call(q, k, vm, n_ct)
        x1 = _out_call(y, bv, g, o, xs, mod, prm, n_ct)
        xs = _ffn_call(x1, mod, prm, n_ct)
    return xs[:, n_ctx:]
```

```python
import functools

import jax
import jax.numpy as jnp
from jax import lax
from jax.experimental import pallas as pl
from jax.experimental.pallas import tpu as pltpu

F32 = jnp.float32
BF16 = jnp.bfloat16

D_MODEL = 1024
DEPTH = 2
GRID_W = 64
D_RWKV = 512
RWKV_HEAD = 64
N_HEADS = 8
LORA_W = 64
LORA_A = 64
LORA_G = 128
RWKV_COLS = 3 * D_RWKV + 2 * LORA_W + 2 * LORA_A + LORA_G
QK_NOPE = 64
QK_ROPE = 32
QK_DIM = QK_NOPE + QK_ROPE
V_HEAD = 64
Q_LORA = 384
KV_LORA = 256
D_FF = 2816
ROPE_BASE = 10000.0
DEEPNORM_ALPHA = (2 * DEPTH) ** 0.25
LN_EPS = 1e-5
RMS_EPS = 1e-6
GN_EPS = 64e-5

LANES = 128
SUBLANES = 8
HEAD_PAD = LANES
D_HEADS = N_HEADS * HEAD_PAD
TM = 256
CHUNK = 64
LOG_CHUNK = 6
WKV_BATCH = 2
PAIR = 2 * RWKV_HEAD
N_PAIRS = N_HEADS // 2
FF_CHUNK = 1408
N_FF = D_FF // FF_CHUNK
IN_COLS_PAD = RWKV_COLS + Q_LORA + KV_LORA + 2 * LANES
VMEM_LIMIT = 56 << 20


def _cparams(sem):
    return pltpu.CompilerParams(dimension_semantics=sem, vmem_limit_bytes=VMEM_LIMIT)


def _mm(a, b):
    return jnp.dot(a.astype(BF16), b.astype(BF16), preferred_element_type=F32)


def _mm_nt(a, b):
    return lax.dot_general(a.astype(BF16), b.astype(BF16), (((1,), (1,)), ((), ())),
                           preferred_element_type=F32)


def _split2(x):
    hi = x.astype(BF16)
    lo = (x - hi.astype(F32)).astype(BF16)
    return hi, lo


def _mm3(a, b):
    ah, al = _split2(a)
    bh, bl = _split2(b)
    d = functools.partial(jnp.dot, preferred_element_type=F32)
    return d(ah, bh) + (d(ah, bl) + d(al, bh))


def _mm_exact_rhs(a, b_bf16):
    a1 = a.astype(BF16)
    r1 = a - a1.astype(F32)
    a2 = r1.astype(BF16)
    a3 = (r1 - a2.astype(F32)).astype(BF16)
    d = functools.partial(jnp.dot, preferred_element_type=F32)
    return d(a1, b_bf16) + (d(a2, b_bf16) + d(a3, b_bf16))


def _mm_exact_lhs(a_bf16, b):
    b1 = b.astype(BF16)
    r1 = b - b1.astype(F32)
    b2 = r1.astype(BF16)
    b3 = (r1 - b2.astype(F32)).astype(BF16)
    d = functools.partial(jnp.dot, preferred_element_type=F32)
    return d(a_bf16, b1) + (d(a_bf16, b2) + d(a_bf16, b3))


def _sigmoid(x):
    return 1.0 / (1.0 + jnp.exp(-x))


def _layer_norm(x, g, b):
    mu = jnp.mean(x, axis=-1, keepdims=True)
    xc = x - mu
    var = jnp.mean(xc * xc, axis=-1, keepdims=True)
    return xc * lax.rsqrt(var + LN_EPS) * g + b


def _rms_norm(x, g):
    return x * lax.rsqrt(jnp.mean(x * x, axis=-1, keepdims=True) + RMS_EPS) * g


ADA_COLS = 1536


def _ada_kernel(c_ref, w_ref, b_ref, o_ref):
    c = c_ref[...]
    o_ref[...] = _mm3(c * _sigmoid(c), w_ref[...]) + b_ref[...]


def _ada_call(cc, w_ada, b_ada):
    n_l = w_ada.shape[0]
    rows = cc.shape[0]
    return pl.pallas_call(
        _ada_kernel,
        grid=(n_l, 6 * D_MODEL // ADA_COLS),
        in_specs=[
            pl.BlockSpec((rows, D_MODEL), lambda l, j: (0, 0)),
            pl.BlockSpec((None, D_MODEL, ADA_COLS), lambda l, j: (l, 0, j)),
            pl.BlockSpec((None, 1, ADA_COLS), lambda l, j: (l, 0, j)),
        ],
        out_specs=pl.BlockSpec((None, rows, ADA_COLS), lambda l, j: (l, 0, j)),
        out_shape=jax.ShapeDtypeStruct((n_l, rows, 6 * D_MODEL), F32),
        compiler_params=_cparams(("arbitrary", "arbitrary")),
        name="ada",
    )(cc, w_ada, b_ada.reshape(n_l, 1, 6 * D_MODEL))


def _in_kernel(x_ref, mod_ref, w_ref, p_ref, cq_ref, ckv_ref, kr_ref):
    m = mod_ref[...]
    u = (x_ref[...] * (1.0 + m[1:2]) + m[0:1]).astype(BF16)
    d = functools.partial(jnp.dot, preferred_element_type=F32)
    o = 0
    for ref in (p_ref, cq_ref, ckv_ref, kr_ref):
        n = ref.shape[-1]
        ref[...] = d(u, w_ref[:, o:o + n])
        o += n


def _mod_spec(n_ct):
    return pl.BlockSpec((None, SUBLANES, D_MODEL), lambda b, i, *_: (2 * b + (i >= n_ct).astype(jnp.int32), 0, 0))


def _row_spec(cols, rows=TM):
    return pl.BlockSpec((None, rows, cols), lambda b, i, *_: (b, i, 0))


def _full_spec(shape):
    nd = len(shape)
    return pl.BlockSpec(shape, lambda *_: (0,) * nd)


def _in_call(xs, mod, w_in_p, n_ct):
    bsz, tt, _ = xs.shape
    nt = tt // TM
    widths = (RWKV_COLS, Q_LORA, KV_LORA, 2 * LANES)
    return pl.pallas_call(
        _in_kernel,
        grid=(bsz, nt),
        in_specs=[_row_spec(D_MODEL), _mod_spec(n_ct), _full_spec(w_in_p.shape)],
        out_specs=[_row_spec(w) for w in widths],
        out_shape=[jax.ShapeDtypeStruct((bsz, tt, w), F32) for w in widths],
        compiler_params=_cparams(("parallel", "parallel")),
        name="in_proj",
    )(xs, mod, w_in_p)


def _halo_specs(cols, tt):
    per = TM // SUBLANES
    last = tt // SUBLANES - 1
    prev = pl.BlockSpec((None, SUBLANES, cols), lambda b, i, *_: (b, jnp.maximum(i * per - 1, 0), 0))
    nxt = pl.BlockSpec((None, SUBLANES, cols), lambda b, i, *_: (b, jnp.minimum((i + 1) * per, last), 0))
    return prev, nxt


def _halo_valid(i, n_ct, nt):
    pv = jnp.where((i != 0) & (i != n_ct), 1.0, 0.0).astype(F32)
    nv = jnp.where((i != n_ct - 1) & (i != nt - 1), 1.0, 0.0).astype(F32)
    return pv, nv


def _seg_sum(x, ones_bd):
    return _mm_exact_rhs(x, ones_bd)


def _prep_kernel(p_ref, pp_ref, pn_ref, conv_ref, w0_ref, wb_ref, a0_ref, ab_ref, gb_ref, vec_ref, ones_ref,
                 r_ref, v_ref, kk_ref, g_ref, bv_ref, lw_ref, kd_ref, bb_ref, *, n_ct, nt):
    i = pl.program_id(1)
    pv, nv = _halo_valid(i, n_ct, nt)
    p = p_ref[...]
    cw = conv_ref[...]
    rows = lax.broadcasted_iota(jnp.int32, (TM, 1), 0)
    p_prev = jnp.where(rows == 0, pp_ref[SUBLANES - 1:SUBLANES, :] * pv, pltpu.roll(p, 1, 0))
    p_next = jnp.where(rows == TM - 1, pn_ref[0:1, :] * nv, pltpu.roll(p, TM - 1, 0))
    z = p_prev * cw[0:1] + p * cw[1:2] + p_next * cw[2:3]

    c = D_RWKV
    r, k, v = z[:, 0:c], z[:, c:2 * c], z[:, 2 * c:3 * c]
    o = 3 * c
    w_lo = z[:, o:o + 2 * LORA_W]
    a_lo = z[:, o + 2 * LORA_W:o + 2 * LORA_W + 2 * LORA_A]
    g_lo = z[:, o + 2 * LORA_W + 2 * LORA_A:]
    vec = vec_ref[...]
    k_k, k_a, r_k = vec[0:1], vec[1:2], vec[2:3]
    ones_bd = ones_ref[...]

    t = -(w0_ref[0:1, :] + _mm(jnp.tanh(w_lo), wb_ref[...]))
    sp = jnp.maximum(t, 0.0) + jnp.log(1.0 + jnp.exp(-jnp.abs(t)))
    lw = -jnp.exp(-sp - 0.5)
    a = _sigmoid(a0_ref[0:1, :] + _mm(a_lo, ab_ref[...]))
    g = _mm(_sigmoid(g_lo), gb_ref[...])

    kk = k * k_k
    nrm = jnp.sqrt(_seg_sum(kk * kk, ones_bd))
    kk = kk / jnp.maximum(nrm, 1e-12)
    kd0 = k * (1.0 + (a[:, :c] - 1.0) * k_a)
    kd1 = k * (1.0 + (a[:, c:] - 1.0) * k_a)
    bonus = _seg_sum(r * (kd0 + kd1) * r_k, ones_bd) * v

    r_ref[...] = r
    v_ref[...] = v
    kk_ref[...] = kk
    g_ref[...] = g
    bv_ref[...] = bonus
    lw_ref[0] = lw[:, :c]
    lw_ref[1] = lw[:, c:]
    kd_ref[0] = kd0
    kd_ref[1] = kd1
    bb_ref[0] = kk * a[:, :c]
    bb_ref[1] = kk * a[:, c:]


def _prep_call(p_rwkv, prm, n_ct):
    bsz, tt, _ = p_rwkv.shape
    nt = tt // TM
    prev, nxt = _halo_specs(RWKV_COLS, tt)
    shared = jax.ShapeDtypeStruct((bsz, tt, D_RWKV), F32)
    per_dir = jax.ShapeDtypeStruct((2, bsz, tt, D_RWKV), F32)
    dir_spec = pl.BlockSpec((2, None, TM, D_RWKV), lambda b, i: (0, b, i, 0))
    params = [prm[n] for n in ("conv", "w0", "wb", "a0", "ab", "gb", "vec", "ones_bd")]
    return pl.pallas_call(
        functools.partial(_prep_kernel, n_ct=n_ct, nt=nt),
        grid=(bsz, nt),
        in_specs=[_row_spec(RWKV_COLS), prev, nxt] + [_full_spec(a.shape) for a in params],
        out_specs=[_row_spec(D_RWKV)] * 5 + [dir_spec] * 3,
        out_shape=[shared] * 5 + [per_dir] * 3,
        compiler_params=_cparams(("parallel", "parallel")),
        name="rwkv_prep",
    )(p_rwkv, p_rwkv, p_rwkv, *params)


def _wkv_kernel(*refs, nb):
    ins, (yf_ref, yr_ref, s_ref) = refs[:12], refs[12:]
    step = pl.program_id(1)

    @pl.when(step == 0)
    def _():
        s_ref[...] = jnp.zeros_like(s_ref)

    n2 = 2 * CHUNK
    ri = lax.broadcasted_iota(jnp.int32, (n2, n2), 0)
    ci = lax.broadcasted_iota(jnp.int32, (n2, n2), 1)
    same_head = (ri >> LOG_CHUNK) == (ci >> LOG_CHUNK)
    rt, ct = ri & (CHUNK - 1), ci & (CHUNK - 1)
    eye = ri == ci
    ti = lax.broadcasted_iota(jnp.int32, (CHUNK, CHUNK), 0)
    tj = lax.broadcasted_iota(jnp.int32, (CHUNK, CHUNK), 1)
    strict = [same_head & (ct < rt), same_head & (ct > rt)]
    incl = [same_head & (ct <= rt), same_head & (ct >= rt)]
    tri = [jnp.where(tj <= ti, 1.0, 0.0).astype(BF16), jnp.where(tj >= ti, 1.0, 0.0).astype(BF16)]

    def bdiag(x):
        return jnp.where(same_head, jnp.concatenate([x, x], axis=0), 0.0)

    ch = []
    for d in range(2):
        r_ref, v_ref, kk_ref, lw_ref, kd_ref, bb_ref = ins[6 * d:6 * d + 6]
        for n in range(nb):
            lw = lw_ref[n]
            lc = _mm_exact_lhs(tri[d], lw)
            lend = jnp.sum(lw, axis=0, keepdims=True)
            e_neg = jnp.exp(-lc)
            e_end = jnp.exp(lend - lc)
            p_end = jnp.exp(lend)
            kt_all = kk_ref[n] * jnp.exp(lc - lw)
            rt_all = r_ref[n] * jnp.exp(lc)
            kh_all, bh_all = kd_ref[n] * e_neg, bb_ref[n] * e_neg
            kc_all, bc_all = kd_ref[n] * e_end, bb_ref[n] * e_end
            v_all = v_ref[n]
            for p in range(N_PAIRS):
                sl = slice(p * PAIR, (p + 1) * PAIR)
                ch.append(dict(d=d, slot=(d * nb + n) * N_PAIRS + p, kt=bdiag(kt_all[:, sl]), rr=bdiag(rt_all[:, sl]),
                               vv=bdiag(v_all[:, sl]), kh=kh_all[:, sl], bh=bh_all[:, sl], kc=kc_all[:, sl],
                               bc=bc_all[:, sl], p_end=p_end[:, sl]))
    for c in ch:
        gram = _mm_nt(jnp.concatenate([c["kt"], c["rr"]], axis=0),
                      jnp.concatenate([c["bh"]] * 2 + [c["kh"]] * 2, axis=0))
        st, ic = strict[c["d"]], incl[c["d"]]
        c["a_kb"] = jnp.where(st, gram[:n2, :n2], 0.0)
        c["a_kk"] = jnp.where(st, gram[:n2, n2:], 0.0)
        c["a_rb"] = jnp.where(ic, gram[n2:, :n2], 0.0)
        c["a_rk"] = jnp.where(ic, gram[n2:, n2:], 0.0)
        c["t"] = jnp.where(eye, 1.0, 0.0) - jnp.where((rt >> 1) == (ct >> 1), c["a_kb"], 0.0)
    for c in ch:
        c["akk_v"] = _mm(c["a_kk"], c["vv"])
        c["ark_v"] = _mm(c["a_rk"], c["vv"])
        c["kc_v"] = _mm(bdiag(c["kc"]).T, c["vv"])
    for ls in range(1, LOG_CHUNK):
        off = ((rt >> (ls + 1)) == (ct >> (ls + 1))) & ((rt >> ls) != (ct >> ls))
        for c in ch:
            c["nt"] = _mm(jnp.where(off, c["a_kb"], 0.0), c["t"])
        for c in ch:
            c["t"] = c["t"] - _mm(c["t"], c["nt"])
    for c in ch:
        c["uw"] = _mm(c["t"], jnp.concatenate([c["akk_v"], c["kt"]], axis=1))
    for c in ch:
        c["yq"] = jnp.concatenate([c["ark_v"], c["rr"]], axis=1) - _mm(c["a_rb"], c["uw"])
        diag_p = jnp.where(eye, jnp.broadcast_to(c["p_end"], (n2, n2)), 0.0)
        c["nm"] = jnp.concatenate([c["kc_v"], diag_p], axis=1) - _mm(bdiag(c["bc"]).T, c["uw"])
    for c in ch:
        c["qm"] = _mm3(jnp.concatenate([c["yq"][:, n2:], c["nm"][:, n2:]], axis=0), s_ref[c["slot"]])
    ys = []
    for c in ch:
        y_bd = c["qm"][:n2] + c["yq"][:, :n2]
        s_ref[c["slot"]] = c["qm"][n2:] + c["nm"][:, :n2]
        ys.append(y_bd[:CHUNK] + y_bd[CHUNK:])
    for d, y_ref in enumerate((yf_ref, yr_ref)):
        for n in range(nb):
            o = (d * nb + n) * N_PAIRS
            y_ref[n] = jnp.concatenate(ys[o:o + N_PAIRS], axis=1)


def _wkv_call(r, v, kk, lw, kd, bb, n_ct):
    bsz, tt, c = r.shape
    nc = tt // CHUNK
    n_cc = n_ct * TM // CHUNK
    nb = WKV_BATCH
    assert bsz % nb == 0

    def rev_chunk(s):
        return jnp.where(s < n_cc, n_cc - 1 - s, nc - 1 - (s - n_cc))

    order = (lambda s: s, rev_chunk)
    specs, args = [], []
    for d in range(2):
        shared = pl.BlockSpec((nb, CHUNK, c), lambda b, s, d=d: (b, order[d](s), 0))
        per_dir = pl.BlockSpec((None, nb, CHUNK, c), lambda b, s, d=d: (d, b, order[d](s), 0))
        specs += [shared, shared, shared, per_dir, per_dir, per_dir]
        args += [r, v, kk, lw, kd, bb]
    out = jax.ShapeDtypeStruct((bsz, tt, c), F32)
    return pl.pallas_call(
        functools.partial(_wkv_kernel, nb=nb),
        grid=(bsz // nb, nc),
        in_specs=specs,
        out_specs=[specs[0], specs[6]],
        out_shape=[out, out],
        scratch_shapes=[pltpu.VMEM((2 * nb * N_PAIRS, PAIR, PAIR), F32)],
        compiler_params=_cparams(("parallel", "arbitrary")),
        name="wkv_scan",
    )(*args)


def _mla_kernel(cq_ref, ckv_ref, kr_ref, cos_ref, sin_ref, ng_ref, wq1_ref, wq2_ref, wkn_ref, wv_ref,
                q_ref, k_ref, v_ref):
    cos, sin = cos_ref[...], sin_ref[...]
    cos8 = jnp.concatenate([cos] * N_HEADS, axis=1)
    sin8 = jnp.concatenate([sin] * N_HEADS, axis=1)
    cqn = _rms_norm(cq_ref[...], ng_ref[0:1, :Q_LORA]).astype(BF16)
    d = functools.partial(jnp.dot, preferred_element_type=F32)
    q = d(cqn, wq1_ref[...]) * cos8 + d(cqn, wq2_ref[...]) * sin8
    q_ref[...] = (q * (QK_DIM ** -0.5)).astype(BF16)
    ckvn = _rms_norm(ckv_ref[...], ng_ref[1:2, :KV_LORA]).astype(BF16)
    kr = kr_ref[:, :LANES] * cos + kr_ref[:, LANES:] * sin
    k_ref[...] = (d(ckvn, wkn_ref[...]) + jnp.concatenate([kr] * N_HEADS, axis=1)).astype(BF16)
    v_ref[...] = d(ckvn, wv_ref[...]).astype(BF16)


def _mla_call(cq, ckv, kr2, cos_t, sin_t, prm):
    bsz, tt, _ = cq.shape
    nt = tt // TM
    tab = pl.BlockSpec((TM, LANES), lambda b, i: (i, 0))
    params = [prm[n] for n in ("norm_g", "wq1", "wq2", "wkn", "wv")]
    out = jax.ShapeDtypeStruct((bsz, tt, D_HEADS), BF16)
    return pl.pallas_call(
        _mla_kernel,
        grid=(bsz, nt),
        in_specs=[_row_spec(Q_LORA), _row_spec(KV_LORA), _row_spec(2 * LANES), tab, tab]
        + [_full_spec(a.shape) for a in params],
        out_specs=[_row_spec(D_HEADS)] * 3,
        out_shape=[out] * 3,
        compiler_params=_cparams(("parallel", "parallel")),
        name="mla_prep",
    )(cq, ckv, kr2, cos_t, sin_t, *params)


def _attn_kernel(q_ref, k_ref, v_ref, o_ref, *, n_ctx_rows, n_ct):
    i = pl.program_id(2)

    def attend(n_keys):
        s = _mm_nt(q_ref[...], k_ref[0:n_keys, :])
        p = jnp.exp(s - jnp.max(s, axis=-1, keepdims=True))
        l = jnp.sum(p, axis=-1, keepdims=True)
        o_ref[...] = (_mm(p, v_ref[0:n_keys, :]) / l).astype(o_ref.dtype)

    @pl.when(i < n_ct)
    def _():
        attend(n_ctx_rows)

    @pl.when(i >= n_ct)
    def _():
        attend(k_ref.shape[0])


def _attn_call(q, k, v, n_ct):
    bsz, tt, _ = q.shape
    nt = tt // TM
    q_spec = pl.BlockSpec((None, TM, HEAD_PAD), lambda b, h, i: (b, i, h))
    kv_spec = pl.BlockSpec((None, tt, HEAD_PAD), lambda b, h, i: (b, 0, h))
    return pl.pallas_call(
        functools.partial(_attn_kernel, n_ctx_rows=n_ct * TM, n_ct=n_ct),
        grid=(bsz, N_HEADS, nt),
        in_specs=[q_spec, kv_spec, kv_spec],
        out_specs=q_spec,
        out_shape=jax.ShapeDtypeStruct((bsz, tt, D_HEADS), BF16),
        compiler_params=_cparams(("parallel", "parallel", "arbitrary")),
        name="attention",
    )(q, k, v)


def _out_kernel(yf_ref, yr_ref, bv_ref, g_ref, o_ref, x_ref, mod_ref, gn_ref, ones_ref, wor_ref, wom_ref, ln_ref,
                out_ref):
    ones_bd = ones_ref[...]
    y = yf_ref[...] + yr_ref[...]
    inv_n = 1.0 / RWKV_HEAD
    yc = y - _seg_sum(y, ones_bd) * inv_n
    var = _seg_sum(yc * yc, ones_bd) * inv_n
    gn = yc * lax.rsqrt(var + GN_EPS) * gn_ref[0:1] + gn_ref[1:2]
    ro = (gn + bv_ref[...]) * g_ref[...]
    m = _mm(ro, wor_ref[...]) + jnp.dot(o_ref[...], wom_ref[...], preferred_element_type=F32)
    mod = mod_ref[...]
    out_ref[...] = _layer_norm(DEEPNORM_ALPHA * x_ref[...] + mod[2:3] * m, ln_ref[0:1], ln_ref[1:2])


def _out_call(yf, yr, bv, g, o, xs, mod, prm, n_ct):
    bsz, tt, _ = xs.shape
    nt = tt // TM
    params = [prm[n] for n in ("gn", "ones_bd", "wo_r", "wo_m", "ln1")]
    return pl.pallas_call(
        _out_kernel,
        grid=(bsz, nt),
        in_specs=[_row_spec(D_RWKV)] * 4 + [_row_spec(D_HEADS), _row_spec(D_MODEL), _mod_spec(n_ct)]
        + [_full_spec(a.shape) for a in params],
        out_specs=_row_spec(D_MODEL),
        out_shape=jax.ShapeDtypeStruct((bsz, tt, D_MODEL), F32),
        compiler_params=_cparams(("parallel", "parallel")),
        name="mixer_out",
    )(yf, yr, bv, g, o, xs, mod, *params)


def _ffn_kernel(x_ref, xp_ref, xn_ref, mod_ref, wg_ref, wv_ref, cg_ref, cv_ref, wd_ref, ln_ref, out_ref,
                u_ref, acc_ref, *, n_ct, nt):
    i = pl.program_id(1)
    j = pl.program_id(2)
    mod = mod_ref[...]

    @pl.when(j == 0)
    def _():
        pv, nv = _halo_valid(i, n_ct, nt)
        sc, sh = 1.0 + mod[4:5], mod[3:4]
        u = jnp.concatenate([(xp_ref[...] * sc + sh) * pv, x_ref[...] * sc + sh, (xn_ref[...] * sc + sh) * nv], axis=0)
        u_ref[...] = u.astype(BF16)
        acc_ref[...] = jnp.zeros_like(acc_ref)

    u = u_ref[...]
    rows = TM + 2 * SUBLANES

    def conv(w_ref, c_ref):
        h = jnp.dot(u, w_ref[...], preferred_element_type=F32)
        c = c_ref[...]
        lo, hi = SUBLANES, SUBLANES + TM
        return (pltpu.roll(h, 1, 0)[lo:hi] * c[0:1] + h[lo:hi] * c[1:2]
                + pltpu.roll(h, rows - 1, 0)[lo:hi] * c[2:3] + c[3:4])

    hg = conv(wg_ref, cg_ref)
    hv = conv(wv_ref, cv_ref)
    act = hg * _sigmoid(hg) * hv
    acc_ref[...] += _mm(act, wd_ref[...])

    @pl.when(j == N_FF - 1)
    def _():
        out_ref[...] = _layer_norm(DEEPNORM_ALPHA * x_ref[...] + mod[5:6] * acc_ref[...], ln_ref[0:1], ln_ref[1:2])


def _ffn_call(xs, mod, prm, n_ct):
    bsz, tt, _ = xs.shape
    nt = tt // TM
    prev, nxt = _halo_specs(D_MODEL, tt)
    w_up, conv, w_down, ln2 = prm["w_up"], prm["ffn_conv"], prm["w_down"], prm["ln2"]
    return pl.pallas_call(
        functools.partial(_ffn_kernel, n_ct=n_ct, nt=nt),
        grid=(bsz, nt, N_FF),
        in_specs=[
            _row_spec(D_MODEL), prev, nxt, _mod_spec(n_ct),
            pl.BlockSpec((D_MODEL, FF_CHUNK), lambda b, i, j: (0, j)),
            pl.BlockSpec((D_MODEL, FF_CHUNK), lambda b, i, j: (0, N_FF + j)),
            pl.BlockSpec((SUBLANES, FF_CHUNK), lambda b, i, j: (0, j)),
            pl.BlockSpec((SUBLANES, FF_CHUNK), lambda b, i, j: (0, N_FF + j)),
            pl.BlockSpec((FF_CHUNK, D_MODEL), lambda b, i, j: (j, 0)),
            _full_spec(ln2.shape),
        ],
        out_specs=_row_spec(D_MODEL),
        out_shape=jax.ShapeDtypeStruct((bsz, tt, D_MODEL), F32),
        scratch_shapes=[pltpu.VMEM((TM + 2 * SUBLANES, D_MODEL), BF16), pltpu.VMEM((TM, D_MODEL), F32)],
        compiler_params=_cparams(("parallel", "parallel", "arbitrary")),
        name="conv_ffn",
    )(xs, xs, xs, mod, w_up, w_up, conv, conv, w_down, ln2)


def _pad_rows(rows, width):
    out = jnp.zeros((SUBLANES, width), F32)
    for n, r in enumerate(rows):
        out = out.at[n, :r.shape[0]].set(r.astype(F32))
    return out


def _pair_swap(w):
    w2 = w.reshape(*w.shape[:-1], w.shape[-1] // 2, 2)
    return jnp.stack([-w2[..., 1], w2[..., 0]], axis=-1).reshape(w.shape)


def _layer_params(l, w_in, rwkv_conv, w0, w_b, a0, a_b, g_b, k_k, k_a, r_k, gn_g, gn_b, q_norm_g, w_uq,
                  kv_norm_g, w_ukv, w_o, ln1_g, ln1_b, w_up, ffn_conv_w, ffn_conv_b, w_down, ln2_g, ln2_b):
    c = D_RWKV
    wi = w_in[l]
    o_mla = RWKV_COLS
    w_kr = wi[:, o_mla + Q_LORA + KV_LORA:]
    pad_l = jnp.zeros((D_MODEL, QK_NOPE), F32)
    pad_r = jnp.zeros((D_MODEL, LANES - QK_DIM), F32)
    w_in_p = jnp.concatenate(
        [wi[:, :o_mla + Q_LORA + KV_LORA], pad_l, w_kr, pad_r, pad_l, _pair_swap(w_kr), pad_r], axis=1).astype(BF16)

    def lora_bd(w):
        z = jnp.zeros_like(w[0])
        return jnp.concatenate([jnp.concatenate([w[0], z], 1), jnp.concatenate([z, w[1]], 1)], 0).astype(BF16)

    head = jnp.arange(c) // RWKV_HEAD
    ones_bd = (head[:, None] == head[None, :]).astype(BF16)

    wq = w_uq[l].reshape(Q_LORA, N_HEADS, QK_DIM)
    q_nope, q_rope = wq[..., :QK_NOPE], wq[..., QK_NOPE:]
    zq = lambda n: jnp.zeros((Q_LORA, N_HEADS, n), F32)
    wq1 = jnp.concatenate([q_nope, q_rope, zq(LANES - QK_DIM)], -1).reshape(Q_LORA, D_HEADS).astype(BF16)
    wq2 = jnp.concatenate([zq(QK_NOPE), _pair_swap(q_rope), zq(LANES - QK_DIM)], -1).reshape(Q_LORA, D_HEADS).astype(BF16)
    wkv = w_ukv[l].reshape(KV_LORA, N_HEADS, QK_NOPE + V_HEAD)
    zk = jnp.zeros((KV_LORA, N_HEADS, HEAD_PAD - QK_NOPE), F32)
    wkn = jnp.concatenate([wkv[..., :QK_NOPE], zk], -1).reshape(KV_LORA, D_HEADS).astype(BF16)
    wv = jnp.concatenate([wkv[..., QK_NOPE:], zk], -1).reshape(KV_LORA, D_HEADS).astype(BF16)
    wo = w_o[l]
    wo_m = wo[c:].reshape(N_HEADS, V_HEAD, D_MODEL)
    wo_m = jnp.concatenate([wo_m, jnp.zeros((N_HEADS, HEAD_PAD - V_HEAD, D_MODEL), F32)], 1)

    return {
        "w_in": w_in_p,
        "conv": _pad_rows(list(rwkv_conv[l]), RWKV_COLS),
        "w0": _pad_rows([w0[l].reshape(-1)], 2 * c),
        "wb": lora_bd(w_b[l]),
        "a0": _pad_rows([a0[l].reshape(-1)], 2 * c),
        "ab": lora_bd(a_b[l]),
        "gb": g_b[l].astype(BF16),
        "vec": _pad_rows([k_k[l], k_a[l], r_k[l].reshape(-1)], c),
        "ones_bd": ones_bd,
        "norm_g": _pad_rows([q_norm_g[l], kv_norm_g[l]], Q_LORA),
        "wq1": wq1, "wq2": wq2, "wkn": wkn, "wv": wv,
        "gn": _pad_rows([gn_g[l], gn_b[l]], c),
        "wo_r": wo[:c].astype(BF16),
        "wo_m": wo_m.reshape(D_HEADS, D_MODEL).astype(BF16),
        "ln1": _pad_rows([ln1_g[l], ln1_b[l]], D_MODEL),
        "w_up": w_up[l].astype(BF16),
        "ffn_conv": _pad_rows(list(ffn_conv_w[l]) + [ffn_conv_b[l]], 2 * D_FF),
        "w_down": w_down[l].astype(BF16),
        "ln2": _pad_rows([ln2_g[l], ln2_b[l]], D_MODEL),
    }


def _rope_tables(n_ctx, n_lat):
    rows = n_lat // GRID_W
    row = jnp.repeat(jnp.arange(rows, dtype=F32), GRID_W)
    col = jnp.tile(jnp.arange(GRID_W, dtype=F32), rows)
    n_pairs = QK_ROPE // 4
    inv = ROPE_BASE ** (-jnp.arange(n_pairs, dtype=F32) / n_pairs)
    ang = jnp.concatenate([row[:, None] * inv, col[:, None] * inv], axis=-1)
    cos = jnp.repeat(jnp.cos(ang), 2, axis=-1)
    sin = jnp.repeat(jnp.sin(ang), 2, axis=-1)
    tt = n_ctx + n_lat
    cos_t = jnp.ones((tt, LANES), F32).at[n_ctx:, QK_NOPE:QK_DIM].set(cos)
    sin_t = jnp.zeros((tt, LANES), F32).at[n_ctx:, QK_NOPE:QK_DIM].set(sin)
    return cos_t, sin_t


def kernel(x, c, ctx, c_ctx, w_ada, b_ada, w_in, rwkv_conv, w0, w_b, a0, a_b, g_b, k_k, k_a, r_k, gn_g, gn_b,
           q_norm_g, w_uq, kv_norm_g, w_ukv, w_o, ln1_g, ln1_b, w_up, ffn_conv_w, ffn_conv_b, w_down, ln2_g, ln2_b):
    bsz, n_lat, _ = x.shape
    n_ctx = ctx.shape[1]
    assert n_ctx % TM == 0 and n_lat % TM == 0 and n_lat % GRID_W == 0
    n_ct = n_ctx // TM
    n_l = w_ada.shape[0]

    ada_rows = -(-(bsz + 1) // SUBLANES) * SUBLANES
    cc = jnp.zeros((ada_rows, D_MODEL), F32).at[:bsz].set(c).at[bsz].set(c_ctx)
    mods = _ada_call(cc, w_ada, b_ada).reshape(n_l, ada_rows, 6, D_MODEL)
    cos_t, sin_t = _rope_tables(n_ctx, n_lat)
    xs = jnp.concatenate([ctx, x], axis=1)

    for l in range(n_l):
        prm = _layer_params(l, w_in, rwkv_conv, w0, w_b, a0, a_b, g_b, k_k, k_a, r_k, gn_g, gn_b, q_norm_g, w_uq,
                            kv_norm_g, w_ukv, w_o, ln1_g, ln1_b, w_up, ffn_conv_w, ffn_conv_b, w_down, ln2_g, ln2_b)
        m_ctx = jnp.broadcast_to(mods[l, bsz][None], (bsz, 6, D_MODEL))
        mod = jnp.stack([m_ctx, mods[l, :bsz]], axis=1)
        mod = jnp.pad(mod, ((0, 0), (0, 0), (0, SUBLANES - 6), (0, 0))).reshape(2 * bsz, SUBLANES, D_MODEL)

        p_rwkv, cq, ckv, kr2 = _in_call(xs, mod, prm["w_in"], n_ct)
        r, v, kk, g, bv, lw, kd, bb = _prep_call(p_rwkv, prm, n_ct)
        yf, yr = _wkv_call(r, v, kk, lw, kd, bb, n_ct)
        q, k, vm = _mla_call(cq, ckv, kr2, cos_t, sin_t, prm)
        o = _attn_call(q, k, vm, n_ct)
        x1 = _out_call(yf, yr, bv, g, o, xs, mod, prm, n_ct)
        xs = _ffn_call(x1, mod, prm, n_ct)
    return xs[:, n_ctx:]
```

```python
import functools

import jax
import jax.numpy as jnp
from jax import lax
from jax.experimental import pallas as pl
from jax.experimental.pallas import tpu as pltpu

F32 = jnp.float32
BF16 = jnp.bfloat16

D_MODEL = 1024
DEPTH = 2
GRID_W = 64
D_RWKV = 512
RWKV_HEAD = 64
N_HEADS = 8
LORA_W = 64
LORA_A = 64
LORA_G = 128
RWKV_COLS = 3 * D_RWKV + 2 * LORA_W + 2 * LORA_A + LORA_G
QK_NOPE = 64
QK_ROPE = 32
QK_DIM = QK_NOPE + QK_ROPE
V_HEAD = 64
Q_LORA = 384
KV_LORA = 256
D_FF = 2816
ROPE_BASE = 10000.0
DEEPNORM_ALPHA = (2 * DEPTH) ** 0.25
LN_EPS = 1e-5
RMS_EPS = 1e-6
GN_EPS = 64e-5

LANES = 128
SUBLANES = 8
HEAD_PAD = LANES
D_HEADS = N_HEADS * HEAD_PAD
TM = 256
HALO = SUBLANES
CHUNK = 64
LOG_CHUNK = 6
WKV_BATCH = 2
PAIR = 2 * RWKV_HEAD
N_PAIRS = N_HEADS // 2
ATTN_KEYS = 512
FF_SUB = 256
N_FF_SUB = D_FF // FF_SUB
IN_COLS_PAD = RWKV_COLS + Q_LORA + KV_LORA + 2 * LANES
VMEM_LIMIT = 56 << 20


def _cparams(sem):
    return pltpu.CompilerParams(dimension_semantics=sem, vmem_limit_bytes=VMEM_LIMIT)


def _mm(a, b):
    return jnp.dot(a.astype(BF16), b.astype(BF16), preferred_element_type=F32)


def _mm_nt(a, b):
    return lax.dot_general(a.astype(BF16), b.astype(BF16), (((1,), (1,)), ((), ())),
                           preferred_element_type=F32)


def _split2(x):
    hi = x.astype(BF16)
    lo = (x - hi.astype(F32)).astype(BF16)
    return hi, lo


def _mm3(a, b):
    ah, al = _split2(a)
    bh, bl = _split2(b)
    d = functools.partial(jnp.dot, preferred_element_type=F32)
    return d(ah, bh) + (d(ah, bl) + d(al, bh))


def _mm_exact_rhs(a, b_bf16):
    a1 = a.astype(BF16)
    r1 = a - a1.astype(F32)
    a2 = r1.astype(BF16)
    a3 = (r1 - a2.astype(F32)).astype(BF16)
    d = functools.partial(jnp.dot, preferred_element_type=F32)
    return d(a1, b_bf16) + (d(a2, b_bf16) + d(a3, b_bf16))


def _mm_exact_lhs(a_bf16, b):
    b1 = b.astype(BF16)
    r1 = b - b1.astype(F32)
    b2 = r1.astype(BF16)
    b3 = (r1 - b2.astype(F32)).astype(BF16)
    d = functools.partial(jnp.dot, preferred_element_type=F32)
    return d(a_bf16, b1) + (d(a_bf16, b2) + d(a_bf16, b3))


def _sigmoid(x):
    return 1.0 / (1.0 + jnp.exp(-x))


def _layer_norm(x, g, b):
    mu = jnp.mean(x, axis=-1, keepdims=True)
    xc = x - mu
    var = jnp.mean(xc * xc, axis=-1, keepdims=True)
    return xc * lax.rsqrt(var + LN_EPS) * g + b


def _rms_norm(x, g):
    return x * lax.rsqrt(jnp.mean(x * x, axis=-1, keepdims=True) + RMS_EPS) * g


def _row_spec(cols, t0=0):
    return pl.BlockSpec((None, TM, cols), lambda b, i, *_: (b, i + t0, 0))


def _mod_spec(n_ct, t0=0):
    return pl.BlockSpec((None, SUBLANES, D_MODEL),
                        lambda b, i, *_: (2 * b + (i + t0 >= n_ct).astype(jnp.int32), 0, 0))


def _full_spec(shape, single_buffer=False):
    nd = len(shape)
    mode = dict(pipeline_mode=pl.Buffered(1)) if single_buffer else {}
    return pl.BlockSpec(shape, lambda *_: (0,) * nd, **mode)


def _halo_specs(cols, tt, t0=0):
    per = TM // HALO
    last = tt // HALO - 1
    prev = pl.BlockSpec((None, HALO, cols), lambda b, i, *_: (b, jnp.maximum((i + t0) * per - 1, 0), 0))
    nxt = pl.BlockSpec((None, HALO, cols), lambda b, i, *_: (b, jnp.minimum((i + t0 + 1) * per, last), 0))
    return prev, nxt


def _with_halo(x_ref, xp_ref, xn_ref, scale, shift, i, n_ct, nt):
    has_prev = (i != 0) & (i != n_ct)
    has_next = (i != n_ct - 1) & (i != nt - 1)
    mod = lambda x: x * scale + shift
    return jnp.concatenate([jnp.where(has_prev, mod(xp_ref[...]), 0.0), mod(x_ref[...]),
                            jnp.where(has_next, mod(xn_ref[...]), 0.0)], axis=0).astype(BF16)


def _conv3(h, taps):
    rows = TM + 2 * HALO
    lo, hi = HALO, HALO + TM
    return (pltpu.roll(h, 1, 0)[lo:hi] * taps[0:1] + h[lo:hi] * taps[1:2]
            + pltpu.roll(h, rows - 1, 0)[lo:hi] * taps[2:3])


ADA_COLS = 1536


def _ada_kernel(c_ref, w_ref, b_ref, o_ref):
    c = c_ref[...]
    o_ref[...] = _mm3(c * _sigmoid(c), w_ref[...]) + b_ref[...]


def _ada_call(cc, w_ada, b_ada):
    n_l = w_ada.shape[0]
    rows = cc.shape[0]
    return pl.pallas_call(
        _ada_kernel,
        grid=(n_l, 6 * D_MODEL // ADA_COLS),
        in_specs=[
            pl.BlockSpec((rows, D_MODEL), lambda l, j: (0, 0)),
            pl.BlockSpec((None, D_MODEL, ADA_COLS), lambda l, j: (l, 0, j)),
            pl.BlockSpec((None, 1, ADA_COLS), lambda l, j: (l, 0, j)),
        ],
        out_specs=pl.BlockSpec((None, rows, ADA_COLS), lambda l, j: (l, 0, j)),
        out_shape=jax.ShapeDtypeStruct((n_l, rows, 6 * D_MODEL), F32),
        compiler_params=_cparams(("arbitrary", "arbitrary")),
        name="ada",
    )(cc, w_ada, b_ada.reshape(n_l, 1, 6 * D_MODEL))


def _seg_sum(x, ones_bd):
    return _mm_exact_rhs(x, ones_bd)


def _front_kernel(x_ref, xp_ref, xn_ref, mod_ref, w_ref, conv_ref, w0_ref, wb_ref, a0_ref, ab_ref, gb_ref, vec_ref,
                  ones_ref, cos_ref, sin_ref, ng_ref, wq1_ref, wq2_ref, wkn_ref, wv_ref,
                  r_ref, v_ref, kk_ref, g_ref, bv_ref, lw_ref, kd_ref, bb_ref, q_ref, k_ref, vm_ref, *, n_ct, nt):
    i = pl.program_id(1)
    m = mod_ref[...]
    u = _with_halo(x_ref, xp_ref, xn_ref, 1.0 + m[1:2], m[0:1], i, n_ct, nt)
    d = functools.partial(jnp.dot, preferred_element_type=F32)
    z = _conv3(d(u, w_ref[:, :RWKV_COLS]), conv_ref[...])
    pm = d(u, w_ref[:, RWKV_COLS:])[HALO:HALO + TM]

    c = D_RWKV
    r, k, v = z[:, 0:c], z[:, c:2 * c], z[:, 2 * c:3 * c]
    o = 3 * c
    w_lo = z[:, o:o + 2 * LORA_W]
    a_lo = z[:, o + 2 * LORA_W:o + 2 * LORA_W + 2 * LORA_A]
    g_lo = z[:, o + 2 * LORA_W + 2 * LORA_A:]
    vec = vec_ref[...]
    k_k, k_a, r_k = vec[0:1], vec[1:2], vec[2:3]
    ones_bd = ones_ref[...]

    t = -(w0_ref[0:1, :] + _mm(jnp.tanh(w_lo), wb_ref[...]))
    sp = jnp.maximum(t, 0.0) + jnp.log(1.0 + jnp.exp(-jnp.abs(t)))
    lw = -jnp.exp(-sp - 0.5)
    a = _sigmoid(a0_ref[0:1, :] + _mm(a_lo, ab_ref[...]))
    g = _mm(_sigmoid(g_lo), gb_ref[...])

    kk = k * k_k
    nrm = jnp.sqrt(_seg_sum(kk * kk, ones_bd))
    kk = kk / jnp.maximum(nrm, 1e-12)
    kd0 = k * (1.0 + (a[:, :c] - 1.0) * k_a)
    kd1 = k * (1.0 + (a[:, c:] - 1.0) * k_a)
    bonus = _seg_sum(r * (kd0 + kd1) * r_k, ones_bd) * v

    r_ref[...] = r.astype(BF16)
    v_ref[...] = v.astype(BF16)
    kk_ref[...] = kk.astype(BF16)
    g_ref[...] = g.astype(BF16)
    bv_ref[...] = bonus.astype(BF16)
    lw_ref[0] = lw[:, :c]
    lw_ref[1] = lw[:, c:]
    kd_ref[0] = kd0.astype(BF16)
    kd_ref[1] = kd1.astype(BF16)
    bb_ref[0] = (kk * a[:, :c]).astype(BF16)
    bb_ref[1] = (kk * a[:, c:]).astype(BF16)

    cq, ckv = pm[:, :Q_LORA], pm[:, Q_LORA:Q_LORA + KV_LORA]
    kr_x, kr_swap = pm[:, Q_LORA + KV_LORA:Q_LORA + KV_LORA + LANES], pm[:, Q_LORA + KV_LORA + LANES:]
    cos, sin = cos_ref[...], sin_ref[...]
    cos8 = jnp.concatenate([cos] * N_HEADS, axis=1)
    sin8 = jnp.concatenate([sin] * N_HEADS, axis=1)
    cqn = _rms_norm(cq, ng_ref[0:1, :Q_LORA]).astype(BF16)
    q = d(cqn, wq1_ref[...]) * cos8 + d(cqn, wq2_ref[...]) * sin8
    q_ref[...] = (q * (QK_DIM ** -0.5)).astype(BF16)
    ckvn = _rms_norm(ckv, ng_ref[1:2, :KV_LORA]).astype(BF16)
    kr = kr_x * cos + kr_swap * sin
    k_ref[...] = (d(ckvn, wkn_ref[...]) + jnp.concatenate([kr] * N_HEADS, axis=1)).astype(BF16)
    vm_ref[...] = d(ckvn, wv_ref[...]).astype(BF16)


def _front_call(xs, mod, cos_t, sin_t, prm, n_ct):
    bsz, tt, _ = xs.shape
    nt = tt // TM
    prev, nxt = _halo_specs(D_MODEL, tt)
    tab = pl.BlockSpec((TM, LANES), lambda b, i: (i, 0))
    shared = jax.ShapeDtypeStruct((bsz, tt, D_RWKV), BF16)
    dir_spec = pl.BlockSpec((2, None, TM, D_RWKV), lambda b, i: (0, b, i, 0))
    dir_shape = lambda dt: jax.ShapeDtypeStruct((2, bsz, tt, D_RWKV), dt)
    heads = jax.ShapeDtypeStruct((bsz, tt, D_HEADS), BF16)
    p_rwkv = [prm[n] for n in ("w_in", "conv", "w0", "wb", "a0", "ab", "gb", "vec", "ones_bd")]
    p_mla = [prm[n] for n in ("norm_g", "wq1", "wq2", "wkn", "wv")]
    return pl.pallas_call(
        functools.partial(_front_kernel, n_ct=n_ct, nt=nt),
        grid=(bsz, nt),
        in_specs=[_row_spec(D_MODEL), prev, nxt, _mod_spec(n_ct)] + [_full_spec(a.shape) for a in p_rwkv]
        + [tab, tab] + [_full_spec(a.shape) for a in p_mla],
        out_specs=[_row_spec(D_RWKV)] * 5 + [dir_spec] * 3 + [_row_spec(D_HEADS)] * 3,
        out_shape=[shared] * 5 + [dir_shape(F32), dir_shape(BF16), dir_shape(BF16)] + [heads] * 3,
        compiler_params=_cparams(("parallel", "parallel")),
        name="front",
    )(xs, xs, xs, mod, *p_rwkv, cos_t, sin_t, *p_mla)


def _wkv_kernel(*refs, nb):
    ins, (yf_ref, yr_ref, s_ref) = refs[:12], refs[12:]
    step = pl.program_id(1)

    @pl.when(step == 0)
    def _():
        s_ref[...] = jnp.zeros_like(s_ref)

    n2 = 2 * CHUNK
    ri = lax.broadcasted_iota(jnp.int32, (n2, n2), 0)
    ci = lax.broadcasted_iota(jnp.int32, (n2, n2), 1)
    same_head = (ri >> LOG_CHUNK) == (ci >> LOG_CHUNK)
    rt, ct = ri & (CHUNK - 1), ci & (CHUNK - 1)
    eye = ri == ci
    ti = lax.broadcasted_iota(jnp.int32, (CHUNK, CHUNK), 0)
    tj = lax.broadcasted_iota(jnp.int32, (CHUNK, CHUNK), 1)
    strict = [same_head & (ct < rt), same_head & (ct > rt)]
    incl = [same_head & (ct <= rt), same_head & (ct >= rt)]
    tri = [jnp.where(tj <= ti, 1.0, 0.0).astype(BF16), jnp.where(tj >= ti, 1.0, 0.0).astype(BF16)]

    def bdiag(x):
        return jnp.where(same_head, jnp.concatenate([x, x], axis=0), 0.0)

    ch = []
    for d in range(2):
        r_ref, v_ref, kk_ref, lw_ref, kd_ref, bb_ref = ins[6 * d:6 * d + 6]
        for n in range(nb):
            lw = lw_ref[n]
            lc = _mm_exact_lhs(tri[d], lw)
            lend = jnp.sum(lw, axis=0, keepdims=True)
            e_neg = jnp.exp(-lc)
            e_end = jnp.exp(lend - lc)
            p_end = jnp.exp(lend)
            kt_all = kk_ref[n].astype(F32) * jnp.exp(lc - lw)
            rt_all = r_ref[n].astype(F32) * jnp.exp(lc)
            kd, bb = kd_ref[n].astype(F32), bb_ref[n].astype(F32)
            kh_all, bh_all = kd * e_neg, bb * e_neg
            kc_all, bc_all = kd * e_end, bb * e_end
            v_all = v_ref[n]
            for p in range(N_PAIRS):
                sl = slice(p * PAIR, (p + 1) * PAIR)
                ch.append(dict(d=d, slot=(d * nb + n) * N_PAIRS + p, kt=bdiag(kt_all[:, sl]), rr=bdiag(rt_all[:, sl]),
                               vv=bdiag(v_all[:, sl]), kh=kh_all[:, sl], bh=bh_all[:, sl], kc=kc_all[:, sl],
                               bc=bc_all[:, sl], p_end=p_end[:, sl]))
    for c in ch:
        gram = _mm_nt(jnp.concatenate([c["kt"], c["rr"]], axis=0),
                      jnp.concatenate([c["bh"]] * 2 + [c["kh"]] * 2, axis=0))
        st, ic = strict[c["d"]], incl[c["d"]]
        c["a_kb"] = jnp.where(st, gram[:n2, :n2], 0.0)
        c["a_kk"] = jnp.where(st, gram[:n2, n2:], 0.0)
        c["a_rb"] = jnp.where(ic, gram[n2:, :n2], 0.0)
        c["a_rk"] = jnp.where(ic, gram[n2:, n2:], 0.0)
        c["t"] = jnp.where(eye, 1.0, 0.0) - jnp.where((rt >> 1) == (ct >> 1), c["a_kb"], 0.0)
    for c in ch:
        c["akk_v"] = _mm(c["a_kk"], c["vv"])
        c["ark_v"] = _mm(c["a_rk"], c["vv"])
        c["kc_v"] = _mm(bdiag(c["kc"]).T, c["vv"])
    for ls in range(1, LOG_CHUNK):
        off = ((rt >> (ls + 1)) == (ct >> (ls + 1))) & ((rt >> ls) != (ct >> ls))
        for c in ch:
            c["nt"] = _mm(jnp.where(off, c["a_kb"], 0.0), c["t"])
        for c in ch:
            c["t"] = c["t"] - _mm(c["t"], c["nt"])
    for c in ch:
        c["uw"] = _mm(c["t"], jnp.concatenate([c["akk_v"], c["kt"]], axis=1))
    for c in ch:
        c["yq"] = jnp.concatenate([c["ark_v"], c["rr"]], axis=1) - _mm(c["a_rb"], c["uw"])
        diag_p = jnp.where(eye, jnp.broadcast_to(c["p_end"], (n2, n2)), 0.0)
        c["nm"] = jnp.concatenate([c["kc_v"], diag_p], axis=1) - _mm(bdiag(c["bc"]).T, c["uw"])
    for c in ch:
        c["qm"] = _mm3(jnp.concatenate([c["yq"][:, n2:], c["nm"][:, n2:]], axis=0), s_ref[c["slot"]])
    ys = []
    for c in ch:
        y_bd = c["qm"][:n2] + c["yq"][:, :n2]
        s_ref[c["slot"]] = c["qm"][n2:] + c["nm"][:, :n2]
        ys.append(y_bd[:CHUNK] + y_bd[CHUNK:])
    for d, y_ref in enumerate((yf_ref, yr_ref)):
        for n in range(nb):
            o = (d * nb + n) * N_PAIRS
            y_ref[n] = jnp.concatenate(ys[o:o + N_PAIRS], axis=1)


def _wkv_call(r, v, kk, lw, kd, bb, n_ct):
    bsz, tt, c = r.shape
    nc = tt // CHUNK
    n_cc = n_ct * TM // CHUNK
    nb = WKV_BATCH
    assert bsz % nb == 0

    def rev_chunk(s):
        return jnp.where(s < n_cc, n_cc - 1 - s, nc - 1 - (s - n_cc))

    order = (lambda s: s, rev_chunk)
    specs, args = [], []
    for d in range(2):
        shared = pl.BlockSpec((nb, CHUNK, c), lambda b, s, d=d: (b, order[d](s), 0))
        per_dir = pl.BlockSpec((None, nb, CHUNK, c), lambda b, s, d=d: (d, b, order[d](s), 0))
        specs += [shared, shared, shared, per_dir, per_dir, per_dir]
        args += [r, v, kk, lw, kd, bb]
    out = jax.ShapeDtypeStruct((bsz, tt, c), F32)
    return pl.pallas_call(
        functools.partial(_wkv_kernel, nb=nb),
        grid=(bsz // nb, nc),
        in_specs=specs,
        out_specs=[specs[0], specs[6]],
        out_shape=[out, out],
        scratch_shapes=[pltpu.VMEM((2 * nb * N_PAIRS, PAIR, PAIR), F32)],
        compiler_params=_cparams(("parallel", "arbitrary")),
        name="wkv_scan",
    )(*args)


def _attn_kernel(q_ref, k_ref, v_ref, o_ref, *, n_ctx_rows, n_ct, t0):
    i = pl.program_id(2) + t0
    n_keys = k_ref.shape[0]
    kc = ATTN_KEYS if (n_keys - n_ctx_rows) % ATTN_KEYS == 0 else TM

    def attend(chunks):
        q = q_ref[...]
        scores = lambda c: _mm_nt(q, k_ref[c[0]:c[0] + c[1], :])
        s_next = scores(chunks[0])
        m = l = acc = None
        for n, c in enumerate(chunks):
            s = s_next
            if n + 1 < len(chunks):
                s_next = scores(chunks[n + 1])
            m_c = jnp.max(s, axis=-1, keepdims=True)
            m_new = m_c if m is None else jnp.maximum(m, m_c)
            p = jnp.exp(s - m_new)
            l_c = jnp.sum(p, axis=-1, keepdims=True)
            pv = _mm(p, v_ref[c[0]:c[0] + c[1], :])
            if m is None:
                l, acc = l_c, pv
            else:
                alpha = jnp.exp(m - m_new)
                l, acc = alpha * l + l_c, alpha * acc + pv
            m = m_new
        o_ref[...] = (acc / l).astype(o_ref.dtype)

    ctx_chunk = [(0, n_ctx_rows)]
    lat_chunks = ctx_chunk + [(s, kc) for s in range(n_ctx_rows, n_keys, kc)]
    if t0 >= n_ct:
        attend(lat_chunks)
    else:
        @pl.when(i < n_ct)
        def _():
            attend(ctx_chunk)

        @pl.when(i >= n_ct)
        def _():
            attend(lat_chunks)


def _attn_call(q, k, v, n_ct, t0):
    bsz, tt, _ = q.shape
    nt = tt // TM
    q_spec = pl.BlockSpec((None, TM, HEAD_PAD), lambda b, h, i: (b, i + t0, h))
    kv_spec = pl.BlockSpec((None, tt, HEAD_PAD), lambda b, h, i: (b, 0, h))
    return pl.pallas_call(
        functools.partial(_attn_kernel, n_ctx_rows=n_ct * TM, n_ct=n_ct, t0=t0),
        grid=(bsz, N_HEADS, nt - t0),
        in_specs=[q_spec, kv_spec, kv_spec],
        out_specs=q_spec,
        out_shape=jax.ShapeDtypeStruct((bsz, tt, D_HEADS), BF16),
        compiler_params=_cparams(("parallel", "parallel", "arbitrary")),
        name="attention",
    )(q, k, v)


def _out_kernel(yf_ref, yr_ref, bv_ref, g_ref, o_ref, x_ref, mod_ref, gn_ref, ones_ref, wor_ref, wom_ref, ln_ref,
                out_ref):
    ones_bd = ones_ref[...]
    y = yf_ref[...] + yr_ref[...]
    inv_n = 1.0 / RWKV_HEAD
    yc = y - _seg_sum(y, ones_bd) * inv_n
    var = _seg_sum(yc * yc, ones_bd) * inv_n
    gn = yc * lax.rsqrt(var + GN_EPS) * gn_ref[0:1] + gn_ref[1:2]
    ro = (gn + bv_ref[...]) * g_ref[...]
    m = _mm(ro, wor_ref[...]) + jnp.dot(o_ref[...], wom_ref[...], preferred_element_type=F32)
    mod = mod_ref[...]
    out_ref[...] = _layer_norm(DEEPNORM_ALPHA * x_ref[...] + mod[2:3] * m, ln_ref[0:1], ln_ref[1:2])


def _out_call(yf, yr, bv, g, o, xs, mod, prm, n_ct, t0):
    bsz, tt, _ = xs.shape
    nt = tt // TM
    params = [prm[n] for n in ("gn", "ones_bd", "wo_r", "wo_m", "ln1")]
    return pl.pallas_call(
        _out_kernel,
        grid=(bsz, nt - t0),
        in_specs=[_row_spec(D_RWKV, t0)] * 4 + [_row_spec(D_HEADS, t0), _row_spec(D_MODEL, t0), _mod_spec(n_ct, t0)]
        + [_full_spec(a.shape) for a in params],
        out_specs=_row_spec(D_MODEL),
        out_shape=jax.ShapeDtypeStruct((bsz, tt - t0 * TM, D_MODEL), F32),
        compiler_params=_cparams(("parallel", "parallel")),
        name="mixer_out",
    )(yf, yr, bv, g, o, xs, mod, *params)


def _ffn_kernel(x_ref, xp_ref, xn_ref, mod_ref, wup_ref, conv_ref, wd_ref, ln_ref, out_ref, *, n_ct, nt):
    i = pl.program_id(1)
    mod = mod_ref[...]
    u = _with_halo(x_ref, xp_ref, xn_ref, 1.0 + mod[4:5], mod[3:4], i, n_ct, nt)

    def up(j):
        hs = []
        for o in (j * FF_SUB, D_FF + j * FF_SUB):
            h = jnp.dot(u, wup_ref[:, o:o + FF_SUB], preferred_element_type=F32)
            taps = conv_ref[:, o:o + FF_SUB]
            hs.append(_conv3(h, taps) + taps[3:4])
        return hs

    acc = None
    h_next = up(0)
    for j in range(N_FF_SUB):
        hg, hv = h_next
        if j + 1 < N_FF_SUB:
            h_next = up(j + 1)
        part = _mm(hg * _sigmoid(hg) * hv, wd_ref[j * FF_SUB:(j + 1) * FF_SUB, :])
        acc = part if acc is None else acc + part
    out_ref[...] = _layer_norm(DEEPNORM_ALPHA * x_ref[...] + mod[5:6] * acc, ln_ref[0:1], ln_ref[1:2])


def _ffn_call(xs, mod, prm, n_ct):
    bsz, tt, _ = xs.shape
    nt = tt // TM
    prev, nxt = _halo_specs(D_MODEL, tt)
    w_up, conv, w_down, ln2 = prm["w_up"], prm["ffn_conv"], prm["w_down"], prm["ln2"]
    return pl.pallas_call(
        functools.partial(_ffn_kernel, n_ct=n_ct, nt=nt),
        grid=(bsz, nt),
        in_specs=[_row_spec(D_MODEL), prev, nxt, _mod_spec(n_ct), _full_spec(w_up.shape, True),
                  _full_spec(conv.shape), _full_spec(w_down.shape, True), _full_spec(ln2.shape)],
        out_specs=_row_spec(D_MODEL),
        out_shape=jax.ShapeDtypeStruct((bsz, tt, D_MODEL), F32),
        compiler_params=_cparams(("parallel", "parallel")),
        name="conv_ffn",
    )(xs, xs, xs, mod, w_up, conv, w_down, ln2)


def _pad_rows(rows, width):
    out = jnp.zeros((SUBLANES, width), F32)
    for n, r in enumerate(rows):
        out = out.at[n, :r.shape[0]].set(r.astype(F32))
    return out


def _pair_swap(w):
    w2 = w.reshape(*w.shape[:-1], w.shape[-1] // 2, 2)
    return jnp.stack([-w2[..., 1], w2[..., 0]], axis=-1).reshape(w.shape)


def _layer_params(l, w_in, rwkv_conv, w0, w_b, a0, a_b, g_b, k_k, k_a, r_k, gn_g, gn_b, q_norm_g, w_uq,
                  kv_norm_g, w_ukv, w_o, ln1_g, ln1_b, w_up, ffn_conv_w, ffn_conv_b, w_down, ln2_g, ln2_b):
    c = D_RWKV
    wi = w_in[l]
    o_mla = RWKV_COLS
    w_kr = wi[:, o_mla + Q_LORA + KV_LORA:]
    pad_l = jnp.zeros((D_MODEL, QK_NOPE), F32)
    pad_r = jnp.zeros((D_MODEL, LANES - QK_DIM), F32)
    w_in_p = jnp.concatenate(
        [wi[:, :o_mla + Q_LORA + KV_LORA], pad_l, w_kr, pad_r, pad_l, _pair_swap(w_kr), pad_r], axis=1).astype(BF16)

    def lora_bd(w):
        z = jnp.zeros_like(w[0])
        return jnp.concatenate([jnp.concatenate([w[0], z], 1), jnp.concatenate([z, w[1]], 1)], 0).astype(BF16)

    head = jnp.arange(c) // RWKV_HEAD
    ones_bd = (head[:, None] == head[None, :]).astype(BF16)

    wq = w_uq[l].reshape(Q_LORA, N_HEADS, QK_DIM)
    q_nope, q_rope = wq[..., :QK_NOPE], wq[..., QK_NOPE:]
    zq = lambda n: jnp.zeros((Q_LORA, N_HEADS, n), F32)
    wq1 = jnp.concatenate([q_nope, q_rope, zq(LANES - QK_DIM)], -1).reshape(Q_LORA, D_HEADS).astype(BF16)
    wq2 = jnp.concatenate([zq(QK_NOPE), _pair_swap(q_rope), zq(LANES - QK_DIM)], -1).reshape(Q_LORA, D_HEADS).astype(BF16)
    wkv = w_ukv[l].reshape(KV_LORA, N_HEADS, QK_NOPE + V_HEAD)
    zk = jnp.zeros((KV_LORA, N_HEADS, HEAD_PAD - QK_NOPE), F32)
    wkn = jnp.concatenate([wkv[..., :QK_NOPE], zk], -1).reshape(KV_LORA, D_HEADS).astype(BF16)
    wv = jnp.concatenate([wkv[..., QK_NOPE:], zk], -1).reshape(KV_LORA, D_HEADS).astype(BF16)
    wo = w_o[l]
    wo_m = wo[c:].reshape(N_HEADS, V_HEAD, D_MODEL)
    wo_m = jnp.concatenate([wo_m, jnp.zeros((N_HEADS, HEAD_PAD - V_HEAD, D_MODEL), F32)], 1)

    return {
        "w_in": w_in_p,
        "conv": _pad_rows(list(rwkv_conv[l]), RWKV_COLS),
        "w0": _pad_rows([w0[l].reshape(-1)], 2 * c),
        "wb": lora_bd(w_b[l]),
        "a0": _pad_rows([a0[l].reshape(-1)], 2 * c),
        "ab": lora_bd(a_b[l]),
        "gb": g_b[l].astype(BF16),
        "vec": _pad_rows([k_k[l], k_a[l], r_k[l].reshape(-1)], c),
        "ones_bd": ones_bd,
        "norm_g": _pad_rows([q_norm_g[l], kv_norm_g[l]], Q_LORA),
        "wq1": wq1, "wq2": wq2, "wkn": wkn, "wv": wv,
        "gn": _pad_rows([gn_g[l], gn_b[l]], c),
        "wo_r": wo[:c].astype(BF16),
        "wo_m": wo_m.reshape(D_HEADS, D_MODEL).astype(BF16),
        "ln1": _pad_rows([ln1_g[l], ln1_b[l]], D_MODEL),
        "w_up": w_up[l].astype(BF16),
        "ffn_conv": _pad_rows(list(ffn_conv_w[l]) + [ffn_conv_b[l]], 2 * D_FF),
        "w_down": w_down[l].astype(BF16),
        "ln2": _pad_rows([ln2_g[l], ln2_b[l]], D_MODEL),
    }


def _rope_tables(n_ctx, n_lat):
    rows = n_lat // GRID_W
    row = jnp.repeat(jnp.arange(rows, dtype=F32), GRID_W)
    col = jnp.tile(jnp.arange(GRID_W, dtype=F32), rows)
    n_pairs = QK_ROPE // 4
    inv = ROPE_BASE ** (-jnp.arange(n_pairs, dtype=F32) / n_pairs)
    ang = jnp.concatenate([row[:, None] * inv, col[:, None] * inv], axis=-1)
    cos = jnp.repeat(jnp.cos(ang), 2, axis=-1)
    sin = jnp.repeat(jnp.sin(ang), 2, axis=-1)
    tt = n_ctx + n_lat
    cos_t = jnp.ones((tt, LANES), F32).at[n_ctx:, QK_NOPE:QK_DIM].set(cos)
    sin_t = jnp.zeros((tt, LANES), F32).at[n_ctx:, QK_NOPE:QK_DIM].set(sin)
    return cos_t, sin_t


def kernel(x, c, ctx, c_ctx, w_ada, b_ada, w_in, rwkv_conv, w0, w_b, a0, a_b, g_b, k_k, k_a, r_k, gn_g, gn_b,
           q_norm_g, w_uq, kv_norm_g, w_ukv, w_o, ln1_g, ln1_b, w_up, ffn_conv_w, ffn_conv_b, w_down, ln2_g, ln2_b):
    bsz, n_lat, _ = x.shape
    n_ctx = ctx.shape[1]
    assert n_ctx % TM == 0 and n_lat % TM == 0 and n_lat % GRID_W == 0
    n_ct = n_ctx // TM
    n_l = w_ada.shape[0]

    ada_rows = -(-(bsz + 1) // SUBLANES) * SUBLANES
    cc = jnp.zeros((ada_rows, D_MODEL), F32).at[:bsz].set(c).at[bsz].set(c_ctx)
    mods = _ada_call(cc, w_ada, b_ada).reshape(n_l, ada_rows, 6, D_MODEL)
    cos_t, sin_t = _rope_tables(n_ctx, n_lat)
    xs = jnp.concatenate([ctx, x], axis=1)

    for l in range(n_l):
        prm = _layer_params(l, w_in, rwkv_conv, w0, w_b, a0, a_b, g_b, k_k, k_a, r_k, gn_g, gn_b, q_norm_g, w_uq,
                            kv_norm_g, w_ukv, w_o, ln1_g, ln1_b, w_up, ffn_conv_w, ffn_conv_b, w_down, ln2_g, ln2_b)
        m_ctx = jnp.broadcast_to(mods[l, bsz][None], (bsz, 6, D_MODEL))
        mod = jnp.stack([m_ctx, mods[l, :bsz]], axis=1)
        mod = jnp.pad(mod, ((0, 0), (0, 0), (0, SUBLANES - 6), (0, 0))).reshape(2 * bsz, SUBLANES, D_MODEL)

        last = l == n_l - 1
        t0 = n_ct if last else 0
        r, v, kk, g, bv, lw, kd, bb, q, k, vm = _front_call(xs, mod, cos_t, sin_t, prm, n_ct)
        yf, yr = _wkv_call(r, v, kk, lw, kd, bb, n_ct)
        o = _attn_call(q, k, vm, n_ct, t0)
        x1 = _out_call(yf, yr, bv, g, o, xs, mod, prm, n_ct, t0)
        xs = _ffn_call(x1, mod, prm, n_ct - t0)
    return xs
```

```python
import functools

import jax
import jax.numpy as jnp
from jax import lax
from jax.experimental import pallas as pl
from jax.experimental.pallas import tpu as pltpu

F32 = jnp.float32
BF16 = jnp.bfloat16

D_MODEL = 1024
DEPTH = 2
GRID_W = 64
D_RWKV = 512
RWKV_HEAD = 64
N_HEADS = 8
LORA_W = 64
LORA_A = 64
LORA_G = 128
RWKV_COLS = 3 * D_RWKV + 2 * LORA_W + 2 * LORA_A + LORA_G
QK_NOPE = 64
QK_ROPE = 32
QK_DIM = QK_NOPE + QK_ROPE
V_HEAD = 64
Q_LORA = 384
KV_LORA = 256
D_FF = 2816
ROPE_BASE = 10000.0
DEEPNORM_ALPHA = (2 * DEPTH) ** 0.25
LN_EPS = 1e-5
RMS_EPS = 1e-6
GN_EPS = 64e-5
LOG2_E = 1.4426950408889634

LANES = 128
SUBLANES = 8
HEAD_PAD = LANES
D_HEADS = N_HEADS * HEAD_PAD
TM = 256
HALO = SUBLANES
CHUNK = 64
LOG_CHUNK = 6
WKV_BATCH = 2
PAIR = 2 * RWKV_HEAD
N_PAIRS = N_HEADS // 2
ATTN_KEYS = 512
ATTN_HEADS = 4
FF_SUB = 256
N_FF_SUB = D_FF // FF_SUB
FF_DOWN = 2
IN_COLS_PAD = RWKV_COLS + Q_LORA + KV_LORA + 2 * LANES
VMEM_LIMIT = 56 << 20


def _cparams(sem):
    return pltpu.CompilerParams(dimension_semantics=sem, vmem_limit_bytes=VMEM_LIMIT)


def _mm(a, b):
    return jnp.dot(a.astype(BF16), b.astype(BF16), preferred_element_type=F32)


def _mm_nt(a, b):
    return lax.dot_general(a.astype(BF16), b.astype(BF16), (((1,), (1,)), ((), ())),
                           preferred_element_type=F32)


def _split2(x):
    hi = x.astype(BF16)
    lo = (x - hi.astype(F32)).astype(BF16)
    return hi, lo


def _mm3(a, b):
    ah, al = _split2(a)
    bh, bl = _split2(b)
    d = functools.partial(jnp.dot, preferred_element_type=F32)
    return d(ah, bh) + (d(ah, bl) + d(al, bh))


def _mm_exact_rhs(a, b_bf16):
    a1 = a.astype(BF16)
    r1 = a - a1.astype(F32)
    a2 = r1.astype(BF16)
    a3 = (r1 - a2.astype(F32)).astype(BF16)
    d = functools.partial(jnp.dot, preferred_element_type=F32)
    return d(a1, b_bf16) + (d(a2, b_bf16) + d(a3, b_bf16))


def _mm_exact_lhs(a_bf16, b):
    b1 = b.astype(BF16)
    r1 = b - b1.astype(F32)
    b2 = r1.astype(BF16)
    b3 = (r1 - b2.astype(F32)).astype(BF16)
    d = functools.partial(jnp.dot, preferred_element_type=F32)
    return d(a_bf16, b1) + (d(a_bf16, b2) + d(a_bf16, b3))


def _sigmoid(x):
    return 1.0 / (1.0 + jnp.exp(-x))


def _layer_norm(x, g, b):
    mu = jnp.mean(x, axis=-1, keepdims=True)
    xc = x - mu
    var = jnp.mean(xc * xc, axis=-1, keepdims=True)
    return xc * lax.rsqrt(var + LN_EPS) * g + b


def _rms_norm(x, g):
    return x * lax.rsqrt(jnp.mean(x * x, axis=-1, keepdims=True) + RMS_EPS) * g


def _row_spec(cols, t0=0):
    return pl.BlockSpec((None, TM, cols), lambda b, i, *_: (b, i + t0, 0))


def _mod_spec(n_ct, t0=0):
    return pl.BlockSpec((None, SUBLANES, D_MODEL),
                        lambda b, i, *_: (2 * b + (i + t0 >= n_ct).astype(jnp.int32), 0, 0))


def _full_spec(shape, single_buffer=False):
    nd = len(shape)
    mode = dict(pipeline_mode=pl.Buffered(1)) if single_buffer else {}
    return pl.BlockSpec(shape, lambda *_: (0,) * nd, **mode)


def _halo_specs(cols, tt, t0=0):
    per = TM // HALO
    last = tt // HALO - 1
    prev = pl.BlockSpec((None, HALO, cols), lambda b, i, *_: (b, jnp.maximum((i + t0) * per - 1, 0), 0))
    nxt = pl.BlockSpec((None, HALO, cols), lambda b, i, *_: (b, jnp.minimum((i + t0 + 1) * per, last), 0))
    return prev, nxt


def _with_halo(x_ref, xp_ref, xn_ref, scale, shift, i, n_ct, nt):
    has_prev = (i != 0) & (i != n_ct)
    has_next = (i != n_ct - 1) & (i != nt - 1)
    mod = lambda x: x * scale + shift
    return jnp.concatenate([jnp.where(has_prev, mod(xp_ref[...]), 0.0), mod(x_ref[...]),
                            jnp.where(has_next, mod(xn_ref[...]), 0.0)], axis=0).astype(BF16)


def _conv3(h, taps):
    rows = TM + 2 * HALO
    lo, hi = HALO, HALO + TM
    return (pltpu.roll(h, 1, 0)[lo:hi] * taps[0:1] + h[lo:hi] * taps[1:2]
            + pltpu.roll(h, rows - 1, 0)[lo:hi] * taps[2:3])


ADA_COLS = 1536


def _ada_kernel(c_ref, w_ref, b_ref, o_ref):
    c = c_ref[...]
    o_ref[...] = _mm3(c * _sigmoid(c), w_ref[...]) + b_ref[...]


def _ada_call(cc, w_ada, b_ada):
    n_l = w_ada.shape[0]
    rows = cc.shape[0]
    return pl.pallas_call(
        _ada_kernel,
        grid=(n_l, 6 * D_MODEL // ADA_COLS),
        in_specs=[
            pl.BlockSpec((rows, D_MODEL), lambda l, j: (0, 0)),
            pl.BlockSpec((None, D_MODEL, ADA_COLS), lambda l, j: (l, 0, j)),
            pl.BlockSpec((None, 1, ADA_COLS), lambda l, j: (l, 0, j)),
        ],
        out_specs=pl.BlockSpec((None, rows, ADA_COLS), lambda l, j: (l, 0, j)),
        out_shape=jax.ShapeDtypeStruct((n_l, rows, 6 * D_MODEL), F32),
        compiler_params=_cparams(("arbitrary", "arbitrary")),
        name="ada",
    )(cc, w_ada, b_ada.reshape(n_l, 1, 6 * D_MODEL))


def _seg_sum(x, ones_bd):
    return _mm_exact_rhs(x, ones_bd)


def _front_kernel(x_ref, xp_ref, xn_ref, mod_ref, w_ref, conv_ref, w0_ref, wb_ref, a0_ref, ab_ref, gb_ref, vec_ref,
                  ones_ref, cos_ref, sin_ref, ng_ref, wq1_ref, wq2_ref, wkn_ref, wv_ref,
                  r_ref, v_ref, kk_ref, g_ref, bv_ref, lw_ref, kd_ref, bb_ref, q_ref, k_ref, vm_ref, *, n_ct, nt):
    i = pl.program_id(1)
    m = mod_ref[...]
    u = _with_halo(x_ref, xp_ref, xn_ref, 1.0 + m[1:2], m[0:1], i, n_ct, nt)
    d = functools.partial(jnp.dot, preferred_element_type=F32)
    z = _conv3(d(u, w_ref[:, :RWKV_COLS]), conv_ref[...])
    pm = d(u, w_ref[:, RWKV_COLS:])[HALO:HALO + TM]

    c = D_RWKV
    r, k, v = z[:, 0:c], z[:, c:2 * c], z[:, 2 * c:3 * c]
    o = 3 * c
    w_lo = z[:, o:o + 2 * LORA_W]
    a_lo = z[:, o + 2 * LORA_W:o + 2 * LORA_W + 2 * LORA_A]
    g_lo = z[:, o + 2 * LORA_W + 2 * LORA_A:]
    vec = vec_ref[...]
    k_k, k_a, r_k = vec[0:1], vec[1:2], vec[2:3]
    ones_bd = ones_ref[...]

    t = -(w0_ref[0:1, :] + _mm(jnp.tanh(w_lo), wb_ref[...]))
    sp = jnp.maximum(t, 0.0) + jnp.log(1.0 + jnp.exp(-jnp.abs(t)))
    lw = -jnp.exp(-sp - 0.5)
    a = _sigmoid(a0_ref[0:1, :] + _mm(a_lo, ab_ref[...]))
    g = _mm(_sigmoid(g_lo), gb_ref[...])

    kk = k * k_k
    nrm = jnp.sqrt(_seg_sum(kk * kk, ones_bd))
    kk = kk / jnp.maximum(nrm, 1e-12)
    kd0 = k * (1.0 + (a[:, :c] - 1.0) * k_a)
    kd1 = k * (1.0 + (a[:, c:] - 1.0) * k_a)
    bonus = _seg_sum(r * (kd0 + kd1) * r_k, ones_bd) * v

    r_ref[...] = r.astype(BF16)
    v_ref[...] = v.astype(BF16)
    kk_ref[...] = kk.astype(BF16)
    g_ref[...] = g.astype(BF16)
    bv_ref[...] = bonus.astype(BF16)
    lw_ref[0] = lw[:, :c]
    lw_ref[1] = lw[:, c:]
    kd_ref[0] = kd0.astype(BF16)
    kd_ref[1] = kd1.astype(BF16)
    bb_ref[0] = (kk * a[:, :c]).astype(BF16)
    bb_ref[1] = (kk * a[:, c:]).astype(BF16)

    cq, ckv = pm[:, :Q_LORA], pm[:, Q_LORA:Q_LORA + KV_LORA]
    kr_x, kr_swap = pm[:, Q_LORA + KV_LORA:Q_LORA + KV_LORA + LANES], pm[:, Q_LORA + KV_LORA + LANES:]
    cos, sin = cos_ref[...], sin_ref[...]
    cos8 = jnp.concatenate([cos] * N_HEADS, axis=1)
    sin8 = jnp.concatenate([sin] * N_HEADS, axis=1)
    cqn = _rms_norm(cq, ng_ref[0:1, :Q_LORA]).astype(BF16)
    q = d(cqn, wq1_ref[...]) * cos8 + d(cqn, wq2_ref[...]) * sin8
    q_ref[...] = (q * (QK_DIM ** -0.5 * LOG2_E)).astype(BF16)
    ckvn = _rms_norm(ckv, ng_ref[1:2, :KV_LORA]).astype(BF16)
    kr = kr_x * cos + kr_swap * sin
    k_ref[...] = (d(ckvn, wkn_ref[...]) + jnp.concatenate([kr] * N_HEADS, axis=1)).astype(BF16)
    vm_ref[...] = d(ckvn, wv_ref[...]).T.astype(BF16)


def _front_call(xs, mod, cos_t, sin_t, prm, n_ct):
    bsz, tt, _ = xs.shape
    nt = tt // TM
    prev, nxt = _halo_specs(D_MODEL, tt)
    tab = pl.BlockSpec((TM, LANES), lambda b, i: (i, 0))
    shared = jax.ShapeDtypeStruct((bsz, tt, D_RWKV), BF16)
    dir_spec = pl.BlockSpec((2, None, TM, D_RWKV), lambda b, i: (0, b, i, 0))
    dir_shape = lambda dt: jax.ShapeDtypeStruct((2, bsz, tt, D_RWKV), dt)
    heads = jax.ShapeDtypeStruct((bsz, tt, D_HEADS), BF16)
    p_rwkv = [prm[n] for n in ("w_in", "conv", "w0", "wb", "a0", "ab", "gb", "vec", "ones_bd")]
    p_mla = [prm[n] for n in ("norm_g", "wq1", "wq2", "wkn", "wv")]
    return pl.pallas_call(
        functools.partial(_front_kernel, n_ct=n_ct, nt=nt),
        grid=(bsz, nt),
        in_specs=[_row_spec(D_MODEL), prev, nxt, _mod_spec(n_ct)] + [_full_spec(a.shape) for a in p_rwkv]
        + [tab, tab] + [_full_spec(a.shape) for a in p_mla],
        out_specs=[_row_spec(D_RWKV)] * 5 + [dir_spec] * 3 + [_row_spec(D_HEADS)] * 2
        + [pl.BlockSpec((None, D_HEADS, TM), lambda b, i: (b, 0, i))],
        out_shape=[shared] * 5 + [dir_shape(F32), dir_shape(BF16), dir_shape(BF16)] + [heads] * 2
        + [jax.ShapeDtypeStruct((bsz, D_HEADS, tt), BF16)],
        compiler_params=_cparams(("parallel", "parallel")),
        name="front",
    )(xs, xs, xs, mod, *p_rwkv, cos_t, sin_t, *p_mla)


def _wkv_kernel(*refs, nb):
    ins, (yf_ref, yr_ref, s_ref) = refs[:12], refs[12:]
    step = pl.program_id(1)

    @pl.when(step == 0)
    def _():
        s_ref[...] = jnp.zeros_like(s_ref)

    n2 = 2 * CHUNK
    ri = lax.broadcasted_iota(jnp.int32, (n2, n2), 0)
    ci = lax.broadcasted_iota(jnp.int32, (n2, n2), 1)
    same_head = (ri >> LOG_CHUNK) == (ci >> LOG_CHUNK)
    rt, ct = ri & (CHUNK - 1), ci & (CHUNK - 1)
    eye = ri == ci
    ti = lax.broadcasted_iota(jnp.int32, (CHUNK, CHUNK), 0)
    tj = lax.broadcasted_iota(jnp.int32, (CHUNK, CHUNK), 1)
    strict = [same_head & (ct < rt), same_head & (ct > rt)]
    incl = [same_head & (ct <= rt), same_head & (ct >= rt)]
    tri = [jnp.where(tj <= ti, 1.0, 0.0).astype(BF16), jnp.where(tj >= ti, 1.0, 0.0).astype(BF16)]

    def bdiag(x):
        return jnp.where(same_head, jnp.concatenate([x, x], axis=0), 0.0)

    ch = []
    for d in range(2):
        r_ref, v_ref, kk_ref, lw_ref, kd_ref, bb_ref = ins[6 * d:6 * d + 6]
        for n in range(nb):
            lw = lw_ref[n]
            lc = _mm_exact_lhs(tri[d], lw)
            lend = jnp.sum(lw, axis=0, keepdims=True)
            e_neg = jnp.exp(-lc)
            e_end = jnp.exp(lend - lc)
            p_end = jnp.exp(lend)
            kt_all = kk_ref[n].astype(F32) * jnp.exp(lc - lw)
            rt_all = r_ref[n].astype(F32) * jnp.exp(lc)
            kd, bb = kd_ref[n].astype(F32), bb_ref[n].astype(F32)
            kh_all, bh_all = kd * e_neg, bb * e_neg
            kc_all, bc_all = kd * e_end, bb * e_end
            v_all = v_ref[n]
            for p in range(N_PAIRS):
                sl = slice(p * PAIR, (p + 1) * PAIR)
                ch.append(dict(d=d, slot=(d * nb + n) * N_PAIRS + p, kt=bdiag(kt_all[:, sl]), rr=bdiag(rt_all[:, sl]),
                               vv=bdiag(v_all[:, sl]), kh=kh_all[:, sl], bh=bh_all[:, sl], kc=kc_all[:, sl],
                               bc=bc_all[:, sl], p_end=p_end[:, sl]))
    for c in ch:
        gram = _mm_nt(jnp.concatenate([c["kt"], c["rr"]], axis=0),
                      jnp.concatenate([c["bh"]] * 2 + [c["kh"]] * 2, axis=0))
        st, ic = strict[c["d"]], incl[c["d"]]
        c["a_kb"] = jnp.where(st, gram[:n2, :n2], 0.0)
        c["a_kk"] = jnp.where(st, gram[:n2, n2:], 0.0)
        c["a_rb"] = jnp.where(ic, gram[n2:, :n2], 0.0)
        c["a_rk"] = jnp.where(ic, gram[n2:, n2:], 0.0)
        c["t"] = jnp.where(eye, 1.0, 0.0) - jnp.where((rt >> 1) == (ct >> 1), c["a_kb"], 0.0)
    for c in ch:
        c["akk_v"] = _mm(c["a_kk"], c["vv"])
        c["ark_v"] = _mm(c["a_rk"], c["vv"])
        c["kc_v"] = _mm(bdiag(c["kc"]).T, c["vv"])
    for ls in range(1, LOG_CHUNK):
        off = ((rt >> (ls + 1)) == (ct >> (ls + 1))) & ((rt >> ls) != (ct >> ls))
        for c in ch:
            c["nt"] = _mm(jnp.where(off, c["a_kb"], 0.0), c["t"])
        for c in ch:
            c["t"] = c["t"] - _mm(c["t"], c["nt"])
    for c in ch:
        c["uw"] = _mm(c["t"], jnp.concatenate([c["akk_v"], c["kt"]], axis=1))
    for c in ch:
        c["yq"] = jnp.concatenate([c["ark_v"], c["rr"]], axis=1) - _mm(c["a_rb"], c["uw"])
        diag_p = jnp.where(eye, jnp.broadcast_to(c["p_end"], (n2, n2)), 0.0)
        c["nm"] = jnp.concatenate([c["kc_v"], diag_p], axis=1) - _mm(bdiag(c["bc"]).T, c["uw"])
    for c in ch:
        qm_l = jnp.concatenate([c["yq"][:, n2:], c["nm"][:, n2:]], axis=0).astype(BF16)
        s_hi, s_lo = _split2(s_ref[c["slot"]])
        c["qm"] = (jnp.dot(qm_l, s_hi, preferred_element_type=F32) + jnp.dot(qm_l, s_lo, preferred_element_type=F32))
    ys = []
    for c in ch:
        y_bd = c["qm"][:n2] + c["yq"][:, :n2]
        s_ref[c["slot"]] = c["qm"][n2:] + c["nm"][:, :n2]
        ys.append(y_bd[:CHUNK] + y_bd[CHUNK:])
    for d, y_ref in enumerate((yf_ref, yr_ref)):
        for n in range(nb):
            o = (d * nb + n) * N_PAIRS
            y_ref[n] = jnp.concatenate(ys[o:o + N_PAIRS], axis=1)


def _wkv_call(r, v, kk, lw, kd, bb, n_ct):
    bsz, tt, c = r.shape
    nc = tt // CHUNK
    n_cc = n_ct * TM // CHUNK
    nb = WKV_BATCH
    assert bsz % nb == 0

    def rev_chunk(s):
        return jnp.where(s < n_cc, n_cc - 1 - s, nc - 1 - (s - n_cc))

    order = (lambda s: s, rev_chunk)
    specs, args = [], []
    for d in range(2):
        shared = pl.BlockSpec((nb, CHUNK, c), lambda b, s, d=d: (b, order[d](s), 0))
        per_dir = pl.BlockSpec((None, nb, CHUNK, c), lambda b, s, d=d: (d, b, order[d](s), 0))
        specs += [shared, shared, shared, per_dir, per_dir, per_dir]
        args += [r, v, kk, lw, kd, bb]
    out = jax.ShapeDtypeStruct((bsz, tt, c), F32)
    return pl.pallas_call(
        functools.partial(_wkv_kernel, nb=nb),
        grid=(bsz // nb, nc),
        in_specs=specs,
        out_specs=[specs[0], specs[6]],
        out_shape=[out, out],
        scratch_shapes=[pltpu.VMEM((2 * nb * N_PAIRS, PAIR, PAIR), F32)],
        compiler_params=_cparams(("parallel", "arbitrary")),
        name="wkv_scan",
    )(*args)


def _rows_reduce(x, op, reduce):
    n = x.shape[0]
    while n > SUBLANES and n % 2 == 0:
        n //= 2
        x = op(x[:n], x[n:])
    return reduce(x, axis=0, keepdims=True)


def _attn_kernel(q_ref, k_ref, vt_ref, o_ref, *, n_ctx_rows, n_ct, t0):
    i = pl.program_id(2) + t0
    n_keys = k_ref.shape[0]
    kc = ATTN_KEYS if (n_keys - n_ctx_rows) % ATTN_KEYS == 0 else TM

    def attend(chunks):
        heads = range(ATTN_HEADS)
        hs = [slice(g * HEAD_PAD, (g + 1) * HEAD_PAD) for g in heads]
        qs = [q_ref[:, hs[g]] for g in heads]
        scores = lambda g, c: _mm_nt(k_ref[c[0]:c[0] + c[1], hs[g]], qs[g])
        s_next = [scores(g, chunks[0]) for g in heads]
        m, l, acc = ([None] * ATTN_HEADS for _ in range(3))
        for n, c in enumerate(chunks):
            s = s_next
            if n + 1 < len(chunks):
                s_next = [scores(g, chunks[n + 1]) for g in heads]
            m_c = [_rows_reduce(s[g], jnp.maximum, jnp.max) for g in heads]
            m_new = [m_c[g] if m[g] is None else jnp.maximum(m[g], m_c[g]) for g in heads]
            p = [jnp.exp2(s[g] - m_new[g]) for g in heads]
            l_c = [_rows_reduce(p[g], jnp.add, jnp.sum) for g in heads]
            pv = [_mm(vt_ref[hs[g], c[0]:c[0] + c[1]], p[g]) for g in heads]
            for g in heads:
                if m[g] is None:
                    l[g], acc[g] = l_c[g], pv[g]
                else:
                    alpha = jnp.exp2(m[g] - m_new[g])
                    l[g], acc[g] = alpha * l[g] + l_c[g], alpha * acc[g] + pv[g]
                m[g] = m_new[g]
        for g in heads:
            o_ref[:, hs[g]] = (acc[g] / l[g]).T.astype(o_ref.dtype)

    ctx_chunk = [(0, n_ctx_rows)]
    lat_chunks = ctx_chunk + [(s, kc) for s in range(n_ctx_rows, n_keys, kc)]
    if t0 >= n_ct:
        attend(lat_chunks)
    else:
        @pl.when(i < n_ct)
        def _():
            attend(ctx_chunk)

        @pl.when(i >= n_ct)
        def _():
            attend(lat_chunks)


def _attn_call(q, k, vt, n_ct, t0):
    bsz, tt, _ = q.shape
    nt = tt // TM
    w = ATTN_HEADS * HEAD_PAD
    q_spec = pl.BlockSpec((None, TM, w), lambda b, h, i: (b, i + t0, h))
    k_spec = pl.BlockSpec((None, tt, w), lambda b, h, i: (b, 0, h))
    vt_spec = pl.BlockSpec((None, w, tt), lambda b, h, i: (b, h, 0))
    return pl.pallas_call(
        functools.partial(_attn_kernel, n_ctx_rows=n_ct * TM, n_ct=n_ct, t0=t0),
        grid=(bsz, N_HEADS // ATTN_HEADS, nt - t0),
        in_specs=[q_spec, k_spec, vt_spec],
        out_specs=q_spec,
        out_shape=jax.ShapeDtypeStruct((bsz, tt, D_HEADS), BF16),
        compiler_params=_cparams(("parallel", "parallel", "arbitrary")),
        name="attention",
    )(q, k, vt)


def _out_kernel(yf_ref, yr_ref, bv_ref, g_ref, o_ref, x_ref, mod_ref, gn_ref, ones_ref, wor_ref, wom_ref, ln_ref,
                out_ref):
    ones_bd = ones_ref[...]
    y = yf_ref[...] + yr_ref[...]
    inv_n = 1.0 / RWKV_HEAD
    yc = y - _seg_sum(y, ones_bd) * inv_n
    var = _seg_sum(yc * yc, ones_bd) * inv_n
    gn = yc * lax.rsqrt(var + GN_EPS) * gn_ref[0:1] + gn_ref[1:2]
    ro = (gn + bv_ref[...]) * g_ref[...]
    m = _mm(ro, wor_ref[...]) + jnp.dot(o_ref[...], wom_ref[...], preferred_element_type=F32)
    mod = mod_ref[...]
    out_ref[...] = _layer_norm(DEEPNORM_ALPHA * x_ref[...] + mod[2:3] * m, ln_ref[0:1], ln_ref[1:2])


def _out_call(yf, yr, bv, g, o, xs, mod, prm, n_ct, t0):
    bsz, tt, _ = xs.shape
    nt = tt // TM
    params = [prm[n] for n in ("gn", "ones_bd", "wo_r", "wo_m", "ln1")]
    return pl.pallas_call(
        _out_kernel,
        grid=(bsz, nt - t0),
        in_specs=[_row_spec(D_RWKV, t0)] * 4 + [_row_spec(D_HEADS, t0), _row_spec(D_MODEL, t0), _mod_spec(n_ct, t0)]
        + [_full_spec(a.shape) for a in params],
        out_specs=_row_spec(D_MODEL),
        out_shape=jax.ShapeDtypeStruct((bsz, tt - t0 * TM, D_MODEL), F32),
        compiler_params=_cparams(("parallel", "parallel")),
        name="mixer_out",
    )(yf, yr, bv, g, o, xs, mod, *params)


def _ffn_kernel(x_ref, xp_ref, xn_ref, mod_ref, wup_ref, conv_ref, wd_ref, ln_ref, out_ref, *, n_ct, nt):
    i = pl.program_id(1)
    mod = mod_ref[...]
    u = _with_halo(x_ref, xp_ref, xn_ref, 1.0 + mod[4:5], mod[3:4], i, n_ct, nt)

    def up(j):
        hs = []
        for o in (j * FF_SUB, D_FF + j * FF_SUB):
            h = jnp.dot(u, wup_ref[:, o:o + FF_SUB], preferred_element_type=F32)
            taps = conv_ref[:, o:o + FF_SUB]
            hs.append(_conv3(h, taps) + taps[3:4])
        return hs

    acc = None
    acts = []
    h_next = up(0)
    for j in range(N_FF_SUB):
        hg, hv = h_next
        if j + 1 < N_FF_SUB:
            h_next = up(j + 1)
        half = 0.5 * hg
        acts.append(((half + half * jnp.tanh(half)) * hv).astype(BF16))
        if len(acts) == FF_DOWN or j + 1 == N_FF_SUB:
            lo = (j + 1 - len(acts)) * FF_SUB
            part = jnp.dot(jnp.concatenate(acts, axis=1), wd_ref[lo:(j + 1) * FF_SUB, :], preferred_element_type=F32)
            acc = part if acc is None else acc + part
            acts = []
    out_ref[...] = _layer_norm(DEEPNORM_ALPHA * x_ref[...] + mod[5:6] * acc, ln_ref[0:1], ln_ref[1:2])


def _ffn_call(xs, mod, prm, n_ct):
    bsz, tt, _ = xs.shape
    nt = tt // TM
    prev, nxt = _halo_specs(D_MODEL, tt)
    w_up, conv, w_down, ln2 = prm["w_up"], prm["ffn_conv"], prm["w_down"], prm["ln2"]
    return pl.pallas_call(
        functools.partial(_ffn_kernel, n_ct=n_ct, nt=nt),
        grid=(bsz, nt),
        in_specs=[_row_spec(D_MODEL), prev, nxt, _mod_spec(n_ct), _full_spec(w_up.shape, True),
                  _full_spec(conv.shape), _full_spec(w_down.shape, True), _full_spec(ln2.shape)],
        out_specs=_row_spec(D_MODEL),
        out_shape=jax.ShapeDtypeStruct((bsz, tt, D_MODEL), F32),
        compiler_params=_cparams(("parallel", "parallel")),
        name="conv_ffn",
    )(xs, xs, xs, mod, w_up, conv, w_down, ln2)


def _pad_rows(rows, width):
    out = jnp.zeros((SUBLANES, width), F32)
    for n, r in enumerate(rows):
        out = out.at[n, :r.shape[0]].set(r.astype(F32))
    return out


def _pair_swap(w):
    w2 = w.reshape(*w.shape[:-1], w.shape[-1] // 2, 2)
    return jnp.stack([-w2[..., 1], w2[..., 0]], axis=-1).reshape(w.shape)


def _layer_params(l, w_in, rwkv_conv, w0, w_b, a0, a_b, g_b, k_k, k_a, r_k, gn_g, gn_b, q_norm_g, w_uq,
                  kv_norm_g, w_ukv, w_o, ln1_g, ln1_b, w_up, ffn_conv_w, ffn_conv_b, w_down, ln2_g, ln2_b):
    c = D_RWKV
    wi = w_in[l]
    o_mla = RWKV_COLS
    w_kr = wi[:, o_mla + Q_LORA + KV_LORA:]
    pad_l = jnp.zeros((D_MODEL, QK_NOPE), F32)
    pad_r = jnp.zeros((D_MODEL, LANES - QK_DIM), F32)
    w_in_p = jnp.concatenate(
        [wi[:, :o_mla + Q_LORA + KV_LORA], pad_l, w_kr, pad_r, pad_l, _pair_swap(w_kr), pad_r], axis=1).astype(BF16)

    def lora_bd(w):
        z = jnp.zeros_like(w[0])
        return jnp.concatenate([jnp.concatenate([w[0], z], 1), jnp.concatenate([z, w[1]], 1)], 0).astype(BF16)

    head = jnp.arange(c) // RWKV_HEAD
    ones_bd = (head[:, None] == head[None, :]).astype(BF16)

    wq = w_uq[l].reshape(Q_LORA, N_HEADS, QK_DIM)
    q_nope, q_rope = wq[..., :QK_NOPE], wq[..., QK_NOPE:]
    zq = lambda n: jnp.zeros((Q_LORA, N_HEADS, n), F32)
    wq1 = jnp.concatenate([q_nope, q_rope, zq(LANES - QK_DIM)], -1).reshape(Q_LORA, D_HEADS).astype(BF16)
    wq2 = jnp.concatenate([zq(QK_NOPE), _pair_swap(q_rope), zq(LANES - QK_DIM)], -1).reshape(Q_LORA, D_HEADS).astype(BF16)
    wkv = w_ukv[l].reshape(KV_LORA, N_HEADS, QK_NOPE + V_HEAD)
    zk = jnp.zeros((KV_LORA, N_HEADS, HEAD_PAD - QK_NOPE), F32)
    wkn = jnp.concatenate([wkv[..., :QK_NOPE], zk], -1).reshape(KV_LORA, D_HEADS).astype(BF16)
    wv = jnp.concatenate([wkv[..., QK_NOPE:], zk], -1).reshape(KV_LORA, D_HEADS).astype(BF16)
    wo = w_o[l]
    wo_m = wo[c:].reshape(N_HEADS, V_HEAD, D_MODEL)
    wo_m = jnp.concatenate([wo_m, jnp.zeros((N_HEADS, HEAD_PAD - V_HEAD, D_MODEL), F32)], 1)

    return {
        "w_in": w_in_p,
        "conv": _pad_rows(list(rwkv_conv[l]), RWKV_COLS),
        "w0": _pad_rows([w0[l].reshape(-1)], 2 * c),
        "wb": lora_bd(w_b[l]),
        "a0": _pad_rows([a0[l].reshape(-1)], 2 * c),
        "ab": lora_bd(a_b[l]),
        "gb": g_b[l].astype(BF16),
        "vec": _pad_rows([k_k[l], k_a[l], r_k[l].reshape(-1)], c),
        "ones_bd": ones_bd,
        "norm_g": _pad_rows([q_norm_g[l], kv_norm_g[l]], Q_LORA),
        "wq1": wq1, "wq2": wq2, "wkn": wkn, "wv": wv,
        "gn": _pad_rows([gn_g[l], gn_b[l]], c),
        "wo_r": wo[:c].astype(BF16),
        "wo_m": wo_m.reshape(D_HEADS, D_MODEL).astype(BF16),
        "ln1": _pad_rows([ln1_g[l], ln1_b[l]], D_MODEL),
        "w_up": w_up[l].astype(BF16),
        "ffn_conv": _pad_rows(list(ffn_conv_w[l]) + [ffn_conv_b[l]], 2 * D_FF),
        "w_down": w_down[l].astype(BF16),
        "ln2": _pad_rows([ln2_g[l], ln2_b[l]], D_MODEL),
    }


def _rope_tables(n_ctx, n_lat):
    rows = n_lat // GRID_W
    row = jnp.repeat(jnp.arange(rows, dtype=F32), GRID_W)
    col = jnp.tile(jnp.arange(GRID_W, dtype=F32), rows)
    n_pairs = QK_ROPE // 4
    inv = ROPE_BASE ** (-jnp.arange(n_pairs, dtype=F32) / n_pairs)
    ang = jnp.concatenate([row[:, None] * inv, col[:, None] * inv], axis=-1)
    cos = jnp.repeat(jnp.cos(ang), 2, axis=-1)
    sin = jnp.repeat(jnp.sin(ang), 2, axis=-1)
    tt = n_ctx + n_lat
    cos_t = jnp.ones((tt, LANES), F32).at[n_ctx:, QK_NOPE:QK_DIM].set(cos)
    sin_t = jnp.zeros((tt, LANES), F32).at[n_ctx:, QK_NOPE:QK_DIM].set(sin)
    return cos_t, sin_t


def kernel(x, c, ctx, c_ctx, w_ada, b_ada, w_in, rwkv_conv, w0, w_b, a0, a_b, g_b, k_k, k_a, r_k, gn_g, gn_b,
           q_norm_g, w_uq, kv_norm_g, w_ukv, w_o, ln1_g, ln1_b, w_up, ffn_conv_w, ffn_conv_b, w_down, ln2_g, ln2_b):
    bsz, n_lat, _ = x.shape
    n_ctx = ctx.shape[1]
    assert n_ctx % TM == 0 and n_lat % TM == 0 and n_lat % GRID_W == 0
    n_ct = n_ctx // TM
    n_l = w_ada.shape[0]

    ada_rows = -(-(bsz + 1) // SUBLANES) * SUBLANES
    cc = jnp.zeros((ada_rows, D_MODEL), F32).at[:bsz].set(c).at[bsz].set(c_ctx)
    mods = _ada_call(cc, w_ada, b_ada).reshape(n_l, ada_rows, 6, D_MODEL)
    cos_t, sin_t = _rope_tables(n_ctx, n_lat)
    xs = jnp.concatenate([ctx, x], axis=1)

    for l in range(n_l):
        prm = _layer_params(l, w_in, rwkv_conv, w0, w_b, a0, a_b, g_b, k_k, k_a, r_k, gn_g, gn_b, q_norm_g, w_uq,
                            kv_norm_g, w_ukv, w_o, ln1_g, ln1_b, w_up, ffn_conv_w, ffn_conv_b, w_down, ln2_g, ln2_b)
        m_ctx = jnp.broadcast_to(mods[l, bsz][None], (bsz, 6, D_MODEL))
        mod = jnp.stack([m_ctx, mods[l, :bsz]], axis=1)
        mod = jnp.pad(mod, ((0, 0), (0, 0), (0, SUBLANES - 6), (0, 0))).reshape(2 * bsz, SUBLANES, D_MODEL)

        last = l == n_l - 1
        t0 = n_ct if last else 0
        r, v, kk, g, bv, lw, kd, bb, q, k, vm = _front_call(xs, mod, cos_t, sin_t, prm, n_ct)
        yf, yr = _wkv_call(r, v, kk, lw, kd, bb, n_ct)
        o = _attn_call(q, k, vm, n_ct, t0)
        x1 = _out_call(yf, yr, bv, g, o, xs, mod, prm, n_ct, t0)
        xs = _ffn_call(x1, mod, prm, n_ct - t0)
    return xs
```

```python
import functools

import jax
import jax.numpy as jnp
from jax import lax
from jax.experimental import pallas as pl
from jax.experimental.pallas import tpu as pltpu

F32 = jnp.float32
BF16 = jnp.bfloat16

D_MODEL = 1024
DEPTH = 2
GRID_W = 64
D_RWKV = 512
RWKV_HEAD = 64
N_HEADS = 8
LORA_W = 64
LORA_A = 64
LORA_G = 128
RWKV_COLS = 3 * D_RWKV + 2 * LORA_W + 2 * LORA_A + LORA_G
QK_NOPE = 64
QK_ROPE = 32
QK_DIM = QK_NOPE + QK_ROPE
V_HEAD = 64
Q_LORA = 384
KV_LORA = 256
D_FF = 2816
ROPE_BASE = 10000.0
DEEPNORM_ALPHA = (2 * DEPTH) ** 0.25
LN_EPS = 1e-5
RMS_EPS = 1e-6
GN_EPS = 64e-5
LOG2_E = 1.4426950408889634

LANES = 128
SUBLANES = 8
HEAD_PAD = LANES
D_HEADS = N_HEADS * HEAD_PAD
TM = 256
HALO = SUBLANES
CHUNK = 64
LOG_CHUNK = 6
WKV_BATCH = 2
PAIR = 2 * RWKV_HEAD
N_PAIRS = N_HEADS // 2
ATTN_KEYS = 1024
ATTN_HEADS = 4
FF_SUB = 256
N_FF_SUB = D_FF // FF_SUB
FF_DOWN = 2
IN_COLS_PAD = RWKV_COLS + Q_LORA + KV_LORA + 2 * LANES
VMEM_LIMIT = 56 << 20


def _cparams(sem):
    return pltpu.CompilerParams(dimension_semantics=sem, vmem_limit_bytes=VMEM_LIMIT)


def _mm(a, b):
    return jnp.dot(a.astype(BF16), b.astype(BF16), preferred_element_type=F32)


def _mm_nt(a, b):
    return lax.dot_general(a.astype(BF16), b.astype(BF16), (((1,), (1,)), ((), ())),
                           preferred_element_type=F32)


def _split2(x):
    hi = x.astype(BF16)
    lo = (x - hi.astype(F32)).astype(BF16)
    return hi, lo


def _mm3(a, b):
    ah, al = _split2(a)
    bh, bl = _split2(b)
    d = functools.partial(jnp.dot, preferred_element_type=F32)
    return d(ah, bh) + (d(ah, bl) + d(al, bh))


def _mm_exact_rhs(a, b_bf16):
    a1, a2 = _split2(a)
    d = functools.partial(jnp.dot, preferred_element_type=F32)
    return d(a1, b_bf16) + d(a2, b_bf16)


def _mm_exact_lhs(a_bf16, b):
    b1 = b.astype(BF16)
    r1 = b - b1.astype(F32)
    b2 = r1.astype(BF16)
    b3 = (r1 - b2.astype(F32)).astype(BF16)
    d = functools.partial(jnp.dot, preferred_element_type=F32)
    return d(a_bf16, b1) + (d(a_bf16, b2) + d(a_bf16, b3))


def _sigmoid(x):
    return 1.0 / (1.0 + jnp.exp(-x))


def _layer_norm(x, g, b):
    mu = jnp.mean(x, axis=-1, keepdims=True)
    xc = x - mu
    var = jnp.mean(xc * xc, axis=-1, keepdims=True)
    return xc * lax.rsqrt(var + LN_EPS) * g + b


def _rms_norm(x, g):
    return x * lax.rsqrt(jnp.mean(x * x, axis=-1, keepdims=True) + RMS_EPS) * g


def _row_spec(cols, t0=0):
    return pl.BlockSpec((None, TM, cols), lambda b, i, *_: (b, i + t0, 0))


def _mod_spec(n_ct, t0=0):
    return pl.BlockSpec((None, SUBLANES, D_MODEL),
                        lambda b, i, *_: (2 * b + (i + t0 >= n_ct).astype(jnp.int32), 0, 0))


def _full_spec(shape, single_buffer=False):
    nd = len(shape)
    mode = dict(pipeline_mode=pl.Buffered(1)) if single_buffer else {}
    return pl.BlockSpec(shape, lambda *_: (0,) * nd, **mode)


def _halo_specs(cols, tt, t0=0):
    per = TM // HALO
    last = tt // HALO - 1
    prev = pl.BlockSpec((None, HALO, cols), lambda b, i, *_: (b, jnp.maximum((i + t0) * per - 1, 0), 0))
    nxt = pl.BlockSpec((None, HALO, cols), lambda b, i, *_: (b, jnp.minimum((i + t0 + 1) * per, last), 0))
    return prev, nxt


def _with_halo(x_ref, xp_ref, xn_ref, scale, shift, i, n_ct, nt):
    has_prev = (i != 0) & (i != n_ct)
    has_next = (i != n_ct - 1) & (i != nt - 1)
    mod = lambda x: x * scale + shift
    return jnp.concatenate([jnp.where(has_prev, mod(xp_ref[...]), 0.0), mod(x_ref[...]),
                            jnp.where(has_next, mod(xn_ref[...]), 0.0)], axis=0).astype(BF16)


def _conv3(h, taps):
    rows = TM + 2 * HALO
    lo, hi = HALO, HALO + TM
    return (pltpu.roll(h, 1, 0)[lo:hi] * taps[0:1] + h[lo:hi] * taps[1:2]
            + pltpu.roll(h, rows - 1, 0)[lo:hi] * taps[2:3])


ADA_COLS = 1536


def _ada_kernel(c_ref, w_ref, b_ref, o_ref):
    c = c_ref[...]
    o_ref[...] = _mm3(c * _sigmoid(c), w_ref[...]) + b_ref[...]


def _ada_call(cc, w_ada, b_ada):
    n_l = w_ada.shape[0]
    rows = cc.shape[0]
    return pl.pallas_call(
        _ada_kernel,
        grid=(n_l, 6 * D_MODEL // ADA_COLS),
        in_specs=[
            pl.BlockSpec((rows, D_MODEL), lambda l, j: (0, 0)),
            pl.BlockSpec((None, D_MODEL, ADA_COLS), lambda l, j: (l, 0, j)),
            pl.BlockSpec((None, 1, ADA_COLS), lambda l, j: (l, 0, j)),
        ],
        out_specs=pl.BlockSpec((None, rows, ADA_COLS), lambda l, j: (l, 0, j)),
        out_shape=jax.ShapeDtypeStruct((n_l, rows, 6 * D_MODEL), F32),
        compiler_params=_cparams(("arbitrary", "arbitrary")),
        name="ada",
    )(cc, w_ada, b_ada.reshape(n_l, 1, 6 * D_MODEL))


def _seg_sum(x, ones_bd):
    w = ones_bd.shape[0]
    return jnp.concatenate([_mm_exact_rhs(x[:, o:o + w], ones_bd) for o in range(0, x.shape[1], w)], axis=1)


def _front_kernel(x_ref, xp_ref, xn_ref, mod_ref, w_ref, conv_ref, w0_ref, wb_ref, a0_ref, ab_ref, gb_ref, vec_ref,
                  ones_ref, cos_ref, sin_ref, ng_ref, wq1_ref, wq2_ref, wkn_ref, wv_ref,
                  r_ref, v_ref, kk_ref, g_ref, bv_ref, lw_ref, kd_ref, bb_ref, q_ref, k_ref, vm_ref, *, n_ct, nt):
    i = pl.program_id(1)
    m = mod_ref[...]
    u = _with_halo(x_ref, xp_ref, xn_ref, 1.0 + m[1:2], m[0:1], i, n_ct, nt)
    d = functools.partial(jnp.dot, preferred_element_type=F32)
    z = _conv3(d(u, w_ref[:, :RWKV_COLS]), conv_ref[...])
    pm = d(u, w_ref[:, RWKV_COLS:])[HALO:HALO + TM]

    c = D_RWKV
    r, k, v = z[:, 0:c], z[:, c:2 * c], z[:, 2 * c:3 * c]
    o = 3 * c
    w_lo = z[:, o:o + 2 * LORA_W]
    a_lo = z[:, o + 2 * LORA_W:o + 2 * LORA_W + 2 * LORA_A]
    g_lo = z[:, o + 2 * LORA_W + 2 * LORA_A:]
    vec = vec_ref[...]
    k_k, k_a, r_k = vec[0:1], vec[1:2], vec[2:3]
    ones_bd = ones_ref[...]

    t = -(w0_ref[0:1, :] + _mm(jnp.tanh(w_lo), wb_ref[...]))
    sp = jnp.maximum(t, 0.0) + jnp.log(1.0 + jnp.exp(-jnp.abs(t)))
    lw = -jnp.exp(-sp - 0.5)
    a = _sigmoid(a0_ref[0:1, :] + _mm(a_lo, ab_ref[...]))
    g = _mm(_sigmoid(g_lo), gb_ref[...])

    kk = k * k_k
    nrm = jnp.sqrt(_seg_sum(kk * kk, ones_bd))
    kk = kk / jnp.maximum(nrm, 1e-12)
    kd0 = k * (1.0 + (a[:, :c] - 1.0) * k_a)
    kd1 = k * (1.0 + (a[:, c:] - 1.0) * k_a)
    bonus = _seg_sum(r * (kd0 + kd1) * r_k, ones_bd) * v

    r_ref[...] = r.astype(BF16)
    v_ref[...] = v.astype(BF16)
    kk_ref[...] = kk.astype(BF16)
    g_ref[...] = g.astype(BF16)
    bv_ref[...] = bonus.astype(BF16)
    lw_ref[0] = lw[:, :c]
    lw_ref[1] = lw[:, c:]
    kd_ref[0] = kd0.astype(BF16)
    kd_ref[1] = kd1.astype(BF16)
    bb_ref[0] = (kk * a[:, :c]).astype(BF16)
    bb_ref[1] = (kk * a[:, c:]).astype(BF16)

    cq, ckv = pm[:, :Q_LORA], pm[:, Q_LORA:Q_LORA + KV_LORA]
    kr_x, kr_swap = pm[:, Q_LORA + KV_LORA:Q_LORA + KV_LORA + LANES], pm[:, Q_LORA + KV_LORA + LANES:]
    cos, sin = cos_ref[...], sin_ref[...]
    cos8 = jnp.concatenate([cos] * N_HEADS, axis=1)
    sin8 = jnp.concatenate([sin] * N_HEADS, axis=1)
    cqn = _rms_norm(cq, ng_ref[0:1, :Q_LORA]).astype(BF16)
    q = d(cqn, wq1_ref[...]) * cos8 + d(cqn, wq2_ref[...]) * sin8
    q_ref[...] = (q * (QK_DIM ** -0.5 * LOG2_E)).astype(BF16)
    ckvn = _rms_norm(ckv, ng_ref[1:2, :KV_LORA]).astype(BF16)
    kr = kr_x * cos + kr_swap * sin
    k_ref[...] = (d(ckvn, wkn_ref[...]) + jnp.concatenate([kr] * N_HEADS, axis=1)).astype(BF16)
    vm_ref[...] = d(ckvn, wv_ref[...]).T.astype(BF16)


def _front_call(xs, mod, cos_t, sin_t, prm, n_ct):
    bsz, tt, _ = xs.shape
    nt = tt // TM
    prev, nxt = _halo_specs(D_MODEL, tt)
    tab = pl.BlockSpec((TM, LANES), lambda b, i: (i, 0))
    shared = jax.ShapeDtypeStruct((bsz, tt, D_RWKV), BF16)
    dir_spec = pl.BlockSpec((2, None, TM, D_RWKV), lambda b, i: (0, b, i, 0))
    dir_shape = lambda dt: jax.ShapeDtypeStruct((2, bsz, tt, D_RWKV), dt)
    heads = jax.ShapeDtypeStruct((bsz, tt, D_HEADS), BF16)
    p_rwkv = [prm[n] for n in ("w_in", "conv", "w0", "wb", "a0", "ab", "gb", "vec", "ones_bd")]
    p_mla = [prm[n] for n in ("norm_g", "wq1", "wq2", "wkn", "wv")]
    return pl.pallas_call(
        functools.partial(_front_kernel, n_ct=n_ct, nt=nt),
        grid=(bsz, nt),
        in_specs=[_row_spec(D_MODEL), prev, nxt, _mod_spec(n_ct)] + [_full_spec(a.shape) for a in p_rwkv]
        + [tab, tab] + [_full_spec(a.shape) for a in p_mla],
        out_specs=[_row_spec(D_RWKV)] * 5 + [dir_spec] * 3 + [_row_spec(D_HEADS)] * 2
        + [pl.BlockSpec((None, D_HEADS, TM), lambda b, i: (b, 0, i))],
        out_shape=[shared] * 5 + [dir_shape(F32), dir_shape(BF16), dir_shape(BF16)] + [heads] * 2
        + [jax.ShapeDtypeStruct((bsz, D_HEADS, tt), BF16)],
        compiler_params=_cparams(("parallel", "parallel")),
        name="front",
    )(xs, xs, xs, mod, *p_rwkv, cos_t, sin_t, *p_mla)


def _wkv_kernel(*refs, nb):
    ins, (yf_ref, yr_ref, s_ref) = refs[:12], refs[12:]
    step = pl.program_id(1)

    @pl.when(step == 0)
    def _():
        s_ref[...] = jnp.zeros_like(s_ref)

    n2 = 2 * CHUNK
    ri = lax.broadcasted_iota(jnp.int32, (n2, n2), 0)
    ci = lax.broadcasted_iota(jnp.int32, (n2, n2), 1)
    same_head = (ri >> LOG_CHUNK) == (ci >> LOG_CHUNK)
    rt, ct = ri & (CHUNK - 1), ci & (CHUNK - 1)
    eye = ri == ci
    ti = lax.broadcasted_iota(jnp.int32, (CHUNK, CHUNK), 0)
    tj = lax.broadcasted_iota(jnp.int32, (CHUNK, CHUNK), 1)
    strict = [same_head & (ct < rt), same_head & (ct > rt)]
    incl = [same_head & (ct <= rt), same_head & (ct >= rt)]
    tri = [jnp.where(tj <= ti, 1.0, 0.0).astype(BF16), jnp.where(tj >= ti, 1.0, 0.0).astype(BF16)]

    def bdiag(x):
        return jnp.where(same_head, jnp.concatenate([x, x], axis=0), 0.0)

    ch = []
    for d in range(2):
        r_ref, v_ref, kk_ref, lw_ref, kd_ref, bb_ref = ins[6 * d:6 * d + 6]
        for n in range(nb):
            lw = lw_ref[n]
            lc = _mm_exact_lhs(tri[d], lw)
            lend = jnp.sum(lw, axis=0, keepdims=True)
            e_neg = jnp.exp(-lc)
            e_end = jnp.exp(lend - lc)
            p_end = jnp.exp(lend)
            kt_all = kk_ref[n].astype(F32) * jnp.exp(lc - lw)
            rt_all = r_ref[n].astype(F32) * jnp.exp(lc)
            kd, bb = kd_ref[n].astype(F32), bb_ref[n].astype(F32)
            kh_all, bh_all = kd * e_neg, bb * e_neg
            kc_all, bc_all = kd * e_end, bb * e_end
            v_all = v_ref[n]
            for p in range(N_PAIRS):
                sl = slice(p * PAIR, (p + 1) * PAIR)
                ch.append(dict(d=d, slot=(d * nb + n) * N_PAIRS + p, kt=bdiag(kt_all[:, sl]), rr=bdiag(rt_all[:, sl]),
                               vv=bdiag(v_all[:, sl]), kh=kh_all[:, sl], bh=bh_all[:, sl], kc=kc_all[:, sl],
                               bc=bc_all[:, sl], p_end=p_end[:, sl]))
    for c in ch:
        gram = _mm_nt(jnp.concatenate([c["kt"], c["rr"]], axis=0),
                      jnp.concatenate([c["bh"]] * 2 + [c["kh"]] * 2, axis=0))
        st, ic = strict[c["d"]], incl[c["d"]]
        c["a_kb"] = jnp.where(st, gram[:n2, :n2], 0.0)
        c["a_kk"] = jnp.where(st, gram[:n2, n2:], 0.0)
        c["a_rb"] = jnp.where(ic, gram[n2:, :n2], 0.0)
        c["a_rk"] = jnp.where(ic, gram[n2:, n2:], 0.0)
        c["t"] = jnp.where(eye, 1.0, 0.0) - jnp.where((rt >> 1) == (ct >> 1), c["a_kb"], 0.0)
    for c in ch:
        c["akk_v"] = _mm(c["a_kk"], c["vv"])
        c["ark_v"] = _mm(c["a_rk"], c["vv"])
        c["kc_v"] = _mm(bdiag(c["kc"]).T, c["vv"])
    for ls in range(1, LOG_CHUNK):
        off = ((rt >> (ls + 1)) == (ct >> (ls + 1))) & ((rt >> ls) != (ct >> ls))
        for c in ch:
            c["nt"] = _mm(jnp.where(off, c["a_kb"], 0.0), c["t"])
        for c in ch:
            c["t"] = c["t"] - _mm(c["t"], c["nt"])
    for c in ch:
        c["uw"] = _mm(c["t"], jnp.concatenate([c["akk_v"], c["kt"]], axis=1))
    for c in ch:
        c["yq"] = jnp.concatenate([c["ark_v"], c["rr"]], axis=1) - _mm(c["a_rb"], c["uw"])
        diag_p = jnp.where(eye, jnp.broadcast_to(c["p_end"], (n2, n2)), 0.0)
        c["nm"] = jnp.concatenate([c["kc_v"], diag_p], axis=1) - _mm(bdiag(c["bc"]).T, c["uw"])
    for c in ch:
        qm_l = jnp.concatenate([c["yq"][:, n2:], c["nm"][:, n2:]], axis=0).astype(BF16)
        s_hi, s_lo = _split2(s_ref[c["slot"]])
        c["qm"] = jnp.dot(qm_l, s_hi, preferred_element_type=F32)
        c["m_lo"] = jnp.dot(qm_l[n2:], s_lo, preferred_element_type=F32)
    ys = []
    for c in ch:
        y_bd = c["qm"][:n2] + c["yq"][:, :n2]
        s_ref[c["slot"]] = (c["qm"][n2:] + c["m_lo"]) + c["nm"][:, :n2]
        ys.append(y_bd[:CHUNK] + y_bd[CHUNK:])
    for d, y_ref in enumerate((yf_ref, yr_ref)):
        for n in range(nb):
            o = (d * nb + n) * N_PAIRS
            y_ref[n] = jnp.concatenate(ys[o:o + N_PAIRS], axis=1)


def _wkv_call(r, v, kk, lw, kd, bb, n_ct):
    bsz, tt, c = r.shape
    nc = tt // CHUNK
    n_cc = n_ct * TM // CHUNK
    nb = WKV_BATCH
    assert bsz % nb == 0

    def rev_chunk(s):
        return jnp.where(s < n_cc, n_cc - 1 - s, nc - 1 - (s - n_cc))

    order = (lambda s: s, rev_chunk)
    specs, args = [], []
    for d in range(2):
        shared = pl.BlockSpec((nb, CHUNK, c), lambda b, s, d=d: (b, order[d](s), 0))
        per_dir = pl.BlockSpec((None, nb, CHUNK, c), lambda b, s, d=d: (d, b, order[d](s), 0))
        specs += [shared, shared, shared, per_dir, per_dir, per_dir]
        args += [r, v, kk, lw, kd, bb]
    out = jax.ShapeDtypeStruct((bsz, tt, c), F32)
    return pl.pallas_call(
        functools.partial(_wkv_kernel, nb=nb),
        grid=(bsz // nb, nc),
        in_specs=specs,
        out_specs=[specs[0], specs[6]],
        out_shape=[out, out],
        scratch_shapes=[pltpu.VMEM((2 * nb * N_PAIRS, PAIR, PAIR), F32)],
        compiler_params=_cparams(("parallel", "arbitrary")),
        name="wkv_scan",
    )(*args)


def _rows_reduce(x, op, reduce):
    n = x.shape[0]
    while n > SUBLANES and n % 2 == 0:
        n //= 2
        x = op(x[:n], x[n:])
    return reduce(x, axis=0, keepdims=True)


def _attn_kernel(q_ref, k_ref, vt_ref, o_ref, *, n_ctx_rows, n_ct, t0):
    i = pl.program_id(2) + t0
    n_keys = k_ref.shape[0]
    kc = ATTN_KEYS if (n_keys - n_ctx_rows) % ATTN_KEYS == 0 else TM

    def attend(chunks):
        heads = range(ATTN_HEADS)
        hs = [slice(g * HEAD_PAD, (g + 1) * HEAD_PAD) for g in heads]
        qs = [q_ref[:, hs[g]] for g in heads]
        scores = lambda g, c: _mm_nt(k_ref[c[0]:c[0] + c[1], hs[g]], qs[g])
        s_next = [scores(g, chunks[0]) for g in heads]
        m, l, acc = ([None] * ATTN_HEADS for _ in range(3))
        for n, c in enumerate(chunks):
            s = s_next
            if n + 1 < len(chunks):
                s_next = [scores(g, chunks[n + 1]) for g in heads]
            m_c = [_rows_reduce(s[g], jnp.maximum, jnp.max) for g in heads]
            m_new = [m_c[g] if m[g] is None else jnp.maximum(m[g], m_c[g]) for g in heads]
            p = [jnp.exp2(s[g] - m_new[g]) for g in heads]
            l_c = [_rows_reduce(p[g], jnp.add, jnp.sum) for g in heads]
            pv = [_mm(vt_ref[hs[g], c[0]:c[0] + c[1]], p[g]) for g in heads]
            for g in heads:
                if m[g] is None:
                    l[g], acc[g] = l_c[g], pv[g]
                else:
                    alpha = jnp.exp2(m[g] - m_new[g])
                    l[g], acc[g] = alpha * l[g] + l_c[g], alpha * acc[g] + pv[g]
                m[g] = m_new[g]
        for g in heads:
            o_ref[:, hs[g]] = (acc[g] / l[g]).T.astype(o_ref.dtype)

    ctx_chunk = [(0, n_ctx_rows)]
    lat_chunks = ctx_chunk + [(s, kc) for s in range(n_ctx_rows, n_keys, kc)]
    if t0 >= n_ct:
        attend(lat_chunks)
    else:
        @pl.when(i < n_ct)
        def _():
            attend(ctx_chunk)

        @pl.when(i >= n_ct)
        def _():
            attend(lat_chunks)


def _attn_call(q, k, vt, n_ct, t0):
    bsz, tt, _ = q.shape
    nt = tt // TM
    w = ATTN_HEADS * HEAD_PAD
    q_spec = pl.BlockSpec((None, TM, w), lambda b, h, i: (b, i + t0, h))
    k_spec = pl.BlockSpec((None, tt, w), lambda b, h, i: (b, 0, h))
    vt_spec = pl.BlockSpec((None, w, tt), lambda b, h, i: (b, h, 0))
    return pl.pallas_call(
        functools.partial(_attn_kernel, n_ctx_rows=n_ct * TM, n_ct=n_ct, t0=t0),
        grid=(bsz, N_HEADS // ATTN_HEADS, nt - t0),
        in_specs=[q_spec, k_spec, vt_spec],
        out_specs=q_spec,
        out_shape=jax.ShapeDtypeStruct((bsz, tt, D_HEADS), BF16),
        compiler_params=_cparams(("parallel", "parallel", "arbitrary")),
        name="attention",
    )(q, k, vt)


def _out_kernel(yf_ref, yr_ref, bv_ref, g_ref, o_ref, x_ref, mod_ref, gn_ref, ones_ref, wor_ref, wom_ref, ln_ref,
                out_ref):
    ones_bd = ones_ref[...]
    y = yf_ref[...] + yr_ref[...]
    inv_n = 1.0 / RWKV_HEAD
    yc = y - _seg_sum(y, ones_bd) * inv_n
    var = _seg_sum(yc * yc, ones_bd) * inv_n
    gn = yc * lax.rsqrt(var + GN_EPS) * gn_ref[0:1] + gn_ref[1:2]
    ro = (gn + bv_ref[...]) * g_ref[...]
    m = _mm(ro, wor_ref[...]) + jnp.dot(o_ref[...], wom_ref[...], preferred_element_type=F32)
    mod = mod_ref[...]
    out_ref[...] = _layer_norm(DEEPNORM_ALPHA * x_ref[...] + mod[2:3] * m, ln_ref[0:1], ln_ref[1:2])


def _out_call(yf, yr, bv, g, o, xs, mod, prm, n_ct, t0):
    bsz, tt, _ = xs.shape
    nt = tt // TM
    params = [prm[n] for n in ("gn", "ones_bd", "wo_r", "wo_m", "ln1")]
    return pl.pallas_call(
        _out_kernel,
        grid=(bsz, nt - t0),
        in_specs=[_row_spec(D_RWKV, t0)] * 4 + [_row_spec(D_HEADS, t0), _row_spec(D_MODEL, t0), _mod_spec(n_ct, t0)]
        + [_full_spec(a.shape) for a in params],
        out_specs=_row_spec(D_MODEL),
        out_shape=jax.ShapeDtypeStruct((bsz, tt - t0 * TM, D_MODEL), F32),
        compiler_params=_cparams(("parallel", "parallel")),
        name="mixer_out",
    )(yf, yr, bv, g, o, xs, mod, *params)


def _ffn_kernel(x_ref, xp_ref, xn_ref, mod_ref, wup_ref, conv_ref, wd_ref, ln_ref, out_ref, *, n_ct, nt):
    i = pl.program_id(1)
    mod = mod_ref[...]
    u = _with_halo(x_ref, xp_ref, xn_ref, 1.0 + mod[4:5], mod[3:4], i, n_ct, nt)

    def up(j):
        hs = []
        for o in (j * FF_SUB, D_FF + j * FF_SUB):
            h = jnp.dot(u, wup_ref[:, o:o + FF_SUB], preferred_element_type=F32)
            taps = conv_ref[:, o:o + FF_SUB]
            hs.append(_conv3(h, taps) + taps[3:4])
        return hs

    acc = None
    acts = []
    h_next = up(0)
    for j in range(N_FF_SUB):
        hg, hv = h_next
        if j + 1 < N_FF_SUB:
            h_next = up(j + 1)
        half = 0.5 * hg
        acts.append(((half + half * jnp.tanh(half)) * hv).astype(BF16))
        if len(acts) == FF_DOWN or j + 1 == N_FF_SUB:
            lo = (j + 1 - len(acts)) * FF_SUB
            part = jnp.dot(jnp.concatenate(acts, axis=1), wd_ref[lo:(j + 1) * FF_SUB, :], preferred_element_type=F32)
            acc = part if acc is None else acc + part
            acts = []
    out_ref[...] = _layer_norm(DEEPNORM_ALPHA * x_ref[...] + mod[5:6] * acc, ln_ref[0:1], ln_ref[1:2])


def _ffn_call(xs, mod, prm, n_ct):
    bsz, tt, _ = xs.shape
    nt = tt // TM
    prev, nxt = _halo_specs(D_MODEL, tt)
    w_up, conv, w_down, ln2 = prm["w_up"], prm["ffn_conv"], prm["w_down"], prm["ln2"]
    return pl.pallas_call(
        functools.partial(_ffn_kernel, n_ct=n_ct, nt=nt),
        grid=(bsz, nt),
        in_specs=[_row_spec(D_MODEL), prev, nxt, _mod_spec(n_ct), _full_spec(w_up.shape, True),
                  _full_spec(conv.shape), _full_spec(w_down.shape, True), _full_spec(ln2.shape)],
        out_specs=_row_spec(D_MODEL),
        out_shape=jax.ShapeDtypeStruct((bsz, tt, D_MODEL), F32),
        compiler_params=_cparams(("parallel", "parallel")),
        name="conv_ffn",
    )(xs, xs, xs, mod, w_up, conv, w_down, ln2)


def _pad_rows(rows, width):
    rows = [r.astype(F32) if r.shape[0] == width else jnp.pad(r.astype(F32), (0, width - r.shape[0])) for r in rows]
    return jnp.pad(jnp.stack(rows), ((0, SUBLANES - len(rows)), (0, 0)))


def _pair_swap(w):
    w2 = w.reshape(*w.shape[:-1], w.shape[-1] // 2, 2)
    return jnp.stack([-w2[..., 1], w2[..., 0]], axis=-1).reshape(w.shape)


def _layer_params(l, w_in, rwkv_conv, w0, w_b, a0, a_b, g_b, k_k, k_a, r_k, gn_g, gn_b, q_norm_g, w_uq,
                  kv_norm_g, w_ukv, w_o, ln1_g, ln1_b, w_up, ffn_conv_w, ffn_conv_b, w_down, ln2_g, ln2_b):
    c = D_RWKV
    wi = w_in[l]
    o_mla = RWKV_COLS
    w_kr = wi[:, o_mla + Q_LORA + KV_LORA:]
    pad_l = jnp.zeros((D_MODEL, QK_NOPE), F32)
    pad_r = jnp.zeros((D_MODEL, LANES - QK_DIM), F32)
    w_in_p = jnp.concatenate(
        [wi[:, :o_mla + Q_LORA + KV_LORA], pad_l, w_kr, pad_r, pad_l, _pair_swap(w_kr), pad_r], axis=1).astype(BF16)

    def lora_bd(w):
        z = jnp.zeros_like(w[0])
        return jnp.concatenate([jnp.concatenate([w[0], z], 1), jnp.concatenate([z, w[1]], 1)], 0).astype(BF16)

    head = jnp.arange(2 * LANES) // RWKV_HEAD
    ones_bd = (head[:, None] == head[None, :]).astype(BF16)

    wq = w_uq[l].reshape(Q_LORA, N_HEADS, QK_DIM)
    q_nope, q_rope = wq[..., :QK_NOPE], wq[..., QK_NOPE:]
    zq = lambda n: jnp.zeros((Q_LORA, N_HEADS, n), F32)
    wq1 = jnp.concatenate([q_nope, q_rope, zq(LANES - QK_DIM)], -1).reshape(Q_LORA, D_HEADS).astype(BF16)
    wq2 = jnp.concatenate([zq(QK_NOPE), _pair_swap(q_rope), zq(LANES - QK_DIM)], -1).reshape(Q_LORA, D_HEADS).astype(BF16)
    wkv = w_ukv[l].reshape(KV_LORA, N_HEADS, QK_NOPE + V_HEAD)
    zk = jnp.zeros((KV_LORA, N_HEADS, HEAD_PAD - QK_NOPE), F32)
    wkn = jnp.concatenate([wkv[..., :QK_NOPE], zk], -1).reshape(KV_LORA, D_HEADS).astype(BF16)
    wv = jnp.concatenate([wkv[..., QK_NOPE:], zk], -1).reshape(KV_LORA, D_HEADS).astype(BF16)
    wo = w_o[l]
    wo_m = wo[c:].reshape(N_HEADS, V_HEAD, D_MODEL)
    wo_m = jnp.concatenate([wo_m, jnp.zeros((N_HEADS, HEAD_PAD - V_HEAD, D_MODEL), F32)], 1)

    return {
        "w_in": w_in_p,
        "conv": _pad_rows(list(rwkv_conv[l]), RWKV_COLS),
        "w0": _pad_rows([w0[l].reshape(-1)], 2 * c),
        "wb": lora_bd(w_b[l]),
        "a0": _pad_rows([a0[l].reshape(-1)], 2 * c),
        "ab": lora_bd(a_b[l]),
        "gb": g_b[l].astype(BF16),
        "vec": _pad_rows([k_k[l], k_a[l], r_k[l].reshape(-1)], c),
        "ones_bd": ones_bd,
        "norm_g": _pad_rows([q_norm_g[l], kv_norm_g[l]], Q_LORA),
        "wq1": wq1, "wq2": wq2, "wkn": wkn, "wv": wv,
        "gn": _pad_rows([gn_g[l], gn_b[l]], c),
        "wo_r": wo[:c].astype(BF16),
        "wo_m": wo_m.reshape(D_HEADS, D_MODEL).astype(BF16),
        "ln1": _pad_rows([ln1_g[l], ln1_b[l]], D_MODEL),
        "w_up": w_up[l].astype(BF16),
        "ffn_conv": _pad_rows(list(ffn_conv_w[l]) + [ffn_conv_b[l]], 2 * D_FF),
        "w_down": w_down[l].astype(BF16),
        "ln2": _pad_rows([ln2_g[l], ln2_b[l]], D_MODEL),
    }


def _rope_tables(n_ctx, n_lat):
    rows = n_lat // GRID_W
    row = jnp.repeat(jnp.arange(rows, dtype=F32), GRID_W)
    col = jnp.tile(jnp.arange(GRID_W, dtype=F32), rows)
    n_pairs = QK_ROPE // 4
    inv = ROPE_BASE ** (-jnp.arange(n_pairs, dtype=F32) / n_pairs)
    ang = jnp.concatenate([row[:, None] * inv, col[:, None] * inv], axis=-1)
    cos = jnp.repeat(jnp.cos(ang), 2, axis=-1)
    sin = jnp.repeat(jnp.sin(ang), 2, axis=-1)
    tt = n_ctx + n_lat
    cos_t = jnp.ones((tt, LANES), F32).at[n_ctx:, QK_NOPE:QK_DIM].set(cos)
    sin_t = jnp.zeros((tt, LANES), F32).at[n_ctx:, QK_NOPE:QK_DIM].set(sin)
    return cos_t, sin_t


def kernel(x, c, ctx, c_ctx, w_ada, b_ada, w_in, rwkv_conv, w0, w_b, a0, a_b, g_b, k_k, k_a, r_k, gn_g, gn_b,
           q_norm_g, w_uq, kv_norm_g, w_ukv, w_o, ln1_g, ln1_b, w_up, ffn_conv_w, ffn_conv_b, w_down, ln2_g, ln2_b):
    bsz, n_lat, _ = x.shape
    n_ctx = ctx.shape[1]
    assert n_ctx % TM == 0 and n_lat % TM == 0 and n_lat % GRID_W == 0
    n_ct = n_ctx // TM
    n_l = w_ada.shape[0]

    ada_rows = -(-(bsz + 1) // SUBLANES) * SUBLANES
    cc = jnp.zeros((ada_rows, D_MODEL), F32).at[:bsz].set(c).at[bsz].set(c_ctx)
    mods = _ada_call(cc, w_ada, b_ada).reshape(n_l, ada_rows, 6, D_MODEL)
    cos_t, sin_t = _rope_tables(n_ctx, n_lat)
    xs = jnp.concatenate([ctx, x], axis=1)

    for l in range(n_l):
        prm = _layer_params(l, w_in, rwkv_conv, w0, w_b, a0, a_b, g_b, k_k, k_a, r_k, gn_g, gn_b, q_norm_g, w_uq,
                            kv_norm_g, w_ukv, w_o, ln1_g, ln1_b, w_up, ffn_conv_w, ffn_conv_b, w_down, ln2_g, ln2_b)
        m_ctx = jnp.broadcast_to(mods[l, bsz][None], (bsz, 6, D_MODEL))
        mod = jnp.stack([m_ctx, mods[l, :bsz]], axis=1)
        mod = jnp.pad(mod, ((0, 0), (0, 0), (0, SUBLANES - 6), (0, 0))).reshape(2 * bsz, SUBLANES, D_MODEL)

        last = l == n_l - 1
        t0 = n_ct if last else 0
        r, v, kk, g, bv, lw, kd, bb, q, k, vm = _front_call(xs, mod, cos_t, sin_t, prm, n_ct)
        yf, yr = _wkv_call(r, v, kk, lw, kd, bb, n_ct)
        o = _attn_call(q, k, vm, n_ct, t0)
        x1 = _out_call(yf, yr, bv, g, o, xs, mod, prm, n_ct, t0)
        xs = _ffn_call(x1, mod, prm, n_ct - t0)
    return xs
```

```python
import functools

import jax
import jax.numpy as jnp
from jax import lax
from jax.experimental import pallas as pl
from jax.experimental.pallas import tpu as pltpu

F32 = jnp.float32
BF16 = jnp.bfloat16

D_MODEL = 1024
DEPTH = 2
GRID_W = 64
D_RWKV = 512
RWKV_HEAD = 64
N_HEADS = 8
LORA_W = 64
LORA_A = 64
LORA_G = 128
RWKV_COLS = 3 * D_RWKV + 2 * LORA_W + 2 * LORA_A + LORA_G
QK_NOPE = 64
QK_ROPE = 32
QK_DIM = QK_NOPE + QK_ROPE
V_HEAD = 64
Q_LORA = 384
KV_LORA = 256
D_FF = 2816
ROPE_BASE = 10000.0
DEEPNORM_ALPHA = (2 * DEPTH) ** 0.25
LN_EPS = 1e-5
RMS_EPS = 1e-6
GN_EPS = 64e-5
LOG2_E = 1.4426950408889634

LANES = 128
SUBLANES = 8
HEAD_PAD = LANES
D_HEADS = N_HEADS * HEAD_PAD
TM = 256
HALO = SUBLANES
CHUNK = 64
LOG_CHUNK = 6
WKV_BATCH = 2
PAIR = 2 * RWKV_HEAD
N_PAIRS = N_HEADS // 2
ATTN_KEYS = 1024
ATTN_HEADS = 4
FF_SUB = 256
N_FF_SUB = D_FF // FF_SUB
FF_DOWN = 2
VMEM_LIMIT = 56 << 20


def _cparams(sem):
    return pltpu.CompilerParams(dimension_semantics=sem, vmem_limit_bytes=VMEM_LIMIT)


def _mm(a, b):
    return jnp.dot(a.astype(BF16), b.astype(BF16), preferred_element_type=F32)


def _mm_nt(a, b):
    return lax.dot_general(a.astype(BF16), b.astype(BF16), (((1,), (1,)), ((), ())),
                           preferred_element_type=F32)


def _split2(x):
    hi = x.astype(BF16)
    lo = (x - hi.astype(F32)).astype(BF16)
    return hi, lo


def _mm3(a, b):
    ah, al = _split2(a)
    bh, bl = _split2(b)
    d = functools.partial(jnp.dot, preferred_element_type=F32)
    return d(ah, bh) + (d(ah, bl) + d(al, bh))


def _mm_exact_rhs(a, b_bf16):
    a1, a2 = _split2(a)
    d = functools.partial(jnp.dot, preferred_element_type=F32)
    return d(a1, b_bf16) + d(a2, b_bf16)


def _mm_exact_lhs(a_bf16, b):
    b1 = b.astype(BF16)
    r1 = b - b1.astype(F32)
    b2 = r1.astype(BF16)
    b3 = (r1 - b2.astype(F32)).astype(BF16)
    d = functools.partial(jnp.dot, preferred_element_type=F32)
    return d(a_bf16, b1) + (d(a_bf16, b2) + d(a_bf16, b3))


def _sigmoid(x):
    return 1.0 / (1.0 + jnp.exp(-x))


def _layer_norm(x, g, b):
    mu = jnp.mean(x, axis=-1, keepdims=True)
    xc = x - mu
    var = jnp.mean(xc * xc, axis=-1, keepdims=True)
    return xc * lax.rsqrt(var + LN_EPS) * g + b


def _rms_norm(x, g):
    return x * lax.rsqrt(jnp.mean(x * x, axis=-1, keepdims=True) + RMS_EPS) * g


def _row_spec(cols, t0=0):
    return pl.BlockSpec((None, TM, cols), lambda b, i, *_: (b, i + t0, 0))


def _mod_spec(n_ct, t0=0):
    return pl.BlockSpec((None, SUBLANES, D_MODEL),
                        lambda b, i, *_: (2 * b + (i + t0 >= n_ct).astype(jnp.int32), 0, 0))


def _full_spec(shape, single_buffer=False):
    nd = len(shape)
    mode = dict(pipeline_mode=pl.Buffered(1)) if single_buffer else {}
    return pl.BlockSpec(shape, lambda *_: (0,) * nd, **mode)


def _halo_specs(cols, tt, t0=0):
    per = TM // HALO
    last = tt // HALO - 1
    prev = pl.BlockSpec((None, HALO, cols), lambda b, i, *_: (b, jnp.maximum((i + t0) * per - 1, 0), 0))
    nxt = pl.BlockSpec((None, HALO, cols), lambda b, i, *_: (b, jnp.minimum((i + t0 + 1) * per, last), 0))
    return prev, nxt


def _with_halo(x_ref, xp_ref, xn_ref, scale, shift, i, n_ct, nt):
    has_prev = (i != 0) & (i != n_ct)
    has_next = (i != n_ct - 1) & (i != nt - 1)
    mod = lambda x: x * scale + shift
    return jnp.concatenate([jnp.where(has_prev, mod(xp_ref[...]), 0.0), mod(x_ref[...]),
                            jnp.where(has_next, mod(xn_ref[...]), 0.0)], axis=0).astype(BF16)


def _conv3(h, taps):
    rows = TM + 2 * HALO
    lo, hi = HALO, HALO + TM
    return (pltpu.roll(h, 1, 0)[lo:hi] * taps[0:1] + h[lo:hi] * taps[1:2]
            + pltpu.roll(h, rows - 1, 0)[lo:hi] * taps[2:3])


ADA_COLS = 1536


def _ada_kernel(c_ref, w_ref, b_ref, o_ref):
    c = c_ref[...]
    o_ref[...] = _mm3(c * _sigmoid(c), w_ref[...]) + b_ref[...]


def _ada_call(cc, w_ada, b_ada):
    n_l = w_ada.shape[0]
    rows = cc.shape[0]
    return pl.pallas_call(
        _ada_kernel,
        grid=(n_l, 6 * D_MODEL // ADA_COLS),
        in_specs=[
            pl.BlockSpec((rows, D_MODEL), lambda l, j: (0, 0)),
            pl.BlockSpec((None, D_MODEL, ADA_COLS), lambda l, j: (l, 0, j)),
            pl.BlockSpec((None, 1, ADA_COLS), lambda l, j: (l, 0, j)),
        ],
        out_specs=pl.BlockSpec((None, rows, ADA_COLS), lambda l, j: (l, 0, j)),
        out_shape=jax.ShapeDtypeStruct((n_l, rows, 6 * D_MODEL), F32),
        compiler_params=_cparams(("arbitrary", "arbitrary")),
        name="ada",
    )(cc, w_ada, b_ada.reshape(n_l, 1, 6 * D_MODEL))


def _seg_sum(x, ones_bd):
    w = ones_bd.shape[0]
    return jnp.concatenate([_mm_exact_rhs(x[:, o:o + w], ones_bd) for o in range(0, x.shape[1], w)], axis=1)


def _front_kernel(x_ref, xp_ref, xn_ref, mod_ref, w_ref, conv_ref, w0_ref, wb_ref, a0_ref, ab_ref, gb_ref, vec_ref,
                  ones_ref, cos_ref, sin_ref, ng_ref, wq_ref, wkn_ref, wv_ref,
                  r_ref, v_ref, kk_ref, g_ref, bv_ref, lw_ref, kd_ref, bb_ref, q_ref, k_ref, vm_ref, *, n_ct, nt):
    i = pl.program_id(1)
    m = mod_ref[...]
    u = _with_halo(x_ref, xp_ref, xn_ref, 1.0 + m[1:2], m[0:1], i, n_ct, nt)
    d = functools.partial(jnp.dot, preferred_element_type=F32)
    z = _conv3(d(u, w_ref[:, :RWKV_COLS]), conv_ref[...])
    pm = d(u, w_ref[:, RWKV_COLS:])[HALO:HALO + TM]

    c = D_RWKV
    r, k, v = z[:, 0:c], z[:, c:2 * c], z[:, 2 * c:3 * c]
    o = 3 * c
    w_lo = z[:, o:o + 2 * LORA_W]
    a_lo = z[:, o + 2 * LORA_W:o + 2 * LORA_W + 2 * LORA_A]
    g_lo = z[:, o + 2 * LORA_W + 2 * LORA_A:]
    vec = vec_ref[...]
    k_k, k_a, r_k = vec[0:1], vec[1:2], vec[2:3]
    ones_bd = ones_ref[...]

    t = -(w0_ref[0:1, :] + _mm(jnp.tanh(w_lo), wb_ref[...]))
    sp = jnp.maximum(t, 0.0) + jnp.log(1.0 + jnp.exp(-jnp.abs(t)))
    lw = -jnp.exp(-sp - 0.5)
    a = _sigmoid(a0_ref[0:1, :] + _mm(a_lo, ab_ref[...]))
    g = _mm(_sigmoid(g_lo), gb_ref[...])

    kk = k * k_k
    nrm = jnp.sqrt(_seg_sum(kk * kk, ones_bd))
    kk = kk / jnp.maximum(nrm, 1e-12)
    kd0 = k * (1.0 + (a[:, :c] - 1.0) * k_a)
    kd1 = k * (1.0 + (a[:, c:] - 1.0) * k_a)
    bonus = _seg_sum(r * (kd0 + kd1) * r_k, ones_bd) * v

    r_ref[...] = r.astype(BF16)
    v_ref[...] = v.astype(BF16)
    kk_ref[...] = kk.astype(BF16)
    g_ref[...] = g.astype(BF16)
    bv_ref[...] = bonus.astype(BF16)
    lw_ref[0] = lw[:, :c]
    lw_ref[1] = lw[:, c:]
    kd_ref[0] = kd0.astype(BF16)
    kd_ref[1] = kd1.astype(BF16)
    bb_ref[0] = (kk * a[:, :c]).astype(BF16)
    bb_ref[1] = (kk * a[:, c:]).astype(BF16)

    def rotary(x, cos, sin_signed):
        n = x.shape[1]
        even = (lax.broadcasted_iota(jnp.int32, x.shape, 1) & 1) == 0
        partner = jnp.where(even, pltpu.roll(x, n - 1, 1), pltpu.roll(x, 1, 1))
        return x * cos + partner * sin_signed

    cq, ckv, kr_x = pm[:, :Q_LORA], pm[:, Q_LORA:Q_LORA + KV_LORA], pm[:, Q_LORA + KV_LORA:]
    cos, sin = cos_ref[...], sin_ref[...]
    cos8 = jnp.concatenate([cos] * N_HEADS, axis=1)
    sin8 = jnp.concatenate([sin] * N_HEADS, axis=1)
    cqn = _rms_norm(cq, ng_ref[0:1, :Q_LORA]).astype(BF16)
    q = rotary(d(cqn, wq_ref[...]), cos8, sin8)
    q_ref[...] = (q * (QK_DIM ** -0.5 * LOG2_E)).astype(BF16)
    ckvn = _rms_norm(ckv, ng_ref[1:2, :KV_LORA]).astype(BF16)
    kr = rotary(kr_x, cos, sin)
    k_ref[...] = (d(ckvn, wkn_ref[...]) + jnp.concatenate([kr] * N_HEADS, axis=1)).astype(BF16)
    vm_ref[...] = d(ckvn, wv_ref[...]).T.astype(BF16)


def _front_call(xs, mod, cos_t, sin_t, prm, n_ct):
    bsz, tt, _ = xs.shape
    nt = tt // TM
    prev, nxt = _halo_specs(D_MODEL, tt)
    tab = pl.BlockSpec((TM, LANES), lambda b, i: (i, 0))
    shared = jax.ShapeDtypeStruct((bsz, tt, D_RWKV), BF16)
    dir_spec = pl.BlockSpec((2, None, TM, D_RWKV), lambda b, i: (0, b, i, 0))
    dir_shape = lambda dt: jax.ShapeDtypeStruct((2, bsz, tt, D_RWKV), dt)
    heads = jax.ShapeDtypeStruct((bsz, tt, D_HEADS), BF16)
    p_rwkv = [prm[n] for n in ("w_in", "conv", "w0", "wb", "a0", "ab", "gb", "vec", "ones_bd")]
    p_mla = [prm[n] for n in ("norm_g", "wq", "wkn", "wv")]
    return pl.pallas_call(
        functools.partial(_front_kernel, n_ct=n_ct, nt=nt),
        grid=(bsz, nt),
        in_specs=[_row_spec(D_MODEL), prev, nxt, _mod_spec(n_ct)] + [_full_spec(a.shape) for a in p_rwkv]
        + [tab, tab] + [_full_spec(a.shape) for a in p_mla],
        out_specs=[_row_spec(D_RWKV)] * 5 + [dir_spec] * 3 + [_row_spec(D_HEADS)] * 2
        + [pl.BlockSpec((None, N_HEADS * V_HEAD, TM), lambda b, i: (b, 0, i))],
        out_shape=[shared] * 5 + [dir_shape(F32), dir_shape(BF16), dir_shape(BF16)] + [heads] * 2
        + [jax.ShapeDtypeStruct((bsz, N_HEADS * V_HEAD, tt), BF16)],
        compiler_params=_cparams(("parallel", "parallel")),
        name="front",
    )(xs, xs, xs, mod, *p_rwkv, cos_t, sin_t, *p_mla)


def _wkv_kernel(*refs, nb):
    ins, (yf_ref, yr_ref, s_ref) = refs[:12], refs[12:]
    step = pl.program_id(1)

    @pl.when(step == 0)
    def _():
        s_ref[...] = jnp.zeros_like(s_ref)

    n2 = 2 * CHUNK
    ri = lax.broadcasted_iota(jnp.int32, (n2, n2), 0)
    ci = lax.broadcasted_iota(jnp.int32, (n2, n2), 1)
    same_head = (ri >> LOG_CHUNK) == (ci >> LOG_CHUNK)
    rt, ct = ri & (CHUNK - 1), ci & (CHUNK - 1)
    eye = ri == ci
    ti = lax.broadcasted_iota(jnp.int32, (CHUNK, CHUNK), 0)
    tj = lax.broadcasted_iota(jnp.int32, (CHUNK, CHUNK), 1)
    strict = [same_head & (ct < rt), same_head & (ct > rt)]
    incl = [same_head & (ct <= rt), same_head & (ct >= rt)]
    tri = [jnp.where(tj <= ti, 1.0, 0.0).astype(BF16), jnp.where(tj >= ti, 1.0, 0.0).astype(BF16)]

    def bdiag(x):
        return jnp.where(same_head, jnp.concatenate([x, x], axis=0), 0.0)

    ch = []
    for d in range(2):
        r_ref, v_ref, kk_ref, lw_ref, kd_ref, bb_ref = ins[6 * d:6 * d + 6]
        for n in range(nb):
            lw = lw_ref[n]
            lc = _mm_exact_lhs(tri[d], lw)
            lend = jnp.sum(lw, axis=0, keepdims=True)
            e_neg = jnp.exp(-lc)
            e_end = jnp.exp(lend - lc)
            p_end = jnp.exp(lend)
            kt_all = kk_ref[n].astype(F32) * jnp.exp(lc - lw)
            rt_all = r_ref[n].astype(F32) * jnp.exp(lc)
            kd, bb = kd_ref[n].astype(F32), bb_ref[n].astype(F32)
            kh_all, bh_all = kd * e_neg, bb * e_neg
            kc_all, bc_all = kd * e_end, bb * e_end
            v_all = v_ref[n]
            for p in range(N_PAIRS):
                sl = slice(p * PAIR, (p + 1) * PAIR)
                ch.append(dict(d=d, slot=(d * nb + n) * N_PAIRS + p, kt=bdiag(kt_all[:, sl]), rr=bdiag(rt_all[:, sl]),
                               vv=bdiag(v_all[:, sl]), kh=kh_all[:, sl], bh=bh_all[:, sl], kc=kc_all[:, sl],
                               bc=bc_all[:, sl], p_end=p_end[:, sl]))
    for c in ch:
        gram = _mm_nt(jnp.concatenate([c["kt"], c["rr"]], axis=0),
                      jnp.concatenate([c["bh"]] * 2 + [c["kh"]] * 2, axis=0))
        st, ic = strict[c["d"]], incl[c["d"]]
        c["a_kb"] = jnp.where(st, gram[:n2, :n2], 0.0)
        c["a_kk"] = jnp.where(st, gram[:n2, n2:], 0.0)
        c["a_rb"] = jnp.where(ic, gram[n2:, :n2], 0.0)
        c["a_rk"] = jnp.where(ic, gram[n2:, n2:], 0.0)
        c["t"] = jnp.where(eye, 1.0, 0.0) - jnp.where((rt >> 1) == (ct >> 1), c["a_kb"], 0.0)
    for c in ch:
        c["akk_v"] = _mm(c["a_kk"], c["vv"])
        c["ark_v"] = _mm(c["a_rk"], c["vv"])
        c["kc_v"] = _mm(bdiag(c["kc"]).T, c["vv"])
    for ls in range(1, LOG_CHUNK):
        off = ((rt >> (ls + 1)) == (ct >> (ls + 1))) & ((rt >> ls) != (ct >> ls))
        for c in ch:
            c["nt"] = _mm(jnp.where(off, c["a_kb"], 0.0), c["t"])
        for c in ch:
            c["t"] = c["t"] - _mm(c["t"], c["nt"])
    for c in ch:
        c["uw"] = _mm(c["t"], jnp.concatenate([c["akk_v"], c["kt"]], axis=1))
    for c in ch:
        c["yq"] = jnp.concatenate([c["ark_v"], c["rr"]], axis=1) - _mm(c["a_rb"], c["uw"])
        diag_p = jnp.where(eye, jnp.broadcast_to(c["p_end"], (n2, n2)), 0.0)
        c["nm"] = jnp.concatenate([c["kc_v"], diag_p], axis=1) - _mm(bdiag(c["bc"]).T, c["uw"])
    for c in ch:
        qm_l = jnp.concatenate([c["yq"][:, n2:], c["nm"][:, n2:]], axis=0).astype(BF16)
        s_hi, s_lo = _split2(s_ref[c["slot"]])
        c["qm"] = jnp.dot(qm_l, s_hi, preferred_element_type=F32)
        c["m_lo"] = jnp.dot(qm_l[n2:], s_lo, preferred_element_type=F32)
    ys = []
    for c in ch:
        y_bd = c["qm"][:n2] + c["yq"][:, :n2]
        s_ref[c["slot"]] = (c["qm"][n2:] + c["m_lo"]) + c["nm"][:, :n2]
        ys.append(y_bd[:CHUNK] + y_bd[CHUNK:])
    for d, y_ref in enumerate((yf_ref, yr_ref)):
        for n in range(nb):
            o = (d * nb + n) * N_PAIRS
            y_ref[n] = jnp.concatenate(ys[o:o + N_PAIRS], axis=1)


def _wkv_call(r, v, kk, lw, kd, bb, n_ct):
    bsz, tt, c = r.shape
    nc = tt // CHUNK
    n_cc = n_ct * TM // CHUNK
    nb = WKV_BATCH
    assert bsz % nb == 0

    def rev_chunk(s):
        return jnp.where(s < n_cc, n_cc - 1 - s, nc - 1 - (s - n_cc))

    order = (lambda s: s, rev_chunk)
    specs, args = [], []
    for d in range(2):
        shared = pl.BlockSpec((nb, CHUNK, c), lambda b, s, d=d: (b, order[d](s), 0))
        per_dir = pl.BlockSpec((None, nb, CHUNK, c), lambda b, s, d=d: (d, b, order[d](s), 0))
        specs += [shared, shared, shared, per_dir, per_dir, per_dir]
        args += [r, v, kk, lw, kd, bb]
    out = jax.ShapeDtypeStruct((bsz, tt, c), F32)
    return pl.pallas_call(
        functools.partial(_wkv_kernel, nb=nb),
        grid=(bsz // nb, nc),
        in_specs=specs,
        out_specs=[specs[0], specs[6]],
        out_shape=[out, out],
        scratch_shapes=[pltpu.VMEM((2 * nb * N_PAIRS, PAIR, PAIR), F32)],
        compiler_params=_cparams(("parallel", "arbitrary")),
        name="wkv_scan",
    )(*args)


def _rows_reduce(x, op, reduce):
    n = x.shape[0]
    while n > SUBLANES and n % 2 == 0:
        n //= 2
        x = op(x[:n], x[n:])
    return reduce(x, axis=0, keepdims=True)


def _attn_kernel(q_ref, k_ref, vt_ref, o_ref, *, n_ctx_rows, n_ct, t0):
    i = pl.program_id(2) + t0
    n_keys = k_ref.shape[0]
    kc = ATTN_KEYS if (n_keys - n_ctx_rows) % ATTN_KEYS == 0 else TM

    def attend(chunks):
        heads = range(ATTN_HEADS)
        hs = [slice(g * HEAD_PAD, (g + 1) * HEAD_PAD) for g in heads]
        qs = [q_ref[:, hs[g]] for g in heads]
        scores = lambda g, c: _mm_nt(k_ref[c[0]:c[0] + c[1], hs[g]], qs[g])
        s_next = [scores(g, chunks[0]) for g in heads]
        m, l, acc = ([None] * ATTN_HEADS for _ in range(3))
        for n, c in enumerate(chunks):
            s = s_next
            if n + 1 < len(chunks):
                s_next = [scores(g, chunks[n + 1]) for g in heads]
            m_c = [_rows_reduce(s[g], jnp.maximum, jnp.max) for g in heads]
            m_new = [m_c[g] if m[g] is None else jnp.maximum(m[g], m_c[g]) for g in heads]
            p = [jnp.exp2(s[g] - m_new[g]) for g in heads]
            l_c = [_rows_reduce(p[g], jnp.add, jnp.sum) for g in heads]
            pv = [_mm(vt_ref[g * V_HEAD:(g + 1) * V_HEAD, c[0]:c[0] + c[1]], p[g]) for g in heads]
            for g in heads:
                if m[g] is None:
                    l[g], acc[g] = l_c[g], pv[g]
                else:
                    alpha = jnp.exp2(m[g] - m_new[g])
                    l[g], acc[g] = alpha * l[g] + l_c[g], alpha * acc[g] + pv[g]
                m[g] = m_new[g]
        o_ref[...] = jnp.concatenate([acc[g] / l[g] for g in heads], axis=0).T.astype(o_ref.dtype)

    ctx_chunk = [(0, n_ctx_rows)]
    lat_chunks = ctx_chunk + [(s, kc) for s in range(n_ctx_rows, n_keys, kc)]
    if t0 >= n_ct:
        attend(lat_chunks)
    else:
        @pl.when(i < n_ct)
        def _():
            attend(ctx_chunk)

        @pl.when(i >= n_ct)
        def _():
            attend(lat_chunks)


def _attn_call(q, k, vt, n_ct, t0):
    bsz, tt, _ = q.shape
    nt = tt // TM
    w = ATTN_HEADS * HEAD_PAD
    q_spec = pl.BlockSpec((None, TM, w), lambda b, h, i: (b, i + t0, h))
    k_spec = pl.BlockSpec((None, tt, w), lambda b, h, i: (b, 0, h))
    vt_spec = pl.BlockSpec((None, ATTN_HEADS * V_HEAD, tt), lambda b, h, i: (b, h, 0))
    o_spec = pl.BlockSpec((None, TM, ATTN_HEADS * V_HEAD), lambda b, h, i: (b, i + t0, h))
    return pl.pallas_call(
        functools.partial(_attn_kernel, n_ctx_rows=n_ct * TM, n_ct=n_ct, t0=t0),
        grid=(bsz, N_HEADS // ATTN_HEADS, nt - t0),
        in_specs=[q_spec, k_spec, vt_spec],
        out_specs=o_spec,
        out_shape=jax.ShapeDtypeStruct((bsz, tt, N_HEADS * V_HEAD), BF16),
        compiler_params=_cparams(("parallel", "parallel", "arbitrary")),
        name="attention",
    )(q, k, vt)


def _out_kernel(yf_ref, yr_ref, bv_ref, g_ref, o_ref, x_ref, mod_ref, gn_ref, ones_ref, wor_ref, wom_ref, ln_ref,
                out_ref):
    ones_bd = ones_ref[...]
    y = yf_ref[...] + yr_ref[...]
    inv_n = 1.0 / RWKV_HEAD
    yc = y - _seg_sum(y, ones_bd) * inv_n
    var = _seg_sum(yc * yc, ones_bd) * inv_n
    gn = yc * lax.rsqrt(var + GN_EPS) * gn_ref[0:1] + gn_ref[1:2]
    ro = (gn + bv_ref[...]) * g_ref[...]
    m = _mm(ro, wor_ref[...]) + jnp.dot(o_ref[...], wom_ref[...], preferred_element_type=F32)
    mod = mod_ref[...]
    out_ref[...] = _layer_norm(DEEPNORM_ALPHA * x_ref[...] + mod[2:3] * m, ln_ref[0:1], ln_ref[1:2])


def _out_call(yf, yr, bv, g, o, xs, mod, prm, n_ct, t0):
    bsz, tt, _ = xs.shape
    nt = tt // TM
    params = [prm[n] for n in ("gn", "ones_bd", "wo_r", "wo_m", "ln1")]
    return pl.pallas_call(
        _out_kernel,
        grid=(bsz, nt - t0),
        in_specs=[_row_spec(D_RWKV, t0)] * 4 + [_row_spec(N_HEADS * V_HEAD, t0), _row_spec(D_MODEL, t0),
                                                _mod_spec(n_ct, t0)]
        + [_full_spec(a.shape) for a in params],
        out_specs=_row_spec(D_MODEL),
        out_shape=jax.ShapeDtypeStruct((bsz, tt - t0 * TM, D_MODEL), F32),
        compiler_params=_cparams(("parallel", "parallel")),
        name="mixer_out",
    )(yf, yr, bv, g, o, xs, mod, *params)


def _ffn_kernel(x_ref, xp_ref, xn_ref, mod_ref, wup_ref, conv_ref, wd_ref, ln_ref, out_ref, *, n_ct, nt):
    i = pl.program_id(1)
    mod = mod_ref[...]
    u = _with_halo(x_ref, xp_ref, xn_ref, 1.0 + mod[4:5], mod[3:4], i, n_ct, nt)

    def up(j):
        hs = []
        for o in (j * FF_SUB, D_FF + j * FF_SUB):
            h = jnp.dot(u, wup_ref[:, o:o + FF_SUB], preferred_element_type=F32)
            taps = conv_ref[:, o:o + FF_SUB]
            hs.append(_conv3(h, taps) + taps[3:4])
        return hs

    acc = None
    acts = []
    h_next = up(0)
    for j in range(N_FF_SUB):
        hg, hv = h_next
        if j + 1 < N_FF_SUB:
            h_next = up(j + 1)
        half = 0.5 * hg
        acts.append(((half + half * jnp.tanh(half)) * hv).astype(BF16))
        if len(acts) == FF_DOWN or j + 1 == N_FF_SUB:
            lo = (j + 1 - len(acts)) * FF_SUB
            part = jnp.dot(jnp.concatenate(acts, axis=1), wd_ref[lo:(j + 1) * FF_SUB, :], preferred_element_type=F32)
            acc = part if acc is None else acc + part
            acts = []
    out_ref[...] = _layer_norm(DEEPNORM_ALPHA * x_ref[...] + mod[5:6] * acc, ln_ref[0:1], ln_ref[1:2])


def _ffn_call(xs, mod, prm, n_ct):
    bsz, tt, _ = xs.shape
    nt = tt // TM
    prev, nxt = _halo_specs(D_MODEL, tt)
    w_up, conv, w_down, ln2 = prm["w_up"], prm["ffn_conv"], prm["w_down"], prm["ln2"]
    return pl.pallas_call(
        functools.partial(_ffn_kernel, n_ct=n_ct, nt=nt),
        grid=(bsz, nt),
        in_specs=[_row_spec(D_MODEL), prev, nxt, _mod_spec(n_ct), _full_spec(w_up.shape, True),
                  _full_spec(conv.shape), _full_spec(w_down.shape, True), _full_spec(ln2.shape)],
        out_specs=_row_spec(D_MODEL),
        out_shape=jax.ShapeDtypeStruct((bsz, tt, D_MODEL), F32),
        compiler_params=_cparams(("parallel", "parallel")),
        name="conv_ffn",
    )(xs, xs, xs, mod, w_up, conv, w_down, ln2)


def _pad_rows(rows, width):
    rows = [r.astype(F32) if r.shape[0] == width else jnp.pad(r.astype(F32), (0, width - r.shape[0])) for r in rows]
    return jnp.pad(jnp.stack(rows), ((0, SUBLANES - len(rows)), (0, 0)))


def _layer_params(l, w_in, rwkv_conv, w0, w_b, a0, a_b, g_b, k_k, k_a, r_k, gn_g, gn_b, q_norm_g, w_uq,
                  kv_norm_g, w_ukv, w_o, ln1_g, ln1_b, w_up, ffn_conv_w, ffn_conv_b, w_down, ln2_g, ln2_b):
    c = D_RWKV
    wi = w_in[l]
    o_mla = RWKV_COLS
    w_kr = wi[:, o_mla + Q_LORA + KV_LORA:]
    pad_l = jnp.zeros((D_MODEL, QK_NOPE), F32)
    pad_r = jnp.zeros((D_MODEL, LANES - QK_DIM), F32)
    w_in_p = jnp.concatenate(
        [wi[:, :o_mla + Q_LORA + KV_LORA], pad_l, w_kr, pad_r], axis=1).astype(BF16)

    def lora_bd(w):
        z = jnp.zeros_like(w[0])
        return jnp.concatenate([jnp.concatenate([w[0], z], 1), jnp.concatenate([z, w[1]], 1)], 0).astype(BF16)

    head = jnp.arange(2 * LANES) // RWKV_HEAD
    ones_bd = (head[:, None] == head[None, :]).astype(BF16)

    wq = w_uq[l].reshape(Q_LORA, N_HEADS, QK_DIM)
    q_nope, q_rope = wq[..., :QK_NOPE], wq[..., QK_NOPE:]
    zq = lambda n: jnp.zeros((Q_LORA, N_HEADS, n), F32)
    wq = jnp.concatenate([q_nope, q_rope, zq(LANES - QK_DIM)], -1).reshape(Q_LORA, D_HEADS).astype(BF16)
    wkv = w_ukv[l].reshape(KV_LORA, N_HEADS, QK_NOPE + V_HEAD)
    zk = jnp.zeros((KV_LORA, N_HEADS, HEAD_PAD - QK_NOPE), F32)
    wkn = jnp.concatenate([wkv[..., :QK_NOPE], zk], -1).reshape(KV_LORA, D_HEADS).astype(BF16)
    wv = wkv[..., QK_NOPE:].reshape(KV_LORA, N_HEADS * V_HEAD).astype(BF16)
    wo = w_o[l]

    return {
        "w_in": w_in_p,
        "conv": _pad_rows(list(rwkv_conv[l]), RWKV_COLS),
        "w0": _pad_rows([w0[l].reshape(-1)], 2 * c),
        "wb": lora_bd(w_b[l]),
        "a0": _pad_rows([a0[l].reshape(-1)], 2 * c),
        "ab": lora_bd(a_b[l]),
        "gb": g_b[l].astype(BF16),
        "vec": _pad_rows([k_k[l], k_a[l], r_k[l].reshape(-1)], c),
        "ones_bd": ones_bd,
        "norm_g": _pad_rows([q_norm_g[l], kv_norm_g[l]], Q_LORA),
        "wq": wq, "wkn": wkn, "wv": wv,
        "gn": _pad_rows([gn_g[l], gn_b[l]], c),
        "wo_r": wo[:c].astype(BF16),
        "wo_m": wo[c:].astype(BF16),
        "ln1": _pad_rows([ln1_g[l], ln1_b[l]], D_MODEL),
        "w_up": w_up[l].astype(BF16),
        "ffn_conv": _pad_rows(list(ffn_conv_w[l]) + [ffn_conv_b[l]], 2 * D_FF),
        "w_down": w_down[l].astype(BF16),
        "ln2": _pad_rows([ln2_g[l], ln2_b[l]], D_MODEL),
    }


def _rope_tables(n_ctx, n_lat):
    rows = n_lat // GRID_W
    row = jnp.repeat(jnp.arange(rows, dtype=F32), GRID_W)
    col = jnp.tile(jnp.arange(GRID_W, dtype=F32), rows)
    n_pairs = QK_ROPE // 4
    inv = ROPE_BASE ** (-jnp.arange(n_pairs, dtype=F32) / n_pairs)
    ang = jnp.concatenate([row[:, None] * inv, col[:, None] * inv], axis=-1)
    cos = jnp.repeat(jnp.cos(ang), 2, axis=-1)
    sin = jnp.repeat(jnp.sin(ang), 2, axis=-1) * jnp.tile(jnp.array([-1.0, 1.0], F32), QK_ROPE // 2)
    tt = n_ctx + n_lat
    cos_t = jnp.ones((tt, LANES), F32).at[n_ctx:, QK_NOPE:QK_DIM].set(cos)
    sin_t = jnp.zeros((tt, LANES), F32).at[n_ctx:, QK_NOPE:QK_DIM].set(sin)
    return cos_t, sin_t


def kernel(x, c, ctx, c_ctx, w_ada, b_ada, w_in, rwkv_conv, w0, w_b, a0, a_b, g_b, k_k, k_a, r_k, gn_g, gn_b,
           q_norm_g, w_uq, kv_norm_g, w_ukv, w_o, ln1_g, ln1_b, w_up, ffn_conv_w, ffn_conv_b, w_down, ln2_g, ln2_b):
    bsz, n_lat, _ = x.shape
    n_ctx = ctx.shape[1]
    assert n_ctx % TM == 0 and n_lat % TM == 0 and n_lat % GRID_W == 0
    n_ct = n_ctx // TM
    n_l = w_ada.shape[0]

    ada_rows = -(-(bsz + 1) // SUBLANES) * SUBLANES
    cc = jnp.zeros((ada_rows, D_MODEL), F32).at[:bsz].set(c).at[bsz].set(c_ctx)
    mods = _ada_call(cc, w_ada, b_ada).reshape(n_l, ada_rows, 6, D_MODEL)
    cos_t, sin_t = _rope_tables(n_ctx, n_lat)
    xs = jnp.concatenate([ctx, x], axis=1)

    for l in range(n_l):
        prm = _layer_params(l, w_in, rwkv_conv, w0, w_b, a0, a_b, g_b, k_k, k_a, r_k, gn_g, gn_b, q_norm_g, w_uq,
                            kv_norm_g, w_ukv, w_o, ln1_g, ln1_b, w_up, ffn_conv_w, ffn_conv_b, w_down, ln2_g, ln2_b)
        m_ctx = jnp.broadcast_to(mods[l, bsz][None], (bsz, 6, D_MODEL))
        mod = jnp.stack([m_ctx, mods[l, :bsz]], axis=1)
        mod = jnp.pad(mod, ((0, 0), (0, 0), (0, SUBLANES - 6), (0, 0))).reshape(2 * bsz, SUBLANES, D_MODEL)

        last = l == n_l - 1
        t0 = n_ct if last else 0
        r, v, kk, g, bv, lw, kd, bb, q, k, vm = _front_call(xs, mod, cos_t, sin_t, prm, n_ct)
        yf, yr = _wkv_call(r, v, kk, lw, kd, bb, n_ct)
        o = _attn_call(q, k, vm, n_ct, t0)
        x1 = _out_call(yf, yr, bv, g, o, xs, mod, prm, n_ct, t0)
        xs = _ffn_call(x1, mod, prm, n_ct - t0)
    return xs
```

```python
import functools

import jax
import jax.numpy as jnp
from jax import lax
from jax.experimental import pallas as pl
from jax.experimental.pallas import tpu as pltpu

F32 = jnp.float32
BF16 = jnp.bfloat16

D_MODEL = 1024
DEPTH = 2
GRID_W = 64
D_RWKV = 512
RWKV_HEAD = 64
N_HEADS = 8
LORA_W = 64
LORA_A = 64
LORA_G = 128
RWKV_COLS = 3 * D_RWKV + 2 * LORA_W + 2 * LORA_A + LORA_G
QK_NOPE = 64
QK_ROPE = 32
QK_DIM = QK_NOPE + QK_ROPE
V_HEAD = 64
Q_LORA = 384
KV_LORA = 256
D_FF = 2816
ROPE_BASE = 10000.0
DEEPNORM_ALPHA = (2 * DEPTH) ** 0.25
LN_EPS = 1e-5
RMS_EPS = 1e-6
GN_EPS = 64e-5
LOG2_E = 1.4426950408889634

LANES = 128
SUBLANES = 8
HEAD_PAD = LANES
D_HEADS = N_HEADS * HEAD_PAD
TM = 256
HALO = SUBLANES
CHUNK = 64
LOG_CHUNK = 6
WKV_BATCH = 2
PAIR = 2 * RWKV_HEAD
N_PAIRS = N_HEADS // 2
VT_ROWS = 80
ATTN_KEYS = 1024
ATTN_HEADS = 4
FF_SUB = 256
N_FF_SUB = D_FF // FF_SUB
FF_DOWN = 6
VMEM_LIMIT = 56 << 20


def _cparams(sem):
    return pltpu.CompilerParams(dimension_semantics=sem, vmem_limit_bytes=VMEM_LIMIT)


def _mm(a, b):
    return jnp.dot(a.astype(BF16), b.astype(BF16), preferred_element_type=F32)


def _mm_nt(a, b):
    return lax.dot_general(a.astype(BF16), b.astype(BF16), (((1,), (1,)), ((), ())),
                           preferred_element_type=F32)


def _split2(x):
    hi = x.astype(BF16)
    lo = (x - hi.astype(F32)).astype(BF16)
    return hi, lo


def _mm3(a, b):
    ah, al = _split2(a)
    bh, bl = _split2(b)
    d = functools.partial(jnp.dot, preferred_element_type=F32)
    return d(ah, bh) + (d(ah, bl) + d(al, bh))


def _mm_exact_rhs(a, b_bf16):
    a1, a2 = _split2(a)
    d = functools.partial(jnp.dot, preferred_element_type=F32)
    return d(a1, b_bf16) + d(a2, b_bf16)


def _mm_exact_lhs(a_bf16, b):
    b1 = b.astype(BF16)
    r1 = b - b1.astype(F32)
    b2 = r1.astype(BF16)
    b3 = (r1 - b2.astype(F32)).astype(BF16)
    d = functools.partial(jnp.dot, preferred_element_type=F32)
    return d(a_bf16, b1) + (d(a_bf16, b2) + d(a_bf16, b3))


def _sigmoid(x):
    return 1.0 / (1.0 + jnp.exp(-x))


def _layer_norm(x, g, b):
    mu = jnp.mean(x, axis=-1, keepdims=True)
    xc = x - mu
    var = jnp.mean(xc * xc, axis=-1, keepdims=True)
    return xc * lax.rsqrt(var + LN_EPS) * g + b


def _rms_norm(x, g):
    return x * lax.rsqrt(jnp.mean(x * x, axis=-1, keepdims=True) + RMS_EPS) * g


def _row_spec(cols, t0=0):
    return pl.BlockSpec((None, TM, cols), lambda b, i, *_: (b, i + t0, 0))


def _mod_spec(n_ct, t0=0):
    return pl.BlockSpec((None, SUBLANES, D_MODEL),
                        lambda b, i, *_: (2 * b + (i + t0 >= n_ct).astype(jnp.int32), 0, 0))


def _full_spec(shape, single_buffer=False):
    nd = len(shape)
    mode = dict(pipeline_mode=pl.Buffered(1)) if single_buffer else {}
    return pl.BlockSpec(shape, lambda *_: (0,) * nd, **mode)


def _halo_specs(cols, tt, t0=0):
    per = TM // HALO
    last = tt // HALO - 1
    prev = pl.BlockSpec((None, HALO, cols), lambda b, i, *_: (b, jnp.maximum((i + t0) * per - 1, 0), 0))
    nxt = pl.BlockSpec((None, HALO, cols), lambda b, i, *_: (b, jnp.minimum((i + t0 + 1) * per, last), 0))
    return prev, nxt


def _with_halo(x_ref, xp_ref, xn_ref, scale, shift, i, n_ct, nt):
    has_prev = (i != 0) & (i != n_ct)
    has_next = (i != n_ct - 1) & (i != nt - 1)
    mod = lambda x: x * scale + shift
    return jnp.concatenate([jnp.where(has_prev, mod(xp_ref[...]), 0.0), mod(x_ref[...]),
                            jnp.where(has_next, mod(xn_ref[...]), 0.0)], axis=0).astype(BF16)


class _Pick:
    def __init__(self, pred, a, b):
        self.pred, self.a, self.b = pred, a, b

    def __getitem__(self, idx):
        return jnp.where(self.pred, self.a[idx], self.b[idx])


def _split_specs(cols, n_ct, n_ctx, n_lat, halo):
    per = TM // HALO
    out = []
    for first, rows in ((0, n_ctx), (n_ct, n_lat)):
        last_t, last_h = rows // TM - 1, rows // HALO - 1
        t = lambda i, first=first, last_t=last_t: jnp.clip(i - first, 0, last_t)
        out.append(pl.BlockSpec((None, TM, cols), lambda b, i, t=t: (b, t(i), 0)))
        if halo:
            out.append(pl.BlockSpec((None, HALO, cols),
                                    lambda b, i, first=first, last_h=last_h: (b, jnp.clip((i - first) * per - 1, 0, last_h), 0)))
            out.append(pl.BlockSpec((None, HALO, cols),
                                    lambda b, i, first=first, last_h=last_h: (b, jnp.clip((i - first + 1) * per, 0, last_h), 0)))
    return out


def _conv3(h, taps):
    rows = TM + 2 * HALO
    lo, hi = HALO, HALO + TM
    return (pltpu.roll(h, 1, 0)[lo:hi] * taps[0:1] + h[lo:hi] * taps[1:2]
            + pltpu.roll(h, rows - 1, 0)[lo:hi] * taps[2:3])


ADA_COLS = 1536


def _ada_kernel(c_ref, w_ref, b_ref, o_ref):
    c = c_ref[...]
    o_ref[...] = _mm3(c * _sigmoid(c), w_ref[...]) + b_ref[...]


def _ada_call(cc, w_ada, b_ada):
    n_l = w_ada.shape[0]
    rows = cc.shape[0]
    return pl.pallas_call(
        _ada_kernel,
        grid=(n_l, 6 * D_MODEL // ADA_COLS),
        in_specs=[
            pl.BlockSpec((rows, D_MODEL), lambda l, j: (0, 0)),
            pl.BlockSpec((None, D_MODEL, ADA_COLS), lambda l, j: (l, 0, j)),
            pl.BlockSpec((None, 1, ADA_COLS), lambda l, j: (l, 0, j)),
        ],
        out_specs=pl.BlockSpec((None, rows, ADA_COLS), lambda l, j: (l, 0, j)),
        out_shape=jax.ShapeDtypeStruct((n_l, rows, 6 * D_MODEL), F32),
        compiler_params=_cparams(("arbitrary", "arbitrary")),
        name="ada",
    )(cc, w_ada, b_ada.reshape(n_l, 1, 6 * D_MODEL))


def _seg_sum(x, ones_bd):
    w = ones_bd.shape[0]
    return jnp.concatenate([_mm_exact_rhs(x[:, o:o + w], ones_bd) for o in range(0, x.shape[1], w)], axis=1)


def _front_kernel(x_ref, xp_ref, xn_ref, mod_ref, w_ref, conv_ref, w0_ref, wb_ref, a0_ref, ab_ref, gb_ref, vec_ref,
                  ones_ref, cos_ref, sin_ref, ng_ref, wq_ref, wkn_ref, wv_ref,
                  r_ref, v_ref, kk_ref, g_ref, bv_ref, lw_ref, kd_ref, bb_ref, q_ref, k_ref, vm_ref, *, n_ct, nt):
    i = pl.program_id(1)
    m = mod_ref[...]
    u = _with_halo(x_ref, xp_ref, xn_ref, 1.0 + m[1:2], m[0:1], i, n_ct, nt)
    d = functools.partial(jnp.dot, preferred_element_type=F32)
    z = _conv3(d(u, w_ref[:, :RWKV_COLS]), conv_ref[...])
    pm = d(u, w_ref[:, RWKV_COLS:])[HALO:HALO + TM]

    c = D_RWKV
    r, k, v = z[:, 0:c], z[:, c:2 * c], z[:, 2 * c:3 * c]
    o = 3 * c
    w_lo = z[:, o:o + 2 * LORA_W]
    a_lo = z[:, o + 2 * LORA_W:o + 2 * LORA_W + 2 * LORA_A]
    g_lo = z[:, o + 2 * LORA_W + 2 * LORA_A:]
    vec = vec_ref[...]
    k_k, k_a, r_k = vec[0:1], vec[1:2], vec[2:3]
    ones_bd = ones_ref[...]

    t = -(w0_ref[0:1, :] + _mm(jnp.tanh(w_lo), wb_ref[...]))
    sp = jnp.maximum(t, 0.0) + jnp.log(1.0 + jnp.exp(-jnp.abs(t)))
    lw = -jnp.exp(-sp - 0.5)
    a = _sigmoid(a0_ref[0:1, :] + _mm(a_lo, ab_ref[...]))
    g = _mm(_sigmoid(g_lo), gb_ref[...])

    kk = k * k_k
    nrm = jnp.sqrt(_seg_sum(kk * kk, ones_bd))
    kk = kk / jnp.maximum(nrm, 1e-12)
    kd0 = k * (1.0 + (a[:, :c] - 1.0) * k_a)
    kd1 = k * (1.0 + (a[:, c:] - 1.0) * k_a)
    bonus = _seg_sum(r * (kd0 + kd1) * r_k, ones_bd) * v

    r_ref[...] = r.astype(BF16)
    v_ref[...] = v.astype(BF16)
    kk_ref[...] = kk.astype(BF16)
    g_ref[...] = g.astype(BF16)
    bv_ref[...] = bonus.astype(BF16)
    lw_ref[0] = lw[:, :c]
    lw_ref[1] = lw[:, c:]
    kd_ref[0] = kd0.astype(BF16)
    kd_ref[1] = kd1.astype(BF16)
    bb_ref[0] = (kk * a[:, :c]).astype(BF16)
    bb_ref[1] = (kk * a[:, c:]).astype(BF16)

    def rotary(x, cos, sin_signed):
        n = x.shape[1]
        even = (lax.broadcasted_iota(jnp.int32, x.shape, 1) & 1) == 0
        partner = jnp.where(even, pltpu.roll(x, n - 1, 1), pltpu.roll(x, 1, 1))
        return x * cos + partner * sin_signed

    cq, ckv, kr_x = pm[:, :Q_LORA], pm[:, Q_LORA:Q_LORA + KV_LORA], pm[:, Q_LORA + KV_LORA:]
    cos, sin = cos_ref[...], sin_ref[...]
    cos8 = jnp.concatenate([cos] * N_HEADS, axis=1)
    sin8 = jnp.concatenate([sin] * N_HEADS, axis=1)
    cqn = _rms_norm(cq, ng_ref[0:1, :Q_LORA]).astype(BF16)
    q = rotary(d(cqn, wq_ref[...]), cos8, sin8)
    q_ref[...] = (q * (QK_DIM ** -0.5 * LOG2_E)).astype(BF16)
    ckvn = _rms_norm(ckv, ng_ref[1:2, :KV_LORA]).astype(BF16)
    kr = rotary(kr_x, cos, sin)
    k_ref[...] = (d(ckvn, wkn_ref[...]) + jnp.concatenate([kr] * N_HEADS, axis=1)).astype(BF16)
    vt = d(ckvn, wv_ref[...]).T
    extra = jnp.where(lax.broadcasted_iota(jnp.int32, (VT_ROWS - V_HEAD, TM), 0) == 0, 1.0, 0.0)
    vm_ref[...] = jnp.concatenate(
        [blk for h in range(N_HEADS) for blk in (vt[h * V_HEAD:(h + 1) * V_HEAD], extra)], axis=0).astype(BF16)


def _front_split_kernel(c_ref, cp_ref, cn_ref, x_ref, xp_ref, xn_ref, *rest, n_ct, nt):
    is_ctx = pl.program_id(1) < n_ct
    _front_kernel(_Pick(is_ctx, c_ref, x_ref), _Pick(is_ctx, cp_ref, xp_ref), _Pick(is_ctx, cn_ref, xn_ref), *rest,
                  n_ct=n_ct, nt=nt)


def _front_call(xs, mod, cos_t, sin_t, prm, n_ct):
    if isinstance(xs, tuple):
        ctx, x = xs
        bsz, tt = x.shape[0], ctx.shape[1] + x.shape[1]
        body, srcs = _front_split_kernel, (ctx, ctx, ctx, x, x, x)
        src_specs = _split_specs(D_MODEL, n_ct, ctx.shape[1], x.shape[1], halo=True)
    else:
        bsz, tt, _ = xs.shape
        body, srcs = _front_kernel, (xs, xs, xs)
        src_specs = [_row_spec(D_MODEL), *_halo_specs(D_MODEL, tt)]
    nt = tt // TM
    tab = pl.BlockSpec((TM, LANES), lambda b, i: (i, 0))
    shared = jax.ShapeDtypeStruct((bsz, tt, D_RWKV), BF16)
    dir_spec = pl.BlockSpec((2, None, TM, D_RWKV), lambda b, i: (0, b, i, 0))
    dir_shape = lambda dt: jax.ShapeDtypeStruct((2, bsz, tt, D_RWKV), dt)
    heads = jax.ShapeDtypeStruct((bsz, tt, D_HEADS), BF16)
    p_rwkv = [prm[n] for n in ("w_in", "conv", "w0", "wb", "a0", "ab", "gb", "vec", "ones_bd")]
    p_mla = [prm[n] for n in ("norm_g", "wq", "wkn", "wv")]
    return pl.pallas_call(
        functools.partial(body, n_ct=n_ct, nt=nt),
        grid=(bsz, nt),
        in_specs=src_specs + [_mod_spec(n_ct)] + [_full_spec(a.shape) for a in p_rwkv]
        + [tab, tab] + [_full_spec(a.shape) for a in p_mla],
        out_specs=[_row_spec(D_RWKV)] * 5 + [dir_spec] * 3 + [_row_spec(D_HEADS)] * 2
        + [pl.BlockSpec((None, N_HEADS * VT_ROWS, TM), lambda b, i: (b, 0, i))],
        out_shape=[shared] * 5 + [dir_shape(F32), dir_shape(BF16), dir_shape(BF16)] + [heads] * 2
        + [jax.ShapeDtypeStruct((bsz, N_HEADS * VT_ROWS, tt), BF16)],
        compiler_params=_cparams(("parallel", "parallel")),
        name="front",
    )(*srcs, mod, *p_rwkv, cos_t, sin_t, *p_mla)


def _wkv_kernel(*refs, nb):
    ins, (yf_ref, yr_ref, s_ref) = refs[:12], refs[12:]
    step = pl.program_id(1)

    @pl.when(step == 0)
    def _():
        s_ref[...] = jnp.zeros_like(s_ref)

    n2 = 2 * CHUNK
    ri = lax.broadcasted_iota(jnp.int32, (n2, n2), 0)
    ci = lax.broadcasted_iota(jnp.int32, (n2, n2), 1)
    same_head = (ri >> LOG_CHUNK) == (ci >> LOG_CHUNK)
    rt, ct = ri & (CHUNK - 1), ci & (CHUNK - 1)
    eye = ri == ci
    ti = lax.broadcasted_iota(jnp.int32, (CHUNK, CHUNK), 0)
    tj = lax.broadcasted_iota(jnp.int32, (CHUNK, CHUNK), 1)
    strict = [same_head & (ct < rt), same_head & (ct > rt)]
    incl = [same_head & (ct <= rt), same_head & (ct >= rt)]
    tri = [jnp.where(tj <= ti, 1.0, 0.0).astype(BF16), jnp.where(tj >= ti, 1.0, 0.0).astype(BF16)]

    def bdiag(x):
        return jnp.where(same_head, jnp.concatenate([x, x], axis=0), 0.0)

    ch = []
    for d in range(2):
        r_ref, v_ref, kk_ref, lw_ref, kd_ref, bb_ref = ins[6 * d:6 * d + 6]
        for n in range(nb):
            lw = lw_ref[n]
            lc = _mm_exact_lhs(tri[d], lw)
            lend = jnp.sum(lw, axis=0, keepdims=True)
            e_neg = jnp.exp(-lc)
            e_end = jnp.exp(lend - lc)
            p_end = jnp.exp(lend)
            kt_all = kk_ref[n].astype(F32) * jnp.exp(lc - lw)
            rt_all = r_ref[n].astype(F32) * jnp.exp(lc)
            kd, bb = kd_ref[n].astype(F32), bb_ref[n].astype(F32)
            kh_all, bh_all = kd * e_neg, bb * e_neg
            kc_all, bc_all = kd * e_end, bb * e_end
            v_all = v_ref[n]
            for p in range(N_PAIRS):
                sl = slice(p * PAIR, (p + 1) * PAIR)
                ch.append(dict(d=d, slot=(d * nb + n) * N_PAIRS + p, kt=bdiag(kt_all[:, sl]), rr=bdiag(rt_all[:, sl]),
                               vv=bdiag(v_all[:, sl]), kh=kh_all[:, sl], bh=bh_all[:, sl], kc=kc_all[:, sl],
                               bc=bc_all[:, sl], p_end=p_end[:, sl]))
    for c in ch:
        gram = _mm_nt(jnp.concatenate([c["kt"], c["rr"]], axis=0),
                      jnp.concatenate([c["bh"]] * 2 + [c["kh"]] * 2, axis=0))
        st, ic = strict[c["d"]], incl[c["d"]]
        c["a_kb"] = jnp.where(st, gram[:n2, :n2], 0.0)
        c["a_kk"] = jnp.where(st, gram[:n2, n2:], 0.0)
        c["a_rb"] = jnp.where(ic, gram[n2:, :n2], 0.0)
        c["a_rk"] = jnp.where(ic, gram[n2:, n2:], 0.0)
        c["t"] = jnp.where(eye, 1.0, 0.0) - jnp.where((rt >> 1) == (ct >> 1), c["a_kb"], 0.0)
    for c in ch:
        c["akk_v"] = _mm(c["a_kk"], c["vv"])
        c["ark_v"] = _mm(c["a_rk"], c["vv"])
        c["kc_v"] = _mm(bdiag(c["kc"]).T, c["vv"])
    for ls in range(1, LOG_CHUNK):
        off = ((rt >> (ls + 1)) == (ct >> (ls + 1))) & ((rt >> ls) != (ct >> ls))
        for c in ch:
            c["nt"] = _mm(jnp.where(off, c["a_kb"], 0.0), c["t"])
        for c in ch:
            c["t"] = c["t"] - _mm(c["t"], c["nt"])
    for c in ch:
        c["uw"] = _mm(c["t"], jnp.concatenate([c["akk_v"], c["kt"]], axis=1))
    for c in ch:
        c["yq"] = jnp.concatenate([c["ark_v"], c["rr"]], axis=1) - _mm(c["a_rb"], c["uw"])
        diag_p = jnp.where(eye, jnp.broadcast_to(c["p_end"], (n2, n2)), 0.0)
        c["nm"] = jnp.concatenate([c["kc_v"], diag_p], axis=1) - _mm(bdiag(c["bc"]).T, c["uw"])
    for c in ch:
        qm_l = jnp.concatenate([c["yq"][:, n2:], c["nm"][:, n2:]], axis=0).astype(BF16)
        s_hi, s_lo = _split2(s_ref[c["slot"]])
        c["qm"] = jnp.dot(qm_l, s_hi, preferred_element_type=F32)
        c["m_lo"] = jnp.dot(qm_l[n2:], s_lo, preferred_element_type=F32)
    ys = []
    for c in ch:
        y_bd = c["qm"][:n2] + c["yq"][:, :n2]
        s_ref[c["slot"]] = (c["qm"][n2:] + c["m_lo"]) + c["nm"][:, :n2]
        ys.append(y_bd[:CHUNK] + y_bd[CHUNK:])
    for d, y_ref in enumerate((yf_ref, yr_ref)):
        for n in range(nb):
            o = (d * nb + n) * N_PAIRS
            y_ref[n] = jnp.concatenate(ys[o:o + N_PAIRS], axis=1).astype(y_ref.dtype)


def _wkv_call(r, v, kk, lw, kd, bb, n_ct):
    bsz, tt, c = r.shape
    nc = tt // CHUNK
    n_cc = n_ct * TM // CHUNK
    nb = WKV_BATCH
    assert bsz % nb == 0

    def rev_chunk(s):
        return jnp.where(s < n_cc, n_cc - 1 - s, nc - 1 - (s - n_cc))

    order = (lambda s: s, rev_chunk)
    specs, args = [], []
    for d in range(2):
        shared = pl.BlockSpec((nb, CHUNK, c), lambda b, s, d=d: (b, order[d](s), 0))
        per_dir = pl.BlockSpec((None, nb, CHUNK, c), lambda b, s, d=d: (d, b, order[d](s), 0))
        specs += [shared, shared, shared, per_dir, per_dir, per_dir]
        args += [r, v, kk, lw, kd, bb]
    out = jax.ShapeDtypeStruct((bsz, tt, c), BF16)
    return pl.pallas_call(
        functools.partial(_wkv_kernel, nb=nb),
        grid=(bsz // nb, nc),
        in_specs=specs,
        out_specs=[specs[0], specs[6]],
        out_shape=[out, out],
        scratch_shapes=[pltpu.VMEM((2 * nb * N_PAIRS, PAIR, PAIR), F32)],
        compiler_params=_cparams(("parallel", "arbitrary")),
        name="wkv_scan",
    )(*args)


def _rows_reduce(x, op, reduce):
    n = x.shape[0]
    while n > SUBLANES and n % 2 == 0:
        n //= 2
        x = op(x[:n], x[n:])
    return reduce(x, axis=0, keepdims=True)


def _attn_kernel(q_ref, k_ref, vt_ref, o_ref, *, n_ctx_rows, n_ct, t0):
    i = pl.program_id(2) + t0
    n_keys = k_ref.shape[0]
    kc = ATTN_KEYS if (n_keys - n_ctx_rows) % ATTN_KEYS == 0 else TM

    def attend(chunks):
        heads = range(ATTN_HEADS)
        hs = [slice(g * HEAD_PAD, (g + 1) * HEAD_PAD) for g in heads]
        qs = [q_ref[:, hs[g]] for g in heads]
        scores = lambda g, c: _mm_nt(k_ref[c[0]:c[0] + c[1], hs[g]], qs[g])
        s_next = [scores(g, chunks[0]) for g in heads]
        m, acc = ([None] * ATTN_HEADS for _ in range(2))
        for n, c in enumerate(chunks):
            s = s_next
            if n + 1 < len(chunks):
                s_next = [scores(g, chunks[n + 1]) for g in heads]
            m_c = [_rows_reduce(s[g], jnp.maximum, jnp.max) for g in heads]
            m_new = [m_c[g] if m[g] is None else jnp.maximum(m[g], m_c[g]) for g in heads]
            p = [jnp.exp2(s[g] - m_new[g]) for g in heads]
            pv = [_mm(vt_ref[g * VT_ROWS:(g + 1) * VT_ROWS, c[0]:c[0] + c[1]], p[g]) for g in heads]
            for g in heads:
                acc[g] = pv[g] if m[g] is None else jnp.exp2(m[g] - m_new[g]) * acc[g] + pv[g]
                m[g] = m_new[g]
        outs = [acc[g][:V_HEAD] / acc[g][V_HEAD:V_HEAD + 1] for g in heads]
        o_ref[...] = jnp.concatenate(outs, axis=0).T.astype(o_ref.dtype)

    ctx_chunk = [(0, n_ctx_rows)]
    lat_chunks = ctx_chunk + [(s, kc) for s in range(n_ctx_rows, n_keys, kc)]
    if t0 >= n_ct:
        attend(lat_chunks)
    else:
        @pl.when(i < n_ct)
        def _():
            attend(ctx_chunk)

        @pl.when(i >= n_ct)
        def _():
            attend(lat_chunks)


def _attn_call(q, k, vt, n_ct, t0):
    bsz, tt, _ = q.shape
    nt = tt // TM
    w = ATTN_HEADS * HEAD_PAD
    q_spec = pl.BlockSpec((None, TM, w), lambda b, h, i: (b, i + t0, h))
    k_spec = pl.BlockSpec((None, tt, w), lambda b, h, i: (b, 0, h))
    vt_spec = pl.BlockSpec((None, ATTN_HEADS * VT_ROWS, tt), lambda b, h, i: (b, h, 0))
    o_spec = pl.BlockSpec((None, TM, ATTN_HEADS * V_HEAD), lambda b, h, i: (b, i + t0, h))
    return pl.pallas_call(
        functools.partial(_attn_kernel, n_ctx_rows=n_ct * TM, n_ct=n_ct, t0=t0),
        grid=(bsz, N_HEADS // ATTN_HEADS, nt - t0),
        in_specs=[q_spec, k_spec, vt_spec],
        out_specs=o_spec,
        out_shape=jax.ShapeDtypeStruct((bsz, tt, N_HEADS * V_HEAD), BF16),
        compiler_params=_cparams(("parallel", "parallel", "arbitrary")),
        name="attention",
    )(q, k, vt)


def _out_kernel(yf_ref, yr_ref, bv_ref, g_ref, o_ref, x_ref, mod_ref, gn_ref, ones_ref, wor_ref, wom_ref, ln_ref,
                out_ref):
    ones_bd = ones_ref[...]
    y = yf_ref[...].astype(F32) + yr_ref[...].astype(F32)
    inv_n = 1.0 / RWKV_HEAD
    yc = y - _seg_sum(y, ones_bd) * inv_n
    var = _seg_sum(yc * yc, ones_bd) * inv_n
    gn = yc * lax.rsqrt(var + GN_EPS) * gn_ref[0:1] + gn_ref[1:2]
    ro = (gn + bv_ref[...]) * g_ref[...]
    m = _mm(ro, wor_ref[...]) + jnp.dot(o_ref[...], wom_ref[...], preferred_element_type=F32)
    mod = mod_ref[...]
    out_ref[...] = _layer_norm(DEEPNORM_ALPHA * x_ref[...] + mod[2:3] * m, ln_ref[0:1], ln_ref[1:2])


def _out_split_kernel(yf_ref, yr_ref, bv_ref, g_ref, o_ref, c_ref, x_ref, *rest, n_ct):
    _out_kernel(yf_ref, yr_ref, bv_ref, g_ref, o_ref, _Pick(pl.program_id(1) < n_ct, c_ref, x_ref), *rest)


def _out_call(yf, yr, bv, g, o, xs, mod, prm, n_ct, t0):
    if isinstance(xs, tuple):
        assert t0 == 0
        ctx, x = xs
        bsz, tt = x.shape[0], ctx.shape[1] + x.shape[1]
        body, srcs = functools.partial(_out_split_kernel, n_ct=n_ct), (ctx, x)
        src_specs = _split_specs(D_MODEL, n_ct, ctx.shape[1], x.shape[1], halo=False)
    else:
        bsz, tt, _ = xs.shape
        body, srcs, src_specs = _out_kernel, (xs,), [_row_spec(D_MODEL, t0)]
    nt = tt // TM
    params = [prm[n] for n in ("gn", "ones_bd", "wo_r", "wo_m", "ln1")]
    return pl.pallas_call(
        body,
        grid=(bsz, nt - t0),
        in_specs=[_row_spec(D_RWKV, t0)] * 4 + [_row_spec(N_HEADS * V_HEAD, t0)] + src_specs + [_mod_spec(n_ct, t0)]
        + [_full_spec(a.shape) for a in params],
        out_specs=_row_spec(D_MODEL),
        out_shape=jax.ShapeDtypeStruct((bsz, tt - t0 * TM, D_MODEL), F32),
        compiler_params=_cparams(("parallel", "parallel")),
        name="mixer_out",
    )(yf, yr, bv, g, o, *srcs, mod, *params)


def _ffn_kernel(x_ref, xp_ref, xn_ref, mod_ref, wup_ref, conv_ref, wd_ref, ln_ref, out_ref, *, n_ct, nt):
    i = pl.program_id(1)
    mod = mod_ref[...]
    u = _with_halo(x_ref, xp_ref, xn_ref, 1.0 + mod[4:5], mod[3:4], i, n_ct, nt)

    def up(j):
        hs = []
        for o in (j * FF_SUB, D_FF + j * FF_SUB):
            h = jnp.dot(u, wup_ref[:, o:o + FF_SUB], preferred_element_type=F32)
            taps = conv_ref[:, o:o + FF_SUB]
            hs.append(_conv3(h, taps) + taps[3:4])
        return hs

    acc = None
    acts = []
    h_next = up(0)
    for j in range(N_FF_SUB):
        hg, hv = h_next
        if j + 1 < N_FF_SUB:
            h_next = up(j + 1)
        half = 0.5 * hg
        acts.append(((half + half * jnp.tanh(half)) * hv).astype(BF16))
        if len(acts) == FF_DOWN or j + 1 == N_FF_SUB:
            lo = (j + 1 - len(acts)) * FF_SUB
            part = jnp.dot(jnp.concatenate(acts, axis=1), wd_ref[lo:(j + 1) * FF_SUB, :], preferred_element_type=F32)
            acc = part if acc is None else acc + part
            acts = []
    out_ref[...] = _layer_norm(DEEPNORM_ALPHA * x_ref[...] + mod[5:6] * acc, ln_ref[0:1], ln_ref[1:2])


def _ffn_call(xs, mod, prm, n_ct):
    bsz, tt, _ = xs.shape
    nt = tt // TM
    prev, nxt = _halo_specs(D_MODEL, tt)
    w_up, conv, w_down, ln2 = prm["w_up"], prm["ffn_conv"], prm["w_down"], prm["ln2"]
    return pl.pallas_call(
        functools.partial(_ffn_kernel, n_ct=n_ct, nt=nt),
        grid=(bsz, nt),
        in_specs=[_row_spec(D_MODEL), prev, nxt, _mod_spec(n_ct), _full_spec(w_up.shape, True),
                  _full_spec(conv.shape), _full_spec(w_down.shape, True), _full_spec(ln2.shape)],
        out_specs=_row_spec(D_MODEL),
        out_shape=jax.ShapeDtypeStruct((bsz, tt, D_MODEL), F32),
        compiler_params=_cparams(("parallel", "parallel")),
        name="conv_ffn",
    )(xs, xs, xs, mod, w_up, conv, w_down, ln2)


def _pad_rows(rows, width):
    rows = [r.astype(F32) if r.shape[0] == width else jnp.pad(r.astype(F32), (0, width - r.shape[0])) for r in rows]
    return jnp.pad(jnp.stack(rows), ((0, SUBLANES - len(rows)), (0, 0)))


def _layer_params(l, w_in, rwkv_conv, w0, w_b, a0, a_b, g_b, k_k, k_a, r_k, gn_g, gn_b, q_norm_g, w_uq,
                  kv_norm_g, w_ukv, w_o, ln1_g, ln1_b, w_up, ffn_conv_w, ffn_conv_b, w_down, ln2_g, ln2_b):
    c = D_RWKV
    wi = w_in[l]
    o_mla = RWKV_COLS
    w_kr = wi[:, o_mla + Q_LORA + KV_LORA:]
    pad_l = jnp.zeros((D_MODEL, QK_NOPE), F32)
    pad_r = jnp.zeros((D_MODEL, LANES - QK_DIM), F32)
    w_in_p = jnp.concatenate(
        [wi[:, :o_mla + Q_LORA + KV_LORA], pad_l, w_kr, pad_r], axis=1).astype(BF16)

    def lora_bd(w):
        z = jnp.zeros_like(w[0])
        return jnp.concatenate([jnp.concatenate([w[0], z], 1), jnp.concatenate([z, w[1]], 1)], 0).astype(BF16)

    head = jnp.arange(2 * LANES) // RWKV_HEAD
    ones_bd = (head[:, None] == head[None, :]).astype(BF16)

    wq = w_uq[l].reshape(Q_LORA, N_HEADS, QK_DIM)
    q_nope, q_rope = wq[..., :QK_NOPE], wq[..., QK_NOPE:]
    zq = lambda n: jnp.zeros((Q_LORA, N_HEADS, n), F32)
    wq = jnp.concatenate([q_nope, q_rope, zq(LANES - QK_DIM)], -1).reshape(Q_LORA, D_HEADS).astype(BF16)
    wkv = w_ukv[l].reshape(KV_LORA, N_HEADS, QK_NOPE + V_HEAD)
    zk = jnp.zeros((KV_LORA, N_HEADS, HEAD_PAD - QK_NOPE), F32)
    wkn = jnp.concatenate([wkv[..., :QK_NOPE], zk], -1).reshape(KV_LORA, D_HEADS).astype(BF16)
    wv = wkv[..., QK_NOPE:].reshape(KV_LORA, N_HEADS * V_HEAD).astype(BF16)
    wo = w_o[l]

    return {
        "w_in": w_in_p,
        "conv": _pad_rows(list(rwkv_conv[l]), RWKV_COLS),
        "w0": _pad_rows([w0[l].reshape(-1)], 2 * c),
        "wb": lora_bd(w_b[l]),
        "a0": _pad_rows([a0[l].reshape(-1)], 2 * c),
        "ab": lora_bd(a_b[l]),
        "gb": g_b[l].astype(BF16),
        "vec": _pad_rows([k_k[l], k_a[l], r_k[l].reshape(-1)], c),
        "ones_bd": ones_bd,
        "norm_g": _pad_rows([q_norm_g[l], kv_norm_g[l]], Q_LORA),
        "wq": wq, "wkn": wkn, "wv": wv,
        "gn": _pad_rows([gn_g[l], gn_b[l]], c),
        "wo_r": wo[:c].astype(BF16),
        "wo_m": wo[c:].astype(BF16),
        "ln1": _pad_rows([ln1_g[l], ln1_b[l]], D_MODEL),
        "w_up": w_up[l].astype(BF16),
        "ffn_conv": _pad_rows(list(ffn_conv_w[l]) + [ffn_conv_b[l]], 2 * D_FF),
        "w_down": w_down[l].astype(BF16),
        "ln2": _pad_rows([ln2_g[l], ln2_b[l]], D_MODEL),
    }


def _rope_tables(n_ctx, n_lat):
    rows = n_lat // GRID_W
    row = jnp.repeat(jnp.arange(rows, dtype=F32), GRID_W)
    col = jnp.tile(jnp.arange(GRID_W, dtype=F32), rows)
    n_pairs = QK_ROPE // 4
    inv = ROPE_BASE ** (-jnp.arange(n_pairs, dtype=F32) / n_pairs)
    ang = jnp.concatenate([row[:, None] * inv, col[:, None] * inv], axis=-1)
    cos = jnp.repeat(jnp.cos(ang), 2, axis=-1)
    sin = jnp.repeat(jnp.sin(ang), 2, axis=-1) * jnp.tile(jnp.array([-1.0, 1.0], F32), QK_ROPE // 2)
    tt = n_ctx + n_lat
    cos_t = jnp.ones((tt, LANES), F32).at[n_ctx:, QK_NOPE:QK_DIM].set(cos)
    sin_t = jnp.zeros((tt, LANES), F32).at[n_ctx:, QK_NOPE:QK_DIM].set(sin)
    return cos_t, sin_t


def kernel(x, c, ctx, c_ctx, w_ada, b_ada, w_in, rwkv_conv, w0, w_b, a0, a_b, g_b, k_k, k_a, r_k, gn_g, gn_b,
           q_norm_g, w_uq, kv_norm_g, w_ukv, w_o, ln1_g, ln1_b, w_up, ffn_conv_w, ffn_conv_b, w_down, ln2_g, ln2_b):
    bsz, n_lat, _ = x.shape
    n_ctx = ctx.shape[1]
    assert n_ctx % TM == 0 and n_lat % TM == 0 and n_lat % GRID_W == 0
    n_ct = n_ctx // TM
    n_l = w_ada.shape[0]

    ada_rows = -(-(bsz + 1) // SUBLANES) * SUBLANES
    cc = jnp.zeros((ada_rows, D_MODEL), F32).at[:bsz].set(c).at[bsz].set(c_ctx)
    mods = _ada_call(cc, w_ada, b_ada).reshape(n_l, ada_rows, 6, D_MODEL)
    cos_t, sin_t = _rope_tables(n_ctx, n_lat)
    xs = (ctx, x)

    for l in range(n_l):
        prm = _layer_params(l, w_in, rwkv_conv, w0, w_b, a0, a_b, g_b, k_k, k_a, r_k, gn_g, gn_b, q_norm_g, w_uq,
                            kv_norm_g, w_ukv, w_o, ln1_g, ln1_b, w_up, ffn_conv_w, ffn_conv_b, w_down, ln2_g, ln2_b)
        m_ctx = jnp.broadcast_to(mods[l, bsz][None], (bsz, 6, D_MODEL))
        mod = jnp.stack([m_ctx, mods[l, :bsz]], axis=1)
        mod = jnp.pad(mod, ((0, 0), (0, 0), (0, SUBLANES - 6), (0, 0))).reshape(2 * bsz, SUBLANES, D_MODEL)

        last = l == n_l - 1
        t0 = n_ct if last else 0
        if last and isinstance(xs, tuple):
            xs = jnp.concatenate(xs, axis=1)
        r, v, kk, g, bv, lw, kd, bb, q, k, vm = _front_call(xs, mod, cos_t, sin_t, prm, n_ct)
        yf, yr = _wkv_call(r, v, kk, lw, kd, bb, n_ct)
        o = _attn_call(q, k, vm, n_ct, t0)
        x1 = _out_call(yf, yr, bv, g, o, xs, mod, prm, n_ct, t0)
        xs = _ffn_call(x1, mod, prm, n_ct - t0)
    return xs
```

```python
import functools

import jax
import jax.numpy as jnp
from jax import lax
from jax.experimental import pallas as pl
from jax.experimental.pallas import tpu as pltpu

F32 = jnp.float32
BF16 = jnp.bfloat16

D_MODEL = 1024
DEPTH = 2
GRID_W = 64
D_RWKV = 512
RWKV_HEAD = 64
N_HEADS = 8
LORA_W = 64
LORA_A = 64
LORA_G = 128
RWKV_COLS = 3 * D_RWKV + 2 * LORA_W + 2 * LORA_A + LORA_G
QK_NOPE = 64
QK_ROPE = 32
QK_DIM = QK_NOPE + QK_ROPE
V_HEAD = 64
Q_LORA = 384
KV_LORA = 256
D_FF = 2816
ROPE_BASE = 10000.0
DEEPNORM_ALPHA = (2 * DEPTH) ** 0.25
LN_EPS = 1e-5
RMS_EPS = 1e-6
GN_EPS = 64e-5
LOG2_E = 1.4426950408889634

LANES = 128
SUBLANES = 8
HEAD_PAD = LANES
D_HEADS = N_HEADS * HEAD_PAD
TM = 256
HALO = SUBLANES
CHUNK = 64
LOG_CHUNK = 6
WKV_BATCH = 2
PAIR = 2 * RWKV_HEAD
N_PAIRS = N_HEADS // 2
VT_ROWS = 80
ATTN_KEYS = 1024
BOUND_LIMIT = 60.0
BOUND_SLACK = 1.01
ATTN_HEADS = 4
FF_SUB = 256
N_FF_SUB = D_FF // FF_SUB
FF_DOWN = 6
VMEM_LIMIT = 56 << 20


def _cparams(sem):
    return pltpu.CompilerParams(dimension_semantics=sem, vmem_limit_bytes=VMEM_LIMIT)


def _mm(a, b):
    return jnp.dot(a.astype(BF16), b.astype(BF16), preferred_element_type=F32)


def _mm_nt(a, b):
    return lax.dot_general(a.astype(BF16), b.astype(BF16), (((1,), (1,)), ((), ())),
                           preferred_element_type=F32)


def _split2(x):
    hi = x.astype(BF16)
    lo = (x - hi.astype(F32)).astype(BF16)
    return hi, lo


def _mm3(a, b):
    ah, al = _split2(a)
    bh, bl = _split2(b)
    d = functools.partial(jnp.dot, preferred_element_type=F32)
    return d(ah, bh) + (d(ah, bl) + d(al, bh))


def _mm_exact_rhs(a, b_bf16):
    a1, a2 = _split2(a)
    d = functools.partial(jnp.dot, preferred_element_type=F32)
    return d(a1, b_bf16) + d(a2, b_bf16)


def _mm_exact_lhs(a_bf16, b):
    b1 = b.astype(BF16)
    r1 = b - b1.astype(F32)
    b2 = r1.astype(BF16)
    b3 = (r1 - b2.astype(F32)).astype(BF16)
    d = functools.partial(jnp.dot, preferred_element_type=F32)
    return d(a_bf16, b1) + (d(a_bf16, b2) + d(a_bf16, b3))


def _sigmoid(x):
    return 1.0 / (1.0 + jnp.exp(-x))


def _layer_norm(x, g, b):
    mu = jnp.mean(x, axis=-1, keepdims=True)
    xc = x - mu
    var = jnp.mean(xc * xc, axis=-1, keepdims=True)
    return xc * lax.rsqrt(var + LN_EPS) * g + b


def _rms_norm(x, g):
    return x * lax.rsqrt(jnp.mean(x * x, axis=-1, keepdims=True) + RMS_EPS) * g


def _row_spec(cols, t0=0):
    return pl.BlockSpec((None, TM, cols), lambda b, i, *_: (b, i + t0, 0))


def _mod_spec(n_ct, t0=0):
    return pl.BlockSpec((None, SUBLANES, D_MODEL),
                        lambda b, i, *_: (2 * b + (i + t0 >= n_ct).astype(jnp.int32), 0, 0))


def _full_spec(shape, single_buffer=False):
    nd = len(shape)
    mode = dict(pipeline_mode=pl.Buffered(1)) if single_buffer else {}
    return pl.BlockSpec(shape, lambda *_: (0,) * nd, **mode)


def _halo_specs(cols, tt, t0=0):
    per = TM // HALO
    last = tt // HALO - 1
    prev = pl.BlockSpec((None, HALO, cols), lambda b, i, *_: (b, jnp.maximum((i + t0) * per - 1, 0), 0))
    nxt = pl.BlockSpec((None, HALO, cols), lambda b, i, *_: (b, jnp.minimum((i + t0 + 1) * per, last), 0))
    return prev, nxt


def _with_halo(x_ref, xp_ref, xn_ref, scale, shift, i, n_ct, nt):
    has_prev = (i != 0) & (i != n_ct)
    has_next = (i != n_ct - 1) & (i != nt - 1)
    mod = lambda x: x * scale + shift
    return jnp.concatenate([jnp.where(has_prev, mod(xp_ref[...]), 0.0), mod(x_ref[...]),
                            jnp.where(has_next, mod(xn_ref[...]), 0.0)], axis=0).astype(BF16)


class _Pick:
    def __init__(self, pred, a, b):
        self.pred, self.a, self.b = pred, a, b

    def __getitem__(self, idx):
        return jnp.where(self.pred, self.a[idx], self.b[idx])


def _split_specs(cols, n_ct, n_ctx, n_lat, halo):
    per = TM // HALO
    out = []
    for first, rows in ((0, n_ctx), (n_ct, n_lat)):
        last_t, last_h = rows // TM - 1, rows // HALO - 1
        t = lambda i, first=first, last_t=last_t: jnp.clip(i - first, 0, last_t)
        out.append(pl.BlockSpec((None, TM, cols), lambda b, i, t=t: (b, t(i), 0)))
        if halo:
            out.append(pl.BlockSpec((None, HALO, cols),
                                    lambda b, i, first=first, last_h=last_h: (b, jnp.clip((i - first) * per - 1, 0, last_h), 0)))
            out.append(pl.BlockSpec((None, HALO, cols),
                                    lambda b, i, first=first, last_h=last_h: (b, jnp.clip((i - first + 1) * per, 0, last_h), 0)))
    return out


def _conv3(h, taps):
    rows = TM + 2 * HALO
    lo, hi = HALO, HALO + TM
    return (pltpu.roll(h, 1, 0)[lo:hi] * taps[0:1] + h[lo:hi] * taps[1:2]
            + pltpu.roll(h, rows - 1, 0)[lo:hi] * taps[2:3])


ADA_COLS = 1536


def _ada_kernel(c_ref, w_ref, b_ref, o_ref):
    c = c_ref[...]
    o_ref[...] = _mm3(c * _sigmoid(c), w_ref[...]) + b_ref[...]


def _ada_call(cc, w_ada, b_ada):
    n_l = w_ada.shape[0]
    rows = cc.shape[0]
    return pl.pallas_call(
        _ada_kernel,
        grid=(n_l, 6 * D_MODEL // ADA_COLS),
        in_specs=[
            pl.BlockSpec((rows, D_MODEL), lambda l, j: (0, 0)),
            pl.BlockSpec((None, D_MODEL, ADA_COLS), lambda l, j: (l, 0, j)),
            pl.BlockSpec((None, 1, ADA_COLS), lambda l, j: (l, 0, j)),
        ],
        out_specs=pl.BlockSpec((None, rows, ADA_COLS), lambda l, j: (l, 0, j)),
        out_shape=jax.ShapeDtypeStruct((n_l, rows, 6 * D_MODEL), F32),
        compiler_params=_cparams(("arbitrary", "arbitrary")),
        name="ada",
    )(cc, w_ada, b_ada.reshape(n_l, 1, 6 * D_MODEL))


def _seg_sum(x, ones_bd):
    w = ones_bd.shape[0]
    return jnp.concatenate([_mm_exact_rhs(x[:, o:o + w], ones_bd) for o in range(0, x.shape[1], w)], axis=1)


def _front_kernel(x_ref, xp_ref, xn_ref, mod_ref, w_ref, conv_ref, w0_ref, wb_ref, a0_ref, ab_ref, gb_ref, vec_ref,
                  ones_ref, cos_ref, sin_ref, ng_ref, wq_ref, wkn_ref, wv_ref,
                  r_ref, v_ref, kk_ref, g_ref, bv_ref, lw_ref, kd_ref, bb_ref, q_ref, k_ref, vm_ref, km_ref, *, n_ct, nt):
    i = pl.program_id(1)
    m = mod_ref[...]
    u = _with_halo(x_ref, xp_ref, xn_ref, 1.0 + m[1:2], m[0:1], i, n_ct, nt)
    d = functools.partial(jnp.dot, preferred_element_type=F32)
    z = _conv3(d(u, w_ref[:, :RWKV_COLS]), conv_ref[...])
    pm = d(u, w_ref[:, RWKV_COLS:])[HALO:HALO + TM]

    def rotary(x, cos, sin_signed):
        n = x.shape[1]
        even = (lax.broadcasted_iota(jnp.int32, x.shape, 1) & 1) == 0
        partner = jnp.where(even, pltpu.roll(x, n - 1, 1), pltpu.roll(x, 1, 1))
        return x * cos + partner * sin_signed

    cq, ckv, kr_x = pm[:, :Q_LORA], pm[:, Q_LORA:Q_LORA + KV_LORA], pm[:, Q_LORA + KV_LORA:]
    cos, sin = cos_ref[...], sin_ref[...]
    cos8 = jnp.concatenate([cos] * N_HEADS, axis=1)
    sin8 = jnp.concatenate([sin] * N_HEADS, axis=1)
    cqn = _rms_norm(cq, ng_ref[0:1, :Q_LORA]).astype(BF16)
    q = rotary(d(cqn, wq_ref[...]), cos8, sin8)
    q_ref[...] = (q * (QK_DIM ** -0.5 * LOG2_E)).astype(BF16)
    ckvn = _rms_norm(ckv, ng_ref[1:2, :KV_LORA]).astype(BF16)
    kr = rotary(kr_x, cos, sin)
    k_bf = (d(ckvn, wkn_ref[...]) + jnp.concatenate([kr] * N_HEADS, axis=1)).astype(BF16)
    k_ref[...] = k_bf
    k2 = k_bf.astype(F32) * k_bf.astype(F32)
    km = [jnp.max(jnp.sum(k2[:, h * HEAD_PAD:(h + 1) * HEAD_PAD], axis=1, keepdims=True), axis=0, keepdims=True)
          for h in range(N_HEADS)]
    km_ref[...] = jnp.concatenate([jnp.broadcast_to(x, (SUBLANES, HEAD_PAD)) for x in km], axis=1)
    vt = d(ckvn, wv_ref[...]).T
    extra = jnp.where(lax.broadcasted_iota(jnp.int32, (VT_ROWS - V_HEAD, TM), 0) == 0, 1.0, 0.0)
    vm_ref[...] = jnp.concatenate(
        [blk for h in range(N_HEADS) for blk in (vt[h * V_HEAD:(h + 1) * V_HEAD], extra)], axis=0).astype(BF16)

    c = D_RWKV
    r, k, v = z[:, 0:c], z[:, c:2 * c], z[:, 2 * c:3 * c]
    o = 3 * c
    w_lo = z[:, o:o + 2 * LORA_W]
    a_lo = z[:, o + 2 * LORA_W:o + 2 * LORA_W + 2 * LORA_A]
    g_lo = z[:, o + 2 * LORA_W + 2 * LORA_A:]
    vec = vec_ref[...]
    k_k, k_a, r_k = vec[0:1], vec[1:2], vec[2:3]
    ones_bd = ones_ref[...]

    t = -(w0_ref[0:1, :] + _mm(jnp.tanh(w_lo), wb_ref[...]))
    sp = jnp.maximum(t, 0.0) + jnp.log(1.0 + jnp.exp(-jnp.abs(t)))
    lw = -jnp.exp(-sp - 0.5)
    a = _sigmoid(a0_ref[0:1, :] + _mm(a_lo, ab_ref[...]))
    g = _mm(_sigmoid(g_lo), gb_ref[...])

    kk = k * k_k
    nrm = jnp.sqrt(_seg_sum(kk * kk, ones_bd))
    kk = kk / jnp.maximum(nrm, 1e-12)
    kd0 = k * (1.0 + (a[:, :c] - 1.0) * k_a)
    kd1 = k * (1.0 + (a[:, c:] - 1.0) * k_a)
    bonus = _seg_sum(r * (kd0 + kd1) * r_k, ones_bd) * v

    r_ref[...] = r.astype(BF16)
    v_ref[...] = v.astype(BF16)
    kk_ref[...] = kk.astype(BF16)
    g_ref[...] = g.astype(BF16)
    bv_ref[...] = bonus.astype(BF16)
    lw_ref[0] = lw[:, :c]
    lw_ref[1] = lw[:, c:]
    kd_ref[0] = kd0.astype(BF16)
    kd_ref[1] = kd1.astype(BF16)
    bb_ref[0] = (kk * a[:, :c]).astype(BF16)
    bb_ref[1] = (kk * a[:, c:]).astype(BF16)


def _front_split_kernel(c_ref, cp_ref, cn_ref, x_ref, xp_ref, xn_ref, *rest, n_ct, nt):
    is_ctx = pl.program_id(1) < n_ct
    _front_kernel(_Pick(is_ctx, c_ref, x_ref), _Pick(is_ctx, cp_ref, xp_ref), _Pick(is_ctx, cn_ref, xn_ref), *rest,
                  n_ct=n_ct, nt=nt)


def _front_call(xs, mod, cos_t, sin_t, prm, n_ct):
    if isinstance(xs, tuple):
        ctx, x = xs
        bsz, tt = x.shape[0], ctx.shape[1] + x.shape[1]
        body, srcs = _front_split_kernel, (ctx, ctx, ctx, x, x, x)
        src_specs = _split_specs(D_MODEL, n_ct, ctx.shape[1], x.shape[1], halo=True)
    else:
        bsz, tt, _ = xs.shape
        body, srcs = _front_kernel, (xs, xs, xs)
        src_specs = [_row_spec(D_MODEL), *_halo_specs(D_MODEL, tt)]
    nt = tt // TM
    tab = pl.BlockSpec((TM, LANES), lambda b, i: (i, 0))
    shared = jax.ShapeDtypeStruct((bsz, tt, D_RWKV), BF16)
    dir_spec = pl.BlockSpec((2, None, TM, D_RWKV), lambda b, i: (0, b, i, 0))
    dir_shape = lambda dt: jax.ShapeDtypeStruct((2, bsz, tt, D_RWKV), dt)
    heads = jax.ShapeDtypeStruct((bsz, tt, D_HEADS), BF16)
    p_rwkv = [prm[n] for n in ("w_in", "conv", "w0", "wb", "a0", "ab", "gb", "vec", "ones_bd")]
    p_mla = [prm[n] for n in ("norm_g", "wq", "wkn", "wv")]
    return pl.pallas_call(
        functools.partial(body, n_ct=n_ct, nt=nt),
        grid=(bsz, nt),
        in_specs=src_specs + [_mod_spec(n_ct)] + [_full_spec(a.shape) for a in p_rwkv]
        + [tab, tab] + [_full_spec(a.shape) for a in p_mla],
        out_specs=[_row_spec(D_RWKV)] * 5 + [dir_spec] * 3 + [_row_spec(D_HEADS)] * 2
        + [pl.BlockSpec((None, N_HEADS * VT_ROWS, TM), lambda b, i: (b, 0, i)),
           pl.BlockSpec((None, None, SUBLANES, D_HEADS), lambda b, i: (b, i, 0, 0))],
        out_shape=[shared] * 5 + [dir_shape(F32), dir_shape(BF16), dir_shape(BF16)] + [heads] * 2
        + [jax.ShapeDtypeStruct((bsz, N_HEADS * VT_ROWS, tt), BF16),
           jax.ShapeDtypeStruct((bsz, nt, SUBLANES, D_HEADS), F32)],
        compiler_params=_cparams(("parallel", "parallel")),
        name="front",
    )(*srcs, mod, *p_rwkv, cos_t, sin_t, *p_mla)


def _wkv_kernel(*refs, nb):
    ins, (yf_ref, yr_ref, s_ref) = refs[:12], refs[12:]
    step = pl.program_id(1)

    @pl.when(step == 0)
    def _():
        s_ref[...] = jnp.zeros_like(s_ref)

    n2 = 2 * CHUNK
    ri = lax.broadcasted_iota(jnp.int32, (n2, n2), 0)
    ci = lax.broadcasted_iota(jnp.int32, (n2, n2), 1)
    same_head = (ri >> LOG_CHUNK) == (ci >> LOG_CHUNK)
    rt, ct = ri & (CHUNK - 1), ci & (CHUNK - 1)
    eye = ri == ci
    ti = lax.broadcasted_iota(jnp.int32, (CHUNK, CHUNK), 0)
    tj = lax.broadcasted_iota(jnp.int32, (CHUNK, CHUNK), 1)
    strict = [same_head & (ct < rt), same_head & (ct > rt)]
    incl = [same_head & (ct <= rt), same_head & (ct >= rt)]
    tri = [jnp.where(tj <= ti, 1.0, 0.0).astype(BF16), jnp.where(tj >= ti, 1.0, 0.0).astype(BF16)]

    def bdiag(x):
        return jnp.where(same_head, jnp.concatenate([x, x], axis=0), 0.0)

    ch = []
    for d in range(2):
        r_ref, v_ref, kk_ref, lw_ref, kd_ref, bb_ref = ins[6 * d:6 * d + 6]
        for n in range(nb):
            lw = lw_ref[n]
            lc = _mm_exact_lhs(tri[d], lw)
            lend = jnp.sum(lw, axis=0, keepdims=True)
            e_neg = jnp.exp(-lc)
            e_end = jnp.exp(lend - lc)
            p_end = jnp.exp(lend)
            kt_all = kk_ref[n].astype(F32) * jnp.exp(lc - lw)
            rt_all = r_ref[n].astype(F32) * jnp.exp(lc)
            kd, bb = kd_ref[n].astype(F32), bb_ref[n].astype(F32)
            kh_all, bh_all = kd * e_neg, bb * e_neg
            kc_all, bc_all = kd * e_end, bb * e_end
            v_all = v_ref[n]
            for p in range(N_PAIRS):
                sl = slice(p * PAIR, (p + 1) * PAIR)
                ch.append(dict(d=d, slot=(d * nb + n) * N_PAIRS + p, kt=bdiag(kt_all[:, sl]), rr=bdiag(rt_all[:, sl]),
                               vv=bdiag(v_all[:, sl]), kh=kh_all[:, sl], bh=bh_all[:, sl], kc=kc_all[:, sl],
                               bc=bc_all[:, sl], p_end=p_end[:, sl]))
    for c in ch:
        gram = _mm_nt(jnp.concatenate([c["kt"], c["rr"]], axis=0),
                      jnp.concatenate([c["bh"]] * 2 + [c["kh"]] * 2, axis=0))
        st, ic = strict[c["d"]], incl[c["d"]]
        c["a_kb"] = jnp.where(st, gram[:n2, :n2], 0.0)
        c["a_kk"] = jnp.where(st, gram[:n2, n2:], 0.0)
        c["a_rb"] = jnp.where(ic, gram[n2:, :n2], 0.0)
        c["a_rk"] = jnp.where(ic, gram[n2:, n2:], 0.0)
        c["t"] = jnp.where(eye, 1.0, 0.0) - jnp.where((rt >> 1) == (ct >> 1), c["a_kb"], 0.0)
    for c in ch:
        c["akk_v"] = _mm(c["a_kk"], c["vv"])
        c["ark_v"] = _mm(c["a_rk"], c["vv"])
        c["kc_v"] = _mm(bdiag(c["kc"]).T, c["vv"])
    for ls in range(1, LOG_CHUNK):
        off = ((rt >> (ls + 1)) == (ct >> (ls + 1))) & ((rt >> ls) != (ct >> ls))
        for c in ch:
            c["nt"] = _mm(jnp.where(off, c["a_kb"], 0.0), c["t"])
        for c in ch:
            c["t"] = c["t"] - _mm(c["t"], c["nt"])
    for c in ch:
        c["uw"] = _mm(c["t"], jnp.concatenate([c["akk_v"], c["kt"]], axis=1))
    for c in ch:
        c["yq"] = jnp.concatenate([c["ark_v"], c["rr"]], axis=1) - _mm(c["a_rb"], c["uw"])
        diag_p = jnp.where(eye, jnp.broadcast_to(c["p_end"], (n2, n2)), 0.0)
        c["nm"] = jnp.concatenate([c["kc_v"], diag_p], axis=1) - _mm(bdiag(c["bc"]).T, c["uw"])
    for c in ch:
        qm_l = jnp.concatenate([c["yq"][:, n2:], c["nm"][:, n2:]], axis=0).astype(BF16)
        s_hi, s_lo = _split2(s_ref[c["slot"]])
        c["qm"] = jnp.dot(qm_l, s_hi, preferred_element_type=F32)
        c["m_lo"] = jnp.dot(qm_l[n2:], s_lo, preferred_element_type=F32)
    ys = []
    for c in ch:
        y_bd = c["qm"][:n2] + c["yq"][:, :n2]
        s_ref[c["slot"]] = (c["qm"][n2:] + c["m_lo"]) + c["nm"][:, :n2]
        ys.append(y_bd[:CHUNK] + y_bd[CHUNK:])
    for d, y_ref in enumerate((yf_ref, yr_ref)):
        for n in range(nb):
            o = (d * nb + n) * N_PAIRS
            y_ref[n] = jnp.concatenate(ys[o:o + N_PAIRS], axis=1).astype(y_ref.dtype)


def _wkv_call(r, v, kk, lw, kd, bb, n_ct):
    bsz, tt, c = r.shape
    nc = tt // CHUNK
    n_cc = n_ct * TM // CHUNK
    nb = WKV_BATCH
    assert bsz % nb == 0

    def rev_chunk(s):
        return jnp.where(s < n_cc, n_cc - 1 - s, nc - 1 - (s - n_cc))

    order = (lambda s: s, rev_chunk)
    specs, args = [], []
    for d in range(2):
        shared = pl.BlockSpec((nb, CHUNK, c), lambda b, s, d=d: (b, order[d](s), 0))
        per_dir = pl.BlockSpec((None, nb, CHUNK, c), lambda b, s, d=d: (d, b, order[d](s), 0))
        specs += [shared, shared, shared, per_dir, per_dir, per_dir]
        args += [r, v, kk, lw, kd, bb]
    out = jax.ShapeDtypeStruct((bsz, tt, c), BF16)
    return pl.pallas_call(
        functools.partial(_wkv_kernel, nb=nb),
        grid=(bsz // nb, nc),
        in_specs=specs,
        out_specs=[specs[0], specs[6]],
        out_shape=[out, out],
        scratch_shapes=[pltpu.VMEM((2 * nb * N_PAIRS, PAIR, PAIR), F32)],
        compiler_params=_cparams(("parallel", "arbitrary")),
        name="wkv_scan",
    )(*args)


def _rows_reduce(x, op, reduce):
    n = x.shape[0]
    while n > SUBLANES and n % 2 == 0:
        n //= 2
        x = op(x[:n], x[n:])
    return reduce(x, axis=0, keepdims=True)


def _attn_kernel(q_ref, k_ref, vt_ref, km_ref, o_ref, *, n_ctx_rows, n_ct, t0):
    i = pl.program_id(2) + t0
    n_keys = k_ref.shape[0]
    kc = ATTN_KEYS if (n_keys - n_ctx_rows) % ATTN_KEYS == 0 else TM
    heads = range(ATTN_HEADS)
    hs = [slice(g * HEAD_PAD, (g + 1) * HEAD_PAD) for g in heads]

    km2 = jnp.max(jnp.max(km_ref[...], axis=0), axis=0, keepdims=True)
    bound = []
    for g in heads:
        qf = q_ref[:, hs[g]].astype(F32)
        q2 = jnp.max(jnp.sum(qf * qf, axis=1, keepdims=True), axis=0, keepdims=True)
        bound.append(jnp.sqrt(q2 * km2[:, g * HEAD_PAD:g * HEAD_PAD + 1]) * BOUND_SLACK)
    bound_ok = functools.reduce(jnp.maximum, [jnp.max(b) for b in bound]) < BOUND_LIMIT

    def finish(acc):
        outs = [acc[g][:V_HEAD] / acc[g][V_HEAD:V_HEAD + 1] for g in heads]
        o_ref[...] = jnp.concatenate(outs, axis=0).T.astype(o_ref.dtype)

    def attend_bounded(n):
        scores = lambda g: _mm_nt(k_ref[0:n, hs[g]], q_ref[:, hs[g]])
        acc = []
        s_next = scores(0)
        for g in heads:
            s = s_next
            if g + 1 < ATTN_HEADS:
                s_next = scores(g + 1)
            acc.append(_mm(vt_ref[g * VT_ROWS:(g + 1) * VT_ROWS, 0:n], jnp.exp2(s - bound[g])))
        finish(acc)

    def attend(chunks):
        qs = [q_ref[:, hs[g]] for g in heads]
        scores = lambda g, c: _mm_nt(k_ref[c[0]:c[0] + c[1], hs[g]], qs[g])
        s_next = [scores(g, chunks[0]) for g in heads]
        m, acc = ([None] * ATTN_HEADS for _ in range(2))
        for n, c in enumerate(chunks):
            s = s_next
            if n + 1 < len(chunks):
                s_next = [scores(g, chunks[n + 1]) for g in heads]
            m_c = [_rows_reduce(s[g], jnp.maximum, jnp.max) for g in heads]
            m_new = [m_c[g] if m[g] is None else jnp.maximum(m[g], m_c[g]) for g in heads]
            p = [jnp.exp2(s[g] - m_new[g]) for g in heads]
            pv = [_mm(vt_ref[g * VT_ROWS:(g + 1) * VT_ROWS, c[0]:c[0] + c[1]], p[g]) for g in heads]
            for g in heads:
                acc[g] = pv[g] if m[g] is None else jnp.exp2(m[g] - m_new[g]) * acc[g] + pv[g]
                m[g] = m_new[g]
        finish(acc)

    ctx_chunk = [(0, n_ctx_rows)]
    lat_chunks = ctx_chunk + [(s, kc) for s in range(n_ctx_rows, n_keys, kc)]
    is_lat = i >= n_ct
    pl.when(is_lat & bound_ok)(lambda: attend_bounded(n_keys))
    pl.when(is_lat & ~bound_ok)(lambda: attend(lat_chunks))
    if t0 < n_ct:
        pl.when(~is_lat & bound_ok)(lambda: attend_bounded(n_ctx_rows))
        pl.when(~is_lat & ~bound_ok)(lambda: attend(ctx_chunk))


def _attn_call(q, k, vt, km, n_ct, t0):
    bsz, tt, _ = q.shape
    nt = tt // TM
    w = ATTN_HEADS * HEAD_PAD
    q_spec = pl.BlockSpec((None, TM, w), lambda b, h, i: (b, i + t0, h))
    k_spec = pl.BlockSpec((None, tt, w), lambda b, h, i: (b, 0, h))
    vt_spec = pl.BlockSpec((None, ATTN_HEADS * VT_ROWS, tt), lambda b, h, i: (b, h, 0))
    o_spec = pl.BlockSpec((None, TM, ATTN_HEADS * V_HEAD), lambda b, h, i: (b, i + t0, h))
    km_spec = pl.BlockSpec((None, nt, SUBLANES, w), lambda b, h, i: (b, 0, 0, h))
    return pl.pallas_call(
        functools.partial(_attn_kernel, n_ctx_rows=n_ct * TM, n_ct=n_ct, t0=t0),
        grid=(bsz, N_HEADS // ATTN_HEADS, nt - t0),
        in_specs=[q_spec, k_spec, vt_spec, km_spec],
        out_specs=o_spec,
        out_shape=jax.ShapeDtypeStruct((bsz, tt, N_HEADS * V_HEAD), BF16),
        compiler_params=_cparams(("parallel", "parallel", "arbitrary")),
        name="attention",
    )(q, k, vt, km)


def _out_kernel(yf_ref, yr_ref, bv_ref, g_ref, o_ref, x_ref, mod_ref, gn_ref, ones_ref, wor_ref, wom_ref, ln_ref,
                out_ref):
    ones_bd = ones_ref[...]
    y = yf_ref[...].astype(F32) + yr_ref[...].astype(F32)
    inv_n = 1.0 / RWKV_HEAD
    yc = y - _seg_sum(y, ones_bd) * inv_n
    var = _seg_sum(yc * yc, ones_bd) * inv_n
    gn = yc * lax.rsqrt(var + GN_EPS) * gn_ref[0:1] + gn_ref[1:2]
    ro = (gn + bv_ref[...]) * g_ref[...]
    m = _mm(ro, wor_ref[...]) + jnp.dot(o_ref[...], wom_ref[...], preferred_element_type=F32)
    mod = mod_ref[...]
    out_ref[...] = _layer_norm(DEEPNORM_ALPHA * x_ref[...] + mod[2:3] * m, ln_ref[0:1], ln_ref[1:2])


def _out_split_kernel(yf_ref, yr_ref, bv_ref, g_ref, o_ref, c_ref, x_ref, *rest, n_ct):
    _out_kernel(yf_ref, yr_ref, bv_ref, g_ref, o_ref, _Pick(pl.program_id(1) < n_ct, c_ref, x_ref), *rest)


def _out_call(yf, yr, bv, g, o, xs, mod, prm, n_ct, t0):
    if isinstance(xs, tuple):
        assert t0 == 0
        ctx, x = xs
        bsz, tt = x.shape[0], ctx.shape[1] + x.shape[1]
        body, srcs = functools.partial(_out_split_kernel, n_ct=n_ct), (ctx, x)
        src_specs = _split_specs(D_MODEL, n_ct, ctx.shape[1], x.shape[1], halo=False)
    else:
        bsz, tt, _ = xs.shape
        body, srcs, src_specs = _out_kernel, (xs,), [_row_spec(D_MODEL, t0)]
    nt = tt // TM
    params = [prm[n] for n in ("gn", "ones_bd", "wo_r", "wo_m", "ln1")]
    return pl.pallas_call(
        body,
        grid=(bsz, nt - t0),
        in_specs=[_row_spec(D_RWKV, t0)] * 4 + [_row_spec(N_HEADS * V_HEAD, t0)] + src_specs + [_mod_spec(n_ct, t0)]
        + [_full_spec(a.shape) for a in params],
        out_specs=_row_spec(D_MODEL),
        out_shape=jax.ShapeDtypeStruct((bsz, tt - t0 * TM, D_MODEL), F32),
        compiler_params=_cparams(("parallel", "parallel")),
        name="mixer_out",
    )(yf, yr, bv, g, o, *srcs, mod, *params)


def _ffn_kernel(x_ref, xp_ref, xn_ref, mod_ref, wup_ref, conv_ref, wd_ref, ln_ref, out_ref, *, n_ct, nt):
    i = pl.program_id(1)
    mod = mod_ref[...]
    u = _with_halo(x_ref, xp_ref, xn_ref, 1.0 + mod[4:5], mod[3:4], i, n_ct, nt)

    def up(j):
        hs = []
        for o in (j * FF_SUB, D_FF + j * FF_SUB):
            h = jnp.dot(u, wup_ref[:, o:o + FF_SUB], preferred_element_type=F32)
            taps = conv_ref[:, o:o + FF_SUB]
            hs.append(_conv3(h, taps) + taps[3:4])
        return hs

    acc = None
    acts = []
    h_next = up(0)
    for j in range(N_FF_SUB):
        hg, hv = h_next
        if j + 1 < N_FF_SUB:
            h_next = up(j + 1)
        half = 0.5 * hg
        acts.append(((half + half * jnp.tanh(half)) * hv).astype(BF16))
        if len(acts) == FF_DOWN or j + 1 == N_FF_SUB:
            lo = (j + 1 - len(acts)) * FF_SUB
            part = jnp.dot(jnp.concatenate(acts, axis=1), wd_ref[lo:(j + 1) * FF_SUB, :], preferred_element_type=F32)
            acc = part if acc is None else acc + part
            acts = []
    out_ref[...] = _layer_norm(DEEPNORM_ALPHA * x_ref[...] + mod[5:6] * acc, ln_ref[0:1], ln_ref[1:2])


def _ffn_call(xs, mod, prm, n_ct):
    bsz, tt, _ = xs.shape
    nt = tt // TM
    prev, nxt = _halo_specs(D_MODEL, tt)
    w_up, conv, w_down, ln2 = prm["w_up"], prm["ffn_conv"], prm["w_down"], prm["ln2"]
    return pl.pallas_call(
        functools.partial(_ffn_kernel, n_ct=n_ct, nt=nt),
        grid=(bsz, nt),
        in_specs=[_row_spec(D_MODEL), prev, nxt, _mod_spec(n_ct), _full_spec(w_up.shape, True),
                  _full_spec(conv.shape), _full_spec(w_down.shape, True), _full_spec(ln2.shape)],
        out_specs=_row_spec(D_MODEL),
        out_shape=jax.ShapeDtypeStruct((bsz, tt, D_MODEL), F32),
        compiler_params=_cparams(("parallel", "parallel")),
        name="conv_ffn",
    )(xs, xs, xs, mod, w_up, conv, w_down, ln2)


def _pad_rows(rows, width):
    rows = [r.astype(F32) if r.shape[0] == width else jnp.pad(r.astype(F32), (0, width - r.shape[0])) for r in rows]
    return jnp.pad(jnp.stack(rows), ((0, SUBLANES - len(rows)), (0, 0)))


def _layer_params(l, w_in, rwkv_conv, w0, w_b, a0, a_b, g_b, k_k, k_a, r_k, gn_g, gn_b, q_norm_g, w_uq,
                  kv_norm_g, w_ukv, w_o, ln1_g, ln1_b, w_up, ffn_conv_w, ffn_conv_b, w_down, ln2_g, ln2_b):
    c = D_RWKV
    wi = w_in[l]
    o_mla = RWKV_COLS
    w_kr = wi[:, o_mla + Q_LORA + KV_LORA:]
    pad_l = jnp.zeros((D_MODEL, QK_NOPE), F32)
    pad_r = jnp.zeros((D_MODEL, LANES - QK_DIM), F32)
    w_in_p = jnp.concatenate(
        [wi[:, :o_mla + Q_LORA + KV_LORA], pad_l, w_kr, pad_r], axis=1).astype(BF16)

    def lora_bd(w):
        z = jnp.zeros_like(w[0])
        return jnp.concatenate([jnp.concatenate([w[0], z], 1), jnp.concatenate([z, w[1]], 1)], 0).astype(BF16)

    head = jnp.arange(2 * LANES) // RWKV_HEAD
    ones_bd = (head[:, None] == head[None, :]).astype(BF16)

    wq = w_uq[l].reshape(Q_LORA, N_HEADS, QK_DIM)
    q_nope, q_rope = wq[..., :QK_NOPE], wq[..., QK_NOPE:]
    zq = lambda n: jnp.zeros((Q_LORA, N_HEADS, n), F32)
    wq = jnp.concatenate([q_nope, q_rope, zq(LANES - QK_DIM)], -1).reshape(Q_LORA, D_HEADS).astype(BF16)
    wkv = w_ukv[l].reshape(KV_LORA, N_HEADS, QK_NOPE + V_HEAD)
    zk = jnp.zeros((KV_LORA, N_HEADS, HEAD_PAD - QK_NOPE), F32)
    wkn = jnp.concatenate([wkv[..., :QK_NOPE], zk], -1).reshape(KV_LORA, D_HEADS).astype(BF16)
    wv = wkv[..., QK_NOPE:].reshape(KV_LORA, N_HEADS * V_HEAD).astype(BF16)
    wo = w_o[l]

    return {
        "w_in": w_in_p,
        "conv": _pad_rows(list(rwkv_conv[l]), RWKV_COLS),
        "w0": _pad_rows([w0[l].reshape(-1)], 2 * c),
        "wb": lora_bd(w_b[l]),
        "a0": _pad_rows([a0[l].reshape(-1)], 2 * c),
        "ab": lora_bd(a_b[l]),
        "gb": g_b[l].astype(BF16),
        "vec": _pad_rows([k_k[l], k_a[l], r_k[l].reshape(-1)], c),
        "ones_bd": ones_bd,
        "norm_g": _pad_rows([q_norm_g[l], kv_norm_g[l]], Q_LORA),
        "wq": wq, "wkn": wkn, "wv": wv,
        "gn": _pad_rows([gn_g[l], gn_b[l]], c),
        "wo_r": wo[:c].astype(BF16),
        "wo_m": wo[c:].astype(BF16),
        "ln1": _pad_rows([ln1_g[l], ln1_b[l]], D_MODEL),
        "w_up": w_up[l].astype(BF16),
        "ffn_conv": _pad_rows(list(ffn_conv_w[l]) + [ffn_conv_b[l]], 2 * D_FF),
        "w_down": w_down[l].astype(BF16),
        "ln2": _pad_rows([ln2_g[l], ln2_b[l]], D_MODEL),
    }


def _rope_tables(n_ctx, n_lat):
    rows = n_lat // GRID_W
    row = jnp.repeat(jnp.arange(rows, dtype=F32), GRID_W)
    col = jnp.tile(jnp.arange(GRID_W, dtype=F32), rows)
    n_pairs = QK_ROPE // 4
    inv = ROPE_BASE ** (-jnp.arange(n_pairs, dtype=F32) / n_pairs)
    ang = jnp.concatenate([row[:, None] * inv, col[:, None] * inv], axis=-1)
    cos = jnp.repeat(jnp.cos(ang), 2, axis=-1)
    sin = jnp.repeat(jnp.sin(ang), 2, axis=-1) * jnp.tile(jnp.array([-1.0, 1.0], F32), QK_ROPE // 2)
    tt = n_ctx + n_lat
    cos_t = jnp.ones((tt, LANES), F32).at[n_ctx:, QK_NOPE:QK_DIM].set(cos)
    sin_t = jnp.zeros((tt, LANES), F32).at[n_ctx:, QK_NOPE:QK_DIM].set(sin)
    return cos_t, sin_t


def kernel(x, c, ctx, c_ctx, w_ada, b_ada, w_in, rwkv_conv, w0, w_b, a0, a_b, g_b, k_k, k_a, r_k, gn_g, gn_b,
           q_norm_g, w_uq, kv_norm_g, w_ukv, w_o, ln1_g, ln1_b, w_up, ffn_conv_w, ffn_conv_b, w_down, ln2_g, ln2_b):
    bsz, n_lat, _ = x.shape
    n_ctx = ctx.shape[1]
    assert n_ctx % TM == 0 and n_lat % TM == 0 and n_lat % GRID_W == 0
    n_ct = n_ctx // TM
    n_l = w_ada.shape[0]

    ada_rows = -(-(bsz + 1) // SUBLANES) * SUBLANES
    cc = jnp.zeros((ada_rows, D_MODEL), F32).at[:bsz].set(c).at[bsz].set(c_ctx)
    mods = _ada_call(cc, w_ada, b_ada).reshape(n_l, ada_rows, 6, D_MODEL)
    cos_t, sin_t = _rope_tables(n_ctx, n_lat)
    xs = (ctx, x)

    for l in range(n_l):
        prm = _layer_params(l, w_in, rwkv_conv, w0, w_b, a0, a_b, g_b, k_k, k_a, r_k, gn_g, gn_b, q_norm_g, w_uq,
                            kv_norm_g, w_ukv, w_o, ln1_g, ln1_b, w_up, ffn_conv_w, ffn_conv_b, w_down, ln2_g, ln2_b)
        m_ctx = jnp.broadcast_to(mods[l, bsz][None], (bsz, 6, D_MODEL))
        mod = jnp.stack([m_ctx, mods[l, :bsz]], axis=1)
        mod = jnp.pad(mod, ((0, 0), (0, 0), (0, SUBLANES - 6), (0, 0))).reshape(2 * bsz, SUBLANES, D_MODEL)

        last = l == n_l - 1
        t0 = n_ct if last else 0
        if last and isinstance(xs, tuple):
            xs = jnp.concatenate(xs, axis=1)
        r, v, kk, g, bv, lw, kd, bb, q, k, vm, km = _front_call(xs, mod, cos_t, sin_t, prm, n_ct)
        yf, yr = _wkv_call(r, v, kk, lw, kd, bb, n_ct)
        o = _attn_call(q, k, vm, km, n_ct, t0)
        x1 = _out_call(yf, yr, bv, g, o, xs, mod, prm, n_ct, t0)
        xs = _ffn_call(x1, mod, prm, n_ct - t0)
    return xs
```

```python
import functools

import jax
import jax.numpy as jnp
from jax import lax
from jax.experimental import pallas as pl
from jax.experimental.pallas import tpu as pltpu

F32 = jnp.float32
BF16 = jnp.bfloat16

D_MODEL = 1024
DEPTH = 2
GRID_W = 64
D_RWKV = 512
RWKV_HEAD = 64
N_HEADS = 8
LORA_W = 64
LORA_A = 64
LORA_G = 128
RWKV_COLS = 3 * D_RWKV + 2 * LORA_W + 2 * LORA_A + LORA_G
QK_NOPE = 64
QK_ROPE = 32
QK_DIM = QK_NOPE + QK_ROPE
V_HEAD = 64
Q_LORA = 384
KV_LORA = 256
D_FF = 2816
ROPE_BASE = 10000.0
DEEPNORM_ALPHA = (2 * DEPTH) ** 0.25
LN_EPS = 1e-5
RMS_EPS = 1e-6
GN_EPS = 64e-5
LOG2_E = 1.4426950408889634

LANES = 128
SUBLANES = 8
HEAD_PAD = LANES
D_HEADS = N_HEADS * HEAD_PAD
TM = 256
HALO = SUBLANES
CHUNK = 64
LOG_CHUNK = 6
WKV_BATCH = 2
PAIR = 2 * RWKV_HEAD
N_PAIRS = N_HEADS // 2
VT_ROWS = 80
ATTN_KEYS = 1024
BOUND_LIMIT = 60.0
BOUND_SLACK = 1.01
ATTN_HEADS = 4
FF_SUB = 256
N_FF_SUB = D_FF // FF_SUB
FF_DOWN = 6
VMEM_LIMIT = 56 << 20


def _cparams(sem):
    return pltpu.CompilerParams(dimension_semantics=sem, vmem_limit_bytes=VMEM_LIMIT)


def _mm(a, b):
    return jnp.dot(a.astype(BF16), b.astype(BF16), preferred_element_type=F32)


def _mm_nt(a, b):
    return lax.dot_general(a.astype(BF16), b.astype(BF16), (((1,), (1,)), ((), ())),
                           preferred_element_type=F32)


def _split2(x):
    hi = x.astype(BF16)
    lo = (x - hi.astype(F32)).astype(BF16)
    return hi, lo


def _mm3(a, b):
    ah, al = _split2(a)
    bh, bl = _split2(b)
    d = functools.partial(jnp.dot, preferred_element_type=F32)
    return d(ah, bh) + (d(ah, bl) + d(al, bh))


def _mm_exact_rhs(a, b_bf16):
    a1, a2 = _split2(a)
    d = functools.partial(jnp.dot, preferred_element_type=F32)
    return d(a1, b_bf16) + d(a2, b_bf16)


def _mm_exact_lhs(a_bf16, b):
    b1 = b.astype(BF16)
    r1 = b - b1.astype(F32)
    b2 = r1.astype(BF16)
    b3 = (r1 - b2.astype(F32)).astype(BF16)
    d = functools.partial(jnp.dot, preferred_element_type=F32)
    return d(a_bf16, b1) + (d(a_bf16, b2) + d(a_bf16, b3))


def _sigmoid(x):
    return 1.0 / (1.0 + jnp.exp(-x))


def _layer_norm(x, g, b):
    mu = jnp.mean(x, axis=-1, keepdims=True)
    xc = x - mu
    var = jnp.mean(xc * xc, axis=-1, keepdims=True)
    return xc * lax.rsqrt(var + LN_EPS) * g + b


def _rms_norm(x, g):
    return x * lax.rsqrt(jnp.mean(x * x, axis=-1, keepdims=True) + RMS_EPS) * g


def _row_spec(cols, t0=0):
    return pl.BlockSpec((None, TM, cols), lambda b, i, *_: (b, i + t0, 0))


def _mod_spec(n_ct, t0=0):
    return pl.BlockSpec((None, SUBLANES, D_MODEL),
                        lambda b, i, *_: (2 * b + (i + t0 >= n_ct).astype(jnp.int32), 0, 0))


def _full_spec(shape, single_buffer=False):
    nd = len(shape)
    mode = dict(pipeline_mode=pl.Buffered(1)) if single_buffer else {}
    return pl.BlockSpec(shape, lambda *_: (0,) * nd, **mode)


def _halo_specs(cols, tt, t0=0):
    per = TM // HALO
    last = tt // HALO - 1
    prev = pl.BlockSpec((None, HALO, cols), lambda b, i, *_: (b, jnp.maximum((i + t0) * per - 1, 0), 0))
    nxt = pl.BlockSpec((None, HALO, cols), lambda b, i, *_: (b, jnp.minimum((i + t0 + 1) * per, last), 0))
    return prev, nxt


def _with_halo(x_ref, xp_ref, xn_ref, scale, shift, i, n_ct, nt):
    has_prev = (i != 0) & (i != n_ct)
    has_next = (i != n_ct - 1) & (i != nt - 1)
    mod = lambda x: x * scale + shift
    return jnp.concatenate([jnp.where(has_prev, mod(xp_ref[...]), 0.0), mod(x_ref[...]),
                            jnp.where(has_next, mod(xn_ref[...]), 0.0)], axis=0).astype(BF16)


class _Pick:
    def __init__(self, pred, a, b):
        self.pred, self.a, self.b = pred, a, b

    def __getitem__(self, idx):
        return jnp.where(self.pred, self.a[idx], self.b[idx])


def _split_specs(cols, n_ct, n_ctx, n_lat, halo):
    per = TM // HALO
    out = []
    for first, rows in ((0, n_ctx), (n_ct, n_lat)):
        last_t, last_h = rows // TM - 1, rows // HALO - 1
        t = lambda i, first=first, last_t=last_t: jnp.clip(i - first, 0, last_t)
        out.append(pl.BlockSpec((None, TM, cols), lambda b, i, t=t: (b, t(i), 0)))
        if halo:
            out.append(pl.BlockSpec((None, HALO, cols),
                                    lambda b, i, first=first, last_h=last_h: (b, jnp.clip((i - first) * per - 1, 0, last_h), 0)))
            out.append(pl.BlockSpec((None, HALO, cols),
                                    lambda b, i, first=first, last_h=last_h: (b, jnp.clip((i - first + 1) * per, 0, last_h), 0)))
    return out


def _conv3(h, taps):
    rows = TM + 2 * HALO
    lo, hi = HALO, HALO + TM
    return (pltpu.roll(h, 1, 0)[lo:hi] * taps[0:1] + h[lo:hi] * taps[1:2]
            + pltpu.roll(h, rows - 1, 0)[lo:hi] * taps[2:3])


ADA_COLS = 1536


def _ada_kernel(c_ref, w_ref, b_ref, o_ref):
    c = c_ref[...]
    o_ref[...] = _mm3(c * _sigmoid(c), w_ref[...]) + b_ref[...]


def _ada_call(cc, w_ada, b_ada):
    n_l = w_ada.shape[0]
    rows = cc.shape[0]
    return pl.pallas_call(
        _ada_kernel,
        grid=(n_l, 6 * D_MODEL // ADA_COLS),
        in_specs=[
            pl.BlockSpec((rows, D_MODEL), lambda l, j: (0, 0)),
            pl.BlockSpec((None, D_MODEL, ADA_COLS), lambda l, j: (l, 0, j)),
            pl.BlockSpec((None, 1, ADA_COLS), lambda l, j: (l, 0, j)),
        ],
        out_specs=pl.BlockSpec((None, rows, ADA_COLS), lambda l, j: (l, 0, j)),
        out_shape=jax.ShapeDtypeStruct((n_l, rows, 6 * D_MODEL), F32),
        compiler_params=_cparams(("arbitrary", "arbitrary")),
        name="ada",
    )(cc, w_ada, b_ada.reshape(n_l, 1, 6 * D_MODEL))


def _seg_sum(x, ones_bd):
    w = ones_bd.shape[0]
    return jnp.concatenate([_mm_exact_rhs(x[:, o:o + w], ones_bd) for o in range(0, x.shape[1], w)], axis=1)


def _front_kernel(x_ref, xp_ref, xn_ref, mod_ref, w_ref, conv_ref, w0_ref, wb_ref, a0_ref, ab_ref, gb_ref, vec_ref,
                  ones_ref, cos_ref, sin_ref, ng_ref, wq_ref, wkn_ref, wv_ref,
                  r_ref, v_ref, kk_ref, g_ref, bv_ref, lw_ref, kd_ref, bb_ref, q_ref, k_ref, vm_ref, km_ref, *, n_ct, nt):
    i = pl.program_id(1)
    m = mod_ref[...]
    u = _with_halo(x_ref, xp_ref, xn_ref, 1.0 + m[1:2], m[0:1], i, n_ct, nt)
    d = functools.partial(jnp.dot, preferred_element_type=F32)
    c = D_RWKV
    o_lora = 3 * c
    proj = lambda o, n: d(u, w_ref[:, o:o + n])
    conv = lambda p, o, n: _conv3(p, conv_ref[:, o:o + n])
    vec = vec_ref[...]
    k_k, k_a, r_k = vec[0:1], vec[1:2], vec[2:3]
    ones_bd = ones_ref[...]

    pm = proj(RWKV_COLS, w_ref.shape[1] - RWKV_COLS)[HALO:HALO + TM]
    p_lo = proj(o_lora, RWKV_COLS - o_lora)
    cq, ckv, kr_x = pm[:, :Q_LORA], pm[:, Q_LORA:Q_LORA + KV_LORA], pm[:, Q_LORA + KV_LORA:]
    cqn = _rms_norm(cq, ng_ref[0:1, :Q_LORA]).astype(BF16)
    ckvn = _rms_norm(ckv, ng_ref[1:2, :KV_LORA]).astype(BF16)

    p_k = proj(c, c)
    z_lo = conv(p_lo, o_lora, RWKV_COLS - o_lora)
    w_lo, a_lo, g_lo = z_lo[:, :2 * LORA_W], z_lo[:, 2 * LORA_W:2 * LORA_W + 2 * LORA_A], z_lo[:, 2 * LORA_W + 2 * LORA_A:]
    t = -(w0_ref[0:1, :] + _mm(jnp.tanh(w_lo), wb_ref[...]))
    sp = jnp.maximum(t, 0.0) + jnp.log(1.0 + jnp.exp(-jnp.abs(t)))
    lw = -jnp.exp(-sp - 0.5)
    lw_ref[0] = lw[:, :c]
    lw_ref[1] = lw[:, c:]
    a = _sigmoid(a0_ref[0:1, :] + _mm(a_lo, ab_ref[...]))
    g_ref[...] = _mm(_sigmoid(g_lo), gb_ref[...]).astype(BF16)

    q = d(cqn, wq_ref[...])
    p_r = proj(0, c)
    k = conv(p_k, c, c)
    kk = k * k_k
    nrm = jnp.sqrt(_seg_sum(kk * kk, ones_bd))
    kk = kk / jnp.maximum(nrm, 1e-12)
    kd0 = k * (1.0 + (a[:, :c] - 1.0) * k_a)
    kd1 = k * (1.0 + (a[:, c:] - 1.0) * k_a)
    kk_ref[...] = kk.astype(BF16)
    kd_ref[0] = kd0.astype(BF16)
    kd_ref[1] = kd1.astype(BF16)
    bb_ref[0] = (kk * a[:, :c]).astype(BF16)
    bb_ref[1] = (kk * a[:, c:]).astype(BF16)

    k_nope = d(ckvn, wkn_ref[...])
    vt = d(ckvn, wv_ref[...])
    p_v = proj(2 * c, c)

    def rotary(x, cos, sin_signed):
        n = x.shape[1]
        even = (lax.broadcasted_iota(jnp.int32, x.shape, 1) & 1) == 0
        partner = jnp.where(even, pltpu.roll(x, n - 1, 1), pltpu.roll(x, 1, 1))
        return x * cos + partner * sin_signed

    cos, sin = cos_ref[...], sin_ref[...]
    cos8 = jnp.concatenate([cos] * N_HEADS, axis=1)
    sin8 = jnp.concatenate([sin] * N_HEADS, axis=1)
    q_ref[...] = (rotary(q, cos8, sin8) * (QK_DIM ** -0.5 * LOG2_E)).astype(BF16)
    kr = rotary(kr_x, cos, sin)
    k_bf = (k_nope + jnp.concatenate([kr] * N_HEADS, axis=1)).astype(BF16)
    k_ref[...] = k_bf
    k2 = k_bf.astype(F32) * k_bf.astype(F32)
    km = [jnp.max(jnp.sum(k2[:, h * HEAD_PAD:(h + 1) * HEAD_PAD], axis=1, keepdims=True), axis=0, keepdims=True)
          for h in range(N_HEADS)]
    km_ref[...] = jnp.concatenate([jnp.broadcast_to(x, (SUBLANES, HEAD_PAD)) for x in km], axis=1)

    r = conv(p_r, 0, c)
    r_ref[...] = r.astype(BF16)
    v = conv(p_v, 2 * c, c)
    v_ref[...] = v.astype(BF16)
    bv_ref[...] = (_seg_sum(r * (kd0 + kd1) * r_k, ones_bd) * v).astype(BF16)

    vt = vt.T
    extra = jnp.where(lax.broadcasted_iota(jnp.int32, (VT_ROWS - V_HEAD, TM), 0) == 0, 1.0, 0.0)
    vm_ref[...] = jnp.concatenate(
        [blk for h in range(N_HEADS) for blk in (vt[h * V_HEAD:(h + 1) * V_HEAD], extra)], axis=0).astype(BF16)


def _front_split_kernel(c_ref, cp_ref, cn_ref, x_ref, xp_ref, xn_ref, *rest, n_ct, nt):
    is_ctx = pl.program_id(1) < n_ct
    _front_kernel(_Pick(is_ctx, c_ref, x_ref), _Pick(is_ctx, cp_ref, xp_ref), _Pick(is_ctx, cn_ref, xn_ref), *rest,
                  n_ct=n_ct, nt=nt)


def _front_call(xs, mod, cos_t, sin_t, prm, n_ct):
    if isinstance(xs, tuple):
        ctx, x = xs
        bsz, tt = x.shape[0], ctx.shape[1] + x.shape[1]
        body, srcs = _front_split_kernel, (ctx, ctx, ctx, x, x, x)
        src_specs = _split_specs(D_MODEL, n_ct, ctx.shape[1], x.shape[1], halo=True)
    else:
        bsz, tt, _ = xs.shape
        body, srcs = _front_kernel, (xs, xs, xs)
        src_specs = [_row_spec(D_MODEL), *_halo_specs(D_MODEL, tt)]
    nt = tt // TM
    tab = pl.BlockSpec((TM, LANES), lambda b, i: (i, 0))
    shared = jax.ShapeDtypeStruct((bsz, tt, D_RWKV), BF16)
    dir_spec = pl.BlockSpec((2, None, TM, D_RWKV), lambda b, i: (0, b, i, 0))
    dir_shape = lambda dt: jax.ShapeDtypeStruct((2, bsz, tt, D_RWKV), dt)
    heads = jax.ShapeDtypeStruct((bsz, tt, D_HEADS), BF16)
    p_rwkv = [prm[n] for n in ("w_in", "conv", "w0", "wb", "a0", "ab", "gb", "vec", "ones_bd")]
    p_mla = [prm[n] for n in ("norm_g", "wq", "wkn", "wv")]
    return pl.pallas_call(
        functools.partial(body, n_ct=n_ct, nt=nt),
        grid=(bsz, nt),
        in_specs=src_specs + [_mod_spec(n_ct)] + [_full_spec(a.shape) for a in p_rwkv]
        + [tab, tab] + [_full_spec(a.shape) for a in p_mla],
        out_specs=[_row_spec(D_RWKV)] * 5 + [dir_spec] * 3 + [_row_spec(D_HEADS)] * 2
        + [pl.BlockSpec((None, N_HEADS * VT_ROWS, TM), lambda b, i: (b, 0, i)),
           pl.BlockSpec((None, None, SUBLANES, D_HEADS), lambda b, i: (b, i, 0, 0))],
        out_shape=[shared] * 5 + [dir_shape(F32), dir_shape(BF16), dir_shape(BF16)] + [heads] * 2
        + [jax.ShapeDtypeStruct((bsz, N_HEADS * VT_ROWS, tt), BF16),
           jax.ShapeDtypeStruct((bsz, nt, SUBLANES, D_HEADS), F32)],
        compiler_params=_cparams(("parallel", "parallel")),
        name="front",
    )(*srcs, mod, *p_rwkv, cos_t, sin_t, *p_mla)


def _wkv_kernel(*refs, nb):
    ins, (yf_ref, yr_ref, s_ref) = refs[:12], refs[12:]
    step = pl.program_id(1)

    @pl.when(step == 0)
    def _():
        s_ref[...] = jnp.zeros_like(s_ref)

    n2 = 2 * CHUNK
    ri = lax.broadcasted_iota(jnp.int32, (n2, n2), 0)
    ci = lax.broadcasted_iota(jnp.int32, (n2, n2), 1)
    same_head = (ri >> LOG_CHUNK) == (ci >> LOG_CHUNK)
    rt, ct = ri & (CHUNK - 1), ci & (CHUNK - 1)
    eye = ri == ci
    ti = lax.broadcasted_iota(jnp.int32, (CHUNK, CHUNK), 0)
    tj = lax.broadcasted_iota(jnp.int32, (CHUNK, CHUNK), 1)
    strict = [same_head & (ct < rt), same_head & (ct > rt)]
    incl = [same_head & (ct <= rt), same_head & (ct >= rt)]
    tri = [jnp.where(tj <= ti, 1.0, 0.0).astype(BF16), jnp.where(tj >= ti, 1.0, 0.0).astype(BF16)]

    def bdiag(x):
        return jnp.where(same_head, jnp.concatenate([x, x], axis=0), 0.0)

    ch = []
    for d in range(2):
        r_ref, v_ref, kk_ref, lw_ref, kd_ref, bb_ref = ins[6 * d:6 * d + 6]
        for n in range(nb):
            lw = lw_ref[n]
            lc = _mm_exact_lhs(tri[d], lw)
            lend = jnp.sum(lw, axis=0, keepdims=True)
            e_neg = jnp.exp(-lc)
            e_end = jnp.exp(lend - lc)
            p_end = jnp.exp(lend)
            kt_all = kk_ref[n].astype(F32) * jnp.exp(lc - lw)
            rt_all = r_ref[n].astype(F32) * jnp.exp(lc)
            kd, bb = kd_ref[n].astype(F32), bb_ref[n].astype(F32)
            kh_all, bh_all = kd * e_neg, bb * e_neg
            kc_all, bc_all = kd * e_end, bb * e_end
            v_all = v_ref[n]
            for p in range(N_PAIRS):
                sl = slice(p * PAIR, (p + 1) * PAIR)
                ch.append(dict(d=d, slot=(d * nb + n) * N_PAIRS + p, kt=bdiag(kt_all[:, sl]), rr=bdiag(rt_all[:, sl]),
                               vv=bdiag(v_all[:, sl]), kh=kh_all[:, sl], bh=bh_all[:, sl], kc=kc_all[:, sl],
                               bc=bc_all[:, sl], p_end=p_end[:, sl]))
    for c in ch:
        gram = _mm_nt(jnp.concatenate([c["kt"], c["rr"]], axis=0),
                      jnp.concatenate([c["bh"]] * 2 + [c["kh"]] * 2, axis=0))
        st, ic = strict[c["d"]], incl[c["d"]]
        c["a_kb"] = jnp.where(st, gram[:n2, :n2], 0.0)
        c["a_kk"] = jnp.where(st, gram[:n2, n2:], 0.0)
        c["a_rb"] = jnp.where(ic, gram[n2:, :n2], 0.0)
        c["a_rk"] = jnp.where(ic, gram[n2:, n2:], 0.0)
        c["t"] = jnp.where(eye, 1.0, 0.0) - jnp.where((rt >> 1) == (ct >> 1), c["a_kb"], 0.0)
    for c in ch:
        c["akk_v"] = _mm(c["a_kk"], c["vv"])
        c["ark_v"] = _mm(c["a_rk"], c["vv"])
        c["kc_v"] = _mm(bdiag(c["kc"]).T, c["vv"])
    for ls in range(1, LOG_CHUNK):
        off = ((rt >> (ls + 1)) == (ct >> (ls + 1))) & ((rt >> ls) != (ct >> ls))
        for c in ch:
            c["nt"] = _mm(jnp.where(off, c["a_kb"], 0.0), c["t"])
        for c in ch:
            c["t"] = c["t"] - _mm(c["t"], c["nt"])
    for c in ch:
        c["uw"] = _mm(c["t"], jnp.concatenate([c["akk_v"], c["kt"]], axis=1))
    for c in ch:
        c["yq"] = jnp.concatenate([c["ark_v"], c["rr"]], axis=1) - _mm(c["a_rb"], c["uw"])
        diag_p = jnp.where(eye, jnp.broadcast_to(c["p_end"], (n2, n2)), 0.0)
        c["nm"] = jnp.concatenate([c["kc_v"], diag_p], axis=1) - _mm(bdiag(c["bc"]).T, c["uw"])
    for c in ch:
        qm_l = jnp.concatenate([c["yq"][:, n2:], c["nm"][:, n2:]], axis=0).astype(BF16)
        s_hi, s_lo = _split2(s_ref[c["slot"]])
        c["qm"] = jnp.dot(qm_l, s_hi, preferred_element_type=F32)
        c["m_lo"] = jnp.dot(qm_l[n2:], s_lo, preferred_element_type=F32)
    ys = []
    for c in ch:
        y_bd = c["qm"][:n2] + c["yq"][:, :n2]
        s_ref[c["slot"]] = (c["qm"][n2:] + c["m_lo"]) + c["nm"][:, :n2]
        ys.append(y_bd[:CHUNK] + y_bd[CHUNK:])
    for d, y_ref in enumerate((yf_ref, yr_ref)):
        for n in range(nb):
            o = (d * nb + n) * N_PAIRS
            y_ref[n] = jnp.concatenate(ys[o:o + N_PAIRS], axis=1).astype(y_ref.dtype)


def _wkv_call(r, v, kk, lw, kd, bb, n_ct):
    bsz, tt, c = r.shape
    nc = tt // CHUNK
    n_cc = n_ct * TM // CHUNK
    nb = WKV_BATCH
    assert bsz % nb == 0

    def rev_chunk(s):
        return jnp.where(s < n_cc, n_cc - 1 - s, nc - 1 - (s - n_cc))

    order = (lambda s: s, rev_chunk)
    specs, args = [], []
    for d in range(2):
        shared = pl.BlockSpec((nb, CHUNK, c), lambda b, s, d=d: (b, order[d](s), 0))
        per_dir = pl.BlockSpec((None, nb, CHUNK, c), lambda b, s, d=d: (d, b, order[d](s), 0))
        specs += [shared, shared, shared, per_dir, per_dir, per_dir]
        args += [r, v, kk, lw, kd, bb]
    out = jax.ShapeDtypeStruct((bsz, tt, c), BF16)
    return pl.pallas_call(
        functools.partial(_wkv_kernel, nb=nb),
        grid=(bsz // nb, nc),
        in_specs=specs,
        out_specs=[specs[0], specs[6]],
        out_shape=[out, out],
        scratch_shapes=[pltpu.VMEM((2 * nb * N_PAIRS, PAIR, PAIR), F32)],
        compiler_params=_cparams(("parallel", "arbitrary")),
        name="wkv_scan",
    )(*args)


def _rows_reduce(x, op, reduce):
    n = x.shape[0]
    while n > SUBLANES and n % 2 == 0:
        n //= 2
        x = op(x[:n], x[n:])
    return reduce(x, axis=0, keepdims=True)


def _attn_kernel(q_ref, k_ref, vt_ref, km_ref, o_ref, *, n_ctx_rows, n_ct, t0):
    i = pl.program_id(2) + t0
    n_keys = k_ref.shape[0]
    kc = ATTN_KEYS if (n_keys - n_ctx_rows) % ATTN_KEYS == 0 else TM
    heads = range(ATTN_HEADS)
    hs = [slice(g * HEAD_PAD, (g + 1) * HEAD_PAD) for g in heads]

    km2 = jnp.max(jnp.max(km_ref[...], axis=0), axis=0, keepdims=True)
    bound = []
    for g in heads:
        qf = q_ref[:, hs[g]].astype(F32)
        q2 = jnp.max(jnp.sum(qf * qf, axis=1, keepdims=True), axis=0, keepdims=True)
        bound.append(jnp.sqrt(q2 * km2[:, g * HEAD_PAD:g * HEAD_PAD + 1]) * BOUND_SLACK)
    bound_ok = functools.reduce(jnp.maximum, [jnp.max(b) for b in bound]) < BOUND_LIMIT

    def finish(acc):
        outs = [acc[g][:V_HEAD] / acc[g][V_HEAD:V_HEAD + 1] for g in heads]
        o_ref[...] = jnp.concatenate(outs, axis=0).T.astype(o_ref.dtype)

    def attend_bounded(n):
        scores = lambda g: _mm_nt(k_ref[0:n, hs[g]], q_ref[:, hs[g]])
        acc = []
        s_next = scores(0)
        for g in heads:
            s = s_next
            if g + 1 < ATTN_HEADS:
                s_next = scores(g + 1)
            acc.append(_mm(vt_ref[g * VT_ROWS:(g + 1) * VT_ROWS, 0:n], jnp.exp2(s - bound[g])))
        finish(acc)

    def attend(chunks):
        qs = [q_ref[:, hs[g]] for g in heads]
        scores = lambda g, c: _mm_nt(k_ref[c[0]:c[0] + c[1], hs[g]], qs[g])
        s_next = [scores(g, chunks[0]) for g in heads]
        m, acc = ([None] * ATTN_HEADS for _ in range(2))
        for n, c in enumerate(chunks):
            s = s_next
            if n + 1 < len(chunks):
                s_next = [scores(g, chunks[n + 1]) for g in heads]
            m_c = [_rows_reduce(s[g], jnp.maximum, jnp.max) for g in heads]
            m_new = [m_c[g] if m[g] is None else jnp.maximum(m[g], m_c[g]) for g in heads]
            p = [jnp.exp2(s[g] - m_new[g]) for g in heads]
            pv = [_mm(vt_ref[g * VT_ROWS:(g + 1) * VT_ROWS, c[0]:c[0] + c[1]], p[g]) for g in heads]
            for g in heads:
                acc[g] = pv[g] if m[g] is None else jnp.exp2(m[g] - m_new[g]) * acc[g] + pv[g]
                m[g] = m_new[g]
        finish(acc)

    ctx_chunk = [(0, n_ctx_rows)]
    lat_chunks = ctx_chunk + [(s, kc) for s in range(n_ctx_rows, n_keys, kc)]
    is_lat = i >= n_ct
    pl.when(is_lat & bound_ok)(lambda: attend_bounded(n_keys))
    pl.when(is_lat & ~bound_ok)(lambda: attend(lat_chunks))
    if t0 < n_ct:
        pl.when(~is_lat & bound_ok)(lambda: attend_bounded(n_ctx_rows))
        pl.when(~is_lat & ~bound_ok)(lambda: attend(ctx_chunk))


def _attn_call(q, k, vt, km, n_ct, t0):
    bsz, tt, _ = q.shape
    nt = tt // TM
    w = ATTN_HEADS * HEAD_PAD
    q_spec = pl.BlockSpec((None, TM, w), lambda b, h, i: (b, i + t0, h))
    k_spec = pl.BlockSpec((None, tt, w), lambda b, h, i: (b, 0, h))
    vt_spec = pl.BlockSpec((None, ATTN_HEADS * VT_ROWS, tt), lambda b, h, i: (b, h, 0))
    o_spec = pl.BlockSpec((None, TM, ATTN_HEADS * V_HEAD), lambda b, h, i: (b, i + t0, h))
    km_spec = pl.BlockSpec((None, nt, SUBLANES, w), lambda b, h, i: (b, 0, 0, h))
    return pl.pallas_call(
        functools.partial(_attn_kernel, n_ctx_rows=n_ct * TM, n_ct=n_ct, t0=t0),
        grid=(bsz, N_HEADS // ATTN_HEADS, nt - t0),
        in_specs=[q_spec, k_spec, vt_spec, km_spec],
        out_specs=o_spec,
        out_shape=jax.ShapeDtypeStruct((bsz, tt, N_HEADS * V_HEAD), BF16),
        compiler_params=_cparams(("parallel", "parallel", "arbitrary")),
        name="attention",
    )(q, k, vt, km)


def _out_kernel(yf_ref, yr_ref, bv_ref, g_ref, o_ref, x_ref, mod_ref, gn_ref, ones_ref, wor_ref, wom_ref, ln_ref,
                out_ref):
    ones_bd = ones_ref[...]
    y = yf_ref[...].astype(F32) + yr_ref[...].astype(F32)
    inv_n = 1.0 / RWKV_HEAD
    yc = y - _seg_sum(y, ones_bd) * inv_n
    var = _seg_sum(yc * yc, ones_bd) * inv_n
    gn = yc * lax.rsqrt(var + GN_EPS) * gn_ref[0:1] + gn_ref[1:2]
    ro = (gn + bv_ref[...]) * g_ref[...]
    m = _mm(ro, wor_ref[...]) + jnp.dot(o_ref[...], wom_ref[...], preferred_element_type=F32)
    mod = mod_ref[...]
    out_ref[...] = _layer_norm(DEEPNORM_ALPHA * x_ref[...] + mod[2:3] * m, ln_ref[0:1], ln_ref[1:2])


def _out_split_kernel(yf_ref, yr_ref, bv_ref, g_ref, o_ref, c_ref, x_ref, *rest, n_ct):
    _out_kernel(yf_ref, yr_ref, bv_ref, g_ref, o_ref, _Pick(pl.program_id(1) < n_ct, c_ref, x_ref), *rest)


def _out_call(yf, yr, bv, g, o, xs, mod, prm, n_ct, t0):
    if isinstance(xs, tuple):
        assert t0 == 0
        ctx, x = xs
        bsz, tt = x.shape[0], ctx.shape[1] + x.shape[1]
        body, srcs = functools.partial(_out_split_kernel, n_ct=n_ct), (ctx, x)
        src_specs = _split_specs(D_MODEL, n_ct, ctx.shape[1], x.shape[1], halo=False)
    else:
        bsz, tt, _ = xs.shape
        body, srcs, src_specs = _out_kernel, (xs,), [_row_spec(D_MODEL, t0)]
    nt = tt // TM
    params = [prm[n] for n in ("gn", "ones_bd", "wo_r", "wo_m", "ln1")]
    return pl.pallas_call(
        body,
        grid=(bsz, nt - t0),
        in_specs=[_row_spec(D_RWKV, t0)] * 4 + [_row_spec(N_HEADS * V_HEAD, t0)] + src_specs + [_mod_spec(n_ct, t0)]
        + [_full_spec(a.shape) for a in params],
        out_specs=_row_spec(D_MODEL),
        out_shape=jax.ShapeDtypeStruct((bsz, tt - t0 * TM, D_MODEL), F32),
        compiler_params=_cparams(("parallel", "parallel")),
        name="mixer_out",
    )(yf, yr, bv, g, o, *srcs, mod, *params)


def _ffn_kernel(x_ref, xp_ref, xn_ref, mod_ref, wup_ref, conv_ref, wd_ref, ln_ref, out_ref, *, n_ct, nt):
    i = pl.program_id(1)
    mod = mod_ref[...]
    u = _with_halo(x_ref, xp_ref, xn_ref, 1.0 + mod[4:5], mod[3:4], i, n_ct, nt)

    def up(j):
        hs = []
        for o in (j * FF_SUB, D_FF + j * FF_SUB):
            h = jnp.dot(u, wup_ref[:, o:o + FF_SUB], preferred_element_type=F32)
            taps = conv_ref[:, o:o + FF_SUB]
            hs.append(_conv3(h, taps) + taps[3:4])
        return hs

    acc = None
    acts = []
    h_next = up(0)
    for j in range(N_FF_SUB):
        hg, hv = h_next
        if j + 1 < N_FF_SUB:
            h_next = up(j + 1)
        half = 0.5 * hg
        acts.append(((half + half * jnp.tanh(half)) * hv).astype(BF16))
        if len(acts) == FF_DOWN or j + 1 == N_FF_SUB:
            lo = (j + 1 - len(acts)) * FF_SUB
            part = jnp.dot(jnp.concatenate(acts, axis=1), wd_ref[lo:(j + 1) * FF_SUB, :], preferred_element_type=F32)
            acc = part if acc is None else acc + part
            acts = []
    out_ref[...] = _layer_norm(DEEPNORM_ALPHA * x_ref[...] + mod[5:6] * acc, ln_ref[0:1], ln_ref[1:2])


def _ffn_call(xs, mod, prm, n_ct):
    bsz, tt, _ = xs.shape
    nt = tt // TM
    prev, nxt = _halo_specs(D_MODEL, tt)
    w_up, conv, w_down, ln2 = prm["w_up"], prm["ffn_conv"], prm["w_down"], prm["ln2"]
    return pl.pallas_call(
        functools.partial(_ffn_kernel, n_ct=n_ct, nt=nt),
        grid=(bsz, nt),
        in_specs=[_row_spec(D_MODEL), prev, nxt, _mod_spec(n_ct), _full_spec(w_up.shape, True),
                  _full_spec(conv.shape), _full_spec(w_down.shape, True), _full_spec(ln2.shape)],
        out_specs=_row_spec(D_MODEL),
        out_shape=jax.ShapeDtypeStruct((bsz, tt, D_MODEL), F32),
        compiler_params=_cparams(("parallel", "parallel")),
        name="conv_ffn",
    )(xs, xs, xs, mod, w_up, conv, w_down, ln2)


def _pad_rows(rows, width):
    rows = [r.astype(F32) if r.shape[0] == width else jnp.pad(r.astype(F32), (0, width - r.shape[0])) for r in rows]
    return jnp.pad(jnp.stack(rows), ((0, SUBLANES - len(rows)), (0, 0)))


def _layer_params(l, w_in, rwkv_conv, w0, w_b, a0, a_b, g_b, k_k, k_a, r_k, gn_g, gn_b, q_norm_g, w_uq,
                  kv_norm_g, w_ukv, w_o, ln1_g, ln1_b, w_up, ffn_conv_w, ffn_conv_b, w_down, ln2_g, ln2_b):
    c = D_RWKV
    wi = w_in[l]
    o_mla = RWKV_COLS
    w_kr = wi[:, o_mla + Q_LORA + KV_LORA:]
    pad_l = jnp.zeros((D_MODEL, QK_NOPE), F32)
    pad_r = jnp.zeros((D_MODEL, LANES - QK_DIM), F32)
    w_in_p = jnp.concatenate(
        [wi[:, :o_mla + Q_LORA + KV_LORA], pad_l, w_kr, pad_r], axis=1).astype(BF16)

    def lora_bd(w):
        z = jnp.zeros_like(w[0])
        return jnp.concatenate([jnp.concatenate([w[0], z], 1), jnp.concatenate([z, w[1]], 1)], 0).astype(BF16)

    head = jnp.arange(2 * LANES) // RWKV_HEAD
    ones_bd = (head[:, None] == head[None, :]).astype(BF16)

    wq = w_uq[l].reshape(Q_LORA, N_HEADS, QK_DIM)
    q_nope, q_rope = wq[..., :QK_NOPE], wq[..., QK_NOPE:]
    zq = lambda n: jnp.zeros((Q_LORA, N_HEADS, n), F32)
    wq = jnp.concatenate([q_nope, q_rope, zq(LANES - QK_DIM)], -1).reshape(Q_LORA, D_HEADS).astype(BF16)
    wkv = w_ukv[l].reshape(KV_LORA, N_HEADS, QK_NOPE + V_HEAD)
    zk = jnp.zeros((KV_LORA, N_HEADS, HEAD_PAD - QK_NOPE), F32)
    wkn = jnp.concatenate([wkv[..., :QK_NOPE], zk], -1).reshape(KV_LORA, D_HEADS).astype(BF16)
    wv = wkv[..., QK_NOPE:].reshape(KV_LORA, N_HEADS * V_HEAD).astype(BF16)
    wo = w_o[l]

    return {
        "w_in": w_in_p,
        "conv": _pad_rows(list(rwkv_conv[l]), RWKV_COLS),
        "w0": _pad_rows([w0[l].reshape(-1)], 2 * c),
        "wb": lora_bd(w_b[l]),
        "a0": _pad_rows([a0[l].reshape(-1)], 2 * c),
        "ab": lora_bd(a_b[l]),
        "gb": g_b[l].astype(BF16),
        "vec": _pad_rows([k_k[l], k_a[l], r_k[l].reshape(-1)], c),
        "ones_bd": ones_bd,
        "norm_g": _pad_rows([q_norm_g[l], kv_norm_g[l]], Q_LORA),
        "wq": wq, "wkn": wkn, "wv": wv,
        "gn": _pad_rows([gn_g[l], gn_b[l]], c),
        "wo_r": wo[:c].astype(BF16),
        "wo_m": wo[c:].astype(BF16),
        "ln1": _pad_rows([ln1_g[l], ln1_b[l]], D_MODEL),
        "w_up": w_up[l].astype(BF16),
        "ffn_conv": _pad_rows(list(ffn_conv_w[l]) + [ffn_conv_b[l]], 2 * D_FF),
        "w_down": w_down[l].astype(BF16),
        "ln2": _pad_rows([ln2_g[l], ln2_b[l]], D_MODEL),
    }


def _rope_tables(n_ctx, n_lat):
    rows = n_lat // GRID_W
    row = jnp.repeat(jnp.arange(rows, dtype=F32), GRID_W)
    col = jnp.tile(jnp.arange(GRID_W, dtype=F32), rows)
    n_pairs = QK_ROPE // 4
    inv = ROPE_BASE ** (-jnp.arange(n_pairs, dtype=F32) / n_pairs)
    ang = jnp.concatenate([row[:, None] * inv, col[:, None] * inv], axis=-1)
    cos = jnp.repeat(jnp.cos(ang), 2, axis=-1)
    sin = jnp.repeat(jnp.sin(ang), 2, axis=-1) * jnp.tile(jnp.array([-1.0, 1.0], F32), QK_ROPE // 2)
    tt = n_ctx + n_lat
    cos_t = jnp.ones((tt, LANES), F32).at[n_ctx:, QK_NOPE:QK_DIM].set(cos)
    sin_t = jnp.zeros((tt, LANES), F32).at[n_ctx:, QK_NOPE:QK_DIM].set(sin)
    return cos_t, sin_t


def kernel(x, c, ctx, c_ctx, w_ada, b_ada, w_in, rwkv_conv, w0, w_b, a0, a_b, g_b, k_k, k_a, r_k, gn_g, gn_b,
           q_norm_g, w_uq, kv_norm_g, w_ukv, w_o, ln1_g, ln1_b, w_up, ffn_conv_w, ffn_conv_b, w_down, ln2_g, ln2_b):
    bsz, n_lat, _ = x.shape
    n_ctx = ctx.shape[1]
    assert n_ctx % TM == 0 and n_lat % TM == 0 and n_lat % GRID_W == 0
    n_ct = n_ctx // TM
    n_l = w_ada.shape[0]

    ada_rows = -(-(bsz + 1) // SUBLANES) * SUBLANES
    cc = jnp.zeros((ada_rows, D_MODEL), F32).at[:bsz].set(c).at[bsz].set(c_ctx)
    mods = _ada_call(cc, w_ada, b_ada).reshape(n_l, ada_rows, 6, D_MODEL)
    cos_t, sin_t = _rope_tables(n_ctx, n_lat)
    xs = (ctx, x)

    for l in range(n_l):
        prm = _layer_params(l, w_in, rwkv_conv, w0, w_b, a0, a_b, g_b, k_k, k_a, r_k, gn_g, gn_b, q_norm_g, w_uq,
                            kv_norm_g, w_ukv, w_o, ln1_g, ln1_b, w_up, ffn_conv_w, ffn_conv_b, w_down, ln2_g, ln2_b)
        m_ctx = jnp.broadcast_to(mods[l, bsz][None], (bsz, 6, D_MODEL))
        mod = jnp.stack([m_ctx, mods[l, :bsz]], axis=1)
        mod = jnp.pad(mod, ((0, 0), (0, 0), (0, SUBLANES - 6), (0, 0))).reshape(2 * bsz, SUBLANES, D_MODEL)

        last = l == n_l - 1
        t0 = n_ct if last else 0
        if last and isinstance(xs, tuple):
            xs = jnp.concatenate(xs, axis=1)
        r, v, kk, g, bv, lw, kd, bb, q, k, vm, km = _front_call(xs, mod, cos_t, sin_t, prm, n_ct)
        yf, yr = _wkv_call(r, v, kk, lw, kd, bb, n_ct)
        o = _attn_call(q, k, vm, km, n_ct, t0)
        x1 = _out_call(yf, yr, bv, g, o, xs, mod, prm, n_ct, t0)
        xs = _ffn_call(x1, mod, prm, n_ct - t0)
    return xs
```

```python
import functools
import math

import jax
import jax.numpy as jnp
from jax import lax
from jax.experimental import pallas as pl
from jax.experimental.pallas import tpu as pltpu

F32 = jnp.float32
BF16 = jnp.bfloat16

D_MODEL = 1024
DEPTH = 2
GRID_W = 64
D_RWKV = 512
RWKV_HEAD = 64
N_HEADS = 8
LORA_W = 64
LORA_A = 64
LORA_G = 128
RWKV_COLS = 3 * D_RWKV + 2 * LORA_W + 2 * LORA_A + LORA_G
QK_NOPE = 64
QK_ROPE = 32
QK_DIM = QK_NOPE + QK_ROPE
V_HEAD = 64
Q_LORA = 384
KV_LORA = 256
D_FF = 2816
ROPE_BASE = 10000.0
DEEPNORM_ALPHA = (2 * DEPTH) ** 0.25
LN_EPS = 1e-5
RMS_EPS = 1e-6
GN_EPS = 64e-5
LOG2_E = 1.4426950408889634

LANES = 128
SUBLANES = 8
HEAD_PAD = LANES
D_HEADS = N_HEADS * HEAD_PAD
TM = 256
HALO = SUBLANES
CHUNK = 64
LOG_CHUNK = 6
WKV_BATCH = 4
PAIR = 2 * RWKV_HEAD
N_PAIRS = N_HEADS // 2
VT_ROWS = 80
ATTN_KEYS = 1024
BOUND_LIMIT = 60.0
BOUND_SLACK = 1.01
ATTN_HEADS = 4
FF_SUB = 256
N_FF_SUB = D_FF // FF_SUB
FF_DOWN = 6
VMEM_LIMIT = 56 << 20


def _cparams(sem):
    return pltpu.CompilerParams(dimension_semantics=sem, vmem_limit_bytes=VMEM_LIMIT)


def _mm(a, b):
    return jnp.dot(a.astype(BF16), b.astype(BF16), preferred_element_type=F32)


def _mm_nt(a, b):
    return lax.dot_general(a.astype(BF16), b.astype(BF16), (((1,), (1,)), ((), ())),
                           preferred_element_type=F32)


def _split2(x):
    hi = x.astype(BF16)
    lo = (x - hi.astype(F32)).astype(BF16)
    return hi, lo


def _mm3(a, b):
    ah, al = _split2(a)
    bh, bl = _split2(b)
    d = functools.partial(jnp.dot, preferred_element_type=F32)
    return d(ah, bh) + (d(ah, bl) + d(al, bh))


def _mm_exact_rhs(a, b_bf16):
    a1, a2 = _split2(a)
    d = functools.partial(jnp.dot, preferred_element_type=F32)
    return d(a1, b_bf16) + d(a2, b_bf16)


def _mm_exact_lhs(a_bf16, b):
    b1 = b.astype(BF16)
    r1 = b - b1.astype(F32)
    b2 = r1.astype(BF16)
    b3 = (r1 - b2.astype(F32)).astype(BF16)
    d = functools.partial(jnp.dot, preferred_element_type=F32)
    return d(a_bf16, b1) + (d(a_bf16, b2) + d(a_bf16, b3))


def _sigmoid(x):
    return 1.0 / (1.0 + jnp.exp(-x))


def _layer_norm(x, g, b):
    mu = jnp.mean(x, axis=-1, keepdims=True)
    xc = x - mu
    var = jnp.mean(xc * xc, axis=-1, keepdims=True)
    return xc * lax.rsqrt(var + LN_EPS) * g + b


def _rms_norm(x, g):
    return x * lax.rsqrt(jnp.mean(x * x, axis=-1, keepdims=True) + RMS_EPS) * g


def _row_spec(cols, t0=0):
    return pl.BlockSpec((None, TM, cols), lambda b, i, *_: (b, i + t0, 0))


def _mod_spec(n_ct, t0=0):
    return pl.BlockSpec((None, SUBLANES, D_MODEL),
                        lambda b, i, *_: (2 * b + (i + t0 >= n_ct).astype(jnp.int32), 0, 0))


def _full_spec(shape, single_buffer=False):
    nd = len(shape)
    mode = dict(pipeline_mode=pl.Buffered(1)) if single_buffer else {}
    return pl.BlockSpec(shape, lambda *_: (0,) * nd, **mode)


def _halo_specs(cols, tt, t0=0):
    per = TM // HALO
    last = tt // HALO - 1
    prev = pl.BlockSpec((None, HALO, cols), lambda b, i, *_: (b, jnp.maximum((i + t0) * per - 1, 0), 0))
    nxt = pl.BlockSpec((None, HALO, cols), lambda b, i, *_: (b, jnp.minimum((i + t0 + 1) * per, last), 0))
    return prev, nxt


def _with_halo(x_ref, xp_ref, xn_ref, scale, shift, i, n_ct, nt):
    has_prev = (i != 0) & (i != n_ct)
    has_next = (i != n_ct - 1) & (i != nt - 1)
    mod = lambda x: x * scale + shift
    return jnp.concatenate([jnp.where(has_prev, mod(xp_ref[...]), 0.0), mod(x_ref[...]),
                            jnp.where(has_next, mod(xn_ref[...]), 0.0)], axis=0).astype(BF16)


class _Pick:
    def __init__(self, pred, a, b):
        self.pred, self.a, self.b = pred, a, b

    def __getitem__(self, idx):
        return jnp.where(self.pred, self.a[idx], self.b[idx])


def _split_specs(cols, n_ct, n_ctx, n_lat, halo):
    per = TM // HALO
    out = []
    for first, rows in ((0, n_ctx), (n_ct, n_lat)):
        last_t, last_h = rows // TM - 1, rows // HALO - 1
        t = lambda i, first=first, last_t=last_t: jnp.clip(i - first, 0, last_t)
        out.append(pl.BlockSpec((None, TM, cols), lambda b, i, t=t: (b, t(i), 0)))
        if halo:
            out.append(pl.BlockSpec((None, HALO, cols),
                                    lambda b, i, first=first, last_h=last_h: (b, jnp.clip((i - first) * per - 1, 0, last_h), 0)))
            out.append(pl.BlockSpec((None, HALO, cols),
                                    lambda b, i, first=first, last_h=last_h: (b, jnp.clip((i - first + 1) * per, 0, last_h), 0)))
    return out


def _conv3(h, taps):
    rows = TM + 2 * HALO
    lo, hi = HALO, HALO + TM
    return (pltpu.roll(h, 1, 0)[lo:hi] * taps[0:1] + h[lo:hi] * taps[1:2]
            + pltpu.roll(h, rows - 1, 0)[lo:hi] * taps[2:3])


ADA_COLS = 1536


def _ada_kernel(c_ref, w_ref, b_ref, o_ref):
    c = c_ref[...]
    o_ref[...] = _mm3(c * _sigmoid(c), w_ref[...]) + b_ref[...]


def _ada_call(cc, w_ada, b_ada):
    n_l = w_ada.shape[0]
    rows = cc.shape[0]
    return pl.pallas_call(
        _ada_kernel,
        grid=(n_l, 6 * D_MODEL // ADA_COLS),
        in_specs=[
            pl.BlockSpec((rows, D_MODEL), lambda l, j: (0, 0)),
            pl.BlockSpec((None, D_MODEL, ADA_COLS), lambda l, j: (l, 0, j)),
            pl.BlockSpec((None, 1, ADA_COLS), lambda l, j: (l, 0, j)),
        ],
        out_specs=pl.BlockSpec((None, rows, ADA_COLS), lambda l, j: (l, 0, j)),
        out_shape=jax.ShapeDtypeStruct((n_l, rows, 6 * D_MODEL), F32),
        compiler_params=_cparams(("arbitrary", "arbitrary")),
        name="ada",
    )(cc, w_ada, b_ada.reshape(n_l, 1, 6 * D_MODEL))


def _seg_sum(x, ones_bd):
    w = ones_bd.shape[0]
    return jnp.concatenate([_mm_exact_rhs(x[:, o:o + w], ones_bd) for o in range(0, x.shape[1], w)], axis=1)


def _front_kernel(x_ref, xp_ref, xn_ref, mod_ref, w_ref, conv_ref, w0_ref, wb_ref, a0_ref, ab_ref, gb_ref, vec_ref,
                  ones_ref, cos_ref, sin_ref, ng_ref, wq_ref, wkn_ref, wv_ref,
                  r_ref, v_ref, kk_ref, g_ref, bv_ref, lw_ref, kd_ref, bb_ref, q_ref, k_ref, vm_ref, km_ref, *, n_ct, nt):
    i = pl.program_id(1)
    m = mod_ref[...]
    u = _with_halo(x_ref, xp_ref, xn_ref, 1.0 + m[1:2], m[0:1], i, n_ct, nt)
    d = functools.partial(jnp.dot, preferred_element_type=F32)
    c = D_RWKV
    o_lora = 3 * c
    proj = lambda o, n: d(u, w_ref[:, o:o + n])
    conv = lambda p, o, n: _conv3(p, conv_ref[:, o:o + n])
    vec = vec_ref[...]
    k_k, k_a, r_k = vec[0:1], vec[1:2], vec[2:3]
    ones_bd = ones_ref[...]

    pm = proj(RWKV_COLS, w_ref.shape[1] - RWKV_COLS)[HALO:HALO + TM]
    p_lo = proj(o_lora, RWKV_COLS - o_lora)
    cq, ckv, kr_x = pm[:, :Q_LORA], pm[:, Q_LORA:Q_LORA + KV_LORA], pm[:, Q_LORA + KV_LORA:]
    cqn = _rms_norm(cq, ng_ref[0:1, :Q_LORA]).astype(BF16)
    ckvn = _rms_norm(ckv, ng_ref[1:2, :KV_LORA]).astype(BF16)

    p_k = proj(c, c)
    z_lo = conv(p_lo, o_lora, RWKV_COLS - o_lora)
    w_lo, a_lo, g_lo = z_lo[:, :2 * LORA_W], z_lo[:, 2 * LORA_W:2 * LORA_W + 2 * LORA_A], z_lo[:, 2 * LORA_W + 2 * LORA_A:]
    t = -(w0_ref[0:1, :] + _mm(jnp.tanh(w_lo), wb_ref[...]))
    sp = jnp.maximum(t, 0.0) + jnp.log(1.0 + jnp.exp(-jnp.abs(t)))
    lw = -jnp.exp(-sp - 0.5)
    lw_ref[0] = lw[:, :c]
    lw_ref[1] = lw[:, c:]
    a = _sigmoid(a0_ref[0:1, :] + _mm(a_lo, ab_ref[...]))
    g_ref[...] = _mm(_sigmoid(g_lo), gb_ref[...]).astype(BF16)

    q = d(cqn, wq_ref[...])
    p_r = proj(0, c)
    k = conv(p_k, c, c)
    kk = k * k_k
    nrm = jnp.sqrt(_seg_sum(kk * kk, ones_bd))
    kk = kk / jnp.maximum(nrm, 1e-12)
    kd0 = k * (1.0 + (a[:, :c] - 1.0) * k_a)
    kd1 = k * (1.0 + (a[:, c:] - 1.0) * k_a)
    kk_ref[...] = kk.astype(BF16)
    kd_ref[0] = kd0.astype(BF16)
    kd_ref[1] = kd1.astype(BF16)
    bb_ref[0] = (kk * a[:, :c]).astype(BF16)
    bb_ref[1] = (kk * a[:, c:]).astype(BF16)

    k_nope = d(ckvn, wkn_ref[...])
    vt = d(ckvn, wv_ref[...])
    p_v = proj(2 * c, c)

    def rotary(x, cos, sin_signed):
        n = x.shape[1]
        even = (lax.broadcasted_iota(jnp.int32, x.shape, 1) & 1) == 0
        partner = jnp.where(even, pltpu.roll(x, n - 1, 1), pltpu.roll(x, 1, 1))
        return x * cos + partner * sin_signed

    cos, sin = cos_ref[...], sin_ref[...]
    cos8 = jnp.concatenate([cos] * N_HEADS, axis=1)
    sin8 = jnp.concatenate([sin] * N_HEADS, axis=1)
    q_ref[...] = (rotary(q, cos8, sin8) * (QK_DIM ** -0.5 * LOG2_E)).astype(BF16)
    kr = rotary(kr_x, cos, sin)
    k_bf = (k_nope + jnp.concatenate([kr] * N_HEADS, axis=1)).astype(BF16)
    k_ref[...] = k_bf
    k2 = k_bf.astype(F32) * k_bf.astype(F32)
    km = [jnp.max(jnp.sum(k2[:, h * HEAD_PAD:(h + 1) * HEAD_PAD], axis=1, keepdims=True), axis=0, keepdims=True)
          for h in range(N_HEADS)]
    km_ref[...] = jnp.concatenate([jnp.broadcast_to(x, (SUBLANES, HEAD_PAD)) for x in km], axis=1)

    r = conv(p_r, 0, c)
    r_ref[...] = r.astype(BF16)
    v = conv(p_v, 2 * c, c)
    v_ref[...] = v.astype(BF16)
    bv_ref[...] = (_seg_sum(r * (kd0 + kd1) * r_k, ones_bd) * v).astype(BF16)

    vt = vt.T
    extra = jnp.where(lax.broadcasted_iota(jnp.int32, (VT_ROWS - V_HEAD, TM), 0) == 0, 1.0, 0.0)
    vm_ref[...] = jnp.concatenate(
        [blk for h in range(N_HEADS) for blk in (vt[h * V_HEAD:(h + 1) * V_HEAD], extra)], axis=0).astype(BF16)


def _front_split_kernel(c_ref, cp_ref, cn_ref, x_ref, xp_ref, xn_ref, *rest, n_ct, nt):
    is_ctx = pl.program_id(1) < n_ct
    _front_kernel(_Pick(is_ctx, c_ref, x_ref), _Pick(is_ctx, cp_ref, xp_ref), _Pick(is_ctx, cn_ref, xn_ref), *rest,
                  n_ct=n_ct, nt=nt)


def _front_call(xs, mod, cos_t, sin_t, prm, n_ct):
    if isinstance(xs, tuple):
        ctx, x = xs
        bsz, tt = x.shape[0], ctx.shape[1] + x.shape[1]
        body, srcs = _front_split_kernel, (ctx, ctx, ctx, x, x, x)
        src_specs = _split_specs(D_MODEL, n_ct, ctx.shape[1], x.shape[1], halo=True)
    else:
        bsz, tt, _ = xs.shape
        body, srcs = _front_kernel, (xs, xs, xs)
        src_specs = [_row_spec(D_MODEL), *_halo_specs(D_MODEL, tt)]
    nt = tt // TM
    tab = pl.BlockSpec((TM, LANES), lambda b, i: (i, 0))
    shared = jax.ShapeDtypeStruct((bsz, tt, D_RWKV), BF16)
    dir_spec = pl.BlockSpec((2, None, TM, D_RWKV), lambda b, i: (0, b, i, 0))
    dir_shape = lambda dt: jax.ShapeDtypeStruct((2, bsz, tt, D_RWKV), dt)
    heads = jax.ShapeDtypeStruct((bsz, tt, D_HEADS), BF16)
    p_rwkv = [prm[n] for n in ("w_in", "conv", "w0", "wb", "a0", "ab", "gb", "vec", "ones_bd")]
    p_mla = [prm[n] for n in ("norm_g", "wq", "wkn", "wv")]
    return pl.pallas_call(
        functools.partial(body, n_ct=n_ct, nt=nt),
        grid=(bsz, nt),
        in_specs=src_specs + [_mod_spec(n_ct)] + [_full_spec(a.shape) for a in p_rwkv]
        + [tab, tab] + [_full_spec(a.shape) for a in p_mla],
        out_specs=[_row_spec(D_RWKV)] * 5 + [dir_spec] * 3 + [_row_spec(D_HEADS)] * 2
        + [pl.BlockSpec((None, N_HEADS * VT_ROWS, TM), lambda b, i: (b, 0, i)),
           pl.BlockSpec((None, None, SUBLANES, D_HEADS), lambda b, i: (b, i, 0, 0))],
        out_shape=[shared] * 5 + [dir_shape(F32), dir_shape(BF16), dir_shape(BF16)] + [heads] * 2
        + [jax.ShapeDtypeStruct((bsz, N_HEADS * VT_ROWS, tt), BF16),
           jax.ShapeDtypeStruct((bsz, nt, SUBLANES, D_HEADS), F32)],
        compiler_params=_cparams(("parallel", "parallel")),
        name="front",
    )(*srcs, mod, *p_rwkv, cos_t, sin_t, *p_mla)


def _wkv_kernel(*refs, nb):
    ins, (yf_ref, yr_ref, s_ref) = refs[:12], refs[12:]
    step = pl.program_id(1)

    @pl.when(step == 0)
    def _():
        s_ref[...] = jnp.zeros_like(s_ref)

    n2 = 2 * CHUNK
    ri = lax.broadcasted_iota(jnp.int32, (n2, n2), 0)
    ci = lax.broadcasted_iota(jnp.int32, (n2, n2), 1)
    same_head = (ri >> LOG_CHUNK) == (ci >> LOG_CHUNK)
    rt = lax.broadcasted_iota(jnp.int32, (CHUNK, n2), 0)
    ct = lax.broadcasted_iota(jnp.int32, (CHUNK, n2), 1) & (CHUNK - 1)
    eye = ct == rt
    ti = lax.broadcasted_iota(jnp.int32, (CHUNK, CHUNK), 0)
    tj = lax.broadcasted_iota(jnp.int32, (CHUNK, CHUNK), 1)
    strict = [ct < rt, ct > rt]
    incl = [ct <= rt, ct >= rt]
    tri = [jnp.where(tj <= ti, 1.0, 0.0).astype(BF16), jnp.where(tj >= ti, 1.0, 0.0).astype(BF16)]

    def bdiag(x):
        return jnp.where(same_head, jnp.concatenate([x, x], axis=0), 0.0)

    def fold_t(x):
        xt = bdiag(x).T
        return xt[:CHUNK] + xt[CHUNK:]

    ch = []
    for d in range(2):
        r_ref, v_ref, kk_ref, lw_ref, kd_ref, bb_ref = ins[6 * d:6 * d + 6]
        for n in range(nb):
            lw = lw_ref[n]
            lc = _mm_exact_lhs(tri[d], lw)
            lend = jnp.sum(lw, axis=0, keepdims=True)
            e_neg = jnp.exp(-lc)
            e_end = jnp.exp(lend - lc)
            p_end = jnp.exp(lend)
            kt_all = kk_ref[n].astype(F32) * jnp.exp(lc - lw)
            rt_all = r_ref[n].astype(F32) * jnp.exp(lc)
            kd, bb = kd_ref[n].astype(F32), bb_ref[n].astype(F32)
            kh_all, bh_all = kd * e_neg, bb * e_neg
            kc_all, bc_all = kd * e_end, bb * e_end
            v_all = v_ref[n]
            for p in range(N_PAIRS):
                sl = slice(p * PAIR, (p + 1) * PAIR)
                ch.append(dict(d=d, slot=(d * nb + n) * N_PAIRS + p, kt=kt_all[:, sl], rr=rt_all[:, sl],
                               vv=bdiag(v_all[:, sl]), kh=bdiag(kh_all[:, sl]), bh=bdiag(bh_all[:, sl]),
                               kc_t=fold_t(kc_all[:, sl]), bc_t=fold_t(bc_all[:, sl]), p_end=p_end[:, sl]))
    for c in ch:
        gram = _mm_nt(jnp.concatenate([c["kt"], c["rr"]], axis=0), jnp.concatenate([c["bh"], c["kh"]], axis=0))
        st, ic = strict[c["d"]], incl[c["d"]]
        c["a_kb"] = jnp.where(st, gram[:CHUNK, :n2], 0.0)
        c["a_kk"] = jnp.where(st, gram[:CHUNK, n2:], 0.0)
        c["a_rb"] = jnp.where(ic, gram[CHUNK:, :n2], 0.0)
        c["a_rk"] = jnp.where(ic, gram[CHUNK:, n2:], 0.0)
        c["t"] = jnp.where(eye, 1.0, 0.0) - jnp.where((rt >> 1) == (ct >> 1), c["a_kb"], 0.0)
    for c in ch:
        av = _mm(jnp.concatenate([c["a_kk"], c["a_rk"]], axis=0), c["vv"])
        c["akk_v"], c["ark_v"] = av[:CHUNK], av[CHUNK:]
        c["kc_v"] = _mm(c["kc_t"], c["vv"])
    for ls in range(1, LOG_CHUNK):
        off = ((rt >> (ls + 1)) == (ct >> (ls + 1))) & ((rt >> ls) != (ct >> ls))
        for c in ch:
            c["nt"] = _mm(jnp.where(off, c["a_kb"], 0.0), bdiag(c["t"]))
        for c in ch:
            c["t"] = c["t"] - _mm(c["t"], bdiag(c["nt"]))
    for c in ch:
        uw = _mm(c["t"], jnp.concatenate([bdiag(c["akk_v"]), bdiag(c["kt"])], axis=1))
        c["uw"] = jnp.concatenate([bdiag(uw[:, :n2]), bdiag(uw[:, n2:])], axis=1)
    for c in ch:
        c["yq"] = jnp.concatenate([c["ark_v"], c["rr"]], axis=1) - _mm(c["a_rb"], c["uw"])
        diag_p = jnp.where(eye, jnp.broadcast_to(c["p_end"], (CHUNK, n2)), 0.0)
        c["nm"] = jnp.concatenate([c["kc_v"], diag_p], axis=1) - _mm(c["bc_t"], c["uw"])
    for c in ch:
        qm_l = jnp.concatenate([c["yq"][:, n2:], c["nm"][:, n2:]], axis=0).astype(BF16)
        s_hi, s_lo = _split2(s_ref[c["slot"]])
        c["qm"] = jnp.dot(qm_l, s_hi, preferred_element_type=F32)
        c["m_lo"] = jnp.dot(qm_l[CHUNK:], s_lo, preferred_element_type=F32)
    ys = []
    for c in ch:
        ys.append(c["qm"][:CHUNK] + c["yq"][:, :n2])
        s_ref[c["slot"]] = bdiag((c["qm"][CHUNK:] + c["m_lo"]) + c["nm"][:, :n2])
    for d, y_ref in enumerate((yf_ref, yr_ref)):
        for n in range(nb):
            o = (d * nb + n) * N_PAIRS
            y_ref[n] = jnp.concatenate(ys[o:o + N_PAIRS], axis=1).astype(y_ref.dtype)


def _wkv_call(r, v, kk, lw, kd, bb, n_ct):
    bsz, tt, c = r.shape
    nc = tt // CHUNK
    n_cc = n_ct * TM // CHUNK
    nb = math.gcd(bsz, WKV_BATCH)

    def rev_chunk(s):
        return jnp.where(s < n_cc, n_cc - 1 - s, nc - 1 - (s - n_cc))

    order = (lambda s: s, rev_chunk)
    specs, args = [], []
    for d in range(2):
        shared = pl.BlockSpec((nb, CHUNK, c), lambda b, s, d=d: (b, order[d](s), 0))
        per_dir = pl.BlockSpec((None, nb, CHUNK, c), lambda b, s, d=d: (d, b, order[d](s), 0))
        specs += [shared, shared, shared, per_dir, per_dir, per_dir]
        args += [r, v, kk, lw, kd, bb]
    out = jax.ShapeDtypeStruct((bsz, tt, c), BF16)
    return pl.pallas_call(
        functools.partial(_wkv_kernel, nb=nb),
        grid=(bsz // nb, nc),
        in_specs=specs,
        out_specs=[specs[0], specs[6]],
        out_shape=[out, out],
        scratch_shapes=[pltpu.VMEM((2 * nb * N_PAIRS, PAIR, PAIR), F32)],
        compiler_params=_cparams(("parallel", "arbitrary")),
        name="wkv_scan",
    )(*args)


def _rows_reduce(x, op, reduce):
    n = x.shape[0]
    while n > SUBLANES and n % 2 == 0:
        n //= 2
        x = op(x[:n], x[n:])
    return reduce(x, axis=0, keepdims=True)


def _attn_kernel(q_ref, k_ref, vt_ref, km_ref, o_ref, *, n_ctx_rows, n_ct, t0):
    i = pl.program_id(2) + t0
    n_keys = k_ref.shape[0]
    kc = ATTN_KEYS if (n_keys - n_ctx_rows) % ATTN_KEYS == 0 else TM
    heads = range(ATTN_HEADS)
    hs = [slice(g * HEAD_PAD, (g + 1) * HEAD_PAD) for g in heads]

    km2 = jnp.max(jnp.max(km_ref[...], axis=0), axis=0, keepdims=True)
    bound = []
    for g in heads:
        qf = q_ref[:, hs[g]].astype(F32)
        q2 = jnp.max(jnp.sum(qf * qf, axis=1, keepdims=True), axis=0, keepdims=True)
        bound.append(jnp.sqrt(q2 * km2[:, g * HEAD_PAD:g * HEAD_PAD + 1]) * BOUND_SLACK)
    bound_ok = functools.reduce(jnp.maximum, [jnp.max(b) for b in bound]) < BOUND_LIMIT

    def finish(acc):
        outs = [acc[g][:V_HEAD] / acc[g][V_HEAD:V_HEAD + 1] for g in heads]
        o_ref[...] = jnp.concatenate(outs, axis=0).T.astype(o_ref.dtype)

    def attend_bounded(n):
        scores = lambda g: _mm_nt(k_ref[0:n, hs[g]], q_ref[:, hs[g]])
        acc = []
        s_next = scores(0)
        for g in heads:
            s = s_next
            if g + 1 < ATTN_HEADS:
                s_next = scores(g + 1)
            acc.append(_mm(vt_ref[g * VT_ROWS:(g + 1) * VT_ROWS, 0:n], jnp.exp2(s - bound[g])))
        finish(acc)

    def attend(chunks):
        qs = [q_ref[:, hs[g]] for g in heads]
        scores = lambda g, c: _mm_nt(k_ref[c[0]:c[0] + c[1], hs[g]], qs[g])
        s_next = [scores(g, chunks[0]) for g in heads]
        m, acc = ([None] * ATTN_HEADS for _ in range(2))
        for n, c in enumerate(chunks):
            s = s_next
            if n + 1 < len(chunks):
                s_next = [scores(g, chunks[n + 1]) for g in heads]
            m_c = [_rows_reduce(s[g], jnp.maximum, jnp.max) for g in heads]
            m_new = [m_c[g] if m[g] is None else jnp.maximum(m[g], m_c[g]) for g in heads]
            p = [jnp.exp2(s[g] - m_new[g]) for g in heads]
            pv = [_mm(vt_ref[g * VT_ROWS:(g + 1) * VT_ROWS, c[0]:c[0] + c[1]], p[g]) for g in heads]
            for g in heads:
                acc[g] = pv[g] if m[g] is None else jnp.exp2(m[g] - m_new[g]) * acc[g] + pv[g]
                m[g] = m_new[g]
        finish(acc)

    ctx_chunk = [(0, n_ctx_rows)]
    lat_chunks = ctx_chunk + [(s, kc) for s in range(n_ctx_rows, n_keys, kc)]
    is_lat = i >= n_ct
    pl.when(is_lat & bound_ok)(lambda: attend_bounded(n_keys))
    pl.when(is_lat & ~bound_ok)(lambda: attend(lat_chunks))
    if t0 < n_ct:
        pl.when(~is_lat & bound_ok)(lambda: attend_bounded(n_ctx_rows))
        pl.when(~is_lat & ~bound_ok)(lambda: attend(ctx_chunk))


def _attn_call(q, k, vt, km, n_ct, t0):
    bsz, tt, _ = q.shape
    nt = tt // TM
    w = ATTN_HEADS * HEAD_PAD
    q_spec = pl.BlockSpec((None, TM, w), lambda b, h, i: (b, i + t0, h))
    k_spec = pl.BlockSpec((None, tt, w), lambda b, h, i: (b, 0, h))
    vt_spec = pl.BlockSpec((None, ATTN_HEADS * VT_ROWS, tt), lambda b, h, i: (b, h, 0))
    o_spec = pl.BlockSpec((None, TM, ATTN_HEADS * V_HEAD), lambda b, h, i: (b, i + t0, h))
    km_spec = pl.BlockSpec((None, nt, SUBLANES, w), lambda b, h, i: (b, 0, 0, h))
    return pl.pallas_call(
        functools.partial(_attn_kernel, n_ctx_rows=n_ct * TM, n_ct=n_ct, t0=t0),
        grid=(bsz, N_HEADS // ATTN_HEADS, nt - t0),
        in_specs=[q_spec, k_spec, vt_spec, km_spec],
        out_specs=o_spec,
        out_shape=jax.ShapeDtypeStruct((bsz, tt, N_HEADS * V_HEAD), BF16),
        compiler_params=_cparams(("parallel", "parallel", "arbitrary")),
        name="attention",
    )(q, k, vt, km)


def _out_kernel(yf_ref, yr_ref, bv_ref, g_ref, o_ref, x_ref, mod_ref, gn_ref, ones_ref, wor_ref, wom_ref, ln_ref,
                out_ref):
    ones_bd = ones_ref[...]
    y = yf_ref[...].astype(F32) + yr_ref[...].astype(F32)
    inv_n = 1.0 / RWKV_HEAD
    yc = y - _seg_sum(y, ones_bd) * inv_n
    var = _seg_sum(yc * yc, ones_bd) * inv_n
    gn = yc * lax.rsqrt(var + GN_EPS) * gn_ref[0:1] + gn_ref[1:2]
    ro = (gn + bv_ref[...]) * g_ref[...]
    m = _mm(ro, wor_ref[...]) + jnp.dot(o_ref[...], wom_ref[...], preferred_element_type=F32)
    mod = mod_ref[...]
    out_ref[...] = _layer_norm(DEEPNORM_ALPHA * x_ref[...] + mod[2:3] * m, ln_ref[0:1], ln_ref[1:2])


def _out_split_kernel(yf_ref, yr_ref, bv_ref, g_ref, o_ref, c_ref, x_ref, *rest, n_ct):
    _out_kernel(yf_ref, yr_ref, bv_ref, g_ref, o_ref, _Pick(pl.program_id(1) < n_ct, c_ref, x_ref), *rest)


def _out_call(yf, yr, bv, g, o, xs, mod, prm, n_ct, t0):
    if isinstance(xs, tuple):
        assert t0 == 0
        ctx, x = xs
        bsz, tt = x.shape[0], ctx.shape[1] + x.shape[1]
        body, srcs = functools.partial(_out_split_kernel, n_ct=n_ct), (ctx, x)
        src_specs = _split_specs(D_MODEL, n_ct, ctx.shape[1], x.shape[1], halo=False)
    else:
        bsz, tt, _ = xs.shape
        body, srcs, src_specs = _out_kernel, (xs,), [_row_spec(D_MODEL, t0)]
    nt = tt // TM
    params = [prm[n] for n in ("gn", "ones_bd", "wo_r", "wo_m", "ln1")]
    return pl.pallas_call(
        body,
        grid=(bsz, nt - t0),
        in_specs=[_row_spec(D_RWKV, t0)] * 4 + [_row_spec(N_HEADS * V_HEAD, t0)] + src_specs + [_mod_spec(n_ct, t0)]
        + [_full_spec(a.shape) for a in params],
        out_specs=_row_spec(D_MODEL),
        out_shape=jax.ShapeDtypeStruct((bsz, tt - t0 * TM, D_MODEL), F32),
        compiler_params=_cparams(("parallel", "parallel")),
        name="mixer_out",
    )(yf, yr, bv, g, o, *srcs, mod, *params)


def _ffn_kernel(x_ref, xp_ref, xn_ref, mod_ref, wup_ref, conv_ref, wd_ref, ln_ref, out_ref, *, n_ct, nt):
    i = pl.program_id(1)
    mod = mod_ref[...]
    u = _with_halo(x_ref, xp_ref, xn_ref, 1.0 + mod[4:5], mod[3:4], i, n_ct, nt)

    def up(j):
        hs = []
        for o in (j * FF_SUB, D_FF + j * FF_SUB):
            h = jnp.dot(u, wup_ref[:, o:o + FF_SUB], preferred_element_type=F32)
            taps = conv_ref[:, o:o + FF_SUB]
            hs.append(_conv3(h, taps) + taps[3:4])
        return hs

    acc = None
    acts = []
    h_next = up(0)
    for j in range(N_FF_SUB):
        hg, hv = h_next
        if j + 1 < N_FF_SUB:
            h_next = up(j + 1)
        half = 0.5 * hg
        acts.append(((half + half * jnp.tanh(half)) * hv).astype(BF16))
        if len(acts) == FF_DOWN or j + 1 == N_FF_SUB:
            lo = (j + 1 - len(acts)) * FF_SUB
            part = jnp.dot(jnp.concatenate(acts, axis=1), wd_ref[lo:(j + 1) * FF_SUB, :], preferred_element_type=F32)
            acc = part if acc is None else acc + part
            acts = []
    out_ref[...] = _layer_norm(DEEPNORM_ALPHA * x_ref[...] + mod[5:6] * acc, ln_ref[0:1], ln_ref[1:2])


def _ffn_call(xs, mod, prm, n_ct):
    bsz, tt, _ = xs.shape
    nt = tt // TM
    prev, nxt = _halo_specs(D_MODEL, tt)
    w_up, conv, w_down, ln2 = prm["w_up"], prm["ffn_conv"], prm["w_down"], prm["ln2"]
    return pl.pallas_call(
        functools.partial(_ffn_kernel, n_ct=n_ct, nt=nt),
        grid=(bsz, nt),
        in_specs=[_row_spec(D_MODEL), prev, nxt, _mod_spec(n_ct), _full_spec(w_up.shape, True),
                  _full_spec(conv.shape), _full_spec(w_down.shape, True), _full_spec(ln2.shape)],
        out_specs=_row_spec(D_MODEL),
        out_shape=jax.ShapeDtypeStruct((bsz, tt, D_MODEL), F32),
        compiler_params=_cparams(("parallel", "parallel")),
        name="conv_ffn",
    )(xs, xs, xs, mod, w_up, conv, w_down, ln2)


def _pad_rows(rows, width):
    rows = [r.astype(F32) if r.shape[0] == width else jnp.pad(r.astype(F32), (0, width - r.shape[0])) for r in rows]
    return jnp.pad(jnp.stack(rows), ((0, SUBLANES - len(rows)), (0, 0)))


def _layer_params(l, w_in, rwkv_conv, w0, w_b, a0, a_b, g_b, k_k, k_a, r_k, gn_g, gn_b, q_norm_g, w_uq,
                  kv_norm_g, w_ukv, w_o, ln1_g, ln1_b, w_up, ffn_conv_w, ffn_conv_b, w_down, ln2_g, ln2_b):
    c = D_RWKV
    wi = w_in[l]
    o_mla = RWKV_COLS
    w_kr = wi[:, o_mla + Q_LORA + KV_LORA:]
    pad_l = jnp.zeros((D_MODEL, QK_NOPE), F32)
    pad_r = jnp.zeros((D_MODEL, LANES - QK_DIM), F32)
    w_in_p = jnp.concatenate(
        [wi[:, :o_mla + Q_LORA + KV_LORA], pad_l, w_kr, pad_r], axis=1).astype(BF16)

    def lora_bd(w):
        z = jnp.zeros_like(w[0])
        return jnp.concatenate([jnp.concatenate([w[0], z], 1), jnp.concatenate([z, w[1]], 1)], 0).astype(BF16)

    head = jnp.arange(2 * LANES) // RWKV_HEAD
    ones_bd = (head[:, None] == head[None, :]).astype(BF16)

    wq = w_uq[l].reshape(Q_LORA, N_HEADS, QK_DIM)
    q_nope, q_rope = wq[..., :QK_NOPE], wq[..., QK_NOPE:]
    zq = lambda n: jnp.zeros((Q_LORA, N_HEADS, n), F32)
    wq = jnp.concatenate([q_nope, q_rope, zq(LANES - QK_DIM)], -1).reshape(Q_LORA, D_HEADS).astype(BF16)
    wkv = w_ukv[l].reshape(KV_LORA, N_HEADS, QK_NOPE + V_HEAD)
    zk = jnp.zeros((KV_LORA, N_HEADS, HEAD_PAD - QK_NOPE), F32)
    wkn = jnp.concatenate([wkv[..., :QK_NOPE], zk], -1).reshape(KV_LORA, D_HEADS).astype(BF16)
    wv = wkv[..., QK_NOPE:].reshape(KV_LORA, N_HEADS * V_HEAD).astype(BF16)
    wo = w_o[l]

    return {
        "w_in": w_in_p,
        "conv": _pad_rows(list(rwkv_conv[l]), RWKV_COLS),
        "w0": _pad_rows([w0[l].reshape(-1)], 2 * c),
        "wb": lora_bd(w_b[l]),
        "a0": _pad_rows([a0[l].reshape(-1)], 2 * c),
        "ab": lora_bd(a_b[l]),
        "gb": g_b[l].astype(BF16),
        "vec": _pad_rows([k_k[l], k_a[l], r_k[l].reshape(-1)], c),
        "ones_bd": ones_bd,
        "norm_g": _pad_rows([q_norm_g[l], kv_norm_g[l]], Q_LORA),
        "wq": wq, "wkn": wkn, "wv": wv,
        "gn": _pad_rows([gn_g[l], gn_b[l]], c),
        "wo_r": wo[:c].astype(BF16),
        "wo_m": wo[c:].astype(BF16),
        "ln1": _pad_rows([ln1_g[l], ln1_b[l]], D_MODEL),
        "w_up": w_up[l].astype(BF16),
        "ffn_conv": _pad_rows(list(ffn_conv_w[l]) + [ffn_conv_b[l]], 2 * D_FF),
        "w_down": w_down[l].astype(BF16),
        "ln2": _pad_rows([ln2_g[l], ln2_b[l]], D_MODEL),
    }


def _rope_tables(n_ctx, n_lat):
    rows = n_lat // GRID_W
    row = jnp.repeat(jnp.arange(rows, dtype=F32), GRID_W)
    col = jnp.tile(jnp.arange(GRID_W, dtype=F32), rows)
    n_pairs = QK_ROPE // 4
    inv = ROPE_BASE ** (-jnp.arange(n_pairs, dtype=F32) / n_pairs)
    ang = jnp.concatenate([row[:, None] * inv, col[:, None] * inv], axis=-1)
    cos = jnp.repeat(jnp.cos(ang), 2, axis=-1)
    sin = jnp.repeat(jnp.sin(ang), 2, axis=-1) * jnp.tile(jnp.array([-1.0, 1.0], F32), QK_ROPE // 2)
    tt = n_ctx + n_lat
    cos_t = jnp.ones((tt, LANES), F32).at[n_ctx:, QK_NOPE:QK_DIM].set(cos)
    sin_t = jnp.zeros((tt, LANES), F32).at[n_ctx:, QK_NOPE:QK_DIM].set(sin)
    return cos_t, sin_t


def kernel(x, c, ctx, c_ctx, w_ada, b_ada, w_in, rwkv_conv, w0, w_b, a0, a_b, g_b, k_k, k_a, r_k, gn_g, gn_b,
           q_norm_g, w_uq, kv_norm_g, w_ukv, w_o, ln1_g, ln1_b, w_up, ffn_conv_w, ffn_conv_b, w_down, ln2_g, ln2_b):
    bsz, n_lat, _ = x.shape
    n_ctx = ctx.shape[1]
    assert n_ctx % TM == 0 and n_lat % TM == 0 and n_lat % GRID_W == 0
    n_ct = n_ctx // TM
    n_l = w_ada.shape[0]

    ada_rows = -(-(bsz + 1) // SUBLANES) * SUBLANES
    cc = jnp.zeros((ada_rows, D_MODEL), F32).at[:bsz].set(c).at[bsz].set(c_ctx)
    mods = _ada_call(cc, w_ada, b_ada).reshape(n_l, ada_rows, 6, D_MODEL)
    cos_t, sin_t = _rope_tables(n_ctx, n_lat)
    xs = (ctx, x)

    for l in range(n_l):
        prm = _layer_params(l, w_in, rwkv_conv, w0, w_b, a0, a_b, g_b, k_k, k_a, r_k, gn_g, gn_b, q_norm_g, w_uq,
                            kv_norm_g, w_ukv, w_o, ln1_g, ln1_b, w_up, ffn_conv_w, ffn_conv_b, w_down, ln2_g, ln2_b)
        m_ctx = jnp.broadcast_to(mods[l, bsz][None], (bsz, 6, D_MODEL))
        mod = jnp.stack([m_ctx, mods[l, :bsz]], axis=1)
        mod = jnp.pad(mod, ((0, 0), (0, 0), (0, SUBLANES - 6), (0, 0))).reshape(2 * bsz, SUBLANES, D_MODEL)

        last = l == n_l - 1
        t0 = n_ct if last else 0
        if last and isinstance(xs, tuple):
            xs = jnp.concatenate(xs, axis=1)
        r, v, kk, g, bv, lw, kd, bb, q, k, vm, km = _front_call(xs, mod, cos_t, sin_t, prm, n_ct)
        yf, yr = _wkv_call(r, v, kk, lw, kd, bb, n_ct)
        o = _attn_call(q, k, vm, km, n_ct, t0)
        x1 = _out_call(yf, yr, bv, g, o, xs, mod, prm, n_ct, t0)
        xs = _ffn_call(x1, mod, prm, n_ct - t0)
    return xs
```

```python
import functools
import math

import jax
import jax.numpy as jnp
from jax import lax
from jax.experimental import pallas as pl
from jax.experimental.pallas import tpu as pltpu

F32 = jnp.float32
BF16 = jnp.bfloat16

D_MODEL = 1024
DEPTH = 2
GRID_W = 64
D_RWKV = 512
RWKV_HEAD = 64
N_HEADS = 8
LORA_W = 64
LORA_A = 64
LORA_G = 128
RWKV_COLS = 3 * D_RWKV + 2 * LORA_W + 2 * LORA_A + LORA_G
QK_NOPE = 64
QK_ROPE = 32
QK_DIM = QK_NOPE + QK_ROPE
V_HEAD = 64
Q_LORA = 384
KV_LORA = 256
D_FF = 2816
ROPE_BASE = 10000.0
DEEPNORM_ALPHA = (2 * DEPTH) ** 0.25
LN_EPS = 1e-5
RMS_EPS = 1e-6
GN_EPS = 64e-5
LOG2_E = 1.4426950408889634

LANES = 128
SUBLANES = 8
HEAD_PAD = LANES
D_HEADS = N_HEADS * HEAD_PAD
TM = 256
HALO = SUBLANES
CHUNK = 64
LOG_CHUNK = 6
WKV_BATCH = 4
PAIR = 2 * RWKV_HEAD
N_PAIRS = N_HEADS // 2
VT_ROWS = 80
ATTN_KEYS = 1024
ATTN_KEYS_BOUNDED = 2048
BOUND_LIMIT = 60.0
BOUND_SLACK = 1.01
ATTN_HEADS = 4
FF_SUB = 256
N_FF_SUB = D_FF // FF_SUB
FF_DOWN = 6
VMEM_LIMIT = 56 << 20


def _cparams(sem):
    return pltpu.CompilerParams(dimension_semantics=sem, vmem_limit_bytes=VMEM_LIMIT)


def _mm(a, b):
    return jnp.dot(a.astype(BF16), b.astype(BF16), preferred_element_type=F32)


def _mm_nt(a, b):
    return lax.dot_general(a.astype(BF16), b.astype(BF16), (((1,), (1,)), ((), ())),
                           preferred_element_type=F32)


def _split2(x):
    hi = x.astype(BF16)
    lo = (x - hi.astype(F32)).astype(BF16)
    return hi, lo


def _mm3(a, b):
    ah, al = _split2(a)
    bh, bl = _split2(b)
    d = functools.partial(jnp.dot, preferred_element_type=F32)
    return d(ah, bh) + (d(ah, bl) + d(al, bh))


def _mm_exact_rhs(a, b_bf16):
    a1, a2 = _split2(a)
    d = functools.partial(jnp.dot, preferred_element_type=F32)
    return d(a1, b_bf16) + d(a2, b_bf16)


def _mm_exact_lhs(a_bf16, b):
    b1 = b.astype(BF16)
    r1 = b - b1.astype(F32)
    b2 = r1.astype(BF16)
    b3 = (r1 - b2.astype(F32)).astype(BF16)
    d = functools.partial(jnp.dot, preferred_element_type=F32)
    return d(a_bf16, b1) + (d(a_bf16, b2) + d(a_bf16, b3))


def _sigmoid(x):
    return 1.0 / (1.0 + jnp.exp(-x))


def _layer_norm(x, g, b):
    mu = jnp.mean(x, axis=-1, keepdims=True)
    xc = x - mu
    var = jnp.mean(xc * xc, axis=-1, keepdims=True)
    return xc * lax.rsqrt(var + LN_EPS) * g + b


def _rms_norm(x, g):
    return x * lax.rsqrt(jnp.mean(x * x, axis=-1, keepdims=True) + RMS_EPS) * g


def _row_spec(cols, t0=0):
    return pl.BlockSpec((None, TM, cols), lambda b, i, *_: (b, i + t0, 0))


def _mod_spec(n_ct, t0=0):
    return pl.BlockSpec((None, SUBLANES, D_MODEL),
                        lambda b, i, *_: (2 * b + (i + t0 >= n_ct).astype(jnp.int32), 0, 0))


def _full_spec(shape, single_buffer=False):
    nd = len(shape)
    mode = dict(pipeline_mode=pl.Buffered(1)) if single_buffer else {}
    return pl.BlockSpec(shape, lambda *_: (0,) * nd, **mode)


def _halo_specs(cols, tt, t0=0):
    per = TM // HALO
    last = tt // HALO - 1
    prev = pl.BlockSpec((None, HALO, cols), lambda b, i, *_: (b, jnp.maximum((i + t0) * per - 1, 0), 0))
    nxt = pl.BlockSpec((None, HALO, cols), lambda b, i, *_: (b, jnp.minimum((i + t0 + 1) * per, last), 0))
    return prev, nxt


def _with_halo(x_ref, xp_ref, xn_ref, scale, shift, i, n_ct, nt):
    has_prev = (i != 0) & (i != n_ct)
    has_next = (i != n_ct - 1) & (i != nt - 1)
    mod = lambda x: x * scale + shift
    return jnp.concatenate([jnp.where(has_prev, mod(xp_ref[...]), 0.0), mod(x_ref[...]),
                            jnp.where(has_next, mod(xn_ref[...]), 0.0)], axis=0).astype(BF16)


class _Pick:
    def __init__(self, pred, a, b):
        self.pred, self.a, self.b = pred, a, b

    def __getitem__(self, idx):
        return jnp.where(self.pred, self.a[idx], self.b[idx])


def _split_specs(cols, n_ct, n_ctx, n_lat, halo):
    per = TM // HALO
    out = []
    for first, rows in ((0, n_ctx), (n_ct, n_lat)):
        last_t, last_h = rows // TM - 1, rows // HALO - 1
        t = lambda i, first=first, last_t=last_t: jnp.clip(i - first, 0, last_t)
        out.append(pl.BlockSpec((None, TM, cols), lambda b, i, t=t: (b, t(i), 0)))
        if halo:
            out.append(pl.BlockSpec((None, HALO, cols),
                                    lambda b, i, first=first, last_h=last_h: (b, jnp.clip((i - first) * per - 1, 0, last_h), 0)))
            out.append(pl.BlockSpec((None, HALO, cols),
                                    lambda b, i, first=first, last_h=last_h: (b, jnp.clip((i - first + 1) * per, 0, last_h), 0)))
    return out


def _conv3(h, taps):
    rows = TM + 2 * HALO
    lo, hi = HALO, HALO + TM
    return (pltpu.roll(h, 1, 0)[lo:hi] * taps[0:1] + h[lo:hi] * taps[1:2]
            + pltpu.roll(h, rows - 1, 0)[lo:hi] * taps[2:3])


ADA_COLS = 1536


def _ada_kernel(c_ref, w_ref, b_ref, o_ref):
    c = c_ref[...]
    o_ref[...] = _mm3(c * _sigmoid(c), w_ref[...]) + b_ref[...]


def _ada_call(cc, w_ada, b_ada):
    n_l = w_ada.shape[0]
    rows = cc.shape[0]
    return pl.pallas_call(
        _ada_kernel,
        grid=(n_l, 6 * D_MODEL // ADA_COLS),
        in_specs=[
            pl.BlockSpec((rows, D_MODEL), lambda l, j: (0, 0)),
            pl.BlockSpec((None, D_MODEL, ADA_COLS), lambda l, j: (l, 0, j)),
            pl.BlockSpec((None, 1, ADA_COLS), lambda l, j: (l, 0, j)),
        ],
        out_specs=pl.BlockSpec((None, rows, ADA_COLS), lambda l, j: (l, 0, j)),
        out_shape=jax.ShapeDtypeStruct((n_l, rows, 6 * D_MODEL), F32),
        compiler_params=_cparams(("arbitrary", "arbitrary")),
        name="ada",
    )(cc, w_ada, b_ada.reshape(n_l, 1, 6 * D_MODEL))


def _seg_sum(x, ones_bd):
    w = ones_bd.shape[0]
    return jnp.concatenate([_mm_exact_rhs(x[:, o:o + w], ones_bd) for o in range(0, x.shape[1], w)], axis=1)


def _front_kernel(x_ref, xp_ref, xn_ref, mod_ref, w_ref, conv_ref, w0_ref, wb_ref, a0_ref, ab_ref, gb_ref, vec_ref,
                  ones_ref, cos_ref, sin_ref, ng_ref, wq_ref, wkn_ref, wv_ref,
                  r_ref, v_ref, kk_ref, g_ref, bv_ref, lw_ref, kd_ref, bb_ref, q_ref, k_ref, vm_ref, km_ref, *, n_ct, nt):
    i = pl.program_id(1)
    m = mod_ref[...]
    u = _with_halo(x_ref, xp_ref, xn_ref, 1.0 + m[1:2], m[0:1], i, n_ct, nt)
    d = functools.partial(jnp.dot, preferred_element_type=F32)
    c = D_RWKV
    o_lora = 3 * c
    proj = lambda o, n: d(u, w_ref[:, o:o + n])
    conv = lambda p, o, n: _conv3(p, conv_ref[:, o:o + n])
    vec = vec_ref[...]
    k_k, k_a, r_k = vec[0:1], vec[1:2], vec[2:3]
    ones_bd = ones_ref[...]

    pm = proj(RWKV_COLS, w_ref.shape[1] - RWKV_COLS)[HALO:HALO + TM]
    p_lo = proj(o_lora, RWKV_COLS - o_lora)
    cq, ckv, kr_x = pm[:, :Q_LORA], pm[:, Q_LORA:Q_LORA + KV_LORA], pm[:, Q_LORA + KV_LORA:]
    cqn = _rms_norm(cq, ng_ref[0:1, :Q_LORA]).astype(BF16)
    ckvn = _rms_norm(ckv, ng_ref[1:2, :KV_LORA]).astype(BF16)

    p_k = proj(c, c)
    z_lo = conv(p_lo, o_lora, RWKV_COLS - o_lora)
    w_lo, a_lo, g_lo = z_lo[:, :2 * LORA_W], z_lo[:, 2 * LORA_W:2 * LORA_W + 2 * LORA_A], z_lo[:, 2 * LORA_W + 2 * LORA_A:]
    lw = -math.exp(-0.5) * _sigmoid(w0_ref[0:1, :] + _mm(jnp.tanh(w_lo), wb_ref[...]))
    lw_ref[0] = lw[:, :c]
    lw_ref[1] = lw[:, c:]
    a = _sigmoid(a0_ref[0:1, :] + _mm(a_lo, ab_ref[...]))
    g_ref[...] = _mm(_sigmoid(g_lo), gb_ref[...]).astype(BF16)

    q = d(cqn, wq_ref[...])
    p_r = proj(0, c)
    k = conv(p_k, c, c)
    kk = k * k_k
    nrm = jnp.sqrt(_seg_sum(kk * kk, ones_bd))
    kk = kk / jnp.maximum(nrm, 1e-12)
    kd0 = k * (1.0 + (a[:, :c] - 1.0) * k_a)
    kd1 = k * (1.0 + (a[:, c:] - 1.0) * k_a)
    kk_ref[...] = kk.astype(BF16)
    kd_ref[0] = kd0.astype(BF16)
    kd_ref[1] = kd1.astype(BF16)
    bb_ref[0] = (kk * a[:, :c]).astype(BF16)
    bb_ref[1] = (kk * a[:, c:]).astype(BF16)

    k_nope = d(ckvn, wkn_ref[...])
    vt = d(ckvn, wv_ref[...])
    p_v = proj(2 * c, c)

    def rotary(x, cos, sin_signed):
        n = x.shape[1]
        even = (lax.broadcasted_iota(jnp.int32, x.shape, 1) & 1) == 0
        partner = jnp.where(even, pltpu.roll(x, n - 1, 1), pltpu.roll(x, 1, 1))
        return x * cos + partner * sin_signed

    cos, sin = cos_ref[...], sin_ref[...]
    cos8 = jnp.concatenate([cos] * N_HEADS, axis=1)
    sin8 = jnp.concatenate([sin] * N_HEADS, axis=1)
    q_ref[...] = (rotary(q, cos8, sin8) * (QK_DIM ** -0.5 * LOG2_E)).astype(BF16)
    kr = rotary(kr_x, cos, sin)
    k_bf = (k_nope + jnp.concatenate([kr] * N_HEADS, axis=1)).astype(BF16)
    k_ref[...] = k_bf
    k2 = k_bf.astype(F32) * k_bf.astype(F32)
    km = [jnp.max(jnp.sum(k2[:, h * HEAD_PAD:(h + 1) * HEAD_PAD], axis=1, keepdims=True), axis=0, keepdims=True)
          for h in range(N_HEADS)]
    km_ref[...] = jnp.concatenate([jnp.broadcast_to(x, (SUBLANES, HEAD_PAD)) for x in km], axis=1)

    r = conv(p_r, 0, c)
    r_ref[...] = r.astype(BF16)
    v = conv(p_v, 2 * c, c)
    v_ref[...] = v.astype(BF16)
    bv_ref[...] = (_seg_sum(r * (kd0 + kd1) * r_k, ones_bd) * v).astype(BF16)

    vt = vt.T
    extra = jnp.where(lax.broadcasted_iota(jnp.int32, (VT_ROWS - V_HEAD, TM), 0) == 0, 1.0, 0.0)
    vm_ref[...] = jnp.concatenate(
        [blk for h in range(N_HEADS) for blk in (vt[h * V_HEAD:(h + 1) * V_HEAD], extra)], axis=0).astype(BF16)


def _front_split_kernel(c_ref, cp_ref, cn_ref, x_ref, xp_ref, xn_ref, *rest, n_ct, nt):
    is_ctx = pl.program_id(1) < n_ct
    _front_kernel(_Pick(is_ctx, c_ref, x_ref), _Pick(is_ctx, cp_ref, xp_ref), _Pick(is_ctx, cn_ref, xn_ref), *rest,
                  n_ct=n_ct, nt=nt)


def _front_call(xs, mod, cos_t, sin_t, prm, n_ct):
    if isinstance(xs, tuple):
        ctx, x = xs
        bsz, tt = x.shape[0], ctx.shape[1] + x.shape[1]
        body, srcs = _front_split_kernel, (ctx, ctx, ctx, x, x, x)
        src_specs = _split_specs(D_MODEL, n_ct, ctx.shape[1], x.shape[1], halo=True)
    else:
        bsz, tt, _ = xs.shape
        body, srcs = _front_kernel, (xs, xs, xs)
        src_specs = [_row_spec(D_MODEL), *_halo_specs(D_MODEL, tt)]
    nt = tt // TM
    tab = pl.BlockSpec((TM, LANES), lambda b, i: (i, 0))
    shared = jax.ShapeDtypeStruct((bsz, tt, D_RWKV), BF16)
    dir_spec = pl.BlockSpec((2, None, TM, D_RWKV), lambda b, i: (0, b, i, 0))
    dir_shape = lambda dt: jax.ShapeDtypeStruct((2, bsz, tt, D_RWKV), dt)
    heads = jax.ShapeDtypeStruct((bsz, tt, D_HEADS), BF16)
    p_rwkv = [prm[n] for n in ("w_in", "conv", "w0", "wb", "a0", "ab", "gb", "vec", "ones_bd")]
    p_mla = [prm[n] for n in ("norm_g", "wq", "wkn", "wv")]
    return pl.pallas_call(
        functools.partial(body, n_ct=n_ct, nt=nt),
        grid=(bsz, nt),
        in_specs=src_specs + [_mod_spec(n_ct)] + [_full_spec(a.shape) for a in p_rwkv]
        + [tab, tab] + [_full_spec(a.shape) for a in p_mla],
        out_specs=[_row_spec(D_RWKV)] * 5 + [dir_spec] * 3 + [_row_spec(D_HEADS)] * 2
        + [pl.BlockSpec((None, N_HEADS * VT_ROWS, TM), lambda b, i: (b, 0, i)),
           pl.BlockSpec((None, None, SUBLANES, D_HEADS), lambda b, i: (b, i, 0, 0))],
        out_shape=[shared] * 5 + [dir_shape(F32), dir_shape(BF16), dir_shape(BF16)] + [heads] * 2
        + [jax.ShapeDtypeStruct((bsz, N_HEADS * VT_ROWS, tt), BF16),
           jax.ShapeDtypeStruct((bsz, nt, SUBLANES, D_HEADS), F32)],
        compiler_params=_cparams(("parallel", "parallel")),
        name="front",
    )(*srcs, mod, *p_rwkv, cos_t, sin_t, *p_mla)


def _wkv_kernel(*refs, nb):
    ins, (yf_ref, yr_ref, s_ref) = refs[:12], refs[12:]
    step = pl.program_id(1)

    @pl.when(step == 0)
    def _():
        s_ref[...] = jnp.zeros_like(s_ref)

    n2 = 2 * CHUNK
    ri = lax.broadcasted_iota(jnp.int32, (n2, n2), 0)
    ci = lax.broadcasted_iota(jnp.int32, (n2, n2), 1)
    same_head = (ri >> LOG_CHUNK) == (ci >> LOG_CHUNK)
    rt = lax.broadcasted_iota(jnp.int32, (CHUNK, n2), 0)
    ct = lax.broadcasted_iota(jnp.int32, (CHUNK, n2), 1) & (CHUNK - 1)
    eye = ct == rt
    ti = lax.broadcasted_iota(jnp.int32, (CHUNK, CHUNK), 0)
    tj = lax.broadcasted_iota(jnp.int32, (CHUNK, CHUNK), 1)
    strict = [ct < rt, ct > rt]
    incl = [ct <= rt, ct >= rt]
    tri = [jnp.where(tj <= ti, 1.0, 0.0).astype(BF16), jnp.where(tj >= ti, 1.0, 0.0).astype(BF16)]

    def bdiag(x):
        return jnp.where(same_head, jnp.concatenate([x, x], axis=0), 0.0)

    def fold_t(x):
        xt = bdiag(x).T
        return xt[:CHUNK] + xt[CHUNK:]

    ch = []
    for d in range(2):
        r_ref, v_ref, kk_ref, lw_ref, kd_ref, bb_ref = ins[6 * d:6 * d + 6]
        for n in range(nb):
            lw = lw_ref[n]
            lc = _mm_exact_lhs(tri[d], lw)
            lend = jnp.sum(lw, axis=0, keepdims=True)
            e_neg = jnp.exp(-lc)
            e_end = jnp.exp(lend - lc)
            p_end = jnp.exp(lend)
            kt_all = kk_ref[n].astype(F32) * jnp.exp(lc - lw)
            rt_all = r_ref[n].astype(F32) * jnp.exp(lc)
            kd, bb = kd_ref[n].astype(F32), bb_ref[n].astype(F32)
            kh_all, bh_all = kd * e_neg, bb * e_neg
            kc_all, bc_all = kd * e_end, bb * e_end
            v_all = v_ref[n]
            for p in range(N_PAIRS):
                sl = slice(p * PAIR, (p + 1) * PAIR)
                ch.append(dict(d=d, slot=(d * nb + n) * N_PAIRS + p, kt=kt_all[:, sl], rr=rt_all[:, sl],
                               vv=bdiag(v_all[:, sl]), kh=bdiag(kh_all[:, sl]), bh=bdiag(bh_all[:, sl]),
                               kc_t=fold_t(kc_all[:, sl]), bc_t=fold_t(bc_all[:, sl]), p_end=p_end[:, sl]))
    for c in ch:
        gram = _mm_nt(jnp.concatenate([c["kt"], c["rr"]], axis=0), jnp.concatenate([c["bh"], c["kh"]], axis=0))
        st, ic = strict[c["d"]], incl[c["d"]]
        c["a_kb"] = jnp.where(st, gram[:CHUNK, :n2], 0.0)
        c["a_kk"] = jnp.where(st, gram[:CHUNK, n2:], 0.0)
        c["a_rb"] = jnp.where(ic, gram[CHUNK:, :n2], 0.0)
        c["a_rk"] = jnp.where(ic, gram[CHUNK:, n2:], 0.0)
        c["t"] = jnp.where(eye, 1.0, 0.0) - jnp.where((rt >> 1) == (ct >> 1), c["a_kb"], 0.0)
    for c in ch:
        av = _mm(jnp.concatenate([c["a_kk"], c["a_rk"]], axis=0), c["vv"])
        c["akk_v"], c["ark_v"] = av[:CHUNK], av[CHUNK:]
        c["kc_v"] = _mm(c["kc_t"], c["vv"])
    for ls in range(1, LOG_CHUNK):
        off = ((rt >> (ls + 1)) == (ct >> (ls + 1))) & ((rt >> ls) != (ct >> ls))
        for c in ch:
            c["nt"] = _mm(jnp.where(off, c["a_kb"], 0.0), bdiag(c["t"]))
        for c in ch:
            c["t"] = c["t"] - _mm(c["t"], bdiag(c["nt"]))
    for c in ch:
        uw = _mm(c["t"], jnp.concatenate([bdiag(c["akk_v"]), bdiag(c["kt"])], axis=1))
        c["uw"] = jnp.concatenate([bdiag(uw[:, :n2]), bdiag(uw[:, n2:])], axis=1)
    for c in ch:
        c["yq"] = jnp.concatenate([c["ark_v"], c["rr"]], axis=1) - _mm(c["a_rb"], c["uw"])
        diag_p = jnp.where(eye, jnp.broadcast_to(c["p_end"], (CHUNK, n2)), 0.0)
        c["nm"] = jnp.concatenate([c["kc_v"], diag_p], axis=1) - _mm(c["bc_t"], c["uw"])
    for c in ch:
        qm_l = jnp.concatenate([c["yq"][:, n2:], c["nm"][:, n2:]], axis=0).astype(BF16)
        s_hi, s_lo = _split2(s_ref[c["slot"]])
        c["qm"] = jnp.dot(qm_l, s_hi, preferred_element_type=F32)
        c["m_lo"] = jnp.dot(qm_l[CHUNK:], s_lo, preferred_element_type=F32)
    ys = []
    for c in ch:
        ys.append(c["qm"][:CHUNK] + c["yq"][:, :n2])
        s_ref[c["slot"]] = bdiag((c["qm"][CHUNK:] + c["m_lo"]) + c["nm"][:, :n2])
    for d, y_ref in enumerate((yf_ref, yr_ref)):
        for n in range(nb):
            o = (d * nb + n) * N_PAIRS
            y_ref[n] = jnp.concatenate(ys[o:o + N_PAIRS], axis=1).astype(y_ref.dtype)


def _wkv_call(r, v, kk, lw, kd, bb, n_ct):
    bsz, tt, c = r.shape
    nc = tt // CHUNK
    n_cc = n_ct * TM // CHUNK
    nb = math.gcd(bsz, WKV_BATCH)

    def rev_chunk(s):
        return jnp.where(s < n_cc, n_cc - 1 - s, nc - 1 - (s - n_cc))

    order = (lambda s: s, rev_chunk)
    specs, args = [], []
    for d in range(2):
        shared = pl.BlockSpec((nb, CHUNK, c), lambda b, s, d=d: (b, order[d](s), 0))
        per_dir = pl.BlockSpec((None, nb, CHUNK, c), lambda b, s, d=d: (d, b, order[d](s), 0))
        specs += [shared, shared, shared, per_dir, per_dir, per_dir]
        args += [r, v, kk, lw, kd, bb]
    out = jax.ShapeDtypeStruct((bsz, tt, c), BF16)
    return pl.pallas_call(
        functools.partial(_wkv_kernel, nb=nb),
        grid=(bsz // nb, nc),
        in_specs=specs,
        out_specs=[specs[0], specs[6]],
        out_shape=[out, out],
        scratch_shapes=[pltpu.VMEM((2 * nb * N_PAIRS, PAIR, PAIR), F32)],
        compiler_params=_cparams(("parallel", "arbitrary")),
        name="wkv_scan",
    )(*args)


def _rows_reduce(x, op, reduce):
    n = x.shape[0]
    while n > SUBLANES and n % 2 == 0:
        n //= 2
        x = op(x[:n], x[n:])
    return reduce(x, axis=0, keepdims=True)


def _attn_kernel(q_ref, k_ref, vt_ref, km_ref, o_ref, *, n_ctx_rows, n_ct, t0):
    i = pl.program_id(2) + t0
    n_keys = k_ref.shape[0]
    kc = ATTN_KEYS if (n_keys - n_ctx_rows) % ATTN_KEYS == 0 else TM
    heads = range(ATTN_HEADS)
    hs = [slice(g * HEAD_PAD, (g + 1) * HEAD_PAD) for g in heads]

    km2 = jnp.max(jnp.max(km_ref[...], axis=0), axis=0, keepdims=True)
    bound = []
    for g in heads:
        qf = q_ref[:, hs[g]].astype(F32)
        q2 = jnp.max(jnp.sum(qf * qf, axis=1, keepdims=True), axis=0, keepdims=True)
        bound.append(jnp.sqrt(q2 * km2[:, g * HEAD_PAD:g * HEAD_PAD + 1]) * BOUND_SLACK)
    bound_ok = functools.reduce(jnp.maximum, [jnp.max(b) for b in bound]) < BOUND_LIMIT

    def finish(acc):
        outs = [acc[g][:V_HEAD] / acc[g][V_HEAD:V_HEAD + 1] for g in heads]
        o_ref[...] = jnp.concatenate(outs, axis=0).T.astype(o_ref.dtype)

    def attend_bounded(chunks):
        pieces = [(c, g) for c in chunks for g in heads]
        scores = lambda c, g: _mm_nt(k_ref[c[0]:c[0] + c[1], hs[g]], q_ref[:, hs[g]])
        acc = [None] * ATTN_HEADS
        s_next = scores(*pieces[0])
        for n, (c, g) in enumerate(pieces):
            s = s_next
            if n + 1 < len(pieces):
                s_next = scores(*pieces[n + 1])
            pv = _mm(vt_ref[g * VT_ROWS:(g + 1) * VT_ROWS, c[0]:c[0] + c[1]], jnp.exp2(s - bound[g]))
            acc[g] = pv if acc[g] is None else acc[g] + pv
        finish(acc)

    def attend(chunks):
        qs = [q_ref[:, hs[g]] for g in heads]
        scores = lambda g, c: _mm_nt(k_ref[c[0]:c[0] + c[1], hs[g]], qs[g])
        s_next = [scores(g, chunks[0]) for g in heads]
        m, acc = ([None] * ATTN_HEADS for _ in range(2))
        for n, c in enumerate(chunks):
            s = s_next
            if n + 1 < len(chunks):
                s_next = [scores(g, chunks[n + 1]) for g in heads]
            m_c = [_rows_reduce(s[g], jnp.maximum, jnp.max) for g in heads]
            m_new = [m_c[g] if m[g] is None else jnp.maximum(m[g], m_c[g]) for g in heads]
            p = [jnp.exp2(s[g] - m_new[g]) for g in heads]
            pv = [_mm(vt_ref[g * VT_ROWS:(g + 1) * VT_ROWS, c[0]:c[0] + c[1]], p[g]) for g in heads]
            for g in heads:
                acc[g] = pv[g] if m[g] is None else jnp.exp2(m[g] - m_new[g]) * acc[g] + pv[g]
                m[g] = m_new[g]
        finish(acc)

    ctx_chunk = [(0, n_ctx_rows)]
    lat_chunks = ctx_chunk + [(s, kc) for s in range(n_ctx_rows, n_keys, kc)]
    is_lat = i >= n_ct
    kb = ATTN_KEYS_BOUNDED if (n_keys - n_ctx_rows) % ATTN_KEYS_BOUNDED == 0 else TM
    pl.when(is_lat & bound_ok)(lambda: attend_bounded(ctx_chunk + [(s, kb) for s in range(n_ctx_rows, n_keys, kb)]))
    pl.when(is_lat & ~bound_ok)(lambda: attend(lat_chunks))
    if t0 < n_ct:
        pl.when(~is_lat & bound_ok)(lambda: attend_bounded(ctx_chunk))
        pl.when(~is_lat & ~bound_ok)(lambda: attend(ctx_chunk))


def _attn_call(q, k, vt, km, n_ct, t0):
    bsz, tt, _ = q.shape
    nt = tt // TM
    w = ATTN_HEADS * HEAD_PAD
    q_spec = pl.BlockSpec((None, TM, w), lambda b, h, i: (b, i + t0, h))
    k_spec = pl.BlockSpec((None, tt, w), lambda b, h, i: (b, 0, h))
    vt_spec = pl.BlockSpec((None, ATTN_HEADS * VT_ROWS, tt), lambda b, h, i: (b, h, 0))
    o_spec = pl.BlockSpec((None, TM, ATTN_HEADS * V_HEAD), lambda b, h, i: (b, i + t0, h))
    km_spec = pl.BlockSpec((None, nt, SUBLANES, w), lambda b, h, i: (b, 0, 0, h))
    return pl.pallas_call(
        functools.partial(_attn_kernel, n_ctx_rows=n_ct * TM, n_ct=n_ct, t0=t0),
        grid=(bsz, N_HEADS // ATTN_HEADS, nt - t0),
        in_specs=[q_spec, k_spec, vt_spec, km_spec],
        out_specs=o_spec,
        out_shape=jax.ShapeDtypeStruct((bsz, tt, N_HEADS * V_HEAD), BF16),
        compiler_params=_cparams(("parallel", "parallel", "arbitrary")),
        name="attention",
    )(q, k, vt, km)


def _out_kernel(yf_ref, yr_ref, bv_ref, g_ref, o_ref, x_ref, mod_ref, gn_ref, ones_ref, wor_ref, wom_ref, ln_ref,
                out_ref):
    ones_bd = ones_ref[...]
    y = yf_ref[...].astype(F32) + yr_ref[...].astype(F32)
    inv_n = 1.0 / RWKV_HEAD
    yc = y - _seg_sum(y, ones_bd) * inv_n
    var = _seg_sum(yc * yc, ones_bd) * inv_n
    gn = yc * lax.rsqrt(var + GN_EPS) * gn_ref[0:1] + gn_ref[1:2]
    ro = (gn + bv_ref[...]) * g_ref[...]
    m = _mm(ro, wor_ref[...]) + jnp.dot(o_ref[...], wom_ref[...], preferred_element_type=F32)
    mod = mod_ref[...]
    out_ref[...] = _layer_norm(DEEPNORM_ALPHA * x_ref[...] + mod[2:3] * m, ln_ref[0:1], ln_ref[1:2])


def _out_split_kernel(yf_ref, yr_ref, bv_ref, g_ref, o_ref, c_ref, x_ref, *rest, n_ct):
    _out_kernel(yf_ref, yr_ref, bv_ref, g_ref, o_ref, _Pick(pl.program_id(1) < n_ct, c_ref, x_ref), *rest)


def _out_call(yf, yr, bv, g, o, xs, mod, prm, n_ct, t0):
    if isinstance(xs, tuple):
        assert t0 == 0
        ctx, x = xs
        bsz, tt = x.shape[0], ctx.shape[1] + x.shape[1]
        body, srcs = functools.partial(_out_split_kernel, n_ct=n_ct), (ctx, x)
        src_specs = _split_specs(D_MODEL, n_ct, ctx.shape[1], x.shape[1], halo=False)
    else:
        bsz, tt, _ = xs.shape
        body, srcs, src_specs = _out_kernel, (xs,), [_row_spec(D_MODEL, t0)]
    nt = tt // TM
    params = [prm[n] for n in ("gn", "ones_bd", "wo_r", "wo_m", "ln1")]
    return pl.pallas_call(
        body,
        grid=(bsz, nt - t0),
        in_specs=[_row_spec(D_RWKV, t0)] * 4 + [_row_spec(N_HEADS * V_HEAD, t0)] + src_specs + [_mod_spec(n_ct, t0)]
        + [_full_spec(a.shape) for a in params],
        out_specs=_row_spec(D_MODEL),
        out_shape=jax.ShapeDtypeStruct((bsz, tt - t0 * TM, D_MODEL), F32),
        compiler_params=_cparams(("parallel", "parallel")),
        name="mixer_out",
    )(yf, yr, bv, g, o, *srcs, mod, *params)


def _ffn_kernel(x_ref, xp_ref, xn_ref, mod_ref, wup_ref, conv_ref, wd_ref, ln_ref, out_ref, *, n_ct, nt):
    i = pl.program_id(1)
    mod = mod_ref[...]
    u = _with_halo(x_ref, xp_ref, xn_ref, 1.0 + mod[4:5], mod[3:4], i, n_ct, nt)

    def up(j):
        hs = []
        for o in (j * FF_SUB, D_FF + j * FF_SUB):
            h = jnp.dot(u, wup_ref[:, o:o + FF_SUB], preferred_element_type=F32)
            taps = conv_ref[:, o:o + FF_SUB]
            hs.append(_conv3(h, taps) + taps[3:4])
        return hs

    acc = None
    acts = []
    h_next = up(0)
    for j in range(N_FF_SUB):
        hg, hv = h_next
        if j + 1 < N_FF_SUB:
            h_next = up(j + 1)
        half = 0.5 * hg
        acts.append(((half + half * jnp.tanh(half)) * hv).astype(BF16))
        if len(acts) == FF_DOWN or j + 1 == N_FF_SUB:
            lo = (j + 1 - len(acts)) * FF_SUB
            part = jnp.dot(jnp.concatenate(acts, axis=1), wd_ref[lo:(j + 1) * FF_SUB, :], preferred_element_type=F32)
            acc = part if acc is None else acc + part
            acts = []
    out_ref[...] = _layer_norm(DEEPNORM_ALPHA * x_ref[...] + mod[5:6] * acc, ln_ref[0:1], ln_ref[1:2])


def _ffn_call(xs, mod, prm, n_ct):
    bsz, tt, _ = xs.shape
    nt = tt // TM
    prev, nxt = _halo_specs(D_MODEL, tt)
    w_up, conv, w_down, ln2 = prm["w_up"], prm["ffn_conv"], prm["w_down"], prm["ln2"]
    return pl.pallas_call(
        functools.partial(_ffn_kernel, n_ct=n_ct, nt=nt),
        grid=(bsz, nt),
        in_specs=[_row_spec(D_MODEL), prev, nxt, _mod_spec(n_ct), _full_spec(w_up.shape, True),
                  _full_spec(conv.shape), _full_spec(w_down.shape, True), _full_spec(ln2.shape)],
        out_specs=_row_spec(D_MODEL),
        out_shape=jax.ShapeDtypeStruct((bsz, tt, D_MODEL), F32),
        compiler_params=_cparams(("parallel", "parallel")),
        name="conv_ffn",
    )(xs, xs, xs, mod, w_up, conv, w_down, ln2)


def _pad_rows(rows, width):
    rows = [r.astype(F32) if r.shape[0] == width else jnp.pad(r.astype(F32), (0, width - r.shape[0])) for r in rows]
    return jnp.pad(jnp.stack(rows), ((0, SUBLANES - len(rows)), (0, 0)))


def _layer_params(l, w_in, rwkv_conv, w0, w_b, a0, a_b, g_b, k_k, k_a, r_k, gn_g, gn_b, q_norm_g, w_uq,
                  kv_norm_g, w_ukv, w_o, ln1_g, ln1_b, w_up, ffn_conv_w, ffn_conv_b, w_down, ln2_g, ln2_b):
    c = D_RWKV
    wi = w_in[l]
    o_mla = RWKV_COLS
    w_kr = wi[:, o_mla + Q_LORA + KV_LORA:]
    pad_l = jnp.zeros((D_MODEL, QK_NOPE), F32)
    pad_r = jnp.zeros((D_MODEL, LANES - QK_DIM), F32)
    w_in_p = jnp.concatenate(
        [wi[:, :o_mla + Q_LORA + KV_LORA], pad_l, w_kr, pad_r], axis=1).astype(BF16)

    def lora_bd(w):
        z = jnp.zeros_like(w[0])
        return jnp.concatenate([jnp.concatenate([w[0], z], 1), jnp.concatenate([z, w[1]], 1)], 0).astype(BF16)

    head = jnp.arange(2 * LANES) // RWKV_HEAD
    ones_bd = (head[:, None] == head[None, :]).astype(BF16)

    wq = w_uq[l].reshape(Q_LORA, N_HEADS, QK_DIM)
    q_nope, q_rope = wq[..., :QK_NOPE], wq[..., QK_NOPE:]
    zq = lambda n: jnp.zeros((Q_LORA, N_HEADS, n), F32)
    wq = jnp.concatenate([q_nope, q_rope, zq(LANES - QK_DIM)], -1).reshape(Q_LORA, D_HEADS).astype(BF16)
    wkv = w_ukv[l].reshape(KV_LORA, N_HEADS, QK_NOPE + V_HEAD)
    zk = jnp.zeros((KV_LORA, N_HEADS, HEAD_PAD - QK_NOPE), F32)
    wkn = jnp.concatenate([wkv[..., :QK_NOPE], zk], -1).reshape(KV_LORA, D_HEADS).astype(BF16)
    wv = wkv[..., QK_NOPE:].reshape(KV_LORA, N_HEADS * V_HEAD).astype(BF16)
    wo = w_o[l]

    return {
        "w_in": w_in_p,
        "conv": _pad_rows(list(rwkv_conv[l]), RWKV_COLS),
        "w0": _pad_rows([w0[l].reshape(-1)], 2 * c),
        "wb": lora_bd(w_b[l]),
        "a0": _pad_rows([a0[l].reshape(-1)], 2 * c),
        "ab": lora_bd(a_b[l]),
        "gb": g_b[l].astype(BF16),
        "vec": _pad_rows([k_k[l], k_a[l], r_k[l].reshape(-1)], c),
        "ones_bd": ones_bd,
        "norm_g": _pad_rows([q_norm_g[l], kv_norm_g[l]], Q_LORA),
        "wq": wq, "wkn": wkn, "wv": wv,
        "gn": _pad_rows([gn_g[l], gn_b[l]], c),
        "wo_r": wo[:c].astype(BF16),
        "wo_m": wo[c:].astype(BF16),
        "ln1": _pad_rows([ln1_g[l], ln1_b[l]], D_MODEL),
        "w_up": w_up[l].astype(BF16),
        "ffn_conv": _pad_rows(list(ffn_conv_w[l]) + [ffn_conv_b[l]], 2 * D_FF),
        "w_down": w_down[l].astype(BF16),
        "ln2": _pad_rows([ln2_g[l], ln2_b[l]], D_MODEL),
    }


def _rope_tables(n_ctx, n_lat):
    rows = n_lat // GRID_W
    row = jnp.repeat(jnp.arange(rows, dtype=F32), GRID_W)
    col = jnp.tile(jnp.arange(GRID_W, dtype=F32), rows)
    n_pairs = QK_ROPE // 4
    inv = ROPE_BASE ** (-jnp.arange(n_pairs, dtype=F32) / n_pairs)
    ang = jnp.concatenate([row[:, None] * inv, col[:, None] * inv], axis=-1)
    cos = jnp.repeat(jnp.cos(ang), 2, axis=-1)
    sin = jnp.repeat(jnp.sin(ang), 2, axis=-1) * jnp.tile(jnp.array([-1.0, 1.0], F32), QK_ROPE // 2)
    tt = n_ctx + n_lat
    cos_t = jnp.ones((tt, LANES), F32).at[n_ctx:, QK_NOPE:QK_DIM].set(cos)
    sin_t = jnp.zeros((tt, LANES), F32).at[n_ctx:, QK_NOPE:QK_DIM].set(sin)
    return cos_t, sin_t


def kernel(x, c, ctx, c_ctx, w_ada, b_ada, w_in, rwkv_conv, w0, w_b, a0, a_b, g_b, k_k, k_a, r_k, gn_g, gn_b,
           q_norm_g, w_uq, kv_norm_g, w_ukv, w_o, ln1_g, ln1_b, w_up, ffn_conv_w, ffn_conv_b, w_down, ln2_g, ln2_b):
    bsz, n_lat, _ = x.shape
    n_ctx = ctx.shape[1]
    assert n_ctx % TM == 0 and n_lat % TM == 0 and n_lat % GRID_W == 0
    n_ct = n_ctx // TM
    n_l = w_ada.shape[0]

    ada_rows = -(-(bsz + 1) // SUBLANES) * SUBLANES
    cc = jnp.zeros((ada_rows, D_MODEL), F32).at[:bsz].set(c).at[bsz].set(c_ctx)
    mods = _ada_call(cc, w_ada, b_ada).reshape(n_l, ada_rows, 6, D_MODEL)
    cos_t, sin_t = _rope_tables(n_ctx, n_lat)
    xs = (ctx, x)

    for l in range(n_l):
        prm = _layer_params(l, w_in, rwkv_conv, w0, w_b, a0, a_b, g_b, k_k, k_a, r_k, gn_g, gn_b, q_norm_g, w_uq,
                            kv_norm_g, w_ukv, w_o, ln1_g, ln1_b, w_up, ffn_conv_w, ffn_conv_b, w_down, ln2_g, ln2_b)
        m_ctx = jnp.broadcast_to(mods[l, bsz][None], (bsz, 6, D_MODEL))
        mod = jnp.stack([m_ctx, mods[l, :bsz]], axis=1)
        mod = jnp.pad(mod, ((0, 0), (0, 0), (0, SUBLANES - 6), (0, 0))).reshape(2 * bsz, SUBLANES, D_MODEL)

        last = l == n_l - 1
        t0 = n_ct if last else 0
        if last and isinstance(xs, tuple):
            xs = jnp.concatenate(xs, axis=1)
        r, v, kk, g, bv, lw, kd, bb, q, k, vm, km = _front_call(xs, mod, cos_t, sin_t, prm, n_ct)
        yf, yr = _wkv_call(r, v, kk, lw, kd, bb, n_ct)
        o = _attn_call(q, k, vm, km, n_ct, t0)
        x1 = _out_call(yf, yr, bv, g, o, xs, mod, prm, n_ct, t0)
        xs = _ffn_call(x1, mod, prm, n_ct - t0)
    return xs
```

```python
import functools
import math

import jax
import jax.numpy as jnp
from jax import lax
from jax.experimental import pallas as pl
from jax.experimental.pallas import tpu as pltpu

F32 = jnp.float32
BF16 = jnp.bfloat16

D_MODEL = 1024
DEPTH = 2
GRID_W = 64
D_RWKV = 512
RWKV_HEAD = 64
N_HEADS = 8
LORA_W = 64
LORA_A = 64
LORA_G = 128
RWKV_COLS = 3 * D_RWKV + 2 * LORA_W + 2 * LORA_A + LORA_G
QK_NOPE = 64
QK_ROPE = 32
QK_DIM = QK_NOPE + QK_ROPE
V_HEAD = 64
Q_LORA = 384
KV_LORA = 256
D_FF = 2816
ROPE_BASE = 10000.0
DEEPNORM_ALPHA = (2 * DEPTH) ** 0.25
LN_EPS = 1e-5
RMS_EPS = 1e-6
GN_EPS = 64e-5
LOG2_E = 1.4426950408889634

LANES = 128
SUBLANES = 8
HEAD_PAD = LANES
D_HEADS = N_HEADS * HEAD_PAD
TM = 256
HALO = SUBLANES
CHUNK = 64
LOG_CHUNK = 6
WKV_BATCH = 4
PAIR = 2 * RWKV_HEAD
N_PAIRS = N_HEADS // 2
VT_ROWS = 80
ATTN_KEYS = 1024
ATTN_KEYS_BOUNDED = 2048
BOUND_LIMIT = 60.0
BOUND_SLACK = 1.01
ATTN_HEADS = 4
FF_SUB = 256
N_FF_SUB = D_FF // FF_SUB
FF_DOWN = 6
VMEM_LIMIT = 56 << 20


def _cparams(sem):
    return pltpu.CompilerParams(dimension_semantics=sem, vmem_limit_bytes=VMEM_LIMIT)


def _mm(a, b):
    return jnp.dot(a.astype(BF16), b.astype(BF16), preferred_element_type=F32)


def _mm_nt(a, b):
    return lax.dot_general(a.astype(BF16), b.astype(BF16), (((1,), (1,)), ((), ())),
                           preferred_element_type=F32)


def _split2(x):
    hi = x.astype(BF16)
    lo = (x - hi.astype(F32)).astype(BF16)
    return hi, lo


def _mm3(a, b):
    ah, al = _split2(a)
    bh, bl = _split2(b)
    d = functools.partial(jnp.dot, preferred_element_type=F32)
    return d(ah, bh) + (d(ah, bl) + d(al, bh))


def _mm_exact_rhs(a, b_bf16):
    a1, a2 = _split2(a)
    d = functools.partial(jnp.dot, preferred_element_type=F32)
    return d(a1, b_bf16) + d(a2, b_bf16)


def _mm_exact_lhs(a_bf16, b):
    b1 = b.astype(BF16)
    r1 = b - b1.astype(F32)
    b2 = r1.astype(BF16)
    b3 = (r1 - b2.astype(F32)).astype(BF16)
    d = functools.partial(jnp.dot, preferred_element_type=F32)
    return d(a_bf16, b1) + (d(a_bf16, b2) + d(a_bf16, b3))


def _sigmoid(x):
    return 1.0 / (1.0 + jnp.exp(-x))


def _layer_norm(x, g, b):
    mu = jnp.mean(x, axis=-1, keepdims=True)
    xc = x - mu
    var = jnp.mean(xc * xc, axis=-1, keepdims=True)
    return xc * lax.rsqrt(var + LN_EPS) * g + b


def _rms_norm(x, g):
    return x * lax.rsqrt(jnp.mean(x * x, axis=-1, keepdims=True) + RMS_EPS) * g


def _row_spec(cols, t0=0):
    return pl.BlockSpec((None, TM, cols), lambda b, i, *_: (b, i + t0, 0))


def _mod_spec(n_ct, t0=0):
    return pl.BlockSpec((None, SUBLANES, D_MODEL),
                        lambda b, i, *_: (2 * b + (i + t0 >= n_ct).astype(jnp.int32), 0, 0))


def _full_spec(shape, single_buffer=False):
    nd = len(shape)
    mode = dict(pipeline_mode=pl.Buffered(1)) if single_buffer else {}
    return pl.BlockSpec(shape, lambda *_: (0,) * nd, **mode)


def _halo_specs(cols, tt, t0=0):
    per = TM // HALO
    last = tt // HALO - 1
    prev = pl.BlockSpec((None, HALO, cols), lambda b, i, *_: (b, jnp.maximum((i + t0) * per - 1, 0), 0))
    nxt = pl.BlockSpec((None, HALO, cols), lambda b, i, *_: (b, jnp.minimum((i + t0 + 1) * per, last), 0))
    return prev, nxt


def _with_halo(x_ref, xp_ref, xn_ref, scale, shift, i, n_ct, nt):
    has_prev = (i != 0) & (i != n_ct)
    has_next = (i != n_ct - 1) & (i != nt - 1)
    mod = lambda x: x * scale + shift
    return jnp.concatenate([jnp.where(has_prev, mod(xp_ref[...]), 0.0), mod(x_ref[...]),
                            jnp.where(has_next, mod(xn_ref[...]), 0.0)], axis=0).astype(BF16)


class _Pick:
    def __init__(self, pred, a, b):
        self.pred, self.a, self.b = pred, a, b

    def __getitem__(self, idx):
        return jnp.where(self.pred, self.a[idx], self.b[idx])


def _split_specs(cols, n_ct, n_ctx, n_lat, halo):
    per = TM // HALO
    out = []
    for first, rows in ((0, n_ctx), (n_ct, n_lat)):
        last_t, last_h = rows // TM - 1, rows // HALO - 1
        t = lambda i, first=first, last_t=last_t: jnp.clip(i - first, 0, last_t)
        out.append(pl.BlockSpec((None, TM, cols), lambda b, i, t=t: (b, t(i), 0)))
        if halo:
            out.append(pl.BlockSpec((None, HALO, cols),
                                    lambda b, i, first=first, last_h=last_h: (b, jnp.clip((i - first) * per - 1, 0, last_h), 0)))
            out.append(pl.BlockSpec((None, HALO, cols),
                                    lambda b, i, first=first, last_h=last_h: (b, jnp.clip((i - first + 1) * per, 0, last_h), 0)))
    return out


def _conv3(h, taps):
    rows = TM + 2 * HALO
    lo, hi = HALO, HALO + TM
    return (pltpu.roll(h, 1, 0)[lo:hi] * taps[0:1] + h[lo:hi] * taps[1:2]
            + pltpu.roll(h, rows - 1, 0)[lo:hi] * taps[2:3])


ADA_COLS = 1536


def _ada_kernel(c_ref, w_ref, b_ref, o_ref):
    c = c_ref[...]
    o_ref[...] = _mm3(c * _sigmoid(c), w_ref[...]) + b_ref[...]


def _ada_call(cc, w_ada, b_ada):
    n_l = w_ada.shape[0]
    rows = cc.shape[0]
    return pl.pallas_call(
        _ada_kernel,
        grid=(n_l, 6 * D_MODEL // ADA_COLS),
        in_specs=[
            pl.BlockSpec((rows, D_MODEL), lambda l, j: (0, 0)),
            pl.BlockSpec((None, D_MODEL, ADA_COLS), lambda l, j: (l, 0, j)),
            pl.BlockSpec((None, 1, ADA_COLS), lambda l, j: (l, 0, j)),
        ],
        out_specs=pl.BlockSpec((None, rows, ADA_COLS), lambda l, j: (l, 0, j)),
        out_shape=jax.ShapeDtypeStruct((n_l, rows, 6 * D_MODEL), F32),
        compiler_params=_cparams(("arbitrary", "arbitrary")),
        name="ada",
    )(cc, w_ada, b_ada.reshape(n_l, 1, 6 * D_MODEL))


def _seg_sum(x, ones_bd):
    w = ones_bd.shape[0]
    return jnp.concatenate([_mm_exact_rhs(x[:, o:o + w], ones_bd) for o in range(0, x.shape[1], w)], axis=1)


def _front_kernel(x_ref, xp_ref, xn_ref, mod_ref, w_ref, conv_ref, w0_ref, wb_ref, a0_ref, ab_ref, gb_ref, vec_ref,
                  ones_ref, cos_ref, sin_ref, ng_ref, wq_ref, wkn_ref, wv_ref, hsel_ref,
                  r_ref, v_ref, kk_ref, g_ref, bv_ref, lw_ref, kd_ref, bb_ref, q_ref, k_ref, vm_ref, km_ref, *, n_ct, nt):
    i = pl.program_id(1)
    m = mod_ref[...]
    u = _with_halo(x_ref, xp_ref, xn_ref, 1.0 + m[1:2], m[0:1], i, n_ct, nt)
    d = functools.partial(jnp.dot, preferred_element_type=F32)
    c = D_RWKV
    o_lora = 3 * c
    proj = lambda o, n: d(u, w_ref[:, o:o + n])
    conv = lambda p, o, n: _conv3(p, conv_ref[:, o:o + n])
    vec = vec_ref[...]
    k_k, k_a, r_k = vec[0:1], vec[1:2], vec[2:3]
    ones_bd = ones_ref[...]

    pm = proj(RWKV_COLS, w_ref.shape[1] - RWKV_COLS)[HALO:HALO + TM]
    p_lo = proj(o_lora, RWKV_COLS - o_lora)
    cq, ckv, kr_x = pm[:, :Q_LORA], pm[:, Q_LORA:Q_LORA + KV_LORA], pm[:, Q_LORA + KV_LORA:]
    cqn = _rms_norm(cq, ng_ref[0:1, :Q_LORA]).astype(BF16)
    ckvn = _rms_norm(ckv, ng_ref[1:2, :KV_LORA]).astype(BF16)

    p_k = proj(c, c)
    z_lo = conv(p_lo, o_lora, RWKV_COLS - o_lora)
    w_lo, a_lo, g_lo = z_lo[:, :2 * LORA_W], z_lo[:, 2 * LORA_W:2 * LORA_W + 2 * LORA_A], z_lo[:, 2 * LORA_W + 2 * LORA_A:]
    lw = -math.exp(-0.5) * _sigmoid(w0_ref[0:1, :] + _mm(jnp.tanh(w_lo), wb_ref[...]))
    lw_ref[0] = lw[:, :c]
    lw_ref[1] = lw[:, c:]
    a = _sigmoid(a0_ref[0:1, :] + _mm(a_lo, ab_ref[...]))
    g_ref[...] = _mm(_sigmoid(g_lo), gb_ref[...]).astype(BF16)

    q = d(cqn, wq_ref[...])
    p_r = proj(0, c)
    k = conv(p_k, c, c)
    kk = k * k_k
    nrm = jnp.sqrt(_seg_sum(kk * kk, ones_bd))
    kk = kk / jnp.maximum(nrm, 1e-12)
    kd0 = k * (1.0 + (a[:, :c] - 1.0) * k_a)
    kd1 = k * (1.0 + (a[:, c:] - 1.0) * k_a)
    kk_ref[...] = kk.astype(BF16)
    kd_ref[0] = kd0.astype(BF16)
    kd_ref[1] = kd1.astype(BF16)
    bb_ref[0] = (kk * a[:, :c]).astype(BF16)
    bb_ref[1] = (kk * a[:, c:]).astype(BF16)

    k_nope = d(ckvn, wkn_ref[...])
    vt = d(ckvn, wv_ref[...])
    p_v = proj(2 * c, c)

    def rotary(x, cos, sin_signed):
        n = x.shape[1]
        even = (lax.broadcasted_iota(jnp.int32, x.shape, 1) & 1) == 0
        partner = jnp.where(even, pltpu.roll(x, n - 1, 1), pltpu.roll(x, 1, 1))
        return x * cos + partner * sin_signed

    cos, sin = cos_ref[...], sin_ref[...]
    cos8 = jnp.concatenate([cos] * N_HEADS, axis=1)
    sin8 = jnp.concatenate([sin] * N_HEADS, axis=1)
    q_ref[...] = (rotary(q, cos8, sin8) * (QK_DIM ** -0.5 * LOG2_E)).astype(BF16)
    kr = rotary(kr_x, cos, sin)
    k_bf = (k_nope + jnp.concatenate([kr] * N_HEADS, axis=1)).astype(BF16)
    k_ref[...] = k_bf
    k2 = _mm(k_bf.astype(F32) * k_bf.astype(F32), hsel_ref[...])
    km_ref[...] = jnp.broadcast_to(_rows_reduce(k2, jnp.maximum, jnp.max), (SUBLANES, LANES))

    r = conv(p_r, 0, c)
    r_ref[...] = r.astype(BF16)
    v = conv(p_v, 2 * c, c)
    v_ref[...] = v.astype(BF16)
    bv_ref[...] = (_seg_sum(r * (kd0 + kd1) * r_k, ones_bd) * v).astype(BF16)

    vt = vt.T
    extra = jnp.where(lax.broadcasted_iota(jnp.int32, (VT_ROWS - V_HEAD, TM), 0) == 0, 1.0, 0.0)
    vm_ref[...] = jnp.concatenate(
        [blk for h in range(N_HEADS) for blk in (vt[h * V_HEAD:(h + 1) * V_HEAD], extra)], axis=0).astype(BF16)


def _front_split_kernel(c_ref, cp_ref, cn_ref, x_ref, xp_ref, xn_ref, *rest, n_ct, nt):
    is_ctx = pl.program_id(1) < n_ct
    _front_kernel(_Pick(is_ctx, c_ref, x_ref), _Pick(is_ctx, cp_ref, xp_ref), _Pick(is_ctx, cn_ref, xn_ref), *rest,
                  n_ct=n_ct, nt=nt)


def _front_call(xs, mod, cos_t, sin_t, prm, n_ct):
    if isinstance(xs, tuple):
        ctx, x = xs
        bsz, tt = x.shape[0], ctx.shape[1] + x.shape[1]
        body, srcs = _front_split_kernel, (ctx, ctx, ctx, x, x, x)
        src_specs = _split_specs(D_MODEL, n_ct, ctx.shape[1], x.shape[1], halo=True)
    else:
        bsz, tt, _ = xs.shape
        body, srcs = _front_kernel, (xs, xs, xs)
        src_specs = [_row_spec(D_MODEL), *_halo_specs(D_MODEL, tt)]
    nt = tt // TM
    tab = pl.BlockSpec((TM, LANES), lambda b, i: (i, 0))
    shared = jax.ShapeDtypeStruct((bsz, tt, D_RWKV), BF16)
    dir_spec = pl.BlockSpec((2, None, TM, D_RWKV), lambda b, i: (0, b, i, 0))
    dir_shape = lambda dt: jax.ShapeDtypeStruct((2, bsz, tt, D_RWKV), dt)
    heads = jax.ShapeDtypeStruct((bsz, tt, D_HEADS), BF16)
    p_rwkv = [prm[n] for n in ("w_in", "conv", "w0", "wb", "a0", "ab", "gb", "vec", "ones_bd")]
    p_mla = [prm[n] for n in ("norm_g", "wq", "wkn", "wv", "head_sel")]
    return pl.pallas_call(
        functools.partial(body, n_ct=n_ct, nt=nt),
        grid=(bsz, nt),
        in_specs=src_specs + [_mod_spec(n_ct)] + [_full_spec(a.shape) for a in p_rwkv]
        + [tab, tab] + [_full_spec(a.shape) for a in p_mla],
        out_specs=[_row_spec(D_RWKV)] * 5 + [dir_spec] * 3 + [_row_spec(D_HEADS)] * 2
        + [pl.BlockSpec((None, N_HEADS * VT_ROWS, TM), lambda b, i: (b, 0, i)),
           pl.BlockSpec((None, None, SUBLANES, LANES), lambda b, i: (b, i, 0, 0))],
        out_shape=[shared] * 5 + [dir_shape(F32), dir_shape(BF16), dir_shape(BF16)] + [heads] * 2
        + [jax.ShapeDtypeStruct((bsz, N_HEADS * VT_ROWS, tt), BF16),
           jax.ShapeDtypeStruct((bsz, nt, SUBLANES, LANES), F32)],
        compiler_params=_cparams(("parallel", "parallel")),
        name="front",
    )(*srcs, mod, *p_rwkv, cos_t, sin_t, *p_mla)


def _wkv_kernel(*refs, nb):
    ins, (yf_ref, yr_ref, s_ref) = refs[:12], refs[12:]
    step = pl.program_id(1)

    @pl.when(step == 0)
    def _():
        s_ref[...] = jnp.zeros_like(s_ref)

    n2 = 2 * CHUNK
    ri = lax.broadcasted_iota(jnp.int32, (n2, n2), 0)
    ci = lax.broadcasted_iota(jnp.int32, (n2, n2), 1)
    same_head = (ri >> LOG_CHUNK) == (ci >> LOG_CHUNK)
    rt = lax.broadcasted_iota(jnp.int32, (CHUNK, n2), 0)
    ct = lax.broadcasted_iota(jnp.int32, (CHUNK, n2), 1) & (CHUNK - 1)
    eye = ct == rt
    ti = lax.broadcasted_iota(jnp.int32, (CHUNK, CHUNK), 0)
    tj = lax.broadcasted_iota(jnp.int32, (CHUNK, CHUNK), 1)
    strict = [ct < rt, ct > rt]
    incl = [ct <= rt, ct >= rt]
    tri = [jnp.where(tj <= ti, 1.0, 0.0).astype(BF16), jnp.where(tj >= ti, 1.0, 0.0).astype(BF16)]

    def bdiag(x):
        return jnp.where(same_head, jnp.concatenate([x, x], axis=0), 0.0)

    def fold_t(x):
        xt = bdiag(x).T
        return xt[:CHUNK] + xt[CHUNK:]

    ch = []
    for d in range(2):
        r_ref, v_ref, kk_ref, lw_ref, kd_ref, bb_ref = ins[6 * d:6 * d + 6]
        for n in range(nb):
            lw = lw_ref[n]
            lc = _mm_exact_lhs(tri[d], lw)
            lend = jnp.sum(lw, axis=0, keepdims=True)
            e_neg = jnp.exp(-lc)
            e_end = jnp.exp(lend - lc)
            p_end = jnp.exp(lend)
            kt_all = kk_ref[n].astype(F32) * jnp.exp(lc - lw)
            rt_all = r_ref[n].astype(F32) * jnp.exp(lc)
            kd, bb = kd_ref[n].astype(F32), bb_ref[n].astype(F32)
            kh_all, bh_all = kd * e_neg, bb * e_neg
            kc_all, bc_all = kd * e_end, bb * e_end
            v_all = v_ref[n]
            for p in range(N_PAIRS):
                sl = slice(p * PAIR, (p + 1) * PAIR)
                ch.append(dict(d=d, slot=(d * nb + n) * N_PAIRS + p, kt=kt_all[:, sl], rr=rt_all[:, sl],
                               vv=bdiag(v_all[:, sl]), kh=bdiag(kh_all[:, sl]), bh=bdiag(bh_all[:, sl]),
                               kc_t=fold_t(kc_all[:, sl]), bc_t=fold_t(bc_all[:, sl]), p_end=p_end[:, sl]))
    for c in ch:
        gram = _mm_nt(jnp.concatenate([c["kt"], c["rr"]], axis=0), jnp.concatenate([c["bh"], c["kh"]], axis=0))
        st, ic = strict[c["d"]], incl[c["d"]]
        c["a_kb"] = jnp.where(st, gram[:CHUNK, :n2], 0.0)
        c["a_kk"] = jnp.where(st, gram[:CHUNK, n2:], 0.0)
        c["a_rb"] = jnp.where(ic, gram[CHUNK:, :n2], 0.0)
        c["a_rk"] = jnp.where(ic, gram[CHUNK:, n2:], 0.0)
        c["t"] = jnp.where(eye, 1.0, 0.0) - jnp.where((rt >> 1) == (ct >> 1), c["a_kb"], 0.0)
    for c in ch:
        av = _mm(jnp.concatenate([c["a_kk"], c["a_rk"]], axis=0), c["vv"])
        c["akk_v"], c["ark_v"] = av[:CHUNK], av[CHUNK:]
        c["kc_v"] = _mm(c["kc_t"], c["vv"])
    for ls in range(1, LOG_CHUNK):
        off = ((rt >> (ls + 1)) == (ct >> (ls + 1))) & ((rt >> ls) != (ct >> ls))
        for c in ch:
            c["nt"] = _mm(jnp.where(off, c["a_kb"], 0.0), bdiag(c["t"]))
        for c in ch:
            c["t"] = c["t"] - _mm(c["t"], bdiag(c["nt"]))
    for c in ch:
        uw = _mm(c["t"], jnp.concatenate([bdiag(c["akk_v"]), bdiag(c["kt"])], axis=1))
        c["uw"] = jnp.concatenate([bdiag(uw[:, :n2]), bdiag(uw[:, n2:])], axis=1)
    for c in ch:
        c["yq"] = jnp.concatenate([c["ark_v"], c["rr"]], axis=1) - _mm(c["a_rb"], c["uw"])
        diag_p = jnp.where(eye, jnp.broadcast_to(c["p_end"], (CHUNK, n2)), 0.0)
        c["nm"] = jnp.concatenate([c["kc_v"], diag_p], axis=1) - _mm(c["bc_t"], c["uw"])
    for c in ch:
        qm_l = jnp.concatenate([c["yq"][:, n2:], c["nm"][:, n2:]], axis=0).astype(BF16)
        s_hi, s_lo = _split2(s_ref[c["slot"]])
        c["qm"] = jnp.dot(qm_l, s_hi, preferred_element_type=F32)
        c["m_lo"] = jnp.dot(qm_l[CHUNK:], s_lo, preferred_element_type=F32)
    ys = []
    for c in ch:
        ys.append(c["qm"][:CHUNK] + c["yq"][:, :n2])
        s_ref[c["slot"]] = bdiag((c["qm"][CHUNK:] + c["m_lo"]) + c["nm"][:, :n2])
    for d, y_ref in enumerate((yf_ref, yr_ref)):
        for n in range(nb):
            o = (d * nb + n) * N_PAIRS
            y_ref[n] = jnp.concatenate(ys[o:o + N_PAIRS], axis=1).astype(y_ref.dtype)


def _wkv_call(r, v, kk, lw, kd, bb, n_ct):
    bsz, tt, c = r.shape
    nc = tt // CHUNK
    n_cc = n_ct * TM // CHUNK
    nb = math.gcd(bsz, WKV_BATCH)

    def rev_chunk(s):
        return jnp.where(s < n_cc, n_cc - 1 - s, nc - 1 - (s - n_cc))

    order = (lambda s: s, rev_chunk)
    specs, args = [], []
    for d in range(2):
        shared = pl.BlockSpec((nb, CHUNK, c), lambda b, s, d=d: (b, order[d](s), 0))
        per_dir = pl.BlockSpec((None, nb, CHUNK, c), lambda b, s, d=d: (d, b, order[d](s), 0))
        specs += [shared, shared, shared, per_dir, per_dir, per_dir]
        args += [r, v, kk, lw, kd, bb]
    out = jax.ShapeDtypeStruct((bsz, tt, c), BF16)
    return pl.pallas_call(
        functools.partial(_wkv_kernel, nb=nb),
        grid=(bsz // nb, nc),
        in_specs=specs,
        out_specs=[specs[0], specs[6]],
        out_shape=[out, out],
        scratch_shapes=[pltpu.VMEM((2 * nb * N_PAIRS, PAIR, PAIR), F32)],
        compiler_params=_cparams(("parallel", "arbitrary")),
        name="wkv_scan",
    )(*args)


def _rows_reduce(x, op, reduce):
    n = x.shape[0]
    while n > SUBLANES and n % 2 == 0:
        n //= 2
        x = op(x[:n], x[n:])
    return reduce(x, axis=0, keepdims=True)


def _attn_kernel(q_ref, k_ref, vt_ref, km_ref, o_ref, *, n_ctx_rows, n_ct, t0):
    i = pl.program_id(2) + t0
    n_keys = k_ref.shape[0]
    kc = ATTN_KEYS if (n_keys - n_ctx_rows) % ATTN_KEYS == 0 else TM
    heads = range(ATTN_HEADS)
    hs = [slice(g * HEAD_PAD, (g + 1) * HEAD_PAD) for g in heads]

    scores = lambda c, g: _mm_nt(k_ref[c[0]:c[0] + c[1], hs[g]], q_ref[:, hs[g]])
    ctx_chunk = [(0, n_ctx_rows)]
    kb = ATTN_KEYS_BOUNDED if (n_keys - n_ctx_rows) % ATTN_KEYS_BOUNDED == 0 else TM
    bounded_lat = ctx_chunk + [(s, kb) for s in range(n_ctx_rows, n_keys, kb)]

    km2 = jnp.max(jnp.max(km_ref[...], axis=0), axis=0, keepdims=True)
    lane = lax.broadcasted_iota(jnp.int32, km2.shape, 1)
    bound = []
    for g in heads:
        qf = q_ref[:, hs[g]].astype(F32)
        q2 = jnp.max(jnp.sum(qf * qf, axis=1, keepdims=True), axis=0, keepdims=True)
        k2 = jnp.max(jnp.where(lane == pl.program_id(1) * ATTN_HEADS + g, km2, 0.0), axis=1, keepdims=True)
        bound.append(jnp.sqrt(q2 * k2) * BOUND_SLACK)
    bound_ok = functools.reduce(jnp.maximum, [jnp.max(b) for b in bound]) < BOUND_LIMIT

    def finish(acc):
        outs = [acc[g][:V_HEAD] / acc[g][V_HEAD:V_HEAD + 1] for g in heads]
        o_ref[...] = jnp.concatenate(outs, axis=0).T.astype(o_ref.dtype)

    def attend_bounded(chunks):
        pieces = [(c, g) for c in chunks for g in heads]
        acc = [None] * ATTN_HEADS
        s_next = scores(*pieces[0])
        for n, (c, g) in enumerate(pieces):
            s = s_next
            if n + 1 < len(pieces):
                s_next = scores(*pieces[n + 1])
            pv = _mm(vt_ref[g * VT_ROWS:(g + 1) * VT_ROWS, c[0]:c[0] + c[1]], jnp.exp2(s - bound[g]))
            acc[g] = pv if acc[g] is None else acc[g] + pv
        finish(acc)

    def attend(chunks):
        s_next = [scores(chunks[0], g) for g in heads]
        m, acc = ([None] * ATTN_HEADS for _ in range(2))
        for n, c in enumerate(chunks):
            s = s_next
            if n + 1 < len(chunks):
                s_next = [scores(chunks[n + 1], g) for g in heads]
            m_c = [_rows_reduce(s[g], jnp.maximum, jnp.max) for g in heads]
            m_new = [m_c[g] if m[g] is None else jnp.maximum(m[g], m_c[g]) for g in heads]
            p = [jnp.exp2(s[g] - m_new[g]) for g in heads]
            pv = [_mm(vt_ref[g * VT_ROWS:(g + 1) * VT_ROWS, c[0]:c[0] + c[1]], p[g]) for g in heads]
            for g in heads:
                acc[g] = pv[g] if m[g] is None else jnp.exp2(m[g] - m_new[g]) * acc[g] + pv[g]
                m[g] = m_new[g]
        finish(acc)

    lat_chunks = ctx_chunk + [(s, kc) for s in range(n_ctx_rows, n_keys, kc)]
    is_lat = i >= n_ct
    pl.when(is_lat & bound_ok)(lambda: attend_bounded(bounded_lat))
    pl.when(is_lat & ~bound_ok)(lambda: attend(lat_chunks))
    if t0 < n_ct:
        pl.when(~is_lat & bound_ok)(lambda: attend_bounded(ctx_chunk))
        pl.when(~is_lat & ~bound_ok)(lambda: attend(ctx_chunk))


def _attn_call(q, k, vt, km, n_ct, t0):
    bsz, tt, _ = q.shape
    nt = tt // TM
    w = ATTN_HEADS * HEAD_PAD
    q_spec = pl.BlockSpec((None, TM, w), lambda b, h, i: (b, i + t0, h))
    k_spec = pl.BlockSpec((None, tt, w), lambda b, h, i: (b, 0, h))
    vt_spec = pl.BlockSpec((None, ATTN_HEADS * VT_ROWS, tt), lambda b, h, i: (b, h, 0))
    o_spec = pl.BlockSpec((None, TM, ATTN_HEADS * V_HEAD), lambda b, h, i: (b, i + t0, h))
    km_spec = pl.BlockSpec((None, nt, SUBLANES, LANES), lambda b, h, i: (b, 0, 0, 0))
    return pl.pallas_call(
        functools.partial(_attn_kernel, n_ctx_rows=n_ct * TM, n_ct=n_ct, t0=t0),
        grid=(bsz, N_HEADS // ATTN_HEADS, nt - t0),
        in_specs=[q_spec, k_spec, vt_spec, km_spec],
        out_specs=o_spec,
        out_shape=jax.ShapeDtypeStruct((bsz, tt, N_HEADS * V_HEAD), BF16),
        compiler_params=_cparams(("parallel", "parallel", "arbitrary")),
        name="attention",
    )(q, k, vt, km)


def _out_kernel(yf_ref, yr_ref, bv_ref, g_ref, o_ref, x_ref, mod_ref, gn_ref, ones_ref, wor_ref, wom_ref, ln_ref,
                out_ref):
    ones_bd = ones_ref[...]
    y = yf_ref[...].astype(F32) + yr_ref[...].astype(F32)
    inv_n = 1.0 / RWKV_HEAD
    yc = y - _seg_sum(y, ones_bd) * inv_n
    var = _seg_sum(yc * yc, ones_bd) * inv_n
    gn = yc * lax.rsqrt(var + GN_EPS) * gn_ref[0:1] + gn_ref[1:2]
    ro = (gn + bv_ref[...]) * g_ref[...]
    m = _mm(ro, wor_ref[...]) + jnp.dot(o_ref[...], wom_ref[...], preferred_element_type=F32)
    mod = mod_ref[...]
    out_ref[...] = _layer_norm(DEEPNORM_ALPHA * x_ref[...] + mod[2:3] * m, ln_ref[0:1], ln_ref[1:2])


def _out_split_kernel(yf_ref, yr_ref, bv_ref, g_ref, o_ref, c_ref, x_ref, *rest, n_ct):
    _out_kernel(yf_ref, yr_ref, bv_ref, g_ref, o_ref, _Pick(pl.program_id(1) < n_ct, c_ref, x_ref), *rest)


def _out_call(yf, yr, bv, g, o, xs, mod, prm, n_ct, t0):
    if isinstance(xs, tuple):
        assert t0 == 0
        ctx, x = xs
        bsz, tt = x.shape[0], ctx.shape[1] + x.shape[1]
        body, srcs = functools.partial(_out_split_kernel, n_ct=n_ct), (ctx, x)
        src_specs = _split_specs(D_MODEL, n_ct, ctx.shape[1], x.shape[1], halo=False)
    else:
        bsz, tt, _ = xs.shape
        body, srcs, src_specs = _out_kernel, (xs,), [_row_spec(D_MODEL, t0)]
    nt = tt // TM
    params = [prm[n] for n in ("gn", "ones_bd", "wo_r", "wo_m", "ln1")]
    return pl.pallas_call(
        body,
        grid=(bsz, nt - t0),
        in_specs=[_row_spec(D_RWKV, t0)] * 4 + [_row_spec(N_HEADS * V_HEAD, t0)] + src_specs + [_mod_spec(n_ct, t0)]
        + [_full_spec(a.shape) for a in params],
        out_specs=_row_spec(D_MODEL),
        out_shape=jax.ShapeDtypeStruct((bsz, tt - t0 * TM, D_MODEL), F32),
        compiler_params=_cparams(("parallel", "parallel")),
        name="mixer_out",
    )(yf, yr, bv, g, o, *srcs, mod, *params)


def _ffn_kernel(x_ref, xp_ref, xn_ref, mod_ref, wup_ref, conv_ref, wd_ref, ln_ref, out_ref, *, n_ct, nt):
    i = pl.program_id(1)
    mod = mod_ref[...]
    u = _with_halo(x_ref, xp_ref, xn_ref, 1.0 + mod[4:5], mod[3:4], i, n_ct, nt)

    def up(j):
        hs = []
        for o in (j * FF_SUB, D_FF + j * FF_SUB):
            h = jnp.dot(u, wup_ref[:, o:o + FF_SUB], preferred_element_type=F32)
            taps = conv_ref[:, o:o + FF_SUB]
            hs.append(_conv3(h, taps) + taps[3:4])
        return hs

    acc = None
    acts = []
    h_next = up(0)
    for j in range(N_FF_SUB):
        hg, hv = h_next
        if j + 1 < N_FF_SUB:
            h_next = up(j + 1)
        half = 0.5 * hg
        acts.append(((half + half * jnp.tanh(half)) * hv).astype(BF16))
        if len(acts) == FF_DOWN or j + 1 == N_FF_SUB:
            lo = (j + 1 - len(acts)) * FF_SUB
            part = jnp.dot(jnp.concatenate(acts, axis=1), wd_ref[lo:(j + 1) * FF_SUB, :], preferred_element_type=F32)
            acc = part if acc is None else acc + part
            acts = []
    out_ref[...] = _layer_norm(DEEPNORM_ALPHA * x_ref[...] + mod[5:6] * acc, ln_ref[0:1], ln_ref[1:2])


def _ffn_call(xs, mod, prm, n_ct):
    bsz, tt, _ = xs.shape
    nt = tt // TM
    prev, nxt = _halo_specs(D_MODEL, tt)
    w_up, conv, w_down, ln2 = prm["w_up"], prm["ffn_conv"], prm["w_down"], prm["ln2"]
    return pl.pallas_call(
        functools.partial(_ffn_kernel, n_ct=n_ct, nt=nt),
        grid=(bsz, nt),
        in_specs=[_row_spec(D_MODEL), prev, nxt, _mod_spec(n_ct), _full_spec(w_up.shape, True),
                  _full_spec(conv.shape), _full_spec(w_down.shape, True), _full_spec(ln2.shape)],
        out_specs=_row_spec(D_MODEL),
        out_shape=jax.ShapeDtypeStruct((bsz, tt, D_MODEL), F32),
        compiler_params=_cparams(("parallel", "parallel")),
        name="conv_ffn",
    )(xs, xs, xs, mod, w_up, conv, w_down, ln2)


def _pad_rows(rows, width):
    rows = [r.astype(F32) if r.shape[0] == width else jnp.pad(r.astype(F32), (0, width - r.shape[0])) for r in rows]
    return jnp.pad(jnp.stack(rows), ((0, SUBLANES - len(rows)), (0, 0)))


def _layer_params(l, w_in, rwkv_conv, w0, w_b, a0, a_b, g_b, k_k, k_a, r_k, gn_g, gn_b, q_norm_g, w_uq,
                  kv_norm_g, w_ukv, w_o, ln1_g, ln1_b, w_up, ffn_conv_w, ffn_conv_b, w_down, ln2_g, ln2_b):
    c = D_RWKV
    wi = w_in[l]
    o_mla = RWKV_COLS
    w_kr = wi[:, o_mla + Q_LORA + KV_LORA:]
    pad_l = jnp.zeros((D_MODEL, QK_NOPE), F32)
    pad_r = jnp.zeros((D_MODEL, LANES - QK_DIM), F32)
    w_in_p = jnp.concatenate(
        [wi[:, :o_mla + Q_LORA + KV_LORA], pad_l, w_kr, pad_r], axis=1).astype(BF16)

    def lora_bd(w):
        z = jnp.zeros_like(w[0])
        return jnp.concatenate([jnp.concatenate([w[0], z], 1), jnp.concatenate([z, w[1]], 1)], 0).astype(BF16)

    head = jnp.arange(2 * LANES) // RWKV_HEAD
    ones_bd = (head[:, None] == head[None, :]).astype(BF16)

    wq = w_uq[l].reshape(Q_LORA, N_HEADS, QK_DIM)
    q_nope, q_rope = wq[..., :QK_NOPE], wq[..., QK_NOPE:]
    zq = lambda n: jnp.zeros((Q_LORA, N_HEADS, n), F32)
    wq = jnp.concatenate([q_nope, q_rope, zq(LANES - QK_DIM)], -1).reshape(Q_LORA, D_HEADS).astype(BF16)
    wkv = w_ukv[l].reshape(KV_LORA, N_HEADS, QK_NOPE + V_HEAD)
    zk = jnp.zeros((KV_LORA, N_HEADS, HEAD_PAD - QK_NOPE), F32)
    wkn = jnp.concatenate([wkv[..., :QK_NOPE], zk], -1).reshape(KV_LORA, D_HEADS).astype(BF16)
    wv = wkv[..., QK_NOPE:].reshape(KV_LORA, N_HEADS * V_HEAD).astype(BF16)
    wo = w_o[l]

    return {
        "w_in": w_in_p,
        "conv": _pad_rows(list(rwkv_conv[l]), RWKV_COLS),
        "w0": _pad_rows([w0[l].reshape(-1)], 2 * c),
        "wb": lora_bd(w_b[l]),
        "a0": _pad_rows([a0[l].reshape(-1)], 2 * c),
        "ab": lora_bd(a_b[l]),
        "gb": g_b[l].astype(BF16),
        "vec": _pad_rows([k_k[l], k_a[l], r_k[l].reshape(-1)], c),
        "ones_bd": ones_bd,
        "norm_g": _pad_rows([q_norm_g[l], kv_norm_g[l]], Q_LORA),
        "wq": wq, "wkn": wkn, "wv": wv,
        "head_sel": (jnp.arange(D_HEADS)[:, None] // HEAD_PAD == jnp.arange(LANES)[None, :]).astype(BF16),
        "gn": _pad_rows([gn_g[l], gn_b[l]], c),
        "wo_r": wo[:c].astype(BF16),
        "wo_m": wo[c:].astype(BF16),
        "ln1": _pad_rows([ln1_g[l], ln1_b[l]], D_MODEL),
        "w_up": w_up[l].astype(BF16),
        "ffn_conv": _pad_rows(list(ffn_conv_w[l]) + [ffn_conv_b[l]], 2 * D_FF),
        "w_down": w_down[l].astype(BF16),
        "ln2": _pad_rows([ln2_g[l], ln2_b[l]], D_MODEL),
    }


def _rope_tables(n_ctx, n_lat):
    rows = n_lat // GRID_W
    row = jnp.repeat(jnp.arange(rows, dtype=F32), GRID_W)
    col = jnp.tile(jnp.arange(GRID_W, dtype=F32), rows)
    n_pairs = QK_ROPE // 4
    inv = ROPE_BASE ** (-jnp.arange(n_pairs, dtype=F32) / n_pairs)
    ang = jnp.concatenate([row[:, None] * inv, col[:, None] * inv], axis=-1)
    cos = jnp.repeat(jnp.cos(ang), 2, axis=-1)
    sin = jnp.repeat(jnp.sin(ang), 2, axis=-1) * jnp.tile(jnp.array([-1.0, 1.0], F32), QK_ROPE // 2)
    tt = n_ctx + n_lat
    cos_t = jnp.ones((tt, LANES), F32).at[n_ctx:, QK_NOPE:QK_DIM].set(cos)
    sin_t = jnp.zeros((tt, LANES), F32).at[n_ctx:, QK_NOPE:QK_DIM].set(sin)
    return cos_t, sin_t


def kernel(x, c, ctx, c_ctx, w_ada, b_ada, w_in, rwkv_conv, w0, w_b, a0, a_b, g_b, k_k, k_a, r_k, gn_g, gn_b,
           q_norm_g, w_uq, kv_norm_g, w_ukv, w_o, ln1_g, ln1_b, w_up, ffn_conv_w, ffn_conv_b, w_down, ln2_g, ln2_b):
    bsz, n_lat, _ = x.shape
    n_ctx = ctx.shape[1]
    assert n_ctx % TM == 0 and n_lat % TM == 0 and n_lat % GRID_W == 0
    n_ct = n_ctx // TM
    n_l = w_ada.shape[0]

    ada_rows = -(-(bsz + 1) // SUBLANES) * SUBLANES
    cc = jnp.zeros((ada_rows, D_MODEL), F32).at[:bsz].set(c).at[bsz].set(c_ctx)
    mods = _ada_call(cc, w_ada, b_ada).reshape(n_l, ada_rows, 6, D_MODEL)
    cos_t, sin_t = _rope_tables(n_ctx, n_lat)
    xs = (ctx, x)

    for l in range(n_l):
        prm = _layer_params(l, w_in, rwkv_conv, w0, w_b, a0, a_b, g_b, k_k, k_a, r_k, gn_g, gn_b, q_norm_g, w_uq,
                            kv_norm_g, w_ukv, w_o, ln1_g, ln1_b, w_up, ffn_conv_w, ffn_conv_b, w_down, ln2_g, ln2_b)
        m_ctx = jnp.broadcast_to(mods[l, bsz][None], (bsz, 6, D_MODEL))
        mod = jnp.stack([m_ctx, mods[l, :bsz]], axis=1)
        mod = jnp.pad(mod, ((0, 0), (0, 0), (0, SUBLANES - 6), (0, 0))).reshape(2 * bsz, SUBLANES, D_MODEL)

        last = l == n_l - 1
        t0 = n_ct if last else 0
        if last and isinstance(xs, tuple):
            xs = jnp.concatenate(xs, axis=1)
        r, v, kk, g, bv, lw, kd, bb, q, k, vm, km = _front_call(xs, mod, cos_t, sin_t, prm, n_ct)
        yf, yr = _wkv_call(r, v, kk, lw, kd, bb, n_ct)
        o = _attn_call(q, k, vm, km, n_ct, t0)
        x1 = _out_call(yf, yr, bv, g, o, xs, mod, prm, n_ct, t0)
        xs = _ffn_call(x1, mod, prm, n_ct - t0)
    return xs
```

```python
import functools
import math

import jax
import jax.numpy as jnp
from jax import lax
from jax.experimental import pallas as pl
from jax.experimental.pallas import tpu as pltpu

F32 = jnp.float32
BF16 = jnp.bfloat16

D_MODEL = 1024
DEPTH = 2
GRID_W = 64
D_RWKV = 512
RWKV_HEAD = 64
N_HEADS = 8
LORA_W = 64
LORA_A = 64
LORA_G = 128
RWKV_COLS = 3 * D_RWKV + 2 * LORA_W + 2 * LORA_A + LORA_G
QK_NOPE = 64
QK_ROPE = 32
QK_DIM = QK_NOPE + QK_ROPE
V_HEAD = 64
Q_LORA = 384
KV_LORA = 256
D_FF = 2816
ROPE_BASE = 10000.0
DEEPNORM_ALPHA = (2 * DEPTH) ** 0.25
LN_EPS = 1e-5
RMS_EPS = 1e-6
GN_EPS = 64e-5
LOG2_E = 1.4426950408889634

LANES = 128
SUBLANES = 8
HEAD_PAD = LANES
D_HEADS = N_HEADS * HEAD_PAD
TM = 256
HALO = SUBLANES
CHUNK = 64
LOG_CHUNK = 6
WKV_BATCH = 4
PAIR = 2 * RWKV_HEAD
N_PAIRS = N_HEADS // 2
VT_ROWS = 80
ATTN_KEYS = 1024
ATTN_KEYS_BOUNDED = 2048
BOUND_LIMIT = 60.0
BOUND_SLACK = 1.01
ATTN_HEADS = 8
FF_SUB = 256
N_FF_SUB = D_FF // FF_SUB
FF_DOWN = 6
VMEM_LIMIT = 56 << 20


def _cparams(sem):
    return pltpu.CompilerParams(dimension_semantics=sem, vmem_limit_bytes=VMEM_LIMIT)


def _mm(a, b):
    return jnp.dot(a.astype(BF16), b.astype(BF16), preferred_element_type=F32)


def _mm_nt(a, b):
    return lax.dot_general(a.astype(BF16), b.astype(BF16), (((1,), (1,)), ((), ())),
                           preferred_element_type=F32)


def _split2(x):
    hi = x.astype(BF16)
    lo = (x - hi.astype(F32)).astype(BF16)
    return hi, lo


def _mm3(a, b):
    ah, al = _split2(a)
    bh, bl = _split2(b)
    d = functools.partial(jnp.dot, preferred_element_type=F32)
    return d(ah, bh) + (d(ah, bl) + d(al, bh))


def _mm_exact_rhs(a, b_bf16):
    a1, a2 = _split2(a)
    d = functools.partial(jnp.dot, preferred_element_type=F32)
    return d(a1, b_bf16) + d(a2, b_bf16)


def _mm_exact_lhs(a_bf16, b):
    b1 = b.astype(BF16)
    r1 = b - b1.astype(F32)
    b2 = r1.astype(BF16)
    b3 = (r1 - b2.astype(F32)).astype(BF16)
    d = functools.partial(jnp.dot, preferred_element_type=F32)
    return d(a_bf16, b1) + (d(a_bf16, b2) + d(a_bf16, b3))


def _sigmoid(x):
    return 1.0 / (1.0 + jnp.exp(-x))


def _layer_norm(x, g, b):
    mu = jnp.mean(x, axis=-1, keepdims=True)
    xc = x - mu
    var = jnp.mean(xc * xc, axis=-1, keepdims=True)
    return xc * lax.rsqrt(var + LN_EPS) * g + b


def _rms_norm(x, g):
    return x * lax.rsqrt(jnp.mean(x * x, axis=-1, keepdims=True) + RMS_EPS) * g


def _row_spec(cols, t0=0):
    return pl.BlockSpec((None, TM, cols), lambda b, i, *_: (b, i + t0, 0))


def _mod_spec(n_ct, t0=0):
    return pl.BlockSpec((None, SUBLANES, D_MODEL),
                        lambda b, i, *_: (2 * b + (i + t0 >= n_ct).astype(jnp.int32), 0, 0))


def _full_spec(shape, single_buffer=False):
    nd = len(shape)
    mode = dict(pipeline_mode=pl.Buffered(1)) if single_buffer else {}
    return pl.BlockSpec(shape, lambda *_: (0,) * nd, **mode)


def _halo_specs(cols, tt, t0=0):
    per = TM // HALO
    last = tt // HALO - 1
    prev = pl.BlockSpec((None, HALO, cols), lambda b, i, *_: (b, jnp.maximum((i + t0) * per - 1, 0), 0))
    nxt = pl.BlockSpec((None, HALO, cols), lambda b, i, *_: (b, jnp.minimum((i + t0 + 1) * per, last), 0))
    return prev, nxt


def _with_halo(x_ref, xp_ref, xn_ref, scale, shift, i, n_ct, nt):
    has_prev = (i != 0) & (i != n_ct)
    has_next = (i != n_ct - 1) & (i != nt - 1)
    mod = lambda x: x * scale + shift
    return jnp.concatenate([jnp.where(has_prev, mod(xp_ref[...]), 0.0), mod(x_ref[...]),
                            jnp.where(has_next, mod(xn_ref[...]), 0.0)], axis=0).astype(BF16)


class _Pick:
    def __init__(self, pred, a, b):
        self.pred, self.a, self.b = pred, a, b

    def __getitem__(self, idx):
        return jnp.where(self.pred, self.a[idx], self.b[idx])


def _split_specs(cols, n_ct, n_ctx, n_lat, halo):
    per = TM // HALO
    out = []
    for first, rows in ((0, n_ctx), (n_ct, n_lat)):
        last_t, last_h = rows // TM - 1, rows // HALO - 1
        t = lambda i, first=first, last_t=last_t: jnp.clip(i - first, 0, last_t)
        out.append(pl.BlockSpec((None, TM, cols), lambda b, i, t=t: (b, t(i), 0)))
        if halo:
            out.append(pl.BlockSpec((None, HALO, cols),
                                    lambda b, i, first=first, last_h=last_h: (b, jnp.clip((i - first) * per - 1, 0, last_h), 0)))
            out.append(pl.BlockSpec((None, HALO, cols),
                                    lambda b, i, first=first, last_h=last_h: (b, jnp.clip((i - first + 1) * per, 0, last_h), 0)))
    return out


def _conv3(h, taps):
    rows = TM + 2 * HALO
    lo, hi = HALO, HALO + TM
    return (pltpu.roll(h, 1, 0)[lo:hi] * taps[0:1] + h[lo:hi] * taps[1:2]
            + pltpu.roll(h, rows - 1, 0)[lo:hi] * taps[2:3])


ADA_COLS = 1536


def _ada_kernel(c_ref, w_ref, b_ref, o_ref):
    c = c_ref[...]
    o_ref[...] = _mm3(c * _sigmoid(c), w_ref[...]) + b_ref[...]


def _ada_call(cc, w_ada, b_ada):
    n_l = w_ada.shape[0]
    rows = cc.shape[0]
    return pl.pallas_call(
        _ada_kernel,
        grid=(n_l, 6 * D_MODEL // ADA_COLS),
        in_specs=[
            pl.BlockSpec((rows, D_MODEL), lambda l, j: (0, 0)),
            pl.BlockSpec((None, D_MODEL, ADA_COLS), lambda l, j: (l, 0, j)),
            pl.BlockSpec((None, 1, ADA_COLS), lambda l, j: (l, 0, j)),
        ],
        out_specs=pl.BlockSpec((None, rows, ADA_COLS), lambda l, j: (l, 0, j)),
        out_shape=jax.ShapeDtypeStruct((n_l, rows, 6 * D_MODEL), F32),
        compiler_params=_cparams(("arbitrary", "arbitrary")),
        name="ada",
    )(cc, w_ada, b_ada.reshape(n_l, 1, 6 * D_MODEL))


def _seg_sum(x, ones_bd):
    w = ones_bd.shape[0]
    return jnp.concatenate([_mm_exact_rhs(x[:, o:o + w], ones_bd) for o in range(0, x.shape[1], w)], axis=1)


def _front_kernel(x_ref, xp_ref, xn_ref, mod_ref, w_ref, conv_ref, w0_ref, wb_ref, a0_ref, ab_ref, gb_ref, vec_ref,
                  ones_ref, cos_ref, sin_ref, ng_ref, wq_ref, wkn_ref, wv_ref, hsel_ref,
                  r_ref, v_ref, kk_ref, g_ref, bv_ref, lw_ref, kd_ref, bb_ref, q_ref, k_ref, vm_ref, km_ref, *, n_ct, nt):
    i = pl.program_id(1)
    m = mod_ref[...]
    u = _with_halo(x_ref, xp_ref, xn_ref, 1.0 + m[1:2], m[0:1], i, n_ct, nt)
    d = functools.partial(jnp.dot, preferred_element_type=F32)
    c = D_RWKV
    o_lora = 3 * c
    proj = lambda o, n: d(u, w_ref[:, o:o + n])
    conv = lambda p, o, n: _conv3(p, conv_ref[:, o:o + n])
    vec = vec_ref[...]
    k_k, k_a, r_k = vec[0:1], vec[1:2], vec[2:3]
    ones_bd = ones_ref[...]

    pm = proj(RWKV_COLS, w_ref.shape[1] - RWKV_COLS)[HALO:HALO + TM]
    p_lo = proj(o_lora, RWKV_COLS - o_lora)
    cq, ckv, kr_x = pm[:, :Q_LORA], pm[:, Q_LORA:Q_LORA + KV_LORA], pm[:, Q_LORA + KV_LORA:]
    cqn = _rms_norm(cq, ng_ref[0:1, :Q_LORA]).astype(BF16)
    ckvn = _rms_norm(ckv, ng_ref[1:2, :KV_LORA]).astype(BF16)

    p_k = proj(c, c)
    z_lo = conv(p_lo, o_lora, RWKV_COLS - o_lora)
    w_lo, a_lo, g_lo = z_lo[:, :2 * LORA_W], z_lo[:, 2 * LORA_W:2 * LORA_W + 2 * LORA_A], z_lo[:, 2 * LORA_W + 2 * LORA_A:]
    lw = -math.exp(-0.5) * _sigmoid(w0_ref[0:1, :] + _mm(jnp.tanh(w_lo), wb_ref[...]))
    lw_ref[0] = lw[:, :c]
    lw_ref[1] = lw[:, c:]
    a = _sigmoid(a0_ref[0:1, :] + _mm(a_lo, ab_ref[...]))
    g_ref[...] = _mm(_sigmoid(g_lo), gb_ref[...]).astype(BF16)

    q = d(cqn, wq_ref[...])
    p_r = proj(0, c)
    k = conv(p_k, c, c)
    kk = k * k_k
    nrm = jnp.sqrt(_seg_sum(kk * kk, ones_bd))
    kk = kk / jnp.maximum(nrm, 1e-12)
    kd0 = k * (1.0 + (a[:, :c] - 1.0) * k_a)
    kd1 = k * (1.0 + (a[:, c:] - 1.0) * k_a)
    kk_ref[...] = kk.astype(BF16)
    kd_ref[0] = kd0.astype(BF16)
    kd_ref[1] = kd1.astype(BF16)
    bb_ref[0] = (kk * a[:, :c]).astype(BF16)
    bb_ref[1] = (kk * a[:, c:]).astype(BF16)

    k_nope = d(ckvn, wkn_ref[...])
    vt = d(ckvn, wv_ref[...])
    p_v = proj(2 * c, c)

    def rotary(x, cos, sin_signed):
        n = x.shape[1]
        even = (lax.broadcasted_iota(jnp.int32, x.shape, 1) & 1) == 0
        partner = jnp.where(even, pltpu.roll(x, n - 1, 1), pltpu.roll(x, 1, 1))
        return x * cos + partner * sin_signed

    cos, sin = cos_ref[...], sin_ref[...]
    cos8 = jnp.concatenate([cos] * N_HEADS, axis=1)
    sin8 = jnp.concatenate([sin] * N_HEADS, axis=1)
    q_ref[...] = (rotary(q, cos8, sin8) * (QK_DIM ** -0.5 * LOG2_E)).astype(BF16)
    kr = rotary(kr_x, cos, sin)
    k_bf = (k_nope + jnp.concatenate([kr] * N_HEADS, axis=1)).astype(BF16)
    k_ref[...] = k_bf
    k2 = _mm(k_bf.astype(F32) * k_bf.astype(F32), hsel_ref[...])
    km_ref[...] = jnp.broadcast_to(_rows_reduce(k2, jnp.maximum, jnp.max), (SUBLANES, LANES))

    r = conv(p_r, 0, c)
    r_ref[...] = r.astype(BF16)
    v = conv(p_v, 2 * c, c)
    v_ref[...] = v.astype(BF16)
    bv_ref[...] = (_seg_sum(r * (kd0 + kd1) * r_k, ones_bd) * v).astype(BF16)

    vt = vt.T
    extra = jnp.where(lax.broadcasted_iota(jnp.int32, (VT_ROWS - V_HEAD, TM), 0) == 0, 1.0, 0.0)
    vm_ref[...] = jnp.concatenate(
        [blk for h in range(N_HEADS) for blk in (vt[h * V_HEAD:(h + 1) * V_HEAD], extra)], axis=0).astype(BF16)


def _front_split_kernel(c_ref, cp_ref, cn_ref, x_ref, xp_ref, xn_ref, *rest, n_ct, nt):
    is_ctx = pl.program_id(1) < n_ct
    _front_kernel(_Pick(is_ctx, c_ref, x_ref), _Pick(is_ctx, cp_ref, xp_ref), _Pick(is_ctx, cn_ref, xn_ref), *rest,
                  n_ct=n_ct, nt=nt)


def _front_call(xs, mod, cos_t, sin_t, prm, n_ct):
    if isinstance(xs, tuple):
        ctx, x = xs
        bsz, tt = x.shape[0], ctx.shape[1] + x.shape[1]
        body, srcs = _front_split_kernel, (ctx, ctx, ctx, x, x, x)
        src_specs = _split_specs(D_MODEL, n_ct, ctx.shape[1], x.shape[1], halo=True)
    else:
        bsz, tt, _ = xs.shape
        body, srcs = _front_kernel, (xs, xs, xs)
        src_specs = [_row_spec(D_MODEL), *_halo_specs(D_MODEL, tt)]
    nt = tt // TM
    tab = pl.BlockSpec((TM, LANES), lambda b, i: (i, 0))
    shared = jax.ShapeDtypeStruct((bsz, tt, D_RWKV), BF16)
    dir_spec = pl.BlockSpec((2, None, TM, D_RWKV), lambda b, i: (0, b, i, 0))
    dir_shape = lambda dt: jax.ShapeDtypeStruct((2, bsz, tt, D_RWKV), dt)
    heads = jax.ShapeDtypeStruct((bsz, tt, D_HEADS), BF16)
    p_rwkv = [prm[n] for n in ("w_in", "conv", "w0", "wb", "a0", "ab", "gb", "vec", "ones_bd")]
    p_mla = [prm[n] for n in ("norm_g", "wq", "wkn", "wv", "head_sel")]
    return pl.pallas_call(
        functools.partial(body, n_ct=n_ct, nt=nt),
        grid=(bsz, nt),
        in_specs=src_specs + [_mod_spec(n_ct)] + [_full_spec(a.shape) for a in p_rwkv]
        + [tab, tab] + [_full_spec(a.shape) for a in p_mla],
        out_specs=[_row_spec(D_RWKV)] * 5 + [dir_spec] * 3 + [_row_spec(D_HEADS)] * 2
        + [pl.BlockSpec((None, N_HEADS * VT_ROWS, TM), lambda b, i: (b, 0, i)),
           pl.BlockSpec((None, None, SUBLANES, LANES), lambda b, i: (b, i, 0, 0))],
        out_shape=[shared] * 5 + [dir_shape(F32), dir_shape(BF16), dir_shape(BF16)] + [heads] * 2
        + [jax.ShapeDtypeStruct((bsz, N_HEADS * VT_ROWS, tt), BF16),
           jax.ShapeDtypeStruct((bsz, nt, SUBLANES, LANES), F32)],
        compiler_params=_cparams(("parallel", "parallel")),
        name="front",
    )(*srcs, mod, *p_rwkv, cos_t, sin_t, *p_mla)


def _wkv_kernel(*refs, nb):
    ins, (yf_ref, yr_ref, s_ref) = refs[:12], refs[12:]
    step = pl.program_id(1)

    @pl.when(step == 0)
    def _():
        s_ref[...] = jnp.zeros_like(s_ref)

    n2 = 2 * CHUNK
    ri = lax.broadcasted_iota(jnp.int32, (n2, n2), 0)
    ci = lax.broadcasted_iota(jnp.int32, (n2, n2), 1)
    same_head = (ri >> LOG_CHUNK) == (ci >> LOG_CHUNK)
    rt = lax.broadcasted_iota(jnp.int32, (CHUNK, n2), 0)
    ct = lax.broadcasted_iota(jnp.int32, (CHUNK, n2), 1) & (CHUNK - 1)
    eye = ct == rt
    ti = lax.broadcasted_iota(jnp.int32, (CHUNK, CHUNK), 0)
    tj = lax.broadcasted_iota(jnp.int32, (CHUNK, CHUNK), 1)
    strict = [ct < rt, ct > rt]
    incl = [ct <= rt, ct >= rt]
    tri = [jnp.where(tj <= ti, 1.0, 0.0).astype(BF16), jnp.where(tj >= ti, 1.0, 0.0).astype(BF16)]

    def bdiag(x):
        return jnp.where(same_head, jnp.concatenate([x, x], axis=0), 0.0)

    def fold_t(x):
        xt = bdiag(x).T
        return xt[:CHUNK] + xt[CHUNK:]

    ch = []
    for d in range(2):
        r_ref, v_ref, kk_ref, lw_ref, kd_ref, bb_ref = ins[6 * d:6 * d + 6]
        for n in range(nb):
            lw = lw_ref[n]
            lc = _mm_exact_lhs(tri[d], lw)
            lend = jnp.sum(lw, axis=0, keepdims=True)
            e_neg = jnp.exp(-lc)
            e_end = jnp.exp(lend - lc)
            p_end = jnp.exp(lend)
            kt_all = kk_ref[n].astype(F32) * jnp.exp(lc - lw)
            rt_all = r_ref[n].astype(F32) * jnp.exp(lc)
            kd, bb = kd_ref[n].astype(F32), bb_ref[n].astype(F32)
            kh_all, bh_all = kd * e_neg, bb * e_neg
            kc_all, bc_all = kd * e_end, bb * e_end
            v_all = v_ref[n]
            for p in range(N_PAIRS):
                sl = slice(p * PAIR, (p + 1) * PAIR)
                ch.append(dict(d=d, slot=(d * nb + n) * N_PAIRS + p, kt=kt_all[:, sl], rr=rt_all[:, sl],
                               vv=bdiag(v_all[:, sl]), kh=bdiag(kh_all[:, sl]), bh=bdiag(bh_all[:, sl]),
                               kc_t=fold_t(kc_all[:, sl]), bc_t=fold_t(bc_all[:, sl]), p_end=p_end[:, sl]))
    for c in ch:
        gram = _mm_nt(jnp.concatenate([c["kt"], c["rr"]], axis=0), jnp.concatenate([c["bh"], c["kh"]], axis=0))
        st, ic = strict[c["d"]], incl[c["d"]]
        c["a_kb"] = jnp.where(st, gram[:CHUNK, :n2], 0.0)
        c["a_kk"] = jnp.where(st, gram[:CHUNK, n2:], 0.0)
        c["a_rb"] = jnp.where(ic, gram[CHUNK:, :n2], 0.0)
        c["a_rk"] = jnp.where(ic, gram[CHUNK:, n2:], 0.0)
        c["t"] = jnp.where(eye, 1.0, 0.0) - jnp.where((rt >> 1) == (ct >> 1), c["a_kb"], 0.0)
    for c in ch:
        av = _mm(jnp.concatenate([c["a_kk"], c["a_rk"]], axis=0), c["vv"])
        c["akk_v"], c["ark_v"] = av[:CHUNK], av[CHUNK:]
        c["kc_v"] = _mm(c["kc_t"], c["vv"])
    for ls in range(1, LOG_CHUNK):
        off = ((rt >> (ls + 1)) == (ct >> (ls + 1))) & ((rt >> ls) != (ct >> ls))
        for c in ch:
            c["nt"] = _mm(jnp.where(off, c["a_kb"], 0.0), bdiag(c["t"]))
        for c in ch:
            c["t"] = c["t"] - _mm(c["t"], bdiag(c["nt"]))
    for c in ch:
        uw = _mm(c["t"], jnp.concatenate([bdiag(c["akk_v"]), bdiag(c["kt"])], axis=1))
        c["uw"] = jnp.concatenate([bdiag(uw[:, :n2]), bdiag(uw[:, n2:])], axis=1)
    for c in ch:
        c["yq"] = jnp.concatenate([c["ark_v"], c["rr"]], axis=1) - _mm(c["a_rb"], c["uw"])
        diag_p = jnp.where(eye, jnp.broadcast_to(c["p_end"], (CHUNK, n2)), 0.0)
        c["nm"] = jnp.concatenate([c["kc_v"], diag_p], axis=1) - _mm(c["bc_t"], c["uw"])
    for c in ch:
        qm_l = jnp.concatenate([c["yq"][:, n2:], c["nm"][:, n2:]], axis=0).astype(BF16)
        s_hi, s_lo = _split2(s_ref[c["slot"]])
        c["qm"] = jnp.dot(qm_l, s_hi, preferred_element_type=F32)
        c["m_lo"] = jnp.dot(qm_l[CHUNK:], s_lo, preferred_element_type=F32)
    ys = []
    for c in ch:
        ys.append(c["qm"][:CHUNK] + c["yq"][:, :n2])
        s_ref[c["slot"]] = bdiag((c["qm"][CHUNK:] + c["m_lo"]) + c["nm"][:, :n2])
    for d, y_ref in enumerate((yf_ref, yr_ref)):
        for n in range(nb):
            o = (d * nb + n) * N_PAIRS
            y_ref[n] = jnp.concatenate(ys[o:o + N_PAIRS], axis=1).astype(y_ref.dtype)


def _wkv_call(r, v, kk, lw, kd, bb, n_ct):
    bsz, tt, c = r.shape
    nc = tt // CHUNK
    n_cc = n_ct * TM // CHUNK
    nb = math.gcd(bsz, WKV_BATCH)

    def rev_chunk(s):
        return jnp.where(s < n_cc, n_cc - 1 - s, nc - 1 - (s - n_cc))

    order = (lambda s: s, rev_chunk)
    specs, args = [], []
    for d in range(2):
        shared = pl.BlockSpec((nb, CHUNK, c), lambda b, s, d=d: (b, order[d](s), 0))
        per_dir = pl.BlockSpec((None, nb, CHUNK, c), lambda b, s, d=d: (d, b, order[d](s), 0))
        specs += [shared, shared, shared, per_dir, per_dir, per_dir]
        args += [r, v, kk, lw, kd, bb]
    out = jax.ShapeDtypeStruct((bsz, tt, c), BF16)
    return pl.pallas_call(
        functools.partial(_wkv_kernel, nb=nb),
        grid=(bsz // nb, nc),
        in_specs=specs,
        out_specs=[specs[0], specs[6]],
        out_shape=[out, out],
        scratch_shapes=[pltpu.VMEM((2 * nb * N_PAIRS, PAIR, PAIR), F32)],
        compiler_params=_cparams(("parallel", "arbitrary")),
        name="wkv_scan",
    )(*args)


def _rows_reduce(x, op, reduce):
    n = x.shape[0]
    while n > SUBLANES and n % 2 == 0:
        n //= 2
        x = op(x[:n], x[n:])
    return reduce(x, axis=0, keepdims=True)


def _attn_kernel(q_ref, k_ref, vt_ref, km_ref, o_ref, *, n_ctx_rows, n_ct, t0):
    i = pl.program_id(2) + t0
    n_keys = k_ref.shape[0]
    kc = ATTN_KEYS if (n_keys - n_ctx_rows) % ATTN_KEYS == 0 else TM
    heads = range(ATTN_HEADS)
    hs = [slice(g * HEAD_PAD, (g + 1) * HEAD_PAD) for g in heads]

    scores = lambda c, g: _mm_nt(k_ref[c[0]:c[0] + c[1], hs[g]], q_ref[:, hs[g]])
    ctx_chunk = [(0, n_ctx_rows)]
    kb = ATTN_KEYS_BOUNDED if (n_keys - n_ctx_rows) % ATTN_KEYS_BOUNDED == 0 else TM
    bounded_lat = ctx_chunk + [(s, kb) for s in range(n_ctx_rows, n_keys, kb)]

    km2 = jnp.max(jnp.max(km_ref[...], axis=0), axis=0, keepdims=True)
    lane = lax.broadcasted_iota(jnp.int32, km2.shape, 1)
    bound = []
    for g in heads:
        qf = q_ref[:, hs[g]].astype(F32)
        q2 = jnp.max(jnp.sum(qf * qf, axis=1, keepdims=True), axis=0, keepdims=True)
        k2 = jnp.max(jnp.where(lane == pl.program_id(1) * ATTN_HEADS + g, km2, 0.0), axis=1, keepdims=True)
        bound.append(jnp.sqrt(q2 * k2) * BOUND_SLACK)
    bound_ok = functools.reduce(jnp.maximum, [jnp.max(b) for b in bound]) < BOUND_LIMIT

    def finish(acc):
        outs = [acc[g][:V_HEAD] / acc[g][V_HEAD:V_HEAD + 1] for g in heads]
        o_ref[...] = jnp.concatenate(outs, axis=0).T.astype(o_ref.dtype)

    def attend_bounded(chunks):
        pieces = [(c, g) for c in chunks for g in heads]
        acc = [None] * ATTN_HEADS
        s_next = scores(*pieces[0])
        for n, (c, g) in enumerate(pieces):
            s = s_next
            if n + 1 < len(pieces):
                s_next = scores(*pieces[n + 1])
            pv = _mm(vt_ref[g * VT_ROWS:(g + 1) * VT_ROWS, c[0]:c[0] + c[1]], jnp.exp2(s - bound[g]))
            acc[g] = pv if acc[g] is None else acc[g] + pv
        finish(acc)

    def attend(chunks):
        s_next = [scores(chunks[0], g) for g in heads]
        m, acc = ([None] * ATTN_HEADS for _ in range(2))
        for n, c in enumerate(chunks):
            s = s_next
            if n + 1 < len(chunks):
                s_next = [scores(chunks[n + 1], g) for g in heads]
            m_c = [_rows_reduce(s[g], jnp.maximum, jnp.max) for g in heads]
            m_new = [m_c[g] if m[g] is None else jnp.maximum(m[g], m_c[g]) for g in heads]
            p = [jnp.exp2(s[g] - m_new[g]) for g in heads]
            pv = [_mm(vt_ref[g * VT_ROWS:(g + 1) * VT_ROWS, c[0]:c[0] + c[1]], p[g]) for g in heads]
            for g in heads:
                acc[g] = pv[g] if m[g] is None else jnp.exp2(m[g] - m_new[g]) * acc[g] + pv[g]
                m[g] = m_new[g]
        finish(acc)

    lat_chunks = ctx_chunk + [(s, kc) for s in range(n_ctx_rows, n_keys, kc)]
    is_lat = i >= n_ct
    pl.when(is_lat & bound_ok)(lambda: attend_bounded(bounded_lat))
    pl.when(is_lat & ~bound_ok)(lambda: attend(lat_chunks))
    if t0 < n_ct:
        pl.when(~is_lat & bound_ok)(lambda: attend_bounded(ctx_chunk))
        pl.when(~is_lat & ~bound_ok)(lambda: attend(ctx_chunk))


def _attn_call(q, k, vt, km, n_ct, t0):
    bsz, tt, _ = q.shape
    nt = tt // TM
    w = ATTN_HEADS * HEAD_PAD
    q_spec = pl.BlockSpec((None, TM, w), lambda b, h, i: (b, i + t0, h))
    k_spec = pl.BlockSpec((None, tt, w), lambda b, h, i: (b, 0, h))
    vt_spec = pl.BlockSpec((None, ATTN_HEADS * VT_ROWS, tt), lambda b, h, i: (b, h, 0))
    o_spec = pl.BlockSpec((None, TM, ATTN_HEADS * V_HEAD), lambda b, h, i: (b, i + t0, h))
    km_spec = pl.BlockSpec((None, nt, SUBLANES, LANES), lambda b, h, i: (b, 0, 0, 0))
    return pl.pallas_call(
        functools.partial(_attn_kernel, n_ctx_rows=n_ct * TM, n_ct=n_ct, t0=t0),
        grid=(bsz, N_HEADS // ATTN_HEADS, nt - t0),
        in_specs=[q_spec, k_spec, vt_spec, km_spec],
        out_specs=o_spec,
        out_shape=jax.ShapeDtypeStruct((bsz, tt, N_HEADS * V_HEAD), BF16),
        compiler_params=_cparams(("parallel", "parallel", "arbitrary")),
        name="attention",
    )(q, k, vt, km)


def _out_kernel(yf_ref, yr_ref, bv_ref, g_ref, o_ref, x_ref, mod_ref, gn_ref, ones_ref, wor_ref, wom_ref, ln_ref,
                out_ref):
    ones_bd = ones_ref[...]
    y = yf_ref[...].astype(F32) + yr_ref[...].astype(F32)
    inv_n = 1.0 / RWKV_HEAD
    yc = y - _seg_sum(y, ones_bd) * inv_n
    var = _seg_sum(yc * yc, ones_bd) * inv_n
    gn = yc * lax.rsqrt(var + GN_EPS) * gn_ref[0:1] + gn_ref[1:2]
    ro = (gn + bv_ref[...]) * g_ref[...]
    m = _mm(ro, wor_ref[...]) + jnp.dot(o_ref[...], wom_ref[...], preferred_element_type=F32)
    mod = mod_ref[...]
    out_ref[...] = _layer_norm(DEEPNORM_ALPHA * x_ref[...] + mod[2:3] * m, ln_ref[0:1], ln_ref[1:2])


def _out_split_kernel(yf_ref, yr_ref, bv_ref, g_ref, o_ref, c_ref, x_ref, *rest, n_ct):
    _out_kernel(yf_ref, yr_ref, bv_ref, g_ref, o_ref, _Pick(pl.program_id(1) < n_ct, c_ref, x_ref), *rest)


def _out_call(yf, yr, bv, g, o, xs, mod, prm, n_ct, t0):
    if isinstance(xs, tuple):
        assert t0 == 0
        ctx, x = xs
        bsz, tt = x.shape[0], ctx.shape[1] + x.shape[1]
        body, srcs = functools.partial(_out_split_kernel, n_ct=n_ct), (ctx, x)
        src_specs = _split_specs(D_MODEL, n_ct, ctx.shape[1], x.shape[1], halo=False)
    else:
        bsz, tt, _ = xs.shape
        body, srcs, src_specs = _out_kernel, (xs,), [_row_spec(D_MODEL, t0)]
    nt = tt // TM
    params = [prm[n] for n in ("gn", "ones_bd", "wo_r", "wo_m", "ln1")]
    return pl.pallas_call(
        body,
        grid=(bsz, nt - t0),
        in_specs=[_row_spec(D_RWKV, t0)] * 4 + [_row_spec(N_HEADS * V_HEAD, t0)] + src_specs + [_mod_spec(n_ct, t0)]
        + [_full_spec(a.shape) for a in params],
        out_specs=_row_spec(D_MODEL),
        out_shape=jax.ShapeDtypeStruct((bsz, tt - t0 * TM, D_MODEL), F32),
        compiler_params=_cparams(("parallel", "parallel")),
        name="mixer_out",
    )(yf, yr, bv, g, o, *srcs, mod, *params)


def _ffn_kernel(x_ref, xp_ref, xn_ref, mod_ref, wup_ref, conv_ref, wd_ref, ln_ref, out_ref, *, n_ct, nt):
    i = pl.program_id(1)
    mod = mod_ref[...]
    u = _with_halo(x_ref, xp_ref, xn_ref, 1.0 + mod[4:5], mod[3:4], i, n_ct, nt)

    def up(j):
        hs = []
        for o in (j * FF_SUB, D_FF + j * FF_SUB):
            h = jnp.dot(u, wup_ref[:, o:o + FF_SUB], preferred_element_type=F32)
            taps = conv_ref[:, o:o + FF_SUB]
            hs.append(_conv3(h, taps) + taps[3:4])
        return hs

    acc = None
    acts = []
    h_next = up(0)
    for j in range(N_FF_SUB):
        hg, hv = h_next
        if j + 1 < N_FF_SUB:
            h_next = up(j + 1)
        half = 0.5 * hg
        acts.append(((half + half * jnp.tanh(half)) * hv).astype(BF16))
        if len(acts) == FF_DOWN or j + 1 == N_FF_SUB:
            lo = (j + 1 - len(acts)) * FF_SUB
            part = jnp.dot(jnp.concatenate(acts, axis=1), wd_ref[lo:(j + 1) * FF_SUB, :], preferred_element_type=F32)
            acc = part if acc is None else acc + part
            acts = []
    out_ref[...] = _layer_norm(DEEPNORM_ALPHA * x_ref[...] + mod[5:6] * acc, ln_ref[0:1], ln_ref[1:2])


def _ffn_call(xs, mod, prm, n_ct):
    bsz, tt, _ = xs.shape
    nt = tt // TM
    prev, nxt = _halo_specs(D_MODEL, tt)
    w_up, conv, w_down, ln2 = prm["w_up"], prm["ffn_conv"], prm["w_down"], prm["ln2"]
    return pl.pallas_call(
        functools.partial(_ffn_kernel, n_ct=n_ct, nt=nt),
        grid=(bsz, nt),
        in_specs=[_row_spec(D_MODEL), prev, nxt, _mod_spec(n_ct), _full_spec(w_up.shape, True),
                  _full_spec(conv.shape), _full_spec(w_down.shape, True), _full_spec(ln2.shape)],
        out_specs=_row_spec(D_MODEL),
        out_shape=jax.ShapeDtypeStruct((bsz, tt, D_MODEL), F32),
        compiler_params=_cparams(("parallel", "parallel")),
        name="conv_ffn",
    )(xs, xs, xs, mod, w_up, conv, w_down, ln2)


def _pad_rows(rows, width):
    rows = [r.astype(F32) if r.shape[0] == width else jnp.pad(r.astype(F32), (0, width - r.shape[0])) for r in rows]
    return jnp.pad(jnp.stack(rows), ((0, SUBLANES - len(rows)), (0, 0)))


def _layer_params(l, w_in, rwkv_conv, w0, w_b, a0, a_b, g_b, k_k, k_a, r_k, gn_g, gn_b, q_norm_g, w_uq,
                  kv_norm_g, w_ukv, w_o, ln1_g, ln1_b, w_up, ffn_conv_w, ffn_conv_b, w_down, ln2_g, ln2_b):
    c = D_RWKV
    wi = w_in[l]
    o_mla = RWKV_COLS
    w_kr = wi[:, o_mla + Q_LORA + KV_LORA:]
    pad_l = jnp.zeros((D_MODEL, QK_NOPE), F32)
    pad_r = jnp.zeros((D_MODEL, LANES - QK_DIM), F32)
    w_in_p = jnp.concatenate(
        [wi[:, :o_mla + Q_LORA + KV_LORA], pad_l, w_kr, pad_r], axis=1).astype(BF16)

    def lora_bd(w):
        z = jnp.zeros_like(w[0])
        return jnp.concatenate([jnp.concatenate([w[0], z], 1), jnp.concatenate([z, w[1]], 1)], 0).astype(BF16)

    head = jnp.arange(2 * LANES) // RWKV_HEAD
    ones_bd = (head[:, None] == head[None, :]).astype(BF16)

    wq = w_uq[l].reshape(Q_LORA, N_HEADS, QK_DIM)
    q_nope, q_rope = wq[..., :QK_NOPE], wq[..., QK_NOPE:]
    zq = lambda n: jnp.zeros((Q_LORA, N_HEADS, n), F32)
    wq = jnp.concatenate([q_nope, q_rope, zq(LANES - QK_DIM)], -1).reshape(Q_LORA, D_HEADS).astype(BF16)
    wkv = w_ukv[l].reshape(KV_LORA, N_HEADS, QK_NOPE + V_HEAD)
    zk = jnp.zeros((KV_LORA, N_HEADS, HEAD_PAD - QK_NOPE), F32)
    wkn = jnp.concatenate([wkv[..., :QK_NOPE], zk], -1).reshape(KV_LORA, D_HEADS).astype(BF16)
    wv = wkv[..., QK_NOPE:].reshape(KV_LORA, N_HEADS * V_HEAD).astype(BF16)
    wo = w_o[l]

    return {
        "w_in": w_in_p,
        "conv": _pad_rows(list(rwkv_conv[l]), RWKV_COLS),
        "w0": _pad_rows([w0[l].reshape(-1)], 2 * c),
        "wb": lora_bd(w_b[l]),
        "a0": _pad_rows([a0[l].reshape(-1)], 2 * c),
        "ab": lora_bd(a_b[l]),
        "gb": g_b[l].astype(BF16),
        "vec": _pad_rows([k_k[l], k_a[l], r_k[l].reshape(-1)], c),
        "ones_bd": ones_bd,
        "norm_g": _pad_rows([q_norm_g[l], kv_norm_g[l]], Q_LORA),
        "wq": wq, "wkn": wkn, "wv": wv,
        "head_sel": (jnp.arange(D_HEADS)[:, None] // HEAD_PAD == jnp.arange(LANES)[None, :]).astype(BF16),
        "gn": _pad_rows([gn_g[l], gn_b[l]], c),
        "wo_r": wo[:c].astype(BF16),
        "wo_m": wo[c:].astype(BF16),
        "ln1": _pad_rows([ln1_g[l], ln1_b[l]], D_MODEL),
        "w_up": w_up[l].astype(BF16),
        "ffn_conv": _pad_rows(list(ffn_conv_w[l]) + [ffn_conv_b[l]], 2 * D_FF),
        "w_down": w_down[l].astype(BF16),
        "ln2": _pad_rows([ln2_g[l], ln2_b[l]], D_MODEL),
    }


def _rope_tables(n_ctx, n_lat):
    rows = n_lat // GRID_W
    row = jnp.repeat(jnp.arange(rows, dtype=F32), GRID_W)
    col = jnp.tile(jnp.arange(GRID_W, dtype=F32), rows)
    n_pairs = QK_ROPE // 4
    inv = ROPE_BASE ** (-jnp.arange(n_pairs, dtype=F32) / n_pairs)
    ang = jnp.concatenate([row[:, None] * inv, col[:, None] * inv], axis=-1)
    cos = jnp.repeat(jnp.cos(ang), 2, axis=-1)
    sin = jnp.repeat(jnp.sin(ang), 2, axis=-1) * jnp.tile(jnp.array([-1.0, 1.0], F32), QK_ROPE // 2)
    tt = n_ctx + n_lat
    cos_t = jnp.ones((tt, LANES), F32).at[n_ctx:, QK_NOPE:QK_DIM].set(cos)
    sin_t = jnp.zeros((tt, LANES), F32).at[n_ctx:, QK_NOPE:QK_DIM].set(sin)
    return cos_t, sin_t


def kernel(x, c, ctx, c_ctx, w_ada, b_ada, w_in, rwkv_conv, w0, w_b, a0, a_b, g_b, k_k, k_a, r_k, gn_g, gn_b,
           q_norm_g, w_uq, kv_norm_g, w_ukv, w_o, ln1_g, ln1_b, w_up, ffn_conv_w, ffn_conv_b, w_down, ln2_g, ln2_b):
    bsz, n_lat, _ = x.shape
    n_ctx = ctx.shape[1]
    assert n_ctx % TM == 0 and n_lat % TM == 0 and n_lat % GRID_W == 0
    n_ct = n_ctx // TM
    n_l = w_ada.shape[0]

    ada_rows = -(-(bsz + 1) // SUBLANES) * SUBLANES
    cc = jnp.zeros((ada_rows, D_MODEL), F32).at[:bsz].set(c).at[bsz].set(c_ctx)
    mods = _ada_call(cc, w_ada, b_ada).reshape(n_l, ada_rows, 6, D_MODEL)
    cos_t, sin_t = _rope_tables(n_ctx, n_lat)
    xs = (ctx, x)

    for l in range(n_l):
        prm = _layer_params(l, w_in, rwkv_conv, w0, w_b, a0, a_b, g_b, k_k, k_a, r_k, gn_g, gn_b, q_norm_g, w_uq,
                            kv_norm_g, w_ukv, w_o, ln1_g, ln1_b, w_up, ffn_conv_w, ffn_conv_b, w_down, ln2_g, ln2_b)
        m_ctx = jnp.broadcast_to(mods[l, bsz][None], (bsz, 6, D_MODEL))
        mod = jnp.stack([m_ctx, mods[l, :bsz]], axis=1)
        mod = jnp.pad(mod, ((0, 0), (0, 0), (0, SUBLANES - 6), (0, 0))).reshape(2 * bsz, SUBLANES, D_MODEL)

        last = l == n_l - 1
        t0 = n_ct if last else 0
        if last and isinstance(xs, tuple):
            xs = jnp.concatenate(xs, axis=1)
        r, v, kk, g, bv, lw, kd, bb, q, k, vm, km = _front_call(xs, mod, cos_t, sin_t, prm, n_ct)
        yf, yr = _wkv_call(r, v, kk, lw, kd, bb, n_ct)
        o = _attn_call(q, k, vm, km, n_ct, t0)
        x1 = _out_call(yf, yr, bv, g, o, xs, mod, prm, n_ct, t0)
        xs = _ffn_call(x1, mod, prm, n_ct - t0)
    return xs
```

```python
import functools
import math

import jax
import jax.numpy as jnp
from jax import lax
from jax.experimental import pallas as pl
from jax.experimental.pallas import tpu as pltpu

F32 = jnp.float32
BF16 = jnp.bfloat16

D_MODEL = 1024
DEPTH = 2
GRID_W = 64
D_RWKV = 512
RWKV_HEAD = 64
N_HEADS = 8
LORA_W = 64
LORA_A = 64
LORA_G = 128
RWKV_COLS = 3 * D_RWKV + 2 * LORA_W + 2 * LORA_A + LORA_G
QK_NOPE = 64
QK_ROPE = 32
QK_DIM = QK_NOPE + QK_ROPE
V_HEAD = 64
Q_LORA = 384
KV_LORA = 256
D_FF = 2816
ROPE_BASE = 10000.0
DEEPNORM_ALPHA = (2 * DEPTH) ** 0.25
LN_EPS = 1e-5
RMS_EPS = 1e-6
GN_EPS = 64e-5
LOG2_E = 1.4426950408889634

LANES = 128
SUBLANES = 8
HEAD_PAD = LANES
D_HEADS = N_HEADS * HEAD_PAD
TM = 256
HALO = SUBLANES
CHUNK = 64
LOG_CHUNK = CHUNK.bit_length() - 1
WKV_BATCH = 8
PAIR = 2 * RWKV_HEAD
N_PAIRS = N_HEADS // 2
VT_ROWS = 80
ATTN_HEADS = 8
ATTN_KEYS = 1024
ATTN_KEYS_BOUNDED = 2048
BOUND_LIMIT = 60.0
BOUND_SLACK = 1.01
FF_SUB = 768
FF_DOWN = 2
VMEM_LIMIT = 56 << 20


def _cparams(sem):
    return pltpu.CompilerParams(dimension_semantics=sem, vmem_limit_bytes=VMEM_LIMIT)


def _mm(a, b):
    return jnp.dot(a.astype(BF16), b.astype(BF16), preferred_element_type=F32)


def _mm_nt(a, b):
    return lax.dot_general(a.astype(BF16), b.astype(BF16), (((1,), (1,)), ((), ())),
                           preferred_element_type=F32)


def _split2(x):
    hi = x.astype(BF16)
    lo = (x - hi.astype(F32)).astype(BF16)
    return hi, lo


def _mm3(a, b):
    ah, al = _split2(a)
    bh, bl = _split2(b)
    d = functools.partial(jnp.dot, preferred_element_type=F32)
    return d(ah, bh) + (d(ah, bl) + d(al, bh))


def _mm_exact_rhs(a, b_bf16):
    a1, a2 = _split2(a)
    d = functools.partial(jnp.dot, preferred_element_type=F32)
    return d(a1, b_bf16) + d(a2, b_bf16)


def _mm_exact_lhs(a_bf16, b):
    b1 = b.astype(BF16)
    r1 = b - b1.astype(F32)
    b2 = r1.astype(BF16)
    b3 = (r1 - b2.astype(F32)).astype(BF16)
    d = functools.partial(jnp.dot, preferred_element_type=F32)
    return d(a_bf16, b1) + (d(a_bf16, b2) + d(a_bf16, b3))


def _sigmoid(x):
    return 1.0 / (1.0 + jnp.exp(-x))


def _layer_norm(x, g, b):
    mu = jnp.mean(x, axis=-1, keepdims=True)
    xc = x - mu
    var = jnp.mean(xc * xc, axis=-1, keepdims=True)
    return xc * lax.rsqrt(var + LN_EPS) * g + b


def _rms_norm(x, g):
    return x * lax.rsqrt(jnp.mean(x * x, axis=-1, keepdims=True) + RMS_EPS) * g


def _row_spec(cols, t0=0):
    return pl.BlockSpec((None, TM, cols), lambda b, i, *_: (b, i + t0, 0))


def _mod_spec(n_ct, t0=0):
    return pl.BlockSpec((None, SUBLANES, D_MODEL),
                        lambda b, i, *_: (2 * b + (i + t0 >= n_ct).astype(jnp.int32), 0, 0))


def _full_spec(shape, single_buffer=False):
    nd = len(shape)
    mode = dict(pipeline_mode=pl.Buffered(1)) if single_buffer else {}
    return pl.BlockSpec(shape, lambda *_: (0,) * nd, **mode)


def _halo_specs(cols, tt, t0=0):
    per = TM // HALO
    last = tt // HALO - 1
    prev = pl.BlockSpec((None, HALO, cols), lambda b, i, *_: (b, jnp.maximum((i + t0) * per - 1, 0), 0))
    nxt = pl.BlockSpec((None, HALO, cols), lambda b, i, *_: (b, jnp.minimum((i + t0 + 1) * per, last), 0))
    return prev, nxt


def _with_halo(x_ref, xp_ref, xn_ref, scale, shift, i, n_ct, nt):
    has_prev = (i != 0) & (i != n_ct)
    has_next = (i != n_ct - 1) & (i != nt - 1)
    mod = lambda x: x * scale + shift
    return jnp.concatenate([jnp.where(has_prev, mod(xp_ref[...]), 0.0), mod(x_ref[...]),
                            jnp.where(has_next, mod(xn_ref[...]), 0.0)], axis=0).astype(BF16)


class _Pick:
    def __init__(self, pred, a, b):
        self.pred, self.a, self.b = pred, a, b

    def __getitem__(self, idx):
        return jnp.where(self.pred, self.a[idx], self.b[idx])


def _split_specs(cols, n_ct, n_ctx, n_lat, halo):
    per = TM // HALO
    out = []
    for first, rows in ((0, n_ctx), (n_ct, n_lat)):
        last_t, last_h = rows // TM - 1, rows // HALO - 1
        t = lambda i, first=first, last_t=last_t: jnp.clip(i - first, 0, last_t)
        out.append(pl.BlockSpec((None, TM, cols), lambda b, i, t=t: (b, t(i), 0)))
        if halo:
            out.append(pl.BlockSpec((None, HALO, cols),
                                    lambda b, i, first=first, last_h=last_h: (b, jnp.clip((i - first) * per - 1, 0, last_h), 0)))
            out.append(pl.BlockSpec((None, HALO, cols),
                                    lambda b, i, first=first, last_h=last_h: (b, jnp.clip((i - first + 1) * per, 0, last_h), 0)))
    return out


def _conv3(h, taps):
    rows = TM + 2 * HALO
    lo, hi = HALO, HALO + TM
    return (pltpu.roll(h, 1, 0)[lo:hi] * taps[0:1] + h[lo:hi] * taps[1:2]
            + pltpu.roll(h, rows - 1, 0)[lo:hi] * taps[2:3])


ADA_COLS = 1536


def _ada_kernel(c_ref, w_ref, b_ref, o_ref):
    c = c_ref[...]
    o_ref[...] = _mm3(c * _sigmoid(c), w_ref[...]) + b_ref[...]


def _ada_call(cc, w_ada, b_ada):
    n_l = w_ada.shape[0]
    rows = cc.shape[0]
    return pl.pallas_call(
        _ada_kernel,
        grid=(n_l, 6 * D_MODEL // ADA_COLS),
        in_specs=[
            pl.BlockSpec((rows, D_MODEL), lambda l, j: (0, 0)),
            pl.BlockSpec((None, D_MODEL, ADA_COLS), lambda l, j: (l, 0, j)),
            pl.BlockSpec((None, 1, ADA_COLS), lambda l, j: (l, 0, j)),
        ],
        out_specs=pl.BlockSpec((None, rows, ADA_COLS), lambda l, j: (l, 0, j)),
        out_shape=jax.ShapeDtypeStruct((n_l, rows, 6 * D_MODEL), F32),
        compiler_params=_cparams(("arbitrary", "arbitrary")),
        name="ada",
    )(cc, w_ada, b_ada.reshape(n_l, 1, 6 * D_MODEL))


def _seg_sum(x, ones_bd):
    w = ones_bd.shape[0]
    return jnp.concatenate([_mm_exact_rhs(x[:, o:o + w], ones_bd) for o in range(0, x.shape[1], w)], axis=1)


def _front_kernel(x_ref, xp_ref, xn_ref, mod_ref, w_ref, conv_ref, w0_ref, wb_ref, a0_ref, ab_ref, gb_ref, vec_ref,
                  ones_ref, cos_ref, sin_ref, ng_ref, wq_ref, wkn_ref, wv_ref, hsel_ref,
                  r_ref, v_ref, kk_ref, g_ref, bv_ref, lw_ref, kd_ref, bb_ref, q_ref, k_ref, vm_ref, km_ref, *, n_ct, nt):
    i = pl.program_id(1)
    m = mod_ref[...]
    u = _with_halo(x_ref, xp_ref, xn_ref, 1.0 + m[1:2], m[0:1], i, n_ct, nt)
    d = functools.partial(jnp.dot, preferred_element_type=F32)
    c = D_RWKV
    o_lora = 3 * c
    proj = lambda o, n: d(u, w_ref[:, o:o + n])
    conv = lambda p, o, n: _conv3(p, conv_ref[:, o:o + n])
    vec = vec_ref[...]
    k_k, k_a, r_k = vec[0:1], vec[1:2], vec[2:3]
    ones_bd = ones_ref[...]

    pm = proj(RWKV_COLS, w_ref.shape[1] - RWKV_COLS)[HALO:HALO + TM]
    p_lo = proj(o_lora, RWKV_COLS - o_lora)
    cq, ckv, kr_x = pm[:, :Q_LORA], pm[:, Q_LORA:Q_LORA + KV_LORA], pm[:, Q_LORA + KV_LORA:]
    cqn = _rms_norm(cq, ng_ref[0:1, :Q_LORA]).astype(BF16)
    ckvn = _rms_norm(ckv, ng_ref[1:2, :KV_LORA]).astype(BF16)

    p_k = proj(c, c)
    z_lo = conv(p_lo, o_lora, RWKV_COLS - o_lora)
    w_lo, a_lo, g_lo = z_lo[:, :2 * LORA_W], z_lo[:, 2 * LORA_W:2 * LORA_W + 2 * LORA_A], z_lo[:, 2 * LORA_W + 2 * LORA_A:]
    lw = -math.exp(-0.5) * _sigmoid(w0_ref[0:1, :] + _mm(jnp.tanh(w_lo), wb_ref[...]))
    lw_ref[0] = lw[:, :c]
    lw_ref[1] = lw[:, c:]
    a = _sigmoid(a0_ref[0:1, :] + _mm(a_lo, ab_ref[...]))
    g_ref[...] = _mm(_sigmoid(g_lo), gb_ref[...]).astype(BF16)

    q = d(cqn, wq_ref[...])
    p_r = proj(0, c)
    k = conv(p_k, c, c)
    kk = k * k_k
    nrm = jnp.sqrt(_seg_sum(kk * kk, ones_bd))
    kk = kk / jnp.maximum(nrm, 1e-12)
    kd0 = k * (1.0 + (a[:, :c] - 1.0) * k_a)
    kd1 = k * (1.0 + (a[:, c:] - 1.0) * k_a)
    kk_ref[...] = kk.astype(BF16)
    kd_ref[0] = kd0.astype(BF16)
    kd_ref[1] = kd1.astype(BF16)
    bb_ref[0] = (kk * a[:, :c]).astype(BF16)
    bb_ref[1] = (kk * a[:, c:]).astype(BF16)

    k_nope = d(ckvn, wkn_ref[...])
    vt = d(ckvn, wv_ref[...])
    p_v = proj(2 * c, c)

    def rotary(x, cos, sin_signed):
        n = x.shape[1]
        even = (lax.broadcasted_iota(jnp.int32, x.shape, 1) & 1) == 0
        partner = jnp.where(even, pltpu.roll(x, n - 1, 1), pltpu.roll(x, 1, 1))
        return x * cos + partner * sin_signed

    cos, sin = cos_ref[...], sin_ref[...]
    cos8 = jnp.concatenate([cos] * N_HEADS, axis=1)
    sin8 = jnp.concatenate([sin] * N_HEADS, axis=1)
    q_ref[...] = (rotary(q, cos8, sin8) * (QK_DIM ** -0.5 * LOG2_E)).astype(BF16)
    kr = rotary(kr_x, cos, sin)
    k_bf = (k_nope + jnp.concatenate([kr] * N_HEADS, axis=1)).astype(BF16)
    k_ref[...] = k_bf
    k2 = _mm(k_bf.astype(F32) * k_bf.astype(F32), hsel_ref[...])
    km_ref[...] = jnp.broadcast_to(_rows_reduce(k2, jnp.maximum, jnp.max), (SUBLANES, LANES))

    r = conv(p_r, 0, c)
    r_ref[...] = r.astype(BF16)
    v = conv(p_v, 2 * c, c)
    v_ref[...] = v.astype(BF16)
    bv_ref[...] = (_seg_sum(r * (kd0 + kd1) * r_k, ones_bd) * v).astype(BF16)

    vt = vt.T
    extra = jnp.where(lax.broadcasted_iota(jnp.int32, (VT_ROWS - V_HEAD, TM), 0) == 0, 1.0, 0.0)
    vm_ref[...] = jnp.concatenate(
        [blk for h in range(N_HEADS) for blk in (vt[h * V_HEAD:(h + 1) * V_HEAD], extra)], axis=0).astype(BF16)


def _front_split_kernel(c_ref, cp_ref, cn_ref, x_ref, xp_ref, xn_ref, *rest, n_ct, nt):
    is_ctx = pl.program_id(1) < n_ct
    _front_kernel(_Pick(is_ctx, c_ref, x_ref), _Pick(is_ctx, cp_ref, xp_ref), _Pick(is_ctx, cn_ref, xn_ref), *rest,
                  n_ct=n_ct, nt=nt)


def _front_call(xs, mod, cos_t, sin_t, prm, n_ct):
    if isinstance(xs, tuple):
        ctx, x = xs
        bsz, tt = x.shape[0], ctx.shape[1] + x.shape[1]
        body, srcs = _front_split_kernel, (ctx, ctx, ctx, x, x, x)
        src_specs = _split_specs(D_MODEL, n_ct, ctx.shape[1], x.shape[1], halo=True)
    else:
        bsz, tt, _ = xs.shape
        body, srcs = _front_kernel, (xs, xs, xs)
        src_specs = [_row_spec(D_MODEL), *_halo_specs(D_MODEL, tt)]
    nt = tt // TM
    tab = pl.BlockSpec((TM, LANES), lambda b, i: (i, 0))
    shared = jax.ShapeDtypeStruct((bsz, tt, D_RWKV), BF16)
    dir_spec = pl.BlockSpec((2, None, TM, D_RWKV), lambda b, i: (0, b, i, 0))
    dir_shape = lambda dt: jax.ShapeDtypeStruct((2, bsz, tt, D_RWKV), dt)
    heads = jax.ShapeDtypeStruct((bsz, tt, D_HEADS), BF16)
    p_rwkv = [prm[n] for n in ("w_in", "conv", "w0", "wb", "a0", "ab", "gb", "vec", "ones_bd")]
    p_mla = [prm[n] for n in ("norm_g", "wq", "wkn", "wv", "head_sel")]
    return pl.pallas_call(
        functools.partial(body, n_ct=n_ct, nt=nt),
        grid=(bsz, nt),
        in_specs=src_specs + [_mod_spec(n_ct)] + [_full_spec(a.shape) for a in p_rwkv]
        + [tab, tab] + [_full_spec(a.shape) for a in p_mla],
        out_specs=[_row_spec(D_RWKV)] * 5 + [dir_spec] * 3 + [_row_spec(D_HEADS)] * 2
        + [pl.BlockSpec((None, N_HEADS * VT_ROWS, TM), lambda b, i: (b, 0, i)),
           pl.BlockSpec((None, None, SUBLANES, LANES), lambda b, i: (b, i, 0, 0))],
        out_shape=[shared] * 5 + [dir_shape(F32), dir_shape(BF16), dir_shape(BF16)] + [heads] * 2
        + [jax.ShapeDtypeStruct((bsz, N_HEADS * VT_ROWS, tt), BF16),
           jax.ShapeDtypeStruct((bsz, nt, SUBLANES, LANES), F32)],
        compiler_params=_cparams(("parallel", "parallel")),
        name="front",
    )(*srcs, mod, *p_rwkv, cos_t, sin_t, *p_mla)


def _wkv_kernel(*refs, nb):
    ins, (yf_ref, yr_ref, s_ref) = refs[:12], refs[12:]
    step = pl.program_id(1)

    @pl.when(step == 0)
    def _():
        s_ref[...] = jnp.zeros_like(s_ref)

    n2 = 2 * CHUNK
    ri = lax.broadcasted_iota(jnp.int32, (n2, n2), 0)
    ci = lax.broadcasted_iota(jnp.int32, (n2, n2), 1)
    same_head = (ri >> LOG_CHUNK) == (ci >> LOG_CHUNK)
    rt = lax.broadcasted_iota(jnp.int32, (CHUNK, n2), 0)
    ct = lax.broadcasted_iota(jnp.int32, (CHUNK, n2), 1) & (CHUNK - 1)
    eye = ct == rt
    ti = lax.broadcasted_iota(jnp.int32, (CHUNK, CHUNK), 0)
    tj = lax.broadcasted_iota(jnp.int32, (CHUNK, CHUNK), 1)
    strict = [ct < rt, ct > rt]
    incl = [ct <= rt, ct >= rt]
    tri = [jnp.where(tj <= ti, 1.0, 0.0).astype(BF16), jnp.where(tj >= ti, 1.0, 0.0).astype(BF16)]

    def bdiag(x):
        return jnp.where(same_head, jnp.concatenate([x, x], axis=0), 0.0)

    def fold_t(x):
        xt = bdiag(x).T
        return xt[:CHUNK] + xt[CHUNK:]

    ch = []
    for d in range(2):
        r_ref, v_ref, kk_ref, lw_ref, kd_ref, bb_ref = ins[6 * d:6 * d + 6]
        for n in range(nb):
            lw = lw_ref[n]
            lc = _mm_exact_lhs(tri[d], lw)
            lend = jnp.sum(lw, axis=0, keepdims=True)
            e_neg = jnp.exp(-lc)
            e_end = jnp.exp(lend - lc)
            p_end = jnp.exp(lend)
            kt_all = kk_ref[n].astype(F32) * jnp.exp(lc - lw)
            rt_all = r_ref[n].astype(F32) * jnp.exp(lc)
            kd, bb = kd_ref[n].astype(F32), bb_ref[n].astype(F32)
            kh_all, bh_all = kd * e_neg, bb * e_neg
            kc_all, bc_all = kd * e_end, bb * e_end
            v_all = v_ref[n]
            for p in range(N_PAIRS):
                sl = slice(p * PAIR, (p + 1) * PAIR)
                ch.append(dict(d=d, slot=(d * nb + n) * N_PAIRS + p, kt=kt_all[:, sl], rr=rt_all[:, sl],
                               vv=bdiag(v_all[:, sl]), kh=bdiag(kh_all[:, sl]), bh=bdiag(bh_all[:, sl]),
                               kc_t=fold_t(kc_all[:, sl]), bc_t=fold_t(bc_all[:, sl]), p_end=p_end[:, sl]))
    for c in ch:
        gram = _mm_nt(jnp.concatenate([c["kt"], c["rr"]], axis=0), jnp.concatenate([c["bh"], c["kh"]], axis=0))
        st, ic = strict[c["d"]], incl[c["d"]]
        c["a_kb"] = jnp.where(st, gram[:CHUNK, :n2], 0.0)
        c["a_kk"] = jnp.where(st, gram[:CHUNK, n2:], 0.0)
        c["a_rb"] = jnp.where(ic, gram[CHUNK:, :n2], 0.0)
        c["a_rk"] = jnp.where(ic, gram[CHUNK:, n2:], 0.0)
        c["t"] = jnp.where(eye, 1.0, 0.0) - jnp.where((rt >> 1) == (ct >> 1), c["a_kb"], 0.0)
    for c in ch:
        av = _mm(jnp.concatenate([c["a_kk"], c["a_rk"]], axis=0), c["vv"])
        c["akk_v"], c["ark_v"] = av[:CHUNK], av[CHUNK:]
        c["kc_v"] = _mm(c["kc_t"], c["vv"])
    for ls in range(1, LOG_CHUNK):
        off = ((rt >> (ls + 1)) == (ct >> (ls + 1))) & ((rt >> ls) != (ct >> ls))
        for c in ch:
            c["nt"] = _mm(jnp.where(off, c["a_kb"], 0.0), bdiag(c["t"]))
        for c in ch:
            c["t"] = c["t"] - _mm(c["t"], bdiag(c["nt"]))
    for c in ch:
        uw = _mm(c["t"], jnp.concatenate([bdiag(c["akk_v"]), bdiag(c["kt"])], axis=1))
        c["uw"] = jnp.concatenate([bdiag(uw[:, :n2]), bdiag(uw[:, n2:])], axis=1)
    for c in ch:
        c["yq"] = jnp.concatenate([c["ark_v"], c["rr"]], axis=1) - _mm(c["a_rb"], c["uw"])
        diag_p = jnp.where(eye, jnp.broadcast_to(c["p_end"], (CHUNK, n2)), 0.0)
        c["nm"] = jnp.concatenate([c["kc_v"], diag_p], axis=1) - _mm(c["bc_t"], c["uw"])
    for c in ch:
        qm_l = jnp.concatenate([c["yq"][:, n2:], c["nm"][:, n2:]], axis=0).astype(BF16)
        s_hi, s_lo = _split2(s_ref[c["slot"]])
        c["qm"] = jnp.dot(qm_l, s_hi, preferred_element_type=F32)
        c["m_lo"] = jnp.dot(qm_l[CHUNK:], s_lo, preferred_element_type=F32)
    ys = []
    for c in ch:
        ys.append(c["qm"][:CHUNK] + c["yq"][:, :n2])
        s_ref[c["slot"]] = bdiag((c["qm"][CHUNK:] + c["m_lo"]) + c["nm"][:, :n2])
    for d, y_ref in enumerate((yf_ref, yr_ref)):
        for n in range(nb):
            o = (d * nb + n) * N_PAIRS
            y_ref[n] = jnp.concatenate(ys[o:o + N_PAIRS], axis=1).astype(y_ref.dtype)


def _wkv_call(r, v, kk, lw, kd, bb, n_ct):
    bsz, tt, c = r.shape
    nc = tt // CHUNK
    n_cc = n_ct * TM // CHUNK
    nb = math.gcd(bsz, WKV_BATCH)

    def rev_chunk(s):
        return jnp.where(s < n_cc, n_cc - 1 - s, nc - 1 - (s - n_cc))

    order = (lambda s: s, rev_chunk)
    specs, args = [], []
    for d in range(2):
        shared = pl.BlockSpec((nb, CHUNK, c), lambda b, s, d=d: (b, order[d](s), 0))
        per_dir = pl.BlockSpec((None, nb, CHUNK, c), lambda b, s, d=d: (d, b, order[d](s), 0))
        specs += [shared, shared, shared, per_dir, per_dir, per_dir]
        args += [r, v, kk, lw, kd, bb]
    out = jax.ShapeDtypeStruct((bsz, tt, c), BF16)
    return pl.pallas_call(
        functools.partial(_wkv_kernel, nb=nb),
        grid=(bsz // nb, nc),
        in_specs=specs,
        out_specs=[specs[0], specs[6]],
        out_shape=[out, out],
        scratch_shapes=[pltpu.VMEM((2 * nb * N_PAIRS, PAIR, PAIR), F32)],
        compiler_params=_cparams(("parallel", "arbitrary")),
        name="wkv_scan",
    )(*args)


def _rows_reduce(x, op, reduce):
    n = x.shape[0]
    while n > SUBLANES and n % 2 == 0:
        n //= 2
        x = op(x[:n], x[n:])
    return reduce(x, axis=0, keepdims=True)


def _attn_kernel(q_ref, k_ref, vt_ref, km_ref, o_ref, *, n_ctx_rows, n_ct, t0):
    i = pl.program_id(2) + t0
    n_keys = k_ref.shape[0]
    kc = ATTN_KEYS if (n_keys - n_ctx_rows) % ATTN_KEYS == 0 else TM
    heads = range(ATTN_HEADS)
    hs = [slice(g * HEAD_PAD, (g + 1) * HEAD_PAD) for g in heads]

    scores = lambda c, g: _mm_nt(k_ref[c[0]:c[0] + c[1], hs[g]], q_ref[:, hs[g]])
    ctx_chunk = [(0, n_ctx_rows)]
    kb = ATTN_KEYS_BOUNDED if (n_keys - n_ctx_rows) % ATTN_KEYS_BOUNDED == 0 else TM
    bounded_lat = ctx_chunk + [(s, kb) for s in range(n_ctx_rows, n_keys, kb)]

    km2 = jnp.max(jnp.max(km_ref[...], axis=0), axis=0, keepdims=True)
    lane = lax.broadcasted_iota(jnp.int32, km2.shape, 1)
    bound = []
    for g in heads:
        qf = q_ref[:, hs[g]].astype(F32)
        q2 = jnp.max(jnp.sum(qf * qf, axis=1, keepdims=True), axis=0, keepdims=True)
        k2 = jnp.max(jnp.where(lane == pl.program_id(1) * ATTN_HEADS + g, km2, 0.0), axis=1, keepdims=True)
        bound.append(jnp.sqrt(q2 * k2) * BOUND_SLACK)
    bound_ok = functools.reduce(jnp.maximum, [jnp.max(b) for b in bound]) < BOUND_LIMIT

    def finish(acc):
        outs = [acc[g][:V_HEAD] / acc[g][V_HEAD:V_HEAD + 1] for g in heads]
        o_ref[...] = jnp.concatenate(outs, axis=0).T.astype(o_ref.dtype)

    def attend_bounded(chunks):
        pieces = [(c, g) for c in chunks for g in heads]
        acc = [None] * ATTN_HEADS
        s_next = scores(*pieces[0])
        for n, (c, g) in enumerate(pieces):
            s = s_next
            if n + 1 < len(pieces):
                s_next = scores(*pieces[n + 1])
            pv = _mm(vt_ref[g * VT_ROWS:(g + 1) * VT_ROWS, c[0]:c[0] + c[1]], jnp.exp2(s - bound[g]))
            acc[g] = pv if acc[g] is None else acc[g] + pv
        finish(acc)

    def attend(chunks):
        s_next = [scores(chunks[0], g) for g in heads]
        m, acc = ([None] * ATTN_HEADS for _ in range(2))
        for n, c in enumerate(chunks):
            s = s_next
            if n + 1 < len(chunks):
                s_next = [scores(chunks[n + 1], g) for g in heads]
            m_c = [_rows_reduce(s[g], jnp.maximum, jnp.max) for g in heads]
            m_new = [m_c[g] if m[g] is None else jnp.maximum(m[g], m_c[g]) for g in heads]
            p = [jnp.exp2(s[g] - m_new[g]) for g in heads]
            pv = [_mm(vt_ref[g * VT_ROWS:(g + 1) * VT_ROWS, c[0]:c[0] + c[1]], p[g]) for g in heads]
            for g in heads:
                acc[g] = pv[g] if m[g] is None else jnp.exp2(m[g] - m_new[g]) * acc[g] + pv[g]
                m[g] = m_new[g]
        finish(acc)

    lat_chunks = ctx_chunk + [(s, kc) for s in range(n_ctx_rows, n_keys, kc)]
    is_lat = i >= n_ct
    pl.when(is_lat & bound_ok)(lambda: attend_bounded(bounded_lat))
    pl.when(is_lat & ~bound_ok)(lambda: attend(lat_chunks))
    if t0 < n_ct:
        pl.when(~is_lat & bound_ok)(lambda: attend_bounded(ctx_chunk))
        pl.when(~is_lat & ~bound_ok)(lambda: attend(ctx_chunk))


def _attn_call(q, k, vt, km, n_ct, t0):
    bsz, tt, _ = q.shape
    nt = tt // TM
    w = ATTN_HEADS * HEAD_PAD
    q_spec = pl.BlockSpec((None, TM, w), lambda b, h, i: (b, i + t0, h))
    k_spec = pl.BlockSpec((None, tt, w), lambda b, h, i: (b, 0, h))
    vt_spec = pl.BlockSpec((None, ATTN_HEADS * VT_ROWS, tt), lambda b, h, i: (b, h, 0))
    o_spec = pl.BlockSpec((None, TM, ATTN_HEADS * V_HEAD), lambda b, h, i: (b, i + t0, h))
    km_spec = pl.BlockSpec((None, nt, SUBLANES, LANES), lambda b, h, i: (b, 0, 0, 0))
    return pl.pallas_call(
        functools.partial(_attn_kernel, n_ctx_rows=n_ct * TM, n_ct=n_ct, t0=t0),
        grid=(bsz, N_HEADS // ATTN_HEADS, nt - t0),
        in_specs=[q_spec, k_spec, vt_spec, km_spec],
        out_specs=o_spec,
        out_shape=jax.ShapeDtypeStruct((bsz, tt, N_HEADS * V_HEAD), BF16),
        compiler_params=_cparams(("parallel", "parallel", "arbitrary")),
        name="attention",
    )(q, k, vt, km)


def _out_kernel(yf_ref, yr_ref, bv_ref, g_ref, o_ref, x_ref, mod_ref, gn_ref, ones_ref, wor_ref, wom_ref, ln_ref,
                out_ref):
    ones_bd = ones_ref[...]
    y = yf_ref[...].astype(F32) + yr_ref[...].astype(F32)
    inv_n = 1.0 / RWKV_HEAD
    yc = y - _seg_sum(y, ones_bd) * inv_n
    var = _seg_sum(yc * yc, ones_bd) * inv_n
    gn = yc * lax.rsqrt(var + GN_EPS) * gn_ref[0:1] + gn_ref[1:2]
    ro = (gn + bv_ref[...]) * g_ref[...]
    m = _mm(ro, wor_ref[...]) + jnp.dot(o_ref[...], wom_ref[...], preferred_element_type=F32)
    mod = mod_ref[...]
    out_ref[...] = _layer_norm(DEEPNORM_ALPHA * x_ref[...] + mod[2:3] * m, ln_ref[0:1], ln_ref[1:2])


def _out_split_kernel(yf_ref, yr_ref, bv_ref, g_ref, o_ref, c_ref, x_ref, *rest, n_ct):
    _out_kernel(yf_ref, yr_ref, bv_ref, g_ref, o_ref, _Pick(pl.program_id(1) < n_ct, c_ref, x_ref), *rest)


def _out_call(yf, yr, bv, g, o, xs, mod, prm, n_ct, t0):
    if isinstance(xs, tuple):
        assert t0 == 0
        ctx, x = xs
        bsz, tt = x.shape[0], ctx.shape[1] + x.shape[1]
        body, srcs = functools.partial(_out_split_kernel, n_ct=n_ct), (ctx, x)
        src_specs = _split_specs(D_MODEL, n_ct, ctx.shape[1], x.shape[1], halo=False)
    else:
        bsz, tt, _ = xs.shape
        body, srcs, src_specs = _out_kernel, (xs,), [_row_spec(D_MODEL, t0)]
    nt = tt // TM
    params = [prm[n] for n in ("gn", "ones_bd", "wo_r", "wo_m", "ln1")]
    return pl.pallas_call(
        body,
        grid=(bsz, nt - t0),
        in_specs=[_row_spec(D_RWKV, t0)] * 4 + [_row_spec(N_HEADS * V_HEAD, t0)] + src_specs + [_mod_spec(n_ct, t0)]
        + [_full_spec(a.shape) for a in params],
        out_specs=_row_spec(D_MODEL),
        out_shape=jax.ShapeDtypeStruct((bsz, tt - t0 * TM, D_MODEL), F32),
        compiler_params=_cparams(("parallel", "parallel")),
        name="mixer_out",
    )(yf, yr, bv, g, o, *srcs, mod, *params)


def _ffn_kernel(x_ref, xp_ref, xn_ref, mod_ref, wup_ref, conv_ref, wd_ref, ln_ref, out_ref, *, n_ct, nt):
    i = pl.program_id(1)
    mod = mod_ref[...]
    u = _with_halo(x_ref, xp_ref, xn_ref, 1.0 + mod[4:5], mod[3:4], i, n_ct, nt)

    subs = [(o, min(FF_SUB, D_FF - o)) for o in range(0, D_FF, FF_SUB)]

    def up(j):
        o, w = subs[j]
        hs = []
        for oo in (o, D_FF + o):
            h = jnp.dot(u, wup_ref[:, oo:oo + w], preferred_element_type=F32)
            taps = conv_ref[:, oo:oo + w]
            hs.append(_conv3(h, taps) + taps[3:4])
        return hs

    acc = None
    acts = []
    h_next = up(0)
    for j in range(len(subs)):
        hg, hv = h_next
        if j + 1 < len(subs):
            h_next = up(j + 1)
        half = 0.5 * hg
        acts.append(((half + half * jnp.tanh(half)) * hv).astype(BF16))
        if len(acts) == FF_DOWN or j + 1 == len(subs):
            lo, hi = subs[j + 1 - len(acts)][0], subs[j][0] + subs[j][1]
            part = jnp.dot(jnp.concatenate(acts, axis=1), wd_ref[lo:hi, :], preferred_element_type=F32)
            acc = part if acc is None else acc + part
            acts = []
    out_ref[...] = _layer_norm(DEEPNORM_ALPHA * x_ref[...] + mod[5:6] * acc, ln_ref[0:1], ln_ref[1:2])


def _ffn_call(xs, mod, prm, n_ct):
    bsz, tt, _ = xs.shape
    nt = tt // TM
    prev, nxt = _halo_specs(D_MODEL, tt)
    w_up, conv, w_down, ln2 = prm["w_up"], prm["ffn_conv"], prm["w_down"], prm["ln2"]
    return pl.pallas_call(
        functools.partial(_ffn_kernel, n_ct=n_ct, nt=nt),
        grid=(bsz, nt),
        in_specs=[_row_spec(D_MODEL), prev, nxt, _mod_spec(n_ct), _full_spec(w_up.shape, True),
                  _full_spec(conv.shape), _full_spec(w_down.shape, True), _full_spec(ln2.shape)],
        out_specs=_row_spec(D_MODEL),
        out_shape=jax.ShapeDtypeStruct((bsz, tt, D_MODEL), F32),
        compiler_params=_cparams(("parallel", "parallel")),
        name="conv_ffn",
    )(xs, xs, xs, mod, w_up, conv, w_down, ln2)


def _pad_rows(rows, width):
    rows = [r.astype(F32) if r.shape[0] == width else jnp.pad(r.astype(F32), (0, width - r.shape[0])) for r in rows]
    return jnp.pad(jnp.stack(rows), ((0, SUBLANES - len(rows)), (0, 0)))


def _layer_params(l, w_in, rwkv_conv, w0, w_b, a0, a_b, g_b, k_k, k_a, r_k, gn_g, gn_b, q_norm_g, w_uq,
                  kv_norm_g, w_ukv, w_o, ln1_g, ln1_b, w_up, ffn_conv_w, ffn_conv_b, w_down, ln2_g, ln2_b):
    c = D_RWKV
    wi = w_in[l]
    o_mla = RWKV_COLS
    w_kr = wi[:, o_mla + Q_LORA + KV_LORA:]
    pad_l = jnp.zeros((D_MODEL, QK_NOPE), F32)
    pad_r = jnp.zeros((D_MODEL, LANES - QK_DIM), F32)
    w_in_p = jnp.concatenate(
        [wi[:, :o_mla + Q_LORA + KV_LORA], pad_l, w_kr, pad_r], axis=1).astype(BF16)

    def lora_bd(w):
        z = jnp.zeros_like(w[0])
        return jnp.concatenate([jnp.concatenate([w[0], z], 1), jnp.concatenate([z, w[1]], 1)], 0).astype(BF16)

    head = jnp.arange(2 * LANES) // RWKV_HEAD
    ones_bd = (head[:, None] == head[None, :]).astype(BF16)

    wq = w_uq[l].reshape(Q_LORA, N_HEADS, QK_DIM)
    q_nope, q_rope = wq[..., :QK_NOPE], wq[..., QK_NOPE:]
    zq = lambda n: jnp.zeros((Q_LORA, N_HEADS, n), F32)
    wq = jnp.concatenate([q_nope, q_rope, zq(LANES - QK_DIM)], -1).reshape(Q_LORA, D_HEADS).astype(BF16)
    wkv = w_ukv[l].reshape(KV_LORA, N_HEADS, QK_NOPE + V_HEAD)
    zk = jnp.zeros((KV_LORA, N_HEADS, HEAD_PAD - QK_NOPE), F32)
    wkn = jnp.concatenate([wkv[..., :QK_NOPE], zk], -1).reshape(KV_LORA, D_HEADS).astype(BF16)
    wv = wkv[..., QK_NOPE:].reshape(KV_LORA, N_HEADS * V_HEAD).astype(BF16)
    wo = w_o[l]

    return {
        "w_in": w_in_p,
        "conv": _pad_rows(list(rwkv_conv[l]), RWKV_COLS),
        "w0": _pad_rows([w0[l].reshape(-1)], 2 * c),
        "wb": lora_bd(w_b[l]),
        "a0": _pad_rows([a0[l].reshape(-1)], 2 * c),
        "ab": lora_bd(a_b[l]),
        "gb": g_b[l].astype(BF16),
        "vec": _pad_rows([k_k[l], k_a[l], r_k[l].reshape(-1)], c),
        "ones_bd": ones_bd,
        "norm_g": _pad_rows([q_norm_g[l], kv_norm_g[l]], Q_LORA),
        "wq": wq, "wkn": wkn, "wv": wv,
        "head_sel": (jnp.arange(D_HEADS)[:, None] // HEAD_PAD == jnp.arange(LANES)[None, :]).astype(BF16),
        "gn": _pad_rows([gn_g[l], gn_b[l]], c),
        "wo_r": wo[:c].astype(BF16),
        "wo_m": wo[c:].astype(BF16),
        "ln1": _pad_rows([ln1_g[l], ln1_b[l]], D_MODEL),
        "w_up": w_up[l].astype(BF16),
        "ffn_conv": _pad_rows(list(ffn_conv_w[l]) + [ffn_conv_b[l]], 2 * D_FF),
        "w_down": w_down[l].astype(BF16),
        "ln2": _pad_rows([ln2_g[l], ln2_b[l]], D_MODEL),
    }


def _rope_tables(n_ctx, n_lat):
    rows = n_lat // GRID_W
    row = jnp.repeat(jnp.arange(rows, dtype=F32), GRID_W)
    col = jnp.tile(jnp.arange(GRID_W, dtype=F32), rows)
    n_pairs = QK_ROPE // 4
    inv = ROPE_BASE ** (-jnp.arange(n_pairs, dtype=F32) / n_pairs)
    ang = jnp.concatenate([row[:, None] * inv, col[:, None] * inv], axis=-1)
    cos = jnp.repeat(jnp.cos(ang), 2, axis=-1)
    sin = jnp.repeat(jnp.sin(ang), 2, axis=-1) * jnp.tile(jnp.array([-1.0, 1.0], F32), QK_ROPE // 2)
    tt = n_ctx + n_lat
    cos_t = jnp.ones((tt, LANES), F32).at[n_ctx:, QK_NOPE:QK_DIM].set(cos)
    sin_t = jnp.zeros((tt, LANES), F32).at[n_ctx:, QK_NOPE:QK_DIM].set(sin)
    return cos_t, sin_t


def kernel(x, c, ctx, c_ctx, w_ada, b_ada, w_in, rwkv_conv, w0, w_b, a0, a_b, g_b, k_k, k_a, r_k, gn_g, gn_b,
           q_norm_g, w_uq, kv_norm_g, w_ukv, w_o, ln1_g, ln1_b, w_up, ffn_conv_w, ffn_conv_b, w_down, ln2_g, ln2_b):
    bsz, n_lat, _ = x.shape
    n_ctx = ctx.shape[1]
    assert n_ctx % TM == 0 and n_lat % TM == 0 and n_lat % GRID_W == 0
    n_ct = n_ctx // TM
    n_l = w_ada.shape[0]

    ada_rows = -(-(bsz + 1) // SUBLANES) * SUBLANES
    cc = jnp.zeros((ada_rows, D_MODEL), F32).at[:bsz].set(c).at[bsz].set(c_ctx)
    mods = _ada_call(cc, w_ada, b_ada).reshape(n_l, ada_rows, 6, D_MODEL)
    cos_t, sin_t = _rope_tables(n_ctx, n_lat)
    xs = (ctx, x)

    for l in range(n_l):
        prm = _layer_params(l, w_in, rwkv_conv, w0, w_b, a0, a_b, g_b, k_k, k_a, r_k, gn_g, gn_b, q_norm_g, w_uq,
                            kv_norm_g, w_ukv, w_o, ln1_g, ln1_b, w_up, ffn_conv_w, ffn_conv_b, w_down, ln2_g, ln2_b)
        m_ctx = jnp.broadcast_to(mods[l, bsz][None], (bsz, 6, D_MODEL))
        mod = jnp.stack([m_ctx, mods[l, :bsz]], axis=1)
        mod = jnp.pad(mod, ((0, 0), (0, 0), (0, SUBLANES - 6), (0, 0))).reshape(2 * bsz, SUBLANES, D_MODEL)

        last = l == n_l - 1
        t0 = n_ct if last else 0
        if last and isinstance(xs, tuple):
            xs = jnp.concatenate(xs, axis=1)
        r, v, kk, g, bv, lw, kd, bb, q, k, vm, km = _front_call(xs, mod, cos_t, sin_t, prm, n_ct)
        yf, yr = _wkv_call(r, v, kk, lw, kd, bb, n_ct)
        o = _attn_call(q, k, vm, km, n_ct, t0)
        x1 = _out_call(yf, yr, bv, g, o, xs, mod, prm, n_ct, t0)
        xs = _ffn_call(x1, mod, prm, n_ct - t0)
    return xs
```

```python
import functools
import math

import jax
import jax.numpy as jnp
from jax import lax
from jax.experimental import pallas as pl
from jax.experimental.pallas import tpu as pltpu

F32 = jnp.float32
BF16 = jnp.bfloat16

D_MODEL = 1024
DEPTH = 2
GRID_W = 64
D_RWKV = 512
RWKV_HEAD = 64
N_HEADS = 8
LORA_W = 64
LORA_A = 64
LORA_G = 128
RWKV_COLS = 3 * D_RWKV + 2 * LORA_W + 2 * LORA_A + LORA_G
QK_NOPE = 64
QK_ROPE = 32
QK_DIM = QK_NOPE + QK_ROPE
V_HEAD = 64
Q_LORA = 384
KV_LORA = 256
D_FF = 2816
ROPE_BASE = 10000.0
DEEPNORM_ALPHA = (2 * DEPTH) ** 0.25
LN_EPS = 1e-5
RMS_EPS = 1e-6
GN_EPS = 64e-5
LOG2_E = 1.4426950408889634

LANES = 128
SUBLANES = 8
HEAD_PAD = LANES
D_HEADS = N_HEADS * HEAD_PAD
TM = 256
HALO = SUBLANES
CHUNK = 64
LOG_CHUNK = CHUNK.bit_length() - 1
WKV_BATCH = 8
PAIR = 2 * RWKV_HEAD
N_PAIRS = N_HEADS // 2
VT_ROWS = 80
ATTN_HEADS = 8
ATTN_KEYS = 1024
ATTN_KEYS_BOUNDED = 2048
BOUND_LIMIT = 60.0
BOUND_SLACK = 1.01
FF_SUB = 768
FF_DOWN = 2
VMEM_LIMIT = 56 << 20


def _cparams(sem):
    return pltpu.CompilerParams(dimension_semantics=sem, vmem_limit_bytes=VMEM_LIMIT)


def _mm(a, b):
    return jnp.dot(a.astype(BF16), b.astype(BF16), preferred_element_type=F32)


def _mm_nt(a, b):
    return lax.dot_general(a.astype(BF16), b.astype(BF16), (((1,), (1,)), ((), ())),
                           preferred_element_type=F32)


def _split2(x):
    hi = x.astype(BF16)
    lo = (x - hi.astype(F32)).astype(BF16)
    return hi, lo


def _mm3(a, b):
    ah, al = _split2(a)
    bh, bl = _split2(b)
    d = functools.partial(jnp.dot, preferred_element_type=F32)
    return d(ah, bh) + (d(ah, bl) + d(al, bh))


def _mm_exact_rhs(a, b_bf16):
    a1, a2 = _split2(a)
    d = functools.partial(jnp.dot, preferred_element_type=F32)
    return d(a1, b_bf16) + d(a2, b_bf16)


def _mm_exact_lhs(a_bf16, b):
    b1 = b.astype(BF16)
    r1 = b - b1.astype(F32)
    b2 = r1.astype(BF16)
    b3 = (r1 - b2.astype(F32)).astype(BF16)
    d = functools.partial(jnp.dot, preferred_element_type=F32)
    return d(a_bf16, b1) + (d(a_bf16, b2) + d(a_bf16, b3))


def _sigmoid(x):
    return 1.0 / (1.0 + jnp.exp(-x))


def _layer_norm(x, g, b):
    mu = jnp.mean(x, axis=-1, keepdims=True)
    xc = x - mu
    var = jnp.mean(xc * xc, axis=-1, keepdims=True)
    return xc * lax.rsqrt(var + LN_EPS) * g + b


def _rms_norm(x, g):
    return x * lax.rsqrt(jnp.mean(x * x, axis=-1, keepdims=True) + RMS_EPS) * g


def _row_spec(cols, t0=0):
    return pl.BlockSpec((None, TM, cols), lambda b, i, *_: (b, i + t0, 0))


def _mod_spec(n_ct, t0=0):
    return pl.BlockSpec((None, SUBLANES, D_MODEL),
                        lambda b, i, *_: (2 * b + (i + t0 >= n_ct).astype(jnp.int32), 0, 0))


def _layer_spec(stacked_shape, l):
    nd = len(stacked_shape) - 1
    return pl.BlockSpec((None,) + tuple(stacked_shape[1:]), lambda *_: (l,) + (0,) * nd, pipeline_mode=pl.Buffered(1))


def _full_spec(shape, single_buffer=False):
    nd = len(shape)
    mode = dict(pipeline_mode=pl.Buffered(1)) if single_buffer else {}
    return pl.BlockSpec(shape, lambda *_: (0,) * nd, **mode)


def _halo_specs(cols, tt, t0=0):
    per = TM // HALO
    last = tt // HALO - 1
    prev = pl.BlockSpec((None, HALO, cols), lambda b, i, *_: (b, jnp.maximum((i + t0) * per - 1, 0), 0))
    nxt = pl.BlockSpec((None, HALO, cols), lambda b, i, *_: (b, jnp.minimum((i + t0 + 1) * per, last), 0))
    return prev, nxt


def _with_halo(x_ref, xp_ref, xn_ref, scale, shift, i, n_ct, nt):
    has_prev = (i != 0) & (i != n_ct)
    has_next = (i != n_ct - 1) & (i != nt - 1)
    mod = lambda x: x * scale + shift
    return jnp.concatenate([jnp.where(has_prev, mod(xp_ref[...]), 0.0), mod(x_ref[...]),
                            jnp.where(has_next, mod(xn_ref[...]), 0.0)], axis=0).astype(BF16)


class _Pick:
    def __init__(self, pred, a, b):
        self.pred, self.a, self.b = pred, a, b

    def __getitem__(self, idx):
        return jnp.where(self.pred, self.a[idx], self.b[idx])


def _split_specs(cols, n_ct, n_ctx, n_lat, halo):
    per = TM // HALO
    out = []
    for first, rows in ((0, n_ctx), (n_ct, n_lat)):
        last_t, last_h = rows // TM - 1, rows // HALO - 1
        t = lambda i, first=first, last_t=last_t: jnp.clip(i - first, 0, last_t)
        out.append(pl.BlockSpec((None, TM, cols), lambda b, i, t=t: (b, t(i), 0)))
        if halo:
            out.append(pl.BlockSpec((None, HALO, cols),
                                    lambda b, i, first=first, last_h=last_h: (b, jnp.clip((i - first) * per - 1, 0, last_h), 0)))
            out.append(pl.BlockSpec((None, HALO, cols),
                                    lambda b, i, first=first, last_h=last_h: (b, jnp.clip((i - first + 1) * per, 0, last_h), 0)))
    return out


def _conv3(h, taps):
    rows = TM + 2 * HALO
    lo, hi = HALO, HALO + TM
    return (pltpu.roll(h, 1, 0)[lo:hi] * taps[0:1] + h[lo:hi] * taps[1:2]
            + pltpu.roll(h, rows - 1, 0)[lo:hi] * taps[2:3])


ADA_COLS = 1536


def _ada_kernel(c_ref, w_ref, b_ref, o_ref):
    c = c_ref[...]
    o_ref[...] = _mm3(c * _sigmoid(c), w_ref[...]) + b_ref[...]


def _ada_call(cc, w_ada, b_ada):
    n_l = w_ada.shape[0]
    rows = cc.shape[0]
    return pl.pallas_call(
        _ada_kernel,
        grid=(n_l, 6 * D_MODEL // ADA_COLS),
        in_specs=[
            pl.BlockSpec((rows, D_MODEL), lambda l, j: (0, 0)),
            pl.BlockSpec((None, D_MODEL, ADA_COLS), lambda l, j: (l, 0, j)),
            pl.BlockSpec((None, 1, ADA_COLS), lambda l, j: (l, 0, j)),
        ],
        out_specs=pl.BlockSpec((None, rows, ADA_COLS), lambda l, j: (l, 0, j)),
        out_shape=jax.ShapeDtypeStruct((n_l, rows, 6 * D_MODEL), F32),
        compiler_params=_cparams(("arbitrary", "arbitrary")),
        name="ada",
    )(cc, w_ada, b_ada.reshape(n_l, 1, 6 * D_MODEL))


def _seg_sum(x, ones_bd):
    w = ones_bd.shape[0]
    return jnp.concatenate([_mm_exact_rhs(x[:, o:o + w], ones_bd) for o in range(0, x.shape[1], w)], axis=1)


def _front_kernel(x_ref, xp_ref, xn_ref, mod_ref, w_ref, conv_ref, w0_ref, wb_ref, a0_ref, ab_ref, gb_ref, vec_ref,
                  ones_ref, cos_ref, sin_ref, ng_ref, wq_ref, wkn_ref, wv_ref, hsel_ref,
                  r_ref, v_ref, kk_ref, g_ref, bv_ref, lw_ref, kd_ref, bb_ref, q_ref, k_ref, vm_ref, km_ref, *, n_ct, nt):
    i = pl.program_id(1)
    m = mod_ref[...]
    u = _with_halo(x_ref, xp_ref, xn_ref, 1.0 + m[1:2], m[0:1], i, n_ct, nt)
    d = functools.partial(jnp.dot, preferred_element_type=F32)
    c = D_RWKV
    o_lora = 3 * c
    proj = lambda o, n: d(u, w_ref[:, o:o + n])
    conv = lambda p, o, n: _conv3(p, conv_ref[:, o:o + n])
    vec = vec_ref[...]
    k_k, k_a, r_k = vec[0:1], vec[1:2], vec[2:3]
    ones_bd = ones_ref[...]

    pm = proj(RWKV_COLS, w_ref.shape[1] - RWKV_COLS)[HALO:HALO + TM]
    p_lo = proj(o_lora, RWKV_COLS - o_lora)
    cq, ckv, kr_x = pm[:, :Q_LORA], pm[:, Q_LORA:Q_LORA + KV_LORA], pm[:, Q_LORA + KV_LORA:]
    cqn = _rms_norm(cq, ng_ref[0:1, :Q_LORA]).astype(BF16)
    ckvn = _rms_norm(ckv, ng_ref[1:2, :KV_LORA]).astype(BF16)

    p_k = proj(c, c)
    z_lo = conv(p_lo, o_lora, RWKV_COLS - o_lora)
    w_lo, a_lo, g_lo = z_lo[:, :2 * LORA_W], z_lo[:, 2 * LORA_W:2 * LORA_W + 2 * LORA_A], z_lo[:, 2 * LORA_W + 2 * LORA_A:]
    lw = -math.exp(-0.5) * _sigmoid(w0_ref[0:1, :] + _mm(jnp.tanh(w_lo), wb_ref[...]))
    lw_ref[0] = lw[:, :c]
    lw_ref[1] = lw[:, c:]
    a = _sigmoid(a0_ref[0:1, :] + _mm(a_lo, ab_ref[...]))
    g_ref[...] = _mm(_sigmoid(g_lo), gb_ref[...]).astype(BF16)

    q = d(cqn, wq_ref[...])
    p_r = proj(0, c)
    k = conv(p_k, c, c)
    kk = k * k_k
    nrm = jnp.sqrt(_seg_sum(kk * kk, ones_bd))
    kk = kk / jnp.maximum(nrm, 1e-12)
    kd0 = k * (1.0 + (a[:, :c] - 1.0) * k_a)
    kd1 = k * (1.0 + (a[:, c:] - 1.0) * k_a)
    kk_ref[...] = kk.astype(BF16)
    kd_ref[0] = kd0.astype(BF16)
    kd_ref[1] = kd1.astype(BF16)
    bb_ref[0] = (kk * a[:, :c]).astype(BF16)
    bb_ref[1] = (kk * a[:, c:]).astype(BF16)

    k_nope = d(ckvn, wkn_ref[...])
    vt = d(ckvn, wv_ref[...])
    p_v = proj(2 * c, c)

    def rotary(x, cos, sin_signed):
        n = x.shape[1]
        even = (lax.broadcasted_iota(jnp.int32, x.shape, 1) & 1) == 0
        partner = jnp.where(even, pltpu.roll(x, n - 1, 1), pltpu.roll(x, 1, 1))
        return x * cos + partner * sin_signed

    cos, sin = cos_ref[...], sin_ref[...]
    cos8 = jnp.concatenate([cos] * N_HEADS, axis=1)
    sin8 = jnp.concatenate([sin] * N_HEADS, axis=1)
    q_ref[...] = (rotary(q, cos8, sin8) * (QK_DIM ** -0.5 * LOG2_E)).astype(BF16)
    kr = rotary(kr_x, cos, sin)
    k_bf = (k_nope + jnp.concatenate([kr] * N_HEADS, axis=1)).astype(BF16)
    k_ref[...] = k_bf
    k2 = _mm(k_bf.astype(F32) * k_bf.astype(F32), hsel_ref[...])
    km_ref[...] = jnp.broadcast_to(_rows_reduce(k2, jnp.maximum, jnp.max), (SUBLANES, LANES))

    r = conv(p_r, 0, c)
    r_ref[...] = r.astype(BF16)
    v = conv(p_v, 2 * c, c)
    v_ref[...] = v.astype(BF16)
    bv_ref[...] = (_seg_sum(r * (kd0 + kd1) * r_k, ones_bd) * v).astype(BF16)

    vt = vt.T
    extra = jnp.where(lax.broadcasted_iota(jnp.int32, (VT_ROWS - V_HEAD, TM), 0) == 0, 1.0, 0.0)
    vm_ref[...] = jnp.concatenate(
        [blk for h in range(N_HEADS) for blk in (vt[h * V_HEAD:(h + 1) * V_HEAD], extra)], axis=0).astype(BF16)


def _front_split_kernel(c_ref, cp_ref, cn_ref, x_ref, xp_ref, xn_ref, *rest, n_ct, nt):
    is_ctx = pl.program_id(1) < n_ct
    _front_kernel(_Pick(is_ctx, c_ref, x_ref), _Pick(is_ctx, cp_ref, xp_ref), _Pick(is_ctx, cn_ref, xn_ref), *rest,
                  n_ct=n_ct, nt=nt)


def _front_call(xs, mod, cos_t, sin_t, prm, n_ct):
    if isinstance(xs, tuple):
        ctx, x = xs
        bsz, tt = x.shape[0], ctx.shape[1] + x.shape[1]
        body, srcs = _front_split_kernel, (ctx, ctx, ctx, x, x, x)
        src_specs = _split_specs(D_MODEL, n_ct, ctx.shape[1], x.shape[1], halo=True)
    else:
        bsz, tt, _ = xs.shape
        body, srcs = _front_kernel, (xs, xs, xs)
        src_specs = [_row_spec(D_MODEL), *_halo_specs(D_MODEL, tt)]
    nt = tt // TM
    tab = pl.BlockSpec((TM, LANES), lambda b, i: (i, 0))
    shared = jax.ShapeDtypeStruct((bsz, tt, D_RWKV), BF16)
    dir_spec = pl.BlockSpec((2, None, TM, D_RWKV), lambda b, i: (0, b, i, 0))
    dir_shape = lambda dt: jax.ShapeDtypeStruct((2, bsz, tt, D_RWKV), dt)
    heads = jax.ShapeDtypeStruct((bsz, tt, D_HEADS), BF16)
    p_rwkv = [prm[n] for n in ("w_in", "conv", "w0", "wb", "a0", "ab", "gb", "vec", "ones_bd")]
    p_mla = [prm[n] for n in ("norm_g", "wq", "wkn", "wv", "head_sel")]
    return pl.pallas_call(
        functools.partial(body, n_ct=n_ct, nt=nt),
        grid=(bsz, nt),
        in_specs=src_specs + [_mod_spec(n_ct)] + [_full_spec(a.shape) for a in p_rwkv]
        + [tab, tab] + [_full_spec(a.shape) for a in p_mla],
        out_specs=[_row_spec(D_RWKV)] * 5 + [dir_spec] * 3 + [_row_spec(D_HEADS)] * 2
        + [pl.BlockSpec((None, N_HEADS * VT_ROWS, TM), lambda b, i: (b, 0, i)),
           pl.BlockSpec((None, None, SUBLANES, LANES), lambda b, i: (b, i, 0, 0))],
        out_shape=[shared] * 5 + [dir_shape(F32), dir_shape(BF16), dir_shape(BF16)] + [heads] * 2
        + [jax.ShapeDtypeStruct((bsz, N_HEADS * VT_ROWS, tt), BF16),
           jax.ShapeDtypeStruct((bsz, nt, SUBLANES, LANES), F32)],
        compiler_params=_cparams(("parallel", "parallel")),
        name="front",
    )(*srcs, mod, *p_rwkv, cos_t, sin_t, *p_mla)


def _wkv_kernel(*refs, nb):
    ins, (yf_ref, yr_ref, s_ref) = refs[:12], refs[12:]
    step = pl.program_id(1)

    @pl.when(step == 0)
    def _():
        s_ref[...] = jnp.zeros_like(s_ref)

    n2 = 2 * CHUNK
    ri = lax.broadcasted_iota(jnp.int32, (n2, n2), 0)
    ci = lax.broadcasted_iota(jnp.int32, (n2, n2), 1)
    same_head = (ri >> LOG_CHUNK) == (ci >> LOG_CHUNK)
    rt = lax.broadcasted_iota(jnp.int32, (CHUNK, n2), 0)
    ct = lax.broadcasted_iota(jnp.int32, (CHUNK, n2), 1) & (CHUNK - 1)
    eye = ct == rt
    ti = lax.broadcasted_iota(jnp.int32, (CHUNK, CHUNK), 0)
    tj = lax.broadcasted_iota(jnp.int32, (CHUNK, CHUNK), 1)
    strict = [ct < rt, ct > rt]
    incl = [ct <= rt, ct >= rt]
    tri = [jnp.where(tj <= ti, 1.0, 0.0).astype(BF16), jnp.where(tj >= ti, 1.0, 0.0).astype(BF16)]

    def bdiag(x):
        return jnp.where(same_head, jnp.concatenate([x, x], axis=0), 0.0)

    def fold_t(x):
        xt = bdiag(x).T
        return xt[:CHUNK] + xt[CHUNK:]

    ch = []
    for d in range(2):
        r_ref, v_ref, kk_ref, lw_ref, kd_ref, bb_ref = ins[6 * d:6 * d + 6]
        for n in range(nb):
            lw = lw_ref[n]
            lc = _mm_exact_lhs(tri[d], lw)
            lend = jnp.sum(lw, axis=0, keepdims=True)
            e_neg = jnp.exp(-lc)
            e_end = jnp.exp(lend - lc)
            p_end = jnp.exp(lend)
            kt_all = kk_ref[n].astype(F32) * jnp.exp(lc - lw)
            rt_all = r_ref[n].astype(F32) * jnp.exp(lc)
            kd, bb = kd_ref[n].astype(F32), bb_ref[n].astype(F32)
            kh_all, bh_all = kd * e_neg, bb * e_neg
            kc_all, bc_all = kd * e_end, bb * e_end
            v_all = v_ref[n]
            for p in range(N_PAIRS):
                sl = slice(p * PAIR, (p + 1) * PAIR)
                ch.append(dict(d=d, slot=(d * nb + n) * N_PAIRS + p, kt=kt_all[:, sl], rr=rt_all[:, sl],
                               vv=bdiag(v_all[:, sl]), kh=bdiag(kh_all[:, sl]), bh=bdiag(bh_all[:, sl]),
                               kc_t=fold_t(kc_all[:, sl]), bc_t=fold_t(bc_all[:, sl]), p_end=p_end[:, sl]))
    for c in ch:
        gram = _mm_nt(jnp.concatenate([c["kt"], c["rr"]], axis=0), jnp.concatenate([c["bh"], c["kh"]], axis=0))
        st, ic = strict[c["d"]], incl[c["d"]]
        c["a_kb"] = jnp.where(st, gram[:CHUNK, :n2], 0.0)
        c["a_kk"] = jnp.where(st, gram[:CHUNK, n2:], 0.0)
        c["a_rb"] = jnp.where(ic, gram[CHUNK:, :n2], 0.0)
        c["a_rk"] = jnp.where(ic, gram[CHUNK:, n2:], 0.0)
        c["t"] = jnp.where(eye, 1.0, 0.0) - jnp.where((rt >> 1) == (ct >> 1), c["a_kb"], 0.0)
    for c in ch:
        av = _mm(jnp.concatenate([c["a_kk"], c["a_rk"]], axis=0), c["vv"])
        c["akk_v"], c["ark_v"] = av[:CHUNK], av[CHUNK:]
        c["kc_v"] = _mm(c["kc_t"], c["vv"])
    for ls in range(1, LOG_CHUNK):
        off = ((rt >> (ls + 1)) == (ct >> (ls + 1))) & ((rt >> ls) != (ct >> ls))
        for c in ch:
            c["nt"] = _mm(jnp.where(off, c["a_kb"], 0.0), bdiag(c["t"]))
        for c in ch:
            c["t"] = c["t"] - _mm(c["t"], bdiag(c["nt"]))
    for c in ch:
        uw = _mm(c["t"], jnp.concatenate([bdiag(c["akk_v"]), bdiag(c["kt"])], axis=1))
        c["uw"] = jnp.concatenate([bdiag(uw[:, :n2]), bdiag(uw[:, n2:])], axis=1)
    for c in ch:
        c["yq"] = jnp.concatenate([c["ark_v"], c["rr"]], axis=1) - _mm(c["a_rb"], c["uw"])
        diag_p = jnp.where(eye, jnp.broadcast_to(c["p_end"], (CHUNK, n2)), 0.0)
        c["nm"] = jnp.concatenate([c["kc_v"], diag_p], axis=1) - _mm(c["bc_t"], c["uw"])
    for c in ch:
        qm_l = jnp.concatenate([c["yq"][:, n2:], c["nm"][:, n2:]], axis=0).astype(BF16)
        s_hi, s_lo = _split2(s_ref[c["slot"]])
        c["qm"] = jnp.dot(qm_l, s_hi, preferred_element_type=F32)
        c["m_lo"] = jnp.dot(qm_l[CHUNK:], s_lo, preferred_element_type=F32)
    ys = []
    for c in ch:
        ys.append(c["qm"][:CHUNK] + c["yq"][:, :n2])
        s_ref[c["slot"]] = bdiag((c["qm"][CHUNK:] + c["m_lo"]) + c["nm"][:, :n2])
    for d, y_ref in enumerate((yf_ref, yr_ref)):
        for n in range(nb):
            o = (d * nb + n) * N_PAIRS
            y_ref[n] = jnp.concatenate(ys[o:o + N_PAIRS], axis=1).astype(y_ref.dtype)


def _wkv_call(r, v, kk, lw, kd, bb, n_ct):
    bsz, tt, c = r.shape
    nc = tt // CHUNK
    n_cc = n_ct * TM // CHUNK
    nb = math.gcd(bsz, WKV_BATCH)

    def rev_chunk(s):
        return jnp.where(s < n_cc, n_cc - 1 - s, nc - 1 - (s - n_cc))

    order = (lambda s: s, rev_chunk)
    specs, args = [], []
    for d in range(2):
        shared = pl.BlockSpec((nb, CHUNK, c), lambda b, s, d=d: (b, order[d](s), 0))
        per_dir = pl.BlockSpec((None, nb, CHUNK, c), lambda b, s, d=d: (d, b, order[d](s), 0))
        specs += [shared, shared, shared, per_dir, per_dir, per_dir]
        args += [r, v, kk, lw, kd, bb]
    out = jax.ShapeDtypeStruct((bsz, tt, c), BF16)
    return pl.pallas_call(
        functools.partial(_wkv_kernel, nb=nb),
        grid=(bsz // nb, nc),
        in_specs=specs,
        out_specs=[specs[0], specs[6]],
        out_shape=[out, out],
        scratch_shapes=[pltpu.VMEM((2 * nb * N_PAIRS, PAIR, PAIR), F32)],
        compiler_params=_cparams(("parallel", "arbitrary")),
        name="wkv_scan",
    )(*args)


def _rows_reduce(x, op, reduce):
    n = x.shape[0]
    while n > SUBLANES and n % 2 == 0:
        n //= 2
        x = op(x[:n], x[n:])
    return reduce(x, axis=0, keepdims=True)


def _attn_kernel(q_ref, k_ref, vt_ref, km_ref, o_ref, *, n_ctx_rows, n_ct, t0):
    i = pl.program_id(2) + t0
    n_keys = k_ref.shape[0]
    kc = ATTN_KEYS if (n_keys - n_ctx_rows) % ATTN_KEYS == 0 else TM
    heads = range(ATTN_HEADS)
    hs = [slice(g * HEAD_PAD, (g + 1) * HEAD_PAD) for g in heads]

    scores = lambda c, g: _mm_nt(k_ref[c[0]:c[0] + c[1], hs[g]], q_ref[:, hs[g]])
    ctx_chunk = [(0, n_ctx_rows)]
    kb = ATTN_KEYS_BOUNDED if (n_keys - n_ctx_rows) % ATTN_KEYS_BOUNDED == 0 else TM
    bounded_lat = ctx_chunk + [(s, kb) for s in range(n_ctx_rows, n_keys, kb)]

    km2 = jnp.max(jnp.max(km_ref[...], axis=0), axis=0, keepdims=True)
    lane = lax.broadcasted_iota(jnp.int32, km2.shape, 1)
    bound = []
    for g in heads:
        qf = q_ref[:, hs[g]].astype(F32)
        q2 = jnp.max(jnp.sum(qf * qf, axis=1, keepdims=True), axis=0, keepdims=True)
        k2 = jnp.max(jnp.where(lane == pl.program_id(1) * ATTN_HEADS + g, km2, 0.0), axis=1, keepdims=True)
        bound.append(jnp.sqrt(q2 * k2) * BOUND_SLACK)
    bound_ok = functools.reduce(jnp.maximum, [jnp.max(b) for b in bound]) < BOUND_LIMIT

    def finish(acc):
        outs = [acc[g][:V_HEAD] / acc[g][V_HEAD:V_HEAD + 1] for g in heads]
        o_ref[...] = jnp.concatenate(outs, axis=0).T.astype(o_ref.dtype)

    def attend_bounded(chunks):
        pieces = [(c, g) for c in chunks for g in heads]
        acc = [None] * ATTN_HEADS
        s_next = scores(*pieces[0])
        for n, (c, g) in enumerate(pieces):
            s = s_next
            if n + 1 < len(pieces):
                s_next = scores(*pieces[n + 1])
            pv = _mm(vt_ref[g * VT_ROWS:(g + 1) * VT_ROWS, c[0]:c[0] + c[1]], jnp.exp2(s - bound[g]))
            acc[g] = pv if acc[g] is None else acc[g] + pv
        finish(acc)

    def attend(chunks):
        s_next = [scores(chunks[0], g) for g in heads]
        m, acc = ([None] * ATTN_HEADS for _ in range(2))
        for n, c in enumerate(chunks):
            s = s_next
            if n + 1 < len(chunks):
                s_next = [scores(chunks[n + 1], g) for g in heads]
            m_c = [_rows_reduce(s[g], jnp.maximum, jnp.max) for g in heads]
            m_new = [m_c[g] if m[g] is None else jnp.maximum(m[g], m_c[g]) for g in heads]
            p = [jnp.exp2(s[g] - m_new[g]) for g in heads]
            pv = [_mm(vt_ref[g * VT_ROWS:(g + 1) * VT_ROWS, c[0]:c[0] + c[1]], p[g]) for g in heads]
            for g in heads:
                acc[g] = pv[g] if m[g] is None else jnp.exp2(m[g] - m_new[g]) * acc[g] + pv[g]
                m[g] = m_new[g]
        finish(acc)

    lat_chunks = ctx_chunk + [(s, kc) for s in range(n_ctx_rows, n_keys, kc)]
    is_lat = i >= n_ct
    pl.when(is_lat & bound_ok)(lambda: attend_bounded(bounded_lat))
    pl.when(is_lat & ~bound_ok)(lambda: attend(lat_chunks))
    if t0 < n_ct:
        pl.when(~is_lat & bound_ok)(lambda: attend_bounded(ctx_chunk))
        pl.when(~is_lat & ~bound_ok)(lambda: attend(ctx_chunk))


def _attn_call(q, k, vt, km, n_ct, t0):
    bsz, tt, _ = q.shape
    nt = tt // TM
    w = ATTN_HEADS * HEAD_PAD
    q_spec = pl.BlockSpec((None, TM, w), lambda b, h, i: (b, i + t0, h))
    k_spec = pl.BlockSpec((None, tt, w), lambda b, h, i: (b, 0, h))
    vt_spec = pl.BlockSpec((None, ATTN_HEADS * VT_ROWS, tt), lambda b, h, i: (b, h, 0))
    o_spec = pl.BlockSpec((None, TM, ATTN_HEADS * V_HEAD), lambda b, h, i: (b, i + t0, h))
    km_spec = pl.BlockSpec((None, nt, SUBLANES, LANES), lambda b, h, i: (b, 0, 0, 0))
    return pl.pallas_call(
        functools.partial(_attn_kernel, n_ctx_rows=n_ct * TM, n_ct=n_ct, t0=t0),
        grid=(bsz, N_HEADS // ATTN_HEADS, nt - t0),
        in_specs=[q_spec, k_spec, vt_spec, km_spec],
        out_specs=o_spec,
        out_shape=jax.ShapeDtypeStruct((bsz, tt, N_HEADS * V_HEAD), BF16),
        compiler_params=_cparams(("parallel", "parallel", "arbitrary")),
        name="attention",
    )(q, k, vt, km)


def _out_kernel(yf_ref, yr_ref, bv_ref, g_ref, o_ref, x_ref, mod_ref, gn_ref, ones_ref, wor_ref, wom_ref, ln_ref,
                out_ref):
    ones_bd = ones_ref[...]
    y = yf_ref[...].astype(F32) + yr_ref[...].astype(F32)
    inv_n = 1.0 / RWKV_HEAD
    yc = y - _seg_sum(y, ones_bd) * inv_n
    var = _seg_sum(yc * yc, ones_bd) * inv_n
    gn = yc * lax.rsqrt(var + GN_EPS) * gn_ref[0:1] + gn_ref[1:2]
    ro = (gn + bv_ref[...]) * g_ref[...]
    m = _mm(ro, wor_ref[...]) + jnp.dot(o_ref[...], wom_ref[...], preferred_element_type=F32)
    mod = mod_ref[...]
    out_ref[...] = _layer_norm(DEEPNORM_ALPHA * x_ref[...] + mod[2:3] * m, ln_ref[0:1], ln_ref[1:2])


def _out_split_kernel(yf_ref, yr_ref, bv_ref, g_ref, o_ref, c_ref, x_ref, *rest, n_ct):
    _out_kernel(yf_ref, yr_ref, bv_ref, g_ref, o_ref, _Pick(pl.program_id(1) < n_ct, c_ref, x_ref), *rest)


def _out_call(yf, yr, bv, g, o, xs, mod, prm, n_ct, t0):
    if isinstance(xs, tuple):
        assert t0 == 0
        ctx, x = xs
        bsz, tt = x.shape[0], ctx.shape[1] + x.shape[1]
        body, srcs = functools.partial(_out_split_kernel, n_ct=n_ct), (ctx, x)
        src_specs = _split_specs(D_MODEL, n_ct, ctx.shape[1], x.shape[1], halo=False)
    else:
        bsz, tt, _ = xs.shape
        body, srcs, src_specs = _out_kernel, (xs,), [_row_spec(D_MODEL, t0)]
    nt = tt // TM
    params = [prm[n] for n in ("gn", "ones_bd", "wo_r", "wo_m", "ln1")]
    return pl.pallas_call(
        body,
        grid=(bsz, nt - t0),
        in_specs=[_row_spec(D_RWKV, t0)] * 4 + [_row_spec(N_HEADS * V_HEAD, t0)] + src_specs + [_mod_spec(n_ct, t0)]
        + [_full_spec(a.shape) for a in params],
        out_specs=_row_spec(D_MODEL),
        out_shape=jax.ShapeDtypeStruct((bsz, tt - t0 * TM, D_MODEL), F32),
        compiler_params=_cparams(("parallel", "parallel")),
        name="mixer_out",
    )(yf, yr, bv, g, o, *srcs, mod, *params)


def _ffn_kernel(x_ref, xp_ref, xn_ref, mod_ref, wup_ref, conv_ref, wd_ref, ln_ref, out_ref, *, n_ct, nt):
    i = pl.program_id(1)
    mod = mod_ref[...]
    u = _with_halo(x_ref, xp_ref, xn_ref, 1.0 + mod[4:5], mod[3:4], i, n_ct, nt)

    subs = [(o, min(FF_SUB, D_FF - o)) for o in range(0, D_FF, FF_SUB)]

    def up(j):
        o, w = subs[j]
        hs = []
        for oo in (o, D_FF + o):
            h = jnp.dot(u, wup_ref[:, oo:oo + w], preferred_element_type=F32)
            taps = conv_ref[:, oo:oo + w]
            hs.append(_conv3(h, taps) + taps[3:4])
        return hs

    acc = None
    acts = []
    h_next = up(0)
    for j in range(len(subs)):
        hg, hv = h_next
        if j + 1 < len(subs):
            h_next = up(j + 1)
        half = 0.5 * hg
        acts.append(((half + half * jnp.tanh(half)) * hv).astype(BF16))
        if len(acts) == FF_DOWN or j + 1 == len(subs):
            lo, hi = subs[j + 1 - len(acts)][0], subs[j][0] + subs[j][1]
            part = jnp.dot(jnp.concatenate(acts, axis=1), wd_ref[lo:hi, :], preferred_element_type=F32)
            acc = part if acc is None else acc + part
            acts = []
    out_ref[...] = _layer_norm(DEEPNORM_ALPHA * x_ref[...] + mod[5:6] * acc, ln_ref[0:1], ln_ref[1:2])


def _ffn_call(xs, mod, prm, n_ct):
    bsz, tt, _ = xs.shape
    nt = tt // TM
    prev, nxt = _halo_specs(D_MODEL, tt)
    w_up, conv, w_down, ln2 = prm["w_up"], prm["ffn_conv"], prm["w_down"], prm["ln2"]
    return pl.pallas_call(
        functools.partial(_ffn_kernel, n_ct=n_ct, nt=nt),
        grid=(bsz, nt),
        in_specs=[_row_spec(D_MODEL), prev, nxt, _mod_spec(n_ct), _layer_spec(w_up.shape, prm["layer"]),
                  _full_spec(conv.shape), _layer_spec(w_down.shape, prm["layer"]), _full_spec(ln2.shape)],
        out_specs=_row_spec(D_MODEL),
        out_shape=jax.ShapeDtypeStruct((bsz, tt, D_MODEL), F32),
        compiler_params=_cparams(("parallel", "parallel")),
        name="conv_ffn",
    )(xs, xs, xs, mod, w_up, conv, w_down, ln2)


def _pad_rows(rows, width):
    rows = [r.astype(F32) if r.shape[0] == width else jnp.pad(r.astype(F32), (0, width - r.shape[0])) for r in rows]
    return jnp.pad(jnp.stack(rows), ((0, SUBLANES - len(rows)), (0, 0)))


def _layer_params(l, w_in, rwkv_conv, w0, w_b, a0, a_b, g_b, k_k, k_a, r_k, gn_g, gn_b, q_norm_g, w_uq,
                  kv_norm_g, w_ukv, w_o, ln1_g, ln1_b, w_up, ffn_conv_w, ffn_conv_b, w_down, ln2_g, ln2_b):
    c = D_RWKV
    wi = w_in[l]
    o_mla = RWKV_COLS
    w_kr = wi[:, o_mla + Q_LORA + KV_LORA:]
    pad_l = jnp.zeros((D_MODEL, QK_NOPE), F32)
    pad_r = jnp.zeros((D_MODEL, LANES - QK_DIM), F32)
    w_in_p = jnp.concatenate(
        [wi[:, :o_mla + Q_LORA + KV_LORA], pad_l, w_kr, pad_r], axis=1).astype(BF16)

    def lora_bd(w):
        z = jnp.zeros_like(w[0])
        return jnp.concatenate([jnp.concatenate([w[0], z], 1), jnp.concatenate([z, w[1]], 1)], 0).astype(BF16)

    head = jnp.arange(2 * LANES) // RWKV_HEAD
    ones_bd = (head[:, None] == head[None, :]).astype(BF16)

    wq = w_uq[l].reshape(Q_LORA, N_HEADS, QK_DIM)
    q_nope, q_rope = wq[..., :QK_NOPE], wq[..., QK_NOPE:]
    zq = lambda n: jnp.zeros((Q_LORA, N_HEADS, n), F32)
    wq = jnp.concatenate([q_nope, q_rope, zq(LANES - QK_DIM)], -1).reshape(Q_LORA, D_HEADS).astype(BF16)
    wkv = w_ukv[l].reshape(KV_LORA, N_HEADS, QK_NOPE + V_HEAD)
    zk = jnp.zeros((KV_LORA, N_HEADS, HEAD_PAD - QK_NOPE), F32)
    wkn = jnp.concatenate([wkv[..., :QK_NOPE], zk], -1).reshape(KV_LORA, D_HEADS).astype(BF16)
    wv = wkv[..., QK_NOPE:].reshape(KV_LORA, N_HEADS * V_HEAD).astype(BF16)
    wo = w_o[l]

    return {
        "w_in": w_in_p,
        "conv": _pad_rows(list(rwkv_conv[l]), RWKV_COLS),
        "w0": _pad_rows([w0[l].reshape(-1)], 2 * c),
        "wb": lora_bd(w_b[l]),
        "a0": _pad_rows([a0[l].reshape(-1)], 2 * c),
        "ab": lora_bd(a_b[l]),
        "gb": g_b[l].astype(BF16),
        "vec": _pad_rows([k_k[l], k_a[l], r_k[l].reshape(-1)], c),
        "ones_bd": ones_bd,
        "norm_g": _pad_rows([q_norm_g[l], kv_norm_g[l]], Q_LORA),
        "wq": wq, "wkn": wkn, "wv": wv,
        "head_sel": (jnp.arange(D_HEADS)[:, None] // HEAD_PAD == jnp.arange(LANES)[None, :]).astype(BF16),
        "gn": _pad_rows([gn_g[l], gn_b[l]], c),
        "wo_r": wo[:c].astype(BF16),
        "wo_m": wo[c:].astype(BF16),
        "ln1": _pad_rows([ln1_g[l], ln1_b[l]], D_MODEL),
        "layer": l,
        "w_up": w_up,
        "ffn_conv": _pad_rows(list(ffn_conv_w[l]) + [ffn_conv_b[l]], 2 * D_FF),
        "w_down": w_down,
        "ln2": _pad_rows([ln2_g[l], ln2_b[l]], D_MODEL),
    }


def _rope_tables(n_ctx, n_lat):
    rows = n_lat // GRID_W
    row = jnp.repeat(jnp.arange(rows, dtype=F32), GRID_W)
    col = jnp.tile(jnp.arange(GRID_W, dtype=F32), rows)
    n_pairs = QK_ROPE // 4
    inv = ROPE_BASE ** (-jnp.arange(n_pairs, dtype=F32) / n_pairs)
    ang = jnp.concatenate([row[:, None] * inv, col[:, None] * inv], axis=-1)
    cos = jnp.repeat(jnp.cos(ang), 2, axis=-1)
    sin = jnp.repeat(jnp.sin(ang), 2, axis=-1) * jnp.tile(jnp.array([-1.0, 1.0], F32), QK_ROPE // 2)
    tt = n_ctx + n_lat
    cos_t = jnp.ones((tt, LANES), F32).at[n_ctx:, QK_NOPE:QK_DIM].set(cos)
    sin_t = jnp.zeros((tt, LANES), F32).at[n_ctx:, QK_NOPE:QK_DIM].set(sin)
    return cos_t, sin_t


def kernel(x, c, ctx, c_ctx, w_ada, b_ada, w_in, rwkv_conv, w0, w_b, a0, a_b, g_b, k_k, k_a, r_k, gn_g, gn_b,
           q_norm_g, w_uq, kv_norm_g, w_ukv, w_o, ln1_g, ln1_b, w_up, ffn_conv_w, ffn_conv_b, w_down, ln2_g, ln2_b):
    bsz, n_lat, _ = x.shape
    n_ctx = ctx.shape[1]
    assert n_ctx % TM == 0 and n_lat % TM == 0 and n_lat % GRID_W == 0
    n_ct = n_ctx // TM
    n_l = w_ada.shape[0]

    ada_rows = -(-(bsz + 1) // SUBLANES) * SUBLANES
    cc = jnp.zeros((ada_rows, D_MODEL), F32).at[:bsz].set(c).at[bsz].set(c_ctx)
    mods = _ada_call(cc, w_ada, b_ada).reshape(n_l, ada_rows, 6, D_MODEL)
    cos_t, sin_t = _rope_tables(n_ctx, n_lat)
    w_up, w_down = w_up.astype(BF16), w_down.astype(BF16)
    xs = (ctx, x)

    for l in range(n_l):
        prm = _layer_params(l, w_in, rwkv_conv, w0, w_b, a0, a_b, g_b, k_k, k_a, r_k, gn_g, gn_b, q_norm_g, w_uq,
                            kv_norm_g, w_ukv, w_o, ln1_g, ln1_b, w_up, ffn_conv_w, ffn_conv_b, w_down, ln2_g, ln2_b)
        m_ctx = jnp.broadcast_to(mods[l, bsz][None], (bsz, 6, D_MODEL))
        mod = jnp.stack([m_ctx, mods[l, :bsz]], axis=1)
        mod = jnp.pad(mod, ((0, 0), (0, 0), (0, SUBLANES - 6), (0, 0))).reshape(2 * bsz, SUBLANES, D_MODEL)

        last = l == n_l - 1
        t0 = n_ct if last else 0
        if last and isinstance(xs, tuple):
            xs = jnp.concatenate(xs, axis=1)
        r, v, kk, g, bv, lw, kd, bb, q, k, vm, km = _front_call(xs, mod, cos_t, sin_t, prm, n_ct)
        yf, yr = _wkv_call(r, v, kk, lw, kd, bb, n_ct)
        o = _attn_call(q, k, vm, km, n_ct, t0)
        x1 = _out_call(yf, yr, bv, g, o, xs, mod, prm, n_ct, t0)
        xs = _ffn_call(x1, mod, prm, n_ct - t0)
    return xs
```
